```python
import math
import jax
import jax.numpy as jnp
from jax import lax
import numpy as np

D_MODEL = 2048
BATCH = 8
SEQ = 2048
DEPTH = 2

MEM_LEN = 256
RMS_EPS = 1e-6
ML_HEADS = 4
ML_DQK = 128
ML_DV = 256
ML_CHUNK = 64
SSM_HEADS = 16
SSM_HEADDIM = 64
SSM_DINNER = SSM_HEADS * SSM_HEADDIM
SSM_GROUPS = 2
SSM_STATE = 128
SSM_CONV = 4
SSM_CHUNK = 128
SSM_CONV_CH = SSM_DINNER + 2 * SSM_GROUPS * SSM_STATE
MLA_HEADS = 8
MLA_Q_LORA = 512
MLA_KV_LORA = 256
MLA_NOPE = 128
MLA_ROPE = 64
MLA_V = 128
ROPE_THETA = 10000.0
ATTN_BLOCK = 128
SWA_HEADS = 16
SWA_KV_HEADS = 4
SWA_HEAD_DIM = 64
SWA_WINDOW = 128
N_BRANCH = 4
BRANCH_W = 1024
XA_HEADS = 4
XA_HEAD_DIM = 128
FFN_DIM = 7168
N_EXPERTS = 8
TOP_K = 2
MOE_BLOCK = 512
N_DENSE = (DEPTH + 1) // 2
N_MOE = DEPTH // 2

IN_SPLITS = (
    ML_HEADS * ML_DQK,
    ML_HEADS * ML_DQK,
    ML_HEADS * ML_DV,
    ML_HEADS * ML_DV,
    ML_HEADS,
    ML_HEADS,
    SSM_DINNER,
    SSM_CONV_CH,
    SSM_HEADS,
    MLA_Q_LORA,
    MLA_KV_LORA,
    MLA_ROPE,
    SWA_HEADS * SWA_HEAD_DIM,
    SWA_KV_HEADS * SWA_HEAD_DIM,
    SWA_KV_HEADS * SWA_HEAD_DIM,
    N_BRANCH * D_MODEL,
)
IN_TOTAL = sum(IN_SPLITS)
IN_OFFSETS = tuple(int(v) for v in np.cumsum(IN_SPLITS)[:-1])

kernel_name = 'hybrid_gated_mlstm_ssd_mla_swa_moe'


def rms_norm(x, w):
    xf = x.astype(jnp.float32)
    y = xf * lax.rsqrt(jnp.mean(xf * xf, axis=-1, keepdims=True) + RMS_EPS)
    return (y * w.astype(jnp.float32)).astype(x.dtype)


def rope_tables(seq, dim):
    inv_freq = 1.0 / (ROPE_THETA ** (jnp.arange(0, dim, 2, dtype=jnp.float32) / dim))
    ang = jnp.arange(seq, dtype=jnp.float32)[:, None] * inv_freq[None, :]
    return jnp.cos(ang), jnp.sin(ang)


def apply_rope(x, cos, sin):
    x1, x2 = jnp.split(x, 2, axis=-1)
    c = cos[None, :, None, :].astype(x.dtype)
    s = sin[None, :, None, :].astype(x.dtype)
    return jnp.concatenate([x1 * c - x2 * s, x2 * c + x1 * s], axis=-1)


def mlstm(q, k, v, i_pre, f_pre):
    f32 = jnp.float32
    bsz, seq, nh, dk = q.shape
    dv = v.shape[-1]
    L = ML_CHUNK
    nc = seq // L

    def chunks(t):
        t = t.astype(f32).reshape((bsz, nc, L, nh) + t.shape[3:])
        return jnp.moveaxis(t, 3, 1)

    qc = chunks(q)
    kc = chunks(k) * dk ** -0.5
    vc = chunks(v)
    ig = chunks(i_pre)
    b = jnp.cumsum(jax.nn.log_sigmoid(chunks(f_pre)), axis=-1)
    causal = jnp.tril(jnp.ones((L, L), dtype=bool))
    dmat = jnp.where(causal, b[..., :, None] - b[..., None, :] + ig[..., None, :], -jnp.inf)
    m_intra = jnp.max(dmat, axis=-1)
    a = b[..., -1:] - b + ig
    m_loc = jnp.max(a, axis=-1)
    wl = jnp.exp(a - m_loc[..., None])
    s_loc = jnp.einsum('bhclv,bhclk->bhcvk', vc * wl[..., None], kc)
    n_loc = jnp.einsum('bhcl,bhclk->bhck', wl, kc)
    b_tot = b[..., -1]

    def step(carry, inp):
        c_st, n_st, m_st = carry
        bt, ml, sl, nl = inp
        m_new = jnp.maximum(bt + m_st, ml)
        da = jnp.exp(bt + m_st - m_new)
        db = jnp.exp(ml - m_new)
        c_new = da[..., None, None] * c_st + db[..., None, None] * sl
        n_new = da[..., None] * n_st + db[..., None] * nl
        return (c_new, n_new, m_new), (c_st, n_st, m_st)

    init = (jnp.zeros((bsz, nh, dv, dk), f32), jnp.zeros((bsz, nh, dk), f32), jnp.zeros((bsz, nh), f32))
    xs = tuple(jnp.moveaxis(t, 2, 0) for t in (b_tot, m_loc, s_loc, n_loc))
    _, (c_prev, n_prev, m_prev) = lax.scan(step, init, xs)
    c_prev = jnp.moveaxis(c_prev, 0, 2)
    n_prev = jnp.moveaxis(n_prev, 0, 2)
    m_prev = jnp.moveaxis(m_prev, 0, 2)

    g = b + m_prev[..., None]
    m_s = jnp.maximum(g, m_intra)
    p = jnp.exp(dmat - m_s[..., None]) * jnp.einsum('bhcsk,bhctk->bhcst', qc, kc)
    inter = jnp.exp(g - m_s)
    num = jnp.einsum('bhcst,bhctv->bhcsv', p, vc) + inter[..., None] * jnp.einsum('bhcvk,bhcsk->bhcsv', c_prev, qc)
    den = jnp.sum(p, axis=-1) + inter * jnp.einsum('bhck,bhcsk->bhcs', n_prev, qc)
    h = num / jnp.maximum(jnp.abs(den), jnp.exp(-m_s))[..., None]
    h = jnp.moveaxis(h, 1, 3).reshape(bsz, seq, nh, dv)
    return h.astype(q.dtype)


def causal_conv(x, w, b):
    K = w.shape[0]
    seq = x.shape[1]
    xp = jnp.pad(x, ((0, 0), (K - 1, 0), (0, 0)))
    y = xp[:, 0:seq] * w[0]
    for j in range(1, K):
        y = y + xp[:, j:j + seq] * w[j]
    return y + b


def ssd(x, dt, a_neg, bm, cm):
    f32 = jnp.float32
    bsz, seq, nh, hp = x.shape
    ng, ns = bm.shape[2], bm.shape[3]
    R = nh // ng
    L = SSM_CHUNK
    nc = seq // L
    xs = (x.astype(f32) * dt[..., None]).reshape(bsz, nc, L, ng, R, hp)
    a = jnp.transpose((dt * a_neg).reshape(bsz, nc, L, ng, R), (0, 3, 4, 1, 2))
    a_cs = jnp.cumsum(a, axis=-1)
    bc = bm.astype(f32).reshape(bsz, nc, L, ng, ns)
    cc = cm.astype(f32).reshape(bsz, nc, L, ng, ns)
    causal = jnp.tril(jnp.ones((L, L), dtype=bool))
    decay = jnp.exp(jnp.where(causal, a_cs[..., :, None] - a_cs[..., None, :], -jnp.inf))
    cb = jnp.einsum('bcsgn,bctgn->bgcst', cc, bc)
    y_diag = jnp.einsum('bgrcst,bctgrp->bcsgrp', decay * cb[:, :, None], xs)
    decay_st = jnp.transpose(jnp.exp(a_cs[..., -1:] - a_cs), (0, 3, 4, 1, 2))
    states = jnp.einsum('bctgn,bctgrp->bcgrpn', bc, xs * decay_st[..., None])
    chunk_decay = jnp.moveaxis(jnp.exp(a_cs[..., -1]), 3, 0)

    def step(st, inp):
        dec, s_new = inp
        return dec[..., None, None] * st + s_new, st

    _, prev = lax.scan(step, jnp.zeros((bsz, ng, R, hp, ns), f32), (chunk_decay, jnp.moveaxis(states, 1, 0)))
    prev = jnp.moveaxis(prev, 0, 1)
    in_decay = jnp.transpose(jnp.exp(a_cs), (0, 3, 4, 1, 2))
    y_off = jnp.einsum('bcsgn,bcgrpn->bcsgrp', cc, prev) * in_decay[..., None]
    return (y_diag + y_off).reshape(bsz, seq, nh, hp)


def causal_block_attention(q, k, v, scale):
    seq = q.shape[1]
    blk = ATTN_BLOCK
    outs = []
    for i in range(seq // blk):
        kend = (i + 1) * blk
        s = jnp.einsum('bqhd,bkhd->bhqk', q[:, i * blk:kend], k[:, :kend]).astype(jnp.float32) * scale
        qpos = i * blk + jnp.arange(blk)
        mask = jnp.arange(kend)[None, :] <= qpos[:, None]
        p = jax.nn.softmax(jnp.where(mask, s, -jnp.inf), axis=-1)
        outs.append(jnp.einsum('bhqk,bkhd->bqhd', p.astype(v.dtype), v[:, :kend]))
    return jnp.concatenate(outs, axis=1)


def swa_sink_attention(q, k, v, sinks):
    f32 = jnp.float32
    bsz, seq, hq, d = q.shape
    hkv = k.shape[2]
    grp = hq // hkv
    W = SWA_WINDOW
    nb = seq // W
    qb = q.reshape(bsz, nb, W, hkv, grp, d)

    def with_prev(t):
        prev = jnp.concatenate([jnp.zeros_like(t[:, :1]), t[:, :-1]], axis=1)
        return jnp.concatenate([prev, t], axis=2)

    kk = with_prev(k.reshape(bsz, nb, W, hkv, d))
    vv = with_prev(v.reshape(bsz, nb, W, hkv, d))
    s = jnp.einsum('bnqkgd,bntkd->bnkgqt', qb, kk).astype(f32) * d ** -0.5
    qpos = jnp.arange(W)[:, None] + W
    tpos = jnp.arange(2 * W)[None, :]
    band = (tpos <= qpos) & (qpos - tpos < W)
    valid = (jnp.arange(nb)[:, None, None] * W + tpos[None] - W) >= 0
    mask = band[None] & valid
    s = jnp.where(mask[None, :, None, None], s, -jnp.inf)
    sink = jnp.broadcast_to(sinks.astype(f32).reshape(1, 1, hkv, grp, 1, 1), s.shape[:-1] + (1,))
    p = jax.nn.softmax(jnp.concatenate([s, sink], axis=-1), axis=-1)[..., :-1]
    o = jnp.einsum('bnkgqt,bntkd->bnqkgd', p.astype(v.dtype), vv)
    return o.reshape(bsz, seq, hq * d)


def hybrid_mixer(xn, w_in, ml_igate_bias, ml_fgate_bias, ml_norm, ssm_conv_w, ssm_conv_b, ssm_dt_bias,
                 ssm_a_log, ssm_d, ssm_norm, mla_q_norm, mla_w_uq, mla_kv_norm, mla_w_ukv, swa_sinks,
                 w_branch, w_out, cos, sin):
    bsz, seq, _ = xn.shape
    u = xn @ w_in
    (ml_q, ml_k, ml_v, ml_o, ml_i, ml_f, ssm_z, ssm_xbc, ssm_dt, mla_cq, mla_ckv, mla_kr,
     swa_q, swa_k, swa_v, gate_pre) = jnp.split(u, IN_OFFSETS, axis=-1)

    h_a = mlstm(ml_q.reshape(bsz, seq, ML_HEADS, ML_DQK), ml_k.reshape(bsz, seq, ML_HEADS, ML_DQK),
                ml_v.reshape(bsz, seq, ML_HEADS, ML_DV), ml_i + ml_igate_bias, ml_f + ml_fgate_bias)
    h_a = rms_norm(h_a, ml_norm.reshape(ML_HEADS, ML_DV))
    y_a = jax.nn.sigmoid(ml_o) * h_a.reshape(bsz, seq, ML_HEADS * ML_DV)

    xbc = jax.nn.silu(causal_conv(ssm_xbc, ssm_conv_w, ssm_conv_b))
    x_s, b_s, c_s = jnp.split(xbc, (SSM_DINNER, SSM_DINNER + SSM_GROUPS * SSM_STATE), axis=-1)
    x_s = x_s.reshape(bsz, seq, SSM_HEADS, SSM_HEADDIM)
    dt = jax.nn.softplus((ssm_dt + ssm_dt_bias).astype(jnp.float32))
    a_neg = -jnp.exp(ssm_a_log.astype(jnp.float32))
    y_s = ssd(x_s, dt, a_neg, b_s.reshape(bsz, seq, SSM_GROUPS, SSM_STATE),
              c_s.reshape(bsz, seq, SSM_GROUPS, SSM_STATE))
    y_s = y_s.astype(xn.dtype) + x_s * ssm_d[:, None]
    y_s = y_s.reshape(bsz, seq, SSM_DINNER) * jax.nn.silu(ssm_z)
    y_b = rms_norm(y_s.reshape(bsz, seq, SSM_GROUPS, SSM_DINNER // SSM_GROUPS),
                   ssm_norm.reshape(SSM_GROUPS, SSM_DINNER // SSM_GROUPS)).reshape(bsz, seq, SSM_DINNER)

    qf = (rms_norm(mla_cq, mla_q_norm) @ mla_w_uq).reshape(bsz, seq, MLA_HEADS, MLA_NOPE + MLA_ROPE)
    q_nope, q_rope = jnp.split(qf, (MLA_NOPE,), axis=-1)
    kvf = (rms_norm(mla_ckv, mla_kv_norm) @ mla_w_ukv).reshape(bsz, seq, MLA_HEADS, MLA_NOPE + MLA_V)
    k_nope, v_c = jnp.split(kvf, (MLA_NOPE,), axis=-1)
    k_rope = apply_rope(mla_kr.reshape(bsz, seq, 1, MLA_ROPE), cos, sin)
    q_c = jnp.concatenate([q_nope, apply_rope(q_rope, cos, sin)], axis=-1)
    k_c = jnp.concatenate([k_nope, jnp.broadcast_to(k_rope, (bsz, seq, MLA_HEADS, MLA_ROPE))], axis=-1)
    y_c = causal_block_attention(q_c, k_c, v_c, (MLA_NOPE + MLA_ROPE) ** -0.5).reshape(bsz, seq, MLA_HEADS * MLA_V)

    y_d = swa_sink_attention(swa_q.reshape(bsz, seq, SWA_HEADS, SWA_HEAD_DIM),
                             swa_k.reshape(bsz, seq, SWA_KV_HEADS, SWA_HEAD_DIM),
                             swa_v.reshape(bsz, seq, SWA_KV_HEADS, SWA_HEAD_DIM), swa_sinks)

    branches = jnp.stack([y_a, y_b, y_c, y_d], axis=2)
    gates = jax.nn.sigmoid(gate_pre).reshape(bsz, seq, N_BRANCH, D_MODEL)
    merged = jnp.sum(gates * jnp.einsum('bsnw,nwd->bsnd', branches, w_branch), axis=2)
    return merged @ w_out


def cross_attention(hn, memn, wq, wkv, wo):
    bsz, seq, _ = hn.shape
    m = memn.shape[1]
    q = (hn @ wq).reshape(bsz, seq, XA_HEADS, XA_HEAD_DIM)
    k, v = jnp.split(memn @ wkv, 2, axis=-1)
    k = k.reshape(bsz, m, XA_HEADS, XA_HEAD_DIM)
    v = v.reshape(bsz, m, XA_HEADS, XA_HEAD_DIM)
    s = jnp.einsum('bshd,bmhd->bhsm', q, k).astype(jnp.float32) * XA_HEAD_DIM ** -0.5
    p = jax.nn.softmax(s, axis=-1)
    o = jnp.einsum('bhsm,bmhd->bshd', p.astype(v.dtype), v).reshape(bsz, seq, XA_HEADS * XA_HEAD_DIM)
    return o @ wo


def swiglu(xn, w13, w2):
    h1, h3 = jnp.split(xn @ w13, 2, axis=-1)
    return (jax.nn.silu(h1) * h3) @ w2


def moe_swiglu(xn, router, w13, w2):
    bsz, seq, dm = xn.shape
    n = bsz * seq
    nk = n * TOP_K
    xf = xn.reshape(n, dm)
    logits = (xf @ router).astype(jnp.float32)
    top_val, top_idx = lax.top_k(logits, TOP_K)
    gate = jax.nn.softmax(top_val, axis=-1).astype(xn.dtype)
    e_flat = top_idx.reshape(-1).astype(jnp.int32)
    tok_flat = jnp.arange(nk, dtype=jnp.int32) // TOP_K
    g_flat = gate.reshape(-1)
    order = jnp.argsort(e_flat)
    e_sorted = e_flat[order]
    counts = jnp.bincount(e_flat, length=N_EXPERTS).astype(jnp.int32)
    padded = (counts + MOE_BLOCK - 1) // MOE_BLOCK * MOE_BLOCK
    start = jnp.cumsum(counts) - counts
    pend = jnp.cumsum(padded)
    pstart = pend - padded
    dest = pstart[e_sorted] + (jnp.arange(nk, dtype=jnp.int32) - start[e_sorted])
    cap = (-(-nk // MOE_BLOCK) + N_EXPERTS) * MOE_BLOCK
    n_blocks = cap // MOE_BLOCK
    row_tok = jnp.full((cap,), n, dtype=jnp.int32).at[dest].set(tok_flat[order])
    row_gate = jnp.zeros((cap,), xn.dtype).at[dest].set(g_flat[order])
    blk_expert = jnp.minimum(
        jnp.searchsorted(pend, jnp.arange(n_blocks, dtype=jnp.int32) * MOE_BLOCK, side='right'), N_EXPERTS - 1)
    x_pad = jnp.concatenate([xf, jnp.zeros((1, dm), xf.dtype)], axis=0)
    xg = x_pad[row_tok].reshape(n_blocks, MOE_BLOCK, dm)

    def expert_block(args):
        xb, e = args
        return swiglu(xb, w13[e], w2[e])

    yg = lax.map(expert_block, (xg, blk_expert)).reshape(cap, dm)
    y = jax.ops.segment_sum(yg * row_gate[:, None], row_tok, num_segments=n + 1)[:n]
    return y.reshape(bsz, seq, dm)


def setup_inputs(seed: int = 0) -> dict:
    key = jax.random.key(seed)
    ks = iter(jax.random.split(key, 40))
    f32 = jnp.float32

    def nrm(shape, fan_in):
        return jax.random.normal(next(ks), shape, f32) * fan_in ** -0.5

    def gain(shape):
        return 1.0 + 0.02 * jax.random.normal(next(ks), shape, f32)

    def small(shape, scale):
        return scale * jax.random.normal(next(ks), shape, f32)

    x = jax.random.normal(next(ks), (BATCH, SEQ, D_MODEL), f32)
    mem = jax.random.normal(next(ks), (BATCH, MEM_LEN, D_MODEL), f32)
    norm_mix = gain((DEPTH, D_MODEL))
    w_in = nrm((DEPTH, D_MODEL, IN_TOTAL), D_MODEL)
    ml_igate_bias = small((DEPTH, ML_HEADS), 0.1)
    ml_fgate_bias = jnp.linspace(3.0, 6.0, ML_HEADS, dtype=f32)[None, :] + small((DEPTH, ML_HEADS), 0.1)
    ml_norm = gain((DEPTH, ML_HEADS * ML_DV))
    ssm_conv_w = nrm((DEPTH, SSM_CONV, SSM_CONV_CH), SSM_CONV)
    ssm_conv_b = small((DEPTH, SSM_CONV_CH), 0.02)
    dt0 = jnp.exp(jax.random.uniform(next(ks), (DEPTH, SSM_HEADS), f32, math.log(1e-3), math.log(1e-1)))
    ssm_dt_bias = dt0 + jnp.log(-jnp.expm1(-dt0))
    ssm_a_log = jnp.log(jax.random.uniform(next(ks), (DEPTH, SSM_HEADS), f32, 1.0, 16.0))
    ssm_d = gain((DEPTH, SSM_HEADS))
    ssm_norm = gain((DEPTH, SSM_DINNER))
    mla_q_norm = gain((DEPTH, MLA_Q_LORA))
    mla_w_uq = nrm((DEPTH, MLA_Q_LORA, MLA_HEADS * (MLA_NOPE + MLA_ROPE)), MLA_Q_LORA)
    mla_kv_norm = gain((DEPTH, MLA_KV_LORA))
    mla_w_ukv = nrm((DEPTH, MLA_KV_LORA, MLA_HEADS * (MLA_NOPE + MLA_V)), MLA_KV_LORA)
    swa_sinks = small((DEPTH, SWA_HEADS), 0.5)
    w_branch = nrm((DEPTH, N_BRANCH, BRANCH_W, D_MODEL), BRANCH_W)
    w_out = nrm((DEPTH, D_MODEL, D_MODEL), D_MODEL)
    norm_cross = gain((DEPTH, D_MODEL))
    norm_mem = gain((DEPTH, D_MODEL))
    xa_wq = nrm((DEPTH, D_MODEL, XA_HEADS * XA_HEAD_DIM), D_MODEL)
    xa_wkv = nrm((DEPTH, D_MODEL, 2 * XA_HEADS * XA_HEAD_DIM), D_MODEL)
    xa_wo = nrm((DEPTH, XA_HEADS * XA_HEAD_DIM, D_MODEL), XA_HEADS * XA_HEAD_DIM)
    norm_ffn = gain((DEPTH, D_MODEL))
    ffn_w13 = nrm((N_DENSE, D_MODEL, 2 * FFN_DIM), D_MODEL)
    ffn_w2 = nrm((N_DENSE, FFN_DIM, D_MODEL), FFN_DIM)
    moe_router = nrm((N_MOE, D_MODEL, N_EXPERTS), D_MODEL)
    moe_w13 = nrm((N_MOE, N_EXPERTS, D_MODEL, 2 * FFN_DIM), D_MODEL)
    moe_w2 = nrm((N_MOE, N_EXPERTS, FFN_DIM, D_MODEL), FFN_DIM)
    norm_final = gain((D_MODEL,))
    return {
        'x': x, 'mem': mem, 'norm_mix': norm_mix, 'w_in': w_in,
        'ml_igate_bias': ml_igate_bias, 'ml_fgate_bias': ml_fgate_bias, 'ml_norm': ml_norm,
        'ssm_conv_w': ssm_conv_w, 'ssm_conv_b': ssm_conv_b, 'ssm_dt_bias': ssm_dt_bias,
        'ssm_a_log': ssm_a_log, 'ssm_d': ssm_d, 'ssm_norm': ssm_norm,
        'mla_q_norm': mla_q_norm, 'mla_w_uq': mla_w_uq, 'mla_kv_norm': mla_kv_norm, 'mla_w_ukv': mla_w_ukv,
        'swa_sinks': swa_sinks, 'w_branch': w_branch, 'w_out': w_out,
        'norm_cross': norm_cross, 'norm_mem': norm_mem, 'xa_wq': xa_wq, 'xa_wkv': xa_wkv, 'xa_wo': xa_wo,
        'norm_ffn': norm_ffn, 'ffn_w13': ffn_w13, 'ffn_w2': ffn_w2,
        'moe_router': moe_router, 'moe_w13': moe_w13, 'moe_w2': moe_w2, 'norm_final': norm_final,
    }


def reference(x, mem, norm_mix, w_in, ml_igate_bias, ml_fgate_bias, ml_norm, ssm_conv_w, ssm_conv_b,
              ssm_dt_bias, ssm_a_log, ssm_d, ssm_norm, mla_q_norm, mla_w_uq, mla_kv_norm, mla_w_ukv,
              swa_sinks, w_branch, w_out, norm_cross, norm_mem, xa_wq, xa_wkv, xa_wo, norm_ffn,
              ffn_w13, ffn_w2, moe_router, moe_w13, moe_w2, norm_final):
    seq = x.shape[1]
    cos, sin = rope_tables(seq, MLA_ROPE)
    h = x
    for l in range(DEPTH):
        xn = rms_norm(h, norm_mix[l])
        h = h + hybrid_mixer(xn, w_in[l], ml_igate_bias[l], ml_fgate_bias[l], ml_norm[l], ssm_conv_w[l],
                             ssm_conv_b[l], ssm_dt_bias[l], ssm_a_log[l], ssm_d[l], ssm_norm[l],
                             mla_q_norm[l], mla_w_uq[l], mla_kv_norm[l], mla_w_ukv[l], swa_sinks[l],
                             w_branch[l], w_out[l], cos, sin)
        h = h + cross_attention(rms_norm(h, norm_cross[l]), rms_norm(mem, norm_mem[l]),
                                xa_wq[l], xa_wkv[l], xa_wo[l])
        hn = rms_norm(h, norm_ffn[l])
        if l % 2 == 0:
            h = h + swiglu(hn, ffn_w13[l // 2], ffn_w2[l // 2])
        else:
            h = h + moe_swiglu(hn, moe_router[l // 2], moe_w13[l // 2], moe_w2[l // 2])
    return rms_norm(h, norm_final)
```

```python
import functools
import math

import jax
import jax.numpy as jnp
from jax import lax
from jax.experimental import pallas as pl
from jax.experimental.pallas import tpu as pltpu

F32 = jnp.float32
BF = jnp.bfloat16

D_MODEL = 2048
RMS_EPS = 1e-6
ML_HEADS, ML_DQK, ML_DV = 4, 128, 256
SSM_HEADS, SSM_HEADDIM, SSM_GROUPS, SSM_STATE, SSM_CONV = 16, 64, 2, 128, 4
SSM_DINNER = SSM_HEADS * SSM_HEADDIM
SSM_BC = 2 * SSM_GROUPS * SSM_STATE
MLA_HEADS, MLA_Q_LORA, MLA_KV_LORA, MLA_NOPE, MLA_ROPE, MLA_V = 8, 512, 256, 128, 64, 128
ROPE_THETA = 10000.0
SWA_HEADS, SWA_KV_HEADS, SWA_HEAD_DIM, SWA_WINDOW = 16, 4, 64, 128
N_BRANCH, BRANCH_W = 4, 1024
XA_HEADS, XA_HEAD_DIM = 4, 128
FFN_DIM = 7168
N_EXPERTS, TOP_K = 8, 2

CHUNK = 128
LANES = 128
VMEM_LIMIT = 56 * 1024 * 1024

U_ML_Q, U_ML_K, U_ML_V, U_ML_O = 0, 512, 1024, 2048
U_SSM_Z, U_SSM_X, U_SWA_Q, U_SSM_BC = 3072, 4096, 5120, 6144
U_MLA_CQ, U_MLA_CKV, U_SWA_K, U_SWA_V, U_MLA_KR = 6656, 7168, 7424, 7680, 7936
U_GATE = 8192
U_TOTAL = U_GATE + N_BRANCH * D_MODEL
_SPLITS = (512, 512, 1024, 1024, 4, 4, 1024, 1536, 16, 512, 256, 64, 1024, 256, 256, 8192)
_OFF = [0]
for _s in _SPLITS:
    _OFF.append(_OFF[-1] + _s)
(O_ML_Q, O_ML_K, O_ML_V, O_ML_O, O_ML_I, O_ML_F, O_SSM_Z, O_SSM_XBC, O_SSM_DT, O_MLA_CQ, O_MLA_CKV,
 O_MLA_KR, O_SWA_Q, O_SWA_K, O_SWA_V, O_GATE, _O_END) = _OFF


def _cparams(sem):
    return pltpu.CompilerParams(dimension_semantics=sem, vmem_limit_bytes=VMEM_LIMIT)


def _dot(a, b):
    return jnp.dot(a, b, preferred_element_type=F32)


def _dot_nt(a, b):
    return lax.dot_general(a, b, (((1,), (1,)), ((), ())), preferred_element_type=F32)


def _dot_tn(a, b):
    return lax.dot_general(a, b, (((0,), (0,)), ((), ())), preferred_element_type=F32)


def _split3(a):
    a1 = a.astype(BF)
    r = a - a1.astype(F32)
    a2 = r.astype(BF)
    a3 = (r - a2.astype(F32)).astype(BF)
    return a1, a2, a3


def _dot_sel(a, sel):
    a1, a2, a3 = _split3(a)
    return _dot(a1, sel) + _dot(a2, sel) + _dot(a3, sel)


def _rms(x, w):
    return x * lax.rsqrt(jnp.mean(x * x, axis=-1, keepdims=True) + RMS_EPS) * w


def _sigmoid(x):
    return 1.0 / (1.0 + jnp.exp(-x))


def _mm_kernel(*refs, has_norm, has_res):
    it = iter(refs)
    x_ref = next(it)
    g_ref = next(it) if has_norm else None
    w_ref = next(it)
    r_ref = next(it) if has_res else None
    o_ref = next(it)
    if has_norm:
        xn_ref = next(it)

        @pl.when(pl.program_id(1) == 0)
        def _():
            xn_ref[...] = _rms(x_ref[...].astype(F32), g_ref[...]).astype(BF)

        xv = xn_ref[...]
    else:
        xv = x_ref[...]
    acc = _dot(xv, w_ref[...])
    if has_res:
        acc = acc + r_ref[...]
    o_ref[...] = acc.astype(o_ref.dtype)


def matmul(x, w, *, norm=None, residual=None, out_dtype=None, tm=1024, tn=1024, x_col_blk=0):
    out_dtype = out_dtype or BF
    m = x.shape[0]
    k, n = w.shape
    tm, tn = min(tm, m), min(tn, n)
    assert m % tm == 0 and n % tn == 0
    in_specs = [pl.BlockSpec((tm, k), lambda i, j: (i, x_col_blk))]
    args = [x]
    scratch = []
    if norm is not None:
        in_specs.append(pl.BlockSpec((1, k), lambda i, j: (0, 0)))
        args.append(norm.reshape(1, k).astype(F32))
        scratch.append(pltpu.VMEM((tm, k), BF))
    in_specs.append(pl.BlockSpec((k, tn), lambda i, j: (0, j)))
    args.append(w)
    if residual is not None:
        in_specs.append(pl.BlockSpec((tm, tn), lambda i, j: (i, j)))
        args.append(residual)
    return pl.pallas_call(
        functools.partial(_mm_kernel, has_norm=norm is not None, has_res=residual is not None),
        grid=(m // tm, n // tn),
        in_specs=in_specs,
        out_specs=pl.BlockSpec((tm, tn), lambda i, j: (i, j)),
        out_shape=jax.ShapeDtypeStruct((m, n), out_dtype),
        scratch_shapes=scratch,
        compiler_params=_cparams(("parallel", "arbitrary")),
    )(*args)


def _cumsum_lanes(x):
    lane = lax.broadcasted_iota(jnp.int32, x.shape, 1)
    s = 1
    while s < x.shape[1]:
        x = x + jnp.where(lane >= s, pltpu.roll(x, s, axis=1), 0.0)
        s *= 2
    return x


def _softplus(x):
    return jnp.maximum(x, 0.0) + jnp.log(1.0 + jnp.exp(-jnp.abs(x)))


def _gates_kernel(i_ref, f_ref, ib_ref, fb_ref, dt_ref, dtb_ref, alog_ref, ig_ref, b_ref, dto_ref, acs_ref):
    ig_ref[...] = i_ref[...] + ib_ref[...]
    b_ref[...] = _cumsum_lanes(-_softplus(-(f_ref[...] + fb_ref[...])))
    dt = _softplus(dt_ref[...] + dtb_ref[...])
    dto_ref[...] = dt
    acs_ref[...] = _cumsum_lanes(dt * (-jnp.exp(alog_ref[...])))


def recurrence_gates(i_rows, f_rows, ib, fb, dt_rows, dtb, alog):
    r1, r2 = i_rows.shape[0], dt_rows.shape[0]
    shp = lambda r: jax.ShapeDtypeStruct((r, CHUNK), F32)
    return pl.pallas_call(
        _gates_kernel,
        out_shape=(shp(r1), shp(r1), shp(r2), shp(r2)),
    )(i_rows, f_rows, ib, fb, dt_rows, dtb, alog)


def _mlstm_kernel(q_ref, k_ref, v_ref, o_ref, igr_ref, br_ref, bc_ref, nw_ref, y_ref, ct_ref, n_ref, m_ref):
    L = CHUNK

    @pl.when(pl.program_id(1) == 0)
    def _():
        ct_ref[...] = jnp.zeros_like(ct_ref)
        n_ref[...] = jnp.zeros_like(n_ref)
        m_ref[...] = jnp.zeros_like(m_ref)

    row = lax.broadcasted_iota(jnp.int32, (L, L), 0)
    col = lax.broadcasted_iota(jnp.int32, (L, L), 1)
    causal = col <= row
    diag = col == row
    scale = ML_DQK ** -0.5
    outs = []
    for h in range(ML_HEADS):
        q = q_ref[:, h * ML_DQK:(h + 1) * ML_DQK]
        k = (k_ref[:, h * ML_DQK:(h + 1) * ML_DQK].astype(F32) * scale).astype(BF)
        v = v_ref[:, h * ML_DV:(h + 1) * ML_DV]
        bcol = bc_ref[:, h:h + 1]
        brow = br_ref[h:h + 1, :]
        igrow = igr_ref[h:h + 1, :]
        m_prev = m_ref[h][0:1, 0:1]
        n_prev = n_ref[h][0:1, :]
        ct_prev = ct_ref[h]

        dmat = jnp.where(causal, bcol - brow + igrow, -jnp.inf)
        m_intra = jnp.max(dmat, axis=1, keepdims=True)
        g = bcol + m_prev
        m_s = jnp.maximum(g, m_intra)
        p = jnp.exp(dmat - m_s) * _dot_nt(q, k)
        inter = jnp.exp(g - m_s)
        num = _dot(p.astype(BF), v) + inter * _dot(q, ct_prev.astype(BF))
        den = jnp.sum(p, axis=1, keepdims=True) + inter * jnp.sum(q.astype(F32) * n_prev, axis=1, keepdims=True)
        hh = num / jnp.maximum(jnp.abs(den), jnp.exp(-m_s))
        hn = _rms(hh, nw_ref[:, h * ML_DV:(h + 1) * ML_DV])
        outs.append(_sigmoid(o_ref[:, h * ML_DV:(h + 1) * ML_DV].astype(F32)) * hn)

        b_tot = brow[:, L - 1:L]
        a = b_tot - brow + igrow
        m_loc = jnp.max(a, axis=1, keepdims=True)
        wl = jnp.exp(a - m_loc)
        kw = _dot(jnp.where(diag, wl, 0.0).astype(BF), k)
        m_new = jnp.maximum(b_tot + m_prev, m_loc)
        da = jnp.exp(b_tot + m_prev - m_new)
        db = jnp.exp(m_loc - m_new)
        ct_ref[h] = da * ct_prev + db * _dot_tn(kw.astype(BF), v)
        n_ref[h] = jnp.broadcast_to(da * n_prev + db * jnp.sum(kw, axis=0, keepdims=True), n_ref.shape[1:])
        m_ref[h] = jnp.broadcast_to(m_new, m_ref.shape[1:])
    y_ref[...] = jnp.concatenate(outs, axis=1).astype(y_ref.dtype)


def mlstm_branch(u3, ig_rows, b_rows, b_cols, norm_w):
    bsz, seq, _ = u3.shape
    nc = seq // CHUNK
    L = CHUNK
    ublk = lambda width, off: pl.BlockSpec((None, L, width), lambda b, c: (b, c, off // width))
    rows = pl.BlockSpec((None, None, ML_HEADS, L), lambda b, c: (b, c, 0, 0))
    return pl.pallas_call(
        _mlstm_kernel,
        grid=(bsz, nc),
        in_specs=[ublk(512, U_ML_Q), ublk(512, U_ML_K), ublk(1024, U_ML_V), ublk(1024, U_ML_O), rows, rows,
                  pl.BlockSpec((None, L, ML_HEADS), lambda b, c: (b, c, 0)),
                  pl.BlockSpec((1, ML_HEADS * ML_DV), lambda b, c: (0, 0))],
        out_specs=pl.BlockSpec((None, L, BRANCH_W), lambda b, c: (b, c, 0)),
        out_shape=jax.ShapeDtypeStruct((bsz, seq, BRANCH_W), BF),
        scratch_shapes=[pltpu.VMEM((ML_HEADS, ML_DQK, ML_DV), F32), pltpu.VMEM((ML_HEADS, 8, ML_DQK), F32),
                        pltpu.VMEM((ML_HEADS, 8, LANES), F32)],
        compiler_params=_cparams(("parallel", "arbitrary")),
    )(u3, u3, u3, u3, ig_rows, b_rows, b_cols, norm_w.reshape(1, -1).astype(F32))


def _ssd_kernel(z_ref, x_ref, bc_ref, dt_ref, ac_ref, ar_ref, cw_ref, cb_ref, d_ref, nw_ref, y_ref, xs_ref, st_ref):
    L = CHUNK
    P, R, NS = SSM_HEADDIM, SSM_HEADS // SSM_GROUPS, SSM_STATE
    GW = R * P

    @pl.when(pl.program_id(1) == 0)
    def _():
        xs_ref[0:8, :] = jnp.zeros((8, xs_ref.shape[1]), F32)
        st_ref[...] = jnp.zeros_like(st_ref)

    xs_ref[8:, :] = jnp.concatenate([x_ref[...], bc_ref[...]], axis=1).astype(F32)
    conv = cb_ref[...] + cw_ref[SSM_CONV - 1:SSM_CONV, :] * xs_ref[8:8 + L, :]
    for sft in range(1, SSM_CONV):
        conv = conv + cw_ref[SSM_CONV - 1 - sft:SSM_CONV - sft, :] * xs_ref[8 - sft:8 - sft + L, :]
    xs_ref[0:8, :] = xs_ref[L:L + 8, :]
    xbc = conv * _sigmoid(conv)
    xh = xbc[:, :SSM_DINNER]
    bmat = xbc[:, SSM_DINNER:SSM_DINNER + SSM_GROUPS * NS].astype(BF)
    cmat = xbc[:, SSM_DINNER + SSM_GROUPS * NS:].astype(BF)

    dtc = dt_ref[...]
    ac = ac_ref[...]
    ar = ar_ref[...]
    a_last = ac[L - 1:L, :]
    hsel = (lax.broadcasted_iota(jnp.int32, (SSM_HEADS, SSM_DINNER), 1) // P
            == lax.broadcasted_iota(jnp.int32, (SSM_HEADS, SSM_DINNER), 0))
    expand = jnp.where(hsel, 1.0, 0.0).astype(BF)
    stack = jnp.concatenate([dtc, jnp.exp(a_last - ac), jnp.exp(ac),
                             jnp.broadcast_to(jnp.exp(a_last), (8, SSM_HEADS))], axis=0)
    ex = _dot_sel(stack, expand)
    dt_full, dst_full, ind_full = ex[0:L], ex[L:2 * L], ex[2 * L:3 * L]
    cdec_full = ex[3 * L:3 * L + 1]
    xdt = xh * dt_full
    xdt_b = xdt.astype(BF)
    xw_b = (xdt * dst_full).astype(BF)

    row = lax.broadcasted_iota(jnp.int32, (L, L), 0)
    col = lax.broadcasted_iota(jnp.int32, (L, L), 1)
    causal = col <= row
    ys = []
    for g in range(SSM_GROUPS):
        bg = bmat[:, g * NS:(g + 1) * NS]
        cg = cmat[:, g * NS:(g + 1) * NS]
        cb = _dot_nt(cg, bg)
        st_prev = st_ref[g]
        yoff = _dot(cg, st_prev.astype(BF))
        st_ref[g] = cdec_full[:, g * GW:(g + 1) * GW] * st_prev + _dot_tn(bg, xw_b[:, g * GW:(g + 1) * GW])
        for r in range(R):
            h = g * R + r
            dec = jnp.exp(jnp.where(causal, ac[:, h:h + 1] - ar[h:h + 1, :], -jnp.inf))
            yd = _dot((dec * cb).astype(BF), xdt_b[:, h * P:(h + 1) * P])
            ys.append(yd + yoff[:, r * P:(r + 1) * P] * ind_full[:, h * P:(h + 1) * P])
    y = jnp.concatenate(ys, axis=1) + xh * d_ref[...]
    zf = z_ref[...].astype(F32)
    y = y * (zf * _sigmoid(zf))
    y_ref[...] = jnp.concatenate(
        [_rms(y[:, g * GW:(g + 1) * GW], nw_ref[:, g * GW:(g + 1) * GW]) for g in range(SSM_GROUPS)],
        axis=1).astype(y_ref.dtype)


def ssd_branch(u3, dt_cols, acs_cols, acs_rows, conv_w, conv_b, d_full, norm_w):
    bsz, seq, _ = u3.shape
    nc = seq // CHUNK
    L = CHUNK
    ublk = lambda width, off: pl.BlockSpec((None, L, width), lambda b, c: (b, c, off // width))
    cols = pl.BlockSpec((None, L, SSM_HEADS), lambda b, c: (b, c, 0))
    const = lambda shape: pl.BlockSpec(shape, lambda b, c: (0, 0))
    cch = SSM_DINNER + SSM_BC
    return pl.pallas_call(
        _ssd_kernel,
        grid=(bsz, nc),
        in_specs=[ublk(1024, U_SSM_Z), ublk(1024, U_SSM_X), ublk(512, U_SSM_BC), cols, cols,
                  pl.BlockSpec((None, None, SSM_HEADS, L), lambda b, c: (b, c, 0, 0)),
                  const((SSM_CONV, cch)), const((1, cch)), const((1, SSM_DINNER)), const((1, SSM_DINNER))],
        out_specs=pl.BlockSpec((None, L, BRANCH_W), lambda b, c: (b, c, 0)),
        out_shape=jax.ShapeDtypeStruct((bsz, seq, BRANCH_W), BF),
        scratch_shapes=[pltpu.VMEM((L + 8, cch), F32),
                        pltpu.VMEM((SSM_GROUPS, SSM_STATE, SSM_DINNER // SSM_GROUPS), F32)],
        compiler_params=_cparams(("parallel", "arbitrary")),
    )(u3, u3, u3, dt_cols, acs_cols, acs_rows, conv_w.astype(F32), conv_b.reshape(1, cch).astype(F32),
      d_full.reshape(1, -1).astype(F32), norm_w.reshape(1, -1).astype(F32))


def _rope128(x, cs):
    t = x.astype(F32) * cs
    return t + pltpu.roll(t, MLA_ROPE, axis=1)


def _mla_kernel(q_ref, csq_ref, kv_ref, kr_ref, csk_ref, o_ref, qn_s, qr_s, m_s, l_s, acc_s, *, tq):
    qi = pl.program_id(2)
    ki = pl.program_id(3)
    scale = (MLA_NOPE + MLA_ROPE) ** -0.5

    @pl.when(ki == 0)
    def _():
        qn_s[...] = q_ref[:, :MLA_NOPE]
        qr_s[...] = _rope128(q_ref[:, MLA_NOPE:], csq_ref[...]).astype(BF)
        m_s[...] = jnp.full(m_s.shape, -jnp.inf, F32)
        l_s[...] = jnp.zeros_like(l_s)
        acc_s[...] = jnp.zeros_like(acc_s)

    @pl.when(ki <= qi)
    def _():
        kn = kv_ref[:, :MLA_NOPE]
        v = kv_ref[:, MLA_NOPE:]
        lane = lax.broadcasted_iota(jnp.int32, (tq, LANES), 1)
        kr = jnp.where(lane < MLA_ROPE, _rope128(kr_ref[...], csk_ref[...]), 0.0).astype(BF)
        s = (_dot_nt(qn_s[...], kn) + _dot_nt(qr_s[...], kr)) * scale
        row = qi * tq + lax.broadcasted_iota(jnp.int32, (tq, tq), 0)
        col = ki * tq + lax.broadcasted_iota(jnp.int32, (tq, tq), 1)
        s = jnp.where(col <= row, s, -jnp.inf)
        m_new = jnp.maximum(m_s[...], jnp.max(s, axis=1, keepdims=True))
        alpha = jnp.exp(m_s[...] - m_new)
        p = jnp.exp(s - m_new)
        l_s[...] = alpha * l_s[...] + jnp.sum(p, axis=1, keepdims=True)
        acc_s[...] = alpha * acc_s[...] + _dot(p.astype(BF), v)
        m_s[...] = m_new

    @pl.when(ki == qi)
    def _():
        o_ref[...] = (acc_s[...] / l_s[...]).astype(o_ref.dtype)


def mla_attention(qf3, kvf3, u3, cs, tq=512):
    bsz, seq, _ = qf3.shape
    tq = min(tq, seq)
    nq = seq // tq
    hw = MLA_NOPE + LANES
    return pl.pallas_call(
        functools.partial(_mla_kernel, tq=tq),
        grid=(bsz, MLA_HEADS, nq, nq),
        in_specs=[pl.BlockSpec((None, tq, hw), lambda b, h, qi, ki: (b, qi, h)),
                  pl.BlockSpec((tq, LANES), lambda b, h, qi, ki: (qi, 0)),
                  pl.BlockSpec((None, tq, MLA_NOPE + MLA_V), lambda b, h, qi, ki: (b, jnp.minimum(ki, qi), h)),
                  pl.BlockSpec((None, tq, LANES), lambda b, h, qi, ki: (b, jnp.minimum(ki, qi), U_MLA_KR // LANES)),
                  pl.BlockSpec((tq, LANES), lambda b, h, qi, ki: (jnp.minimum(ki, qi), 0))],
        out_specs=pl.BlockSpec((None, tq, MLA_V), lambda b, h, qi, ki: (b, qi, h)),
        out_shape=jax.ShapeDtypeStruct((bsz, seq, MLA_HEADS * MLA_V), BF),
        scratch_shapes=[pltpu.VMEM((tq, MLA_NOPE), BF), pltpu.VMEM((tq, LANES), BF), pltpu.VMEM((tq, 1), F32),
                        pltpu.VMEM((tq, 1), F32), pltpu.VMEM((tq, MLA_V), F32)],
        compiler_params=_cparams(("parallel", "parallel", "parallel", "arbitrary")),
    )(qf3, cs, kvf3, u3, cs)


def _swa_kernel(q_ref, kc_ref, kp_ref, vc_ref, vp_ref, sink_ref, o_ref):
    W, d = SWA_WINDOW, SWA_HEAD_DIM
    grp = SWA_HEADS // SWA_KV_HEADS
    n = pl.program_id(1)
    i = lax.broadcasted_iota(jnp.int32, (grp * W, 2 * W), 0) & (W - 1)
    j = lax.broadcasted_iota(jnp.int32, (grp * W, 2 * W), 1)
    valid = (j > i) & (j <= i + W) & ((n > 0) | (j >= W))
    rowid = lax.broadcasted_iota(jnp.int32, (grp * W, 1), 0)
    outs = []
    for kh in range(SWA_KV_HEADS):
        sl = slice(kh * d, (kh + 1) * d)
        kk = jnp.concatenate([kp_ref[:, sl], kc_ref[:, sl]], axis=0)
        vv = jnp.concatenate([vp_ref[:, sl], vc_ref[:, sl]], axis=0)
        qg = jnp.concatenate([q_ref[:, (kh * grp + g) * d:(kh * grp + g + 1) * d] for g in range(grp)], axis=0)
        s = jnp.where(valid, _dot_nt(qg, kk) * d ** -0.5, -jnp.inf)
        sink = jnp.zeros((grp * W, 1), F32)
        for g in range(grp):
            hq = kh * grp + g
            sink = jnp.where(rowid // W == g, sink_ref[0:1, hq:hq + 1], sink)
        m = jnp.maximum(jnp.max(s, axis=1, keepdims=True), sink)
        p = jnp.exp(s - m)
        den = jnp.sum(p, axis=1, keepdims=True) + jnp.exp(sink - m)
        o = _dot((p / den).astype(BF), vv)
        for g in range(grp):
            outs.append(o[g * W:(g + 1) * W, :])
    o_ref[...] = jnp.concatenate(outs, axis=1).astype(o_ref.dtype)


def swa_branch(u3, sinks):
    bsz, seq, _ = u3.shape
    W = SWA_WINDOW
    kvw = SWA_KV_HEADS * SWA_HEAD_DIM
    cur = lambda width, off: pl.BlockSpec((None, W, width), lambda b, n: (b, n, off // width))
    prev = lambda width, off: pl.BlockSpec((None, W, width), lambda b, n: (b, jnp.maximum(n - 1, 0), off // width))
    return pl.pallas_call(
        _swa_kernel,
        grid=(bsz, seq // W),
        in_specs=[cur(1024, U_SWA_Q), cur(kvw, U_SWA_K), prev(kvw, U_SWA_K), cur(kvw, U_SWA_V), prev(kvw, U_SWA_V),
                  pl.BlockSpec((1, SWA_HEADS), lambda b, n: (0, 0))],
        out_specs=pl.BlockSpec((None, W, BRANCH_W), lambda b, n: (b, n, 0)),
        out_shape=jax.ShapeDtypeStruct((bsz, seq, BRANCH_W), BF),
        compiler_params=_cparams(("parallel", "parallel")),
    )(u3, u3, u3, u3, u3, sinks.reshape(1, -1).astype(F32))


def _merge_kernel(ya_ref, yb_ref, yc_ref, yd_ref, w_ref, g0_ref, g1_ref, g2_ref, g3_ref, o_ref):
    acc = None
    for n, (y_ref, g_ref) in enumerate(((ya_ref, g0_ref), (yb_ref, g1_ref), (yc_ref, g2_ref), (yd_ref, g3_ref))):
        t = _sigmoid(g_ref[...].astype(F32)) * _dot(y_ref[...], w_ref[n])
        acc = t if acc is None else acc + t
    o_ref[...] = acc.astype(o_ref.dtype)


def gated_merge(ya, yb, yc, yd, w_branch, u2, tm=1024, tn=512):
    m = ya.shape[0]
    tm = min(tm, m)
    ysp = pl.BlockSpec((tm, BRANCH_W), lambda i, j: (i, 0))
    gsp = lambda n: pl.BlockSpec((tm, tn), lambda i, j: (i, (U_GATE + n * D_MODEL) // tn + j))
    return pl.pallas_call(
        _merge_kernel,
        grid=(m // tm, D_MODEL // tn),
        in_specs=[ysp, ysp, ysp, ysp, pl.BlockSpec((N_BRANCH, BRANCH_W, tn), lambda i, j: (0, 0, j)),
                  gsp(0), gsp(1), gsp(2), gsp(3)],
        out_specs=pl.BlockSpec((tm, tn), lambda i, j: (i, j)),
        out_shape=jax.ShapeDtypeStruct((m, D_MODEL), BF),
        compiler_params=_cparams(("parallel", "parallel")),
    )(ya, yb, yc, yd, w_branch, u2, u2, u2, u2)


def _xattn_kernel(q_ref, kv_ref, o_ref):
    d = XA_HEAD_DIM
    outs = []
    for h in range(XA_HEADS):
        s = _dot_nt(q_ref[:, h * d:(h + 1) * d], kv_ref[:, h * d:(h + 1) * d]) * d ** -0.5
        p = jnp.exp(s - jnp.max(s, axis=1, keepdims=True))
        p = p / jnp.sum(p, axis=1, keepdims=True)
        outs.append(_dot(p.astype(BF), kv_ref[:, (XA_HEADS + h) * d:(XA_HEADS + h + 1) * d]))
    o_ref[...] = jnp.concatenate(outs, axis=1).astype(o_ref.dtype)


def cross_attention(q3, kv3, tq=512):
    bsz, seq, w = q3.shape
    mlen = kv3.shape[1]
    tq = min(tq, seq)
    return pl.pallas_call(
        _xattn_kernel,
        grid=(bsz, seq // tq),
        in_specs=[pl.BlockSpec((None, tq, w), lambda b, i: (b, i, 0)),
                  pl.BlockSpec((None, mlen, 2 * w), lambda b, i: (b, 0, 0))],
        out_specs=pl.BlockSpec((None, tq, w), lambda b, i: (b, i, 0)),
        out_shape=jax.ShapeDtypeStruct((bsz, seq, w), BF),
        compiler_params=_cparams(("parallel", "parallel")),
    )(q3, kv3)


def _ffn_kernel(h_ref, g_ref, w1_ref, w3_ref, w2_ref, o_ref, xn_ref, acc_ref):
    j = pl.program_id(1)

    @pl.when(j == 0)
    def _():
        hv = h_ref[...]
        xn_ref[...] = _rms(hv, g_ref[...]).astype(BF)
        acc_ref[...] = hv

    xn = xn_ref[...]
    h1 = _dot(xn, w1_ref[...])
    h3 = _dot(xn, w3_ref[...])
    act = (h1 * _sigmoid(h1) * h3).astype(BF)
    acc_ref[...] += _dot(act, w2_ref[...])

    @pl.when(j == pl.num_programs(1) - 1)
    def _():
        o_ref[...] = acc_ref[...]


def dense_ffn(h, norm_w, w13, w2, tm=512, tf=1024):
    m, d = h.shape
    f = w2.shape[0]
    tm = min(tm, m)
    nf = f // tf
    return pl.pallas_call(
        _ffn_kernel,
        grid=(m // tm, nf),
        in_specs=[pl.BlockSpec((tm, d), lambda i, j: (i, 0)), pl.BlockSpec((1, d), lambda i, j: (0, 0)),
                  pl.BlockSpec((d, tf), lambda i, j: (0, j)), pl.BlockSpec((d, tf), lambda i, j: (0, j + nf)),
                  pl.BlockSpec((tf, d), lambda i, j: (j, 0))],
        out_specs=pl.BlockSpec((tm, d), lambda i, j: (i, 0)),
        out_shape=jax.ShapeDtypeStruct((m, d), F32),
        scratch_shapes=[pltpu.VMEM((tm, d), BF), pltpu.VMEM((tm, d), F32)],
        compiler_params=_cparams(("parallel", "arbitrary")),
    )(h, norm_w.reshape(1, d).astype(F32), w13, w13, w2)


ROUTE_TM = 512
MOE_TM = 512


def _router_kernel(h_ref, g_ref, r_ref, hn_ref, route_ref, cnt_ref, carry_ref):
    tm = h_ref.shape[0]

    @pl.when(pl.program_id(0) == 0)
    def _():
        carry_ref[...] = jnp.zeros_like(carry_ref)

    hn = _rms(h_ref[...], g_ref[...])
    hn_ref[...] = hn.astype(BF)
    a1, a2, _ = _split3(hn)
    r1, r2, _ = _split3(r_ref[...])
    lane = lax.broadcasted_iota(jnp.int32, (tm, LANES), 1)
    logits = jnp.where(lane < N_EXPERTS, _dot(a1, r1) + (_dot(a1, r2) + _dot(a2, r1)), -jnp.inf)
    v0 = jnp.max(logits, axis=1, keepdims=True)
    i0 = jnp.min(jnp.where(logits == v0, lane, LANES), axis=1, keepdims=True)
    rest = jnp.where(lane == i0, -jnp.inf, logits)
    v1 = jnp.max(rest, axis=1, keepdims=True)
    i1 = jnp.min(jnp.where(rest == v1, lane, LANES), axis=1, keepdims=True)
    ex = jnp.exp(v1 - v0)
    g0 = 1.0 / (1.0 + ex)
    g1 = ex / (1.0 + ex)
    sel0, sel1 = lane == i0, lane == i1
    onehot = jnp.where(sel0 | sel1, 1.0, 0.0)
    below = (lax.broadcasted_iota(jnp.int32, (tm, tm), 1) < lax.broadcasted_iota(jnp.int32, (tm, tm), 0))
    before = carry_ref[0:1, :] + _dot(jnp.where(below, 1.0, 0.0).astype(BF), onehot.astype(BF))
    rank0 = jnp.sum(jnp.where(sel0, before, 0.0), axis=1, keepdims=True)
    rank1 = jnp.sum(jnp.where(sel1, before, 0.0), axis=1, keepdims=True)
    total = carry_ref[0:1, :] + jnp.sum(onehot, axis=0, keepdims=True)
    carry_ref[...] = jnp.broadcast_to(total, carry_ref.shape)
    cnt_ref[...] = jnp.broadcast_to(total, cnt_ref.shape)
    out = jnp.zeros((tm, LANES), F32)
    for pos, val in enumerate((i0.astype(F32), i1.astype(F32), g0, g1, rank0, rank1)):
        out = jnp.where(lane == pos, val, out)
    route_ref[...] = out


def moe_router(h, norm_w, router):
    m, d = h.shape
    tm = min(ROUTE_TM, m)
    rpad = jnp.zeros((d, LANES), F32).at[:, :N_EXPERTS].set(router.astype(F32))
    return pl.pallas_call(
        _router_kernel,
        grid=(m // tm,),
        in_specs=[pl.BlockSpec((tm, d), lambda i: (i, 0)), pl.BlockSpec((1, d), lambda i: (0, 0)),
                  pl.BlockSpec((d, LANES), lambda i: (0, 0))],
        out_specs=[pl.BlockSpec((tm, d), lambda i: (i, 0)), pl.BlockSpec((tm, LANES), lambda i: (i, 0)),
                   pl.BlockSpec((8, LANES), lambda i: (0, 0))],
        out_shape=[jax.ShapeDtypeStruct((m, d), BF), jax.ShapeDtypeStruct((m, LANES), F32),
                   jax.ShapeDtypeStruct((8, LANES), F32)],
        scratch_shapes=[pltpu.VMEM((8, LANES), F32)],
        compiler_params=_cparams(("arbitrary",)),
    )(h, norm_w.reshape(1, d).astype(F32), rpad)


def _gather_rows_kernel(idx_ref, src_ref, o_ref, sem):
    tg = o_ref.shape[0]
    base = pl.program_id(0) * tg

    def row_copy(r):
        return pltpu.make_async_copy(src_ref.at[idx_ref[base + r]], o_ref.at[r], sem)

    def start(r, c):
        row_copy(r).start()
        return c

    def wait(r, c):
        row_copy(r).wait()
        return c

    lax.fori_loop(0, tg, start, 0)
    lax.fori_loop(0, tg, wait, 0)


def gather_rows(src3, idx, tg=256):
    n = idx.shape[0]
    tg = min(tg, n)
    return pl.pallas_call(
        _gather_rows_kernel,
        grid_spec=pltpu.PrefetchScalarGridSpec(
            num_scalar_prefetch=1, grid=(n // tg,),
            in_specs=[pl.BlockSpec(memory_space=pl.ANY)],
            out_specs=pl.BlockSpec((tg,) + src3.shape[1:], lambda i, idx_ref: (i, 0, 0)),
            scratch_shapes=[pltpu.SemaphoreType.DMA(())]),
        out_shape=jax.ShapeDtypeStruct((n,) + src3.shape[1:], src3.dtype),
        compiler_params=_cparams(("arbitrary",)),
    )(idx, src3)


def _moe_up_kernel(te_ref, nt_ref, x_ref, w1_ref, w3_ref, o_ref):
    used = pl.program_id(1) < nt_ref[0]

    @pl.when(used)
    def _():
        x = x_ref[...]
        h1 = _dot(x, w1_ref[...])
        h3 = _dot(x, w3_ref[...])
        o_ref[...] = (h1 * _sigmoid(h1) * h3).astype(o_ref.dtype)

    @pl.when(jnp.logical_not(used))
    def _():
        o_ref[...] = jnp.zeros_like(o_ref)


def _moe_down_kernel(te_ref, nt_ref, a_ref, w2_ref, g_ref, o_ref):
    used = pl.program_id(1) < nt_ref[0]

    @pl.when(used)
    def _():
        o_ref[...] = (_dot(a_ref[...], w2_ref[...]) * g_ref[...]).astype(o_ref.dtype)

    @pl.when(jnp.logical_not(used))
    def _():
        o_ref[...] = jnp.zeros_like(o_ref)


def moe_experts(xg, tile_e, ntiles, row_gate, w13, w2, tn_up=1024, tn_down=512):
    cap, d = xg.shape
    f = w2.shape[1]
    tm = MOE_TM
    nt_max = cap // tm
    nf = f // tn_up
    mt = lambda m, nt: jnp.minimum(m, nt[0] - 1)
    act = pl.pallas_call(
        _moe_up_kernel,
        grid_spec=pltpu.PrefetchScalarGridSpec(
            num_scalar_prefetch=2, grid=(nf, nt_max),
            in_specs=[pl.BlockSpec((tm, d), lambda j, m, te, nt: (mt(m, nt), 0)),
                      pl.BlockSpec((None, d, tn_up), lambda j, m, te, nt: (te[mt(m, nt)], 0, j)),
                      pl.BlockSpec((None, d, tn_up), lambda j, m, te, nt: (te[mt(m, nt)], 0, j + nf))],
            out_specs=pl.BlockSpec((tm, tn_up), lambda j, m, te, nt: (m, j))),
        out_shape=jax.ShapeDtypeStruct((cap, f), BF),
        compiler_params=_cparams(("arbitrary", "arbitrary")),
    )(tile_e, ntiles, xg, w13, w13)
    return pl.pallas_call(
        _moe_down_kernel,
        grid_spec=pltpu.PrefetchScalarGridSpec(
            num_scalar_prefetch=2, grid=(d // tn_down, nt_max),
            in_specs=[pl.BlockSpec((tm, f), lambda j, m, te, nt: (mt(m, nt), 0)),
                      pl.BlockSpec((None, f, tn_down), lambda j, m, te, nt: (te[mt(m, nt)], 0, j)),
                      pl.BlockSpec((tm, 1), lambda j, m, te, nt: (mt(m, nt), 0))],
            out_specs=pl.BlockSpec((tm, tn_down), lambda j, m, te, nt: (m, j))),
        out_shape=jax.ShapeDtypeStruct((cap, d), BF),
        compiler_params=_cparams(("arbitrary", "arbitrary")),
    )(tile_e, ntiles, act, w2, row_gate)


def _combine_kernel(pos_ref, h_ref, y_ref, o_ref, buf, sem):
    tc = h_ref.shape[0]
    base = pl.program_id(0) * tc

    def row_copy(r, k):
        return pltpu.make_async_copy(y_ref.at[pos_ref[TOP_K * (base + r) + k]], buf.at[k, r], sem)

    def start(r, c):
        for k in range(TOP_K):
            row_copy(r, k).start()
        return c

    def wait(r, c):
        for k in range(TOP_K):
            row_copy(r, k).wait()
        return c

    lax.fori_loop(0, tc, start, 0)
    lax.fori_loop(0, tc, wait, 0)
    acc = h_ref[...]
    for k in range(TOP_K):
        acc = acc + buf[k].astype(F32)
    o_ref[...] = acc


def moe_combine(h3, yg3, pos, tc=256):
    n = h3.shape[0]
    tc = min(tc, n)
    blk = (tc,) + h3.shape[1:]
    return pl.pallas_call(
        _combine_kernel,
        grid_spec=pltpu.PrefetchScalarGridSpec(
            num_scalar_prefetch=1, grid=(n // tc,),
            in_specs=[pl.BlockSpec(blk, lambda i, pos_ref: (i, 0, 0)), pl.BlockSpec(memory_space=pl.ANY)],
            out_specs=pl.BlockSpec(blk, lambda i, pos_ref: (i, 0, 0)),
            scratch_shapes=[pltpu.VMEM((TOP_K,) + blk, yg3.dtype), pltpu.SemaphoreType.DMA(())]),
        out_shape=jax.ShapeDtypeStruct(h3.shape, F32),
        compiler_params=_cparams(("arbitrary",)),
    )(pos, h3, yg3)


def moe_ffn(h, norm_w, router, w13, w2):
    n, d = h.shape
    tm = MOE_TM
    hn, route, cnt = moe_router(h, norm_w, router)
    expert = route[:, 0:TOP_K].astype(jnp.int32)
    gate = route[:, TOP_K:2 * TOP_K]
    rank = route[:, 2 * TOP_K:3 * TOP_K].astype(jnp.int32)
    counts = cnt[0, :N_EXPERTS].astype(jnp.int32)
    tiles = (counts + tm - 1) // tm
    tile_end = jnp.cumsum(tiles)
    dest = ((tile_end - tiles) * tm)[expert] + rank
    nt_max = -(-n * TOP_K // tm) + N_EXPERTS
    cap = nt_max * tm
    flat = dest.reshape(-1)
    row_tok = jnp.zeros((cap,), jnp.int32).at[flat].set(jnp.arange(n * TOP_K, dtype=jnp.int32) // TOP_K)
    row_gate = jnp.zeros((cap,), F32).at[flat].set(gate.reshape(-1))
    tile_e = jnp.minimum(jnp.searchsorted(tile_end, jnp.arange(nt_max, dtype=jnp.int32), side="right"),
                         N_EXPERTS - 1).astype(jnp.int32)
    ntiles = tile_end[-1:].astype(jnp.int32)
    s = d // LANES
    xg = gather_rows(hn.reshape(n, s, LANES), row_tok).reshape(cap, d)
    yg = moe_experts(xg, tile_e, ntiles, row_gate.reshape(cap, 1), w13, w2)
    out = moe_combine(h.reshape(n, s, LANES), yg.reshape(cap, s, LANES), flat)
    return out.reshape(n, d)


def _final_kernel(x_ref, w_ref, o_ref):
    o_ref[...] = _rms(x_ref[...], w_ref[...])


def final_norm(h, w, tm=512):
    m, d = h.shape
    tm = min(tm, m)
    return pl.pallas_call(
        _final_kernel,
        grid=(m // tm,),
        in_specs=[pl.BlockSpec((tm, d), lambda i: (i, 0)), pl.BlockSpec((1, d), lambda i: (0, 0))],
        out_specs=pl.BlockSpec((tm, d), lambda i: (i, 0)),
        out_shape=jax.ShapeDtypeStruct((m, d), F32),
        compiler_params=_cparams(("parallel",)),
    )(h, w.reshape(1, d).astype(F32))


def _rot_cols(w):
    half = w.shape[-1] // 2
    return jnp.concatenate([-w[..., half:], w[..., :half]], axis=-1)


def _layout_w_in(w):
    seg = lambda off, width: w[:, off:off + width]
    kr = seg(O_MLA_KR, MLA_ROPE)
    xbc = seg(O_SSM_XBC, SSM_DINNER + SSM_BC)
    main = jnp.concatenate([
        seg(O_ML_Q, 512), seg(O_ML_K, 512), seg(O_ML_V, 1024), seg(O_ML_O, 1024), seg(O_SSM_Z, 1024),
        xbc[:, :SSM_DINNER], seg(O_SWA_Q, 1024), xbc[:, SSM_DINNER:], seg(O_MLA_CQ, 512), seg(O_MLA_CKV, 256),
        seg(O_SWA_K, 256), seg(O_SWA_V, 256), kr, _rot_cols(kr), jnp.zeros((w.shape[0], LANES), w.dtype),
        seg(O_GATE, N_BRANCH * D_MODEL)], axis=1)
    small = jnp.concatenate([seg(O_ML_I, ML_HEADS), seg(O_ML_F, ML_HEADS), seg(O_SSM_DT, SSM_HEADS),
                             jnp.zeros((w.shape[0], LANES - 2 * ML_HEADS - SSM_HEADS), w.dtype)], axis=1)
    return main.astype(BF), small.astype(BF)


def _layout_w_uq(w):
    k = w.shape[0]
    w = w.reshape(k, MLA_HEADS, MLA_NOPE + MLA_ROPE)
    rope = w[..., MLA_NOPE:]
    return jnp.concatenate([w, _rot_cols(rope)], axis=-1).reshape(k, -1).astype(BF)


def _rope_table(seq):
    inv_freq = 1.0 / (ROPE_THETA ** (jnp.arange(0, MLA_ROPE, 2, dtype=F32) / MLA_ROPE))
    ang = jnp.arange(seq, dtype=F32)[:, None] * inv_freq[None, :]
    c, s = jnp.cos(ang), jnp.sin(ang)
    return jnp.concatenate([c, c, s, s], axis=1)


def hybrid_mixer(h, bsz, seq, norm_w, w_main, w_small, ml_ib, ml_fb, ml_norm, conv_w, conv_b, dt_bias, a_log,
                 ssm_d, ssm_norm, q_norm, w_uq, kv_norm, w_ukv, sinks, w_branch, w_out, cs):
    n = bsz * seq
    nc = seq // CHUNK
    u2 = matmul(h, w_main, norm=norm_w, out_dtype=BF)
    small = matmul(h, w_small, norm=norm_w, out_dtype=F32, tn=LANES)
    u3 = u2.reshape(bsz, seq, U_TOTAL)

    def to_rows(cols, heads):
        return cols.reshape(bsz, seq, heads).transpose(0, 2, 1).reshape(bsz * heads * nc, CHUNK)

    def per_row(vec, heads):
        return jnp.broadcast_to(vec.astype(F32)[None, :, None], (bsz, heads, nc)).reshape(-1, 1)

    ig, bcum, dt, acs = recurrence_gates(
        to_rows(small[:, 0:ML_HEADS], ML_HEADS), to_rows(small[:, ML_HEADS:2 * ML_HEADS], ML_HEADS),
        per_row(ml_ib, ML_HEADS), per_row(ml_fb, ML_HEADS),
        to_rows(small[:, 2 * ML_HEADS:2 * ML_HEADS + SSM_HEADS], SSM_HEADS),
        per_row(dt_bias, SSM_HEADS), per_row(a_log, SSM_HEADS))

    def as_rows(x, heads):
        return x.reshape(bsz, heads, nc, CHUNK).transpose(0, 2, 1, 3)

    def as_cols(x, heads):
        return x.reshape(bsz, heads, seq).transpose(0, 2, 1)

    ya = mlstm_branch(u3, as_rows(ig, ML_HEADS), as_rows(bcum, ML_HEADS), as_cols(bcum, ML_HEADS), ml_norm)
    yb = ssd_branch(u3, as_cols(dt, SSM_HEADS), as_cols(acs, SSM_HEADS), as_rows(acs, SSM_HEADS), conv_w, conv_b,
                    jnp.repeat(ssm_d, SSM_HEADDIM), ssm_norm)
    qf = matmul(u2, w_uq, norm=q_norm, x_col_blk=U_MLA_CQ // MLA_Q_LORA)
    kvf = matmul(u2, w_ukv, norm=kv_norm, x_col_blk=U_MLA_CKV // MLA_KV_LORA)
    yc = mla_attention(qf.reshape(bsz, seq, -1), kvf.reshape(bsz, seq, -1), u3, cs)
    yd = swa_branch(u3, sinks)
    merged = gated_merge(ya.reshape(n, -1), yb.reshape(n, -1), yc.reshape(n, -1), yd.reshape(n, -1), w_branch, u2)
    return matmul(merged, w_out, residual=h, out_dtype=F32)


def kernel(x, mem, norm_mix, w_in, ml_igate_bias, ml_fgate_bias, ml_norm, ssm_conv_w, ssm_conv_b, ssm_dt_bias, ssm_a_log, ssm_d, ssm_norm, mla_q_norm, mla_w_uq, mla_kv_norm, mla_w_ukv, swa_sinks, w_branch, w_out, norm_cross, norm_mem, xa_wq, xa_wkv, xa_wo, norm_ffn, ffn_w13, ffn_w2, moe_router, moe_w13, moe_w2, norm_final):
    bsz, seq, d = x.shape
    depth = w_in.shape[0]
    n = bsz * seq
    mlen = mem.shape[1]
    cs = _rope_table(seq)
    h = x.reshape(n, d)
    mem2 = mem.reshape(bsz * mlen, d)
    for l in range(depth):
        w_main, w_small = _layout_w_in(w_in[l])
        h = hybrid_mixer(h, bsz, seq, norm_mix[l], w_main, w_small, ml_igate_bias[l], ml_fgate_bias[l], ml_norm[l],
                         ssm_conv_w[l], ssm_conv_b[l], ssm_dt_bias[l], ssm_a_log[l], ssm_d[l], ssm_norm[l],
                         mla_q_norm[l], _layout_w_uq(mla_w_uq[l]), mla_kv_norm[l], mla_w_ukv[l].astype(BF),
                         swa_sinks[l], w_branch[l].astype(BF), w_out[l].astype(BF), cs)
        q = matmul(h, xa_wq[l].astype(BF), norm=norm_cross[l])
        kv = matmul(mem2, xa_wkv[l].astype(BF), norm=norm_mem[l])
        o = cross_attention(q.reshape(bsz, seq, -1), kv.reshape(bsz, mlen, -1))
        h = matmul(o.reshape(n, -1), xa_wo[l].astype(BF), residual=h, out_dtype=F32)
        if l % 2 == 0:
            h = dense_ffn(h, norm_ffn[l], ffn_w13[l // 2].astype(BF), ffn_w2[l // 2].astype(BF))
        else:
            h = moe_ffn(h, norm_ffn[l], moe_router[l // 2], moe_w13[l // 2].astype(BF), moe_w2[l // 2].astype(BF))
    return final_norm(h, norm_final).reshape(bsz, seq, d)
```

```python
import functools
import math

import jax
import jax.numpy as jnp
from jax import lax
from jax.experimental import pallas as pl
from jax.experimental.pallas import tpu as pltpu

F32 = jnp.float32
BF = jnp.bfloat16

D_MODEL = 2048
RMS_EPS = 1e-6
ML_HEADS, ML_DQK, ML_DV = 4, 128, 256
SSM_HEADS, SSM_HEADDIM, SSM_GROUPS, SSM_STATE, SSM_CONV = 16, 64, 2, 128, 4
SSM_DINNER = SSM_HEADS * SSM_HEADDIM
SSM_BC = 2 * SSM_GROUPS * SSM_STATE
MLA_HEADS, MLA_Q_LORA, MLA_KV_LORA, MLA_NOPE, MLA_ROPE, MLA_V = 8, 512, 256, 128, 64, 128
ROPE_THETA = 10000.0
SWA_HEADS, SWA_KV_HEADS, SWA_HEAD_DIM, SWA_WINDOW = 16, 4, 64, 128
N_BRANCH, BRANCH_W = 4, 1024
XA_HEADS, XA_HEAD_DIM = 4, 128
FFN_DIM = 7168
N_EXPERTS, TOP_K = 8, 2

CHUNK = 128
LANES = 128
VMEM_LIMIT = 56 * 1024 * 1024
LOG2E = math.log2(math.e)

U_ML_Q, U_ML_K, U_ML_V, U_ML_O = 0, 512, 1024, 2048
U_SSM_Z, U_SSM_X, U_SWA_Q, U_SSM_BC = 3072, 4096, 5120, 6144
U_MLA_CQ, U_MLA_CKV, U_SWA_K, U_SWA_V, U_MLA_KR = 6656, 7168, 7424, 7680, 7936
U_GATE = 8192
U_TOTAL = U_GATE + N_BRANCH * D_MODEL
_SPLITS = (512, 512, 1024, 1024, 4, 4, 1024, 1536, 16, 512, 256, 64, 1024, 256, 256, 8192)
_OFF = [0]
for _s in _SPLITS:
    _OFF.append(_OFF[-1] + _s)
(O_ML_Q, O_ML_K, O_ML_V, O_ML_O, O_ML_I, O_ML_F, O_SSM_Z, O_SSM_XBC, O_SSM_DT, O_MLA_CQ, O_MLA_CKV,
 O_MLA_KR, O_SWA_Q, O_SWA_K, O_SWA_V, O_GATE, _O_END) = _OFF


def _cparams(sem):
    return pltpu.CompilerParams(dimension_semantics=sem, vmem_limit_bytes=VMEM_LIMIT)


def _dot(a, b):
    return jnp.dot(a, b, preferred_element_type=F32)


def _dot_nt(a, b):
    return lax.dot_general(a, b, (((1,), (1,)), ((), ())), preferred_element_type=F32)


def _dot_tn(a, b):
    return lax.dot_general(a, b, (((0,), (0,)), ((), ())), preferred_element_type=F32)


def _split3(a):
    a1 = a.astype(BF)
    r = a - a1.astype(F32)
    a2 = r.astype(BF)
    a3 = (r - a2.astype(F32)).astype(BF)
    return a1, a2, a3


def _dot_sel(a, sel):
    a1, a2, a3 = _split3(a)
    return _dot(a1, sel) + _dot(a2, sel) + _dot(a3, sel)


def _rms(x, w):
    return x * lax.rsqrt(jnp.mean(x * x, axis=-1, keepdims=True) + RMS_EPS) * w


def _sigmoid(x):
    return 1.0 / (1.0 + jnp.exp(-x))


def _mm_kernel(*refs, has_norm, has_res):
    it = iter(refs)
    x_ref = next(it)
    g_ref = next(it) if has_norm else None
    w_ref = next(it)
    r_ref = next(it) if has_res else None
    o_ref = next(it)
    if has_norm:
        xn_ref = next(it)

        @pl.when(pl.program_id(1) == 0)
        def _():
            xn_ref[...] = _rms(x_ref[...].astype(F32), g_ref[...]).astype(BF)

        xv = xn_ref[...]
    else:
        xv = x_ref[...]
    acc = _dot(xv, w_ref[...])
    if has_res:
        acc = acc + r_ref[...]
    o_ref[...] = acc.astype(o_ref.dtype)


def matmul(x, w, *, norm=None, residual=None, out_dtype=None, tm=1024, tn=1024, x_col_blk=0):
    out_dtype = out_dtype or BF
    m = x.shape[0]
    k, n = w.shape
    tm, tn = min(tm, m), min(tn, n)
    assert m % tm == 0 and n % tn == 0
    in_specs = [pl.BlockSpec((tm, k), lambda i, j: (i, x_col_blk))]
    args = [x]
    scratch = []
    if norm is not None:
        in_specs.append(pl.BlockSpec((1, k), lambda i, j: (0, 0)))
        args.append(norm.reshape(1, k).astype(F32))
        scratch.append(pltpu.VMEM((tm, k), BF))
    in_specs.append(pl.BlockSpec((k, tn), lambda i, j: (0, j)))
    args.append(w)
    if residual is not None:
        in_specs.append(pl.BlockSpec((tm, tn), lambda i, j: (i, j)))
        args.append(residual)
    return pl.pallas_call(
        functools.partial(_mm_kernel, has_norm=norm is not None, has_res=residual is not None),
        grid=(m // tm, n // tn),
        in_specs=in_specs,
        out_specs=pl.BlockSpec((tm, tn), lambda i, j: (i, j)),
        out_shape=jax.ShapeDtypeStruct((m, n), out_dtype),
        scratch_shapes=scratch,
        compiler_params=_cparams(("parallel", "arbitrary")),
    )(*args)


def _cumsum_lanes(x):
    lane = lax.broadcasted_iota(jnp.int32, x.shape, 1)
    s = 1
    while s < x.shape[1]:
        x = x + jnp.where(lane >= s, pltpu.roll(x, s, axis=1), 0.0)
        s *= 2
    return x


def _softplus(x):
    return jnp.maximum(x, 0.0) + jnp.log(1.0 + jnp.exp(-jnp.abs(x)))


def _gates_kernel(i_ref, f_ref, ib_ref, fb_ref, dt_ref, dtb_ref, alog_ref, ig_ref, b_ref, dto_ref, acs_ref):
    ig_ref[...] = i_ref[...] + ib_ref[...]
    b_ref[...] = _cumsum_lanes(-_softplus(-(f_ref[...] + fb_ref[...])))
    dt = _softplus(dt_ref[...] + dtb_ref[...])
    dto_ref[...] = dt
    acs_ref[...] = _cumsum_lanes(dt * (-jnp.exp(alog_ref[...])))


def recurrence_gates(i_rows, f_rows, ib, fb, dt_rows, dtb, alog):
    r1, r2 = i_rows.shape[0], dt_rows.shape[0]
    shp = lambda r: jax.ShapeDtypeStruct((r, CHUNK), F32)
    return pl.pallas_call(
        _gates_kernel,
        out_shape=(shp(r1), shp(r1), shp(r2), shp(r2)),
    )(i_rows, f_rows, ib, fb, dt_rows, dtb, alog)


def _mlstm_kernel(q_ref, k_ref, v_ref, o_ref, igr_ref, br_ref, bc_ref, nw_ref, y_ref, *state):
    ct_ref, n_ref, m_ref = state[0::3], state[1::3], state[2::3]
    L = CHUNK

    @pl.when(pl.program_id(1) == 0)
    def _():
        for ref in state:
            ref[...] = jnp.zeros_like(ref)

    row = lax.broadcasted_iota(jnp.int32, (L, L), 0)
    col = lax.broadcasted_iota(jnp.int32, (L, L), 1)
    causal = col <= row
    diag = col == row
    scale = ML_DQK ** -0.5
    outs = []
    for h in range(ML_HEADS):
        q = q_ref[:, h * ML_DQK:(h + 1) * ML_DQK]
        k = (k_ref[:, h * ML_DQK:(h + 1) * ML_DQK].astype(F32) * scale).astype(BF)
        v = v_ref[:, h * ML_DV:(h + 1) * ML_DV]
        bcol = bc_ref[:, h:h + 1]
        brow = br_ref[h:h + 1, :]
        igrow = igr_ref[h:h + 1, :]
        m_prev = m_ref[h][0:1, 0:1]
        n_prev = n_ref[h][0:1, :]
        ct_prev = ct_ref[h][...]

        dmat = jnp.where(causal, bcol - brow + igrow, -jnp.inf)
        m_intra = jnp.max(dmat, axis=1, keepdims=True)
        g = bcol + m_prev
        m_s = jnp.maximum(g, m_intra)
        p = jnp.exp(dmat - m_s) * _dot_nt(q, k)
        inter = jnp.exp(g - m_s)
        num = _dot(p.astype(BF), v) + inter * _dot(q, ct_prev.astype(BF))
        den = jnp.sum(p, axis=1, keepdims=True) + inter * jnp.sum(q.astype(F32) * n_prev, axis=1, keepdims=True)
        hh = num / jnp.maximum(jnp.abs(den), jnp.exp(-m_s))
        hn = _rms(hh, nw_ref[:, h * ML_DV:(h + 1) * ML_DV])
        outs.append(_sigmoid(o_ref[:, h * ML_DV:(h + 1) * ML_DV].astype(F32)) * hn)

        b_tot = brow[:, L - 1:L]
        a = b_tot - brow + igrow
        m_loc = jnp.max(a, axis=1, keepdims=True)
        wl = jnp.exp(a - m_loc)
        kw = _dot(jnp.where(diag, wl, 0.0).astype(BF), k)
        m_new = jnp.maximum(b_tot + m_prev, m_loc)
        da = jnp.exp(b_tot + m_prev - m_new)
        db = jnp.exp(m_loc - m_new)
        ct_ref[h][...] = da * ct_prev + db * _dot_tn(kw.astype(BF), v)
        n_ref[h][...] = jnp.broadcast_to(da * n_prev + db * jnp.sum(kw, axis=0, keepdims=True), n_ref[h].shape)
        m_ref[h][...] = jnp.broadcast_to(m_new, m_ref[h].shape)
    y_ref[...] = jnp.concatenate(outs, axis=1).astype(y_ref.dtype)


def mlstm_branch(u3, ig_rows, b_rows, b_cols, norm_w):
    bsz, seq, _ = u3.shape
    nc = seq // CHUNK
    L = CHUNK
    ublk = lambda width, off: pl.BlockSpec((None, L, width), lambda b, c: (b, c, off // width))
    rows = pl.BlockSpec((None, None, ML_HEADS, L), lambda b, c: (b, c, 0, 0))
    return pl.pallas_call(
        _mlstm_kernel,
        grid=(bsz, nc),
        in_specs=[ublk(512, U_ML_Q), ublk(512, U_ML_K), ublk(1024, U_ML_V), ublk(1024, U_ML_O), rows, rows,
                  pl.BlockSpec((None, L, ML_HEADS), lambda b, c: (b, c, 0)),
                  pl.BlockSpec((1, ML_HEADS * ML_DV), lambda b, c: (0, 0))],
        out_specs=pl.BlockSpec((None, L, BRANCH_W), lambda b, c: (b, c, 0)),
        out_shape=jax.ShapeDtypeStruct((bsz, seq, BRANCH_W), BF),
        scratch_shapes=[pltpu.VMEM((ML_DQK, ML_DV), F32), pltpu.VMEM((8, ML_DQK), F32),
                        pltpu.VMEM((8, LANES), F32)] * ML_HEADS,
        compiler_params=_cparams(("parallel", "arbitrary")),
    )(u3, u3, u3, u3, ig_rows, b_rows, b_cols, norm_w.reshape(1, -1).astype(F32))


def _ssd_kernel(z_ref, x_ref, bc_ref, dt_ref, ac_ref, ar_ref, cw_ref, cb_ref, d_ref, nw_ref, y_ref, xs_ref, st_ref):
    L = CHUNK
    P, R, NS = SSM_HEADDIM, SSM_HEADS // SSM_GROUPS, SSM_STATE
    GW = R * P

    @pl.when(pl.program_id(1) == 0)
    def _():
        xs_ref[0:8, :] = jnp.zeros((8, xs_ref.shape[1]), F32)
        st_ref[...] = jnp.zeros_like(st_ref)

    xs_ref[8:, :] = jnp.concatenate([x_ref[...], bc_ref[...]], axis=1).astype(F32)
    conv = cb_ref[...] + cw_ref[SSM_CONV - 1:SSM_CONV, :] * xs_ref[8:8 + L, :]
    for sft in range(1, SSM_CONV):
        conv = conv + cw_ref[SSM_CONV - 1 - sft:SSM_CONV - sft, :] * xs_ref[8 - sft:8 - sft + L, :]
    xs_ref[0:8, :] = xs_ref[L:L + 8, :]
    xbc = conv * _sigmoid(conv)
    xh = xbc[:, :SSM_DINNER]
    bmat = xbc[:, SSM_DINNER:SSM_DINNER + SSM_GROUPS * NS].astype(BF)
    cmat = xbc[:, SSM_DINNER + SSM_GROUPS * NS:].astype(BF)

    dtc = dt_ref[...]
    ac = ac_ref[...]
    ar = ar_ref[...]
    a_last = ac[L - 1:L, :]
    hsel = (lax.broadcasted_iota(jnp.int32, (SSM_HEADS, SSM_DINNER), 1) // P
            == lax.broadcasted_iota(jnp.int32, (SSM_HEADS, SSM_DINNER), 0))
    expand = jnp.where(hsel, 1.0, 0.0).astype(BF)
    stack = jnp.concatenate([dtc, jnp.exp(a_last - ac), jnp.exp(ac),
                             jnp.broadcast_to(jnp.exp(a_last), (8, SSM_HEADS))], axis=0)
    ex = _dot_sel(stack, expand)
    dt_full, dst_full, ind_full = ex[0:L], ex[L:2 * L], ex[2 * L:3 * L]
    cdec_full = ex[3 * L:3 * L + 1]
    xdt = xh * dt_full
    xdt_b = xdt.astype(BF)
    xw_b = (xdt * dst_full).astype(BF)

    row = lax.broadcasted_iota(jnp.int32, (L, L), 0)
    col = lax.broadcasted_iota(jnp.int32, (L, L), 1)
    causal = col <= row
    ys = []
    for g in range(SSM_GROUPS):
        bg = bmat[:, g * NS:(g + 1) * NS]
        cg = cmat[:, g * NS:(g + 1) * NS]
        cb = _dot_nt(cg, bg)
        st_prev = st_ref[g]
        yoff = _dot(cg, st_prev.astype(BF))
        st_ref[g] = cdec_full[:, g * GW:(g + 1) * GW] * st_prev + _dot_tn(bg, xw_b[:, g * GW:(g + 1) * GW])
        for r in range(R):
            h = g * R + r
            dec = jnp.exp(jnp.where(causal, ac[:, h:h + 1] - ar[h:h + 1, :], -jnp.inf))
            yd = _dot((dec * cb).astype(BF), xdt_b[:, h * P:(h + 1) * P])
            ys.append(yd + yoff[:, r * P:(r + 1) * P] * ind_full[:, h * P:(h + 1) * P])
    y = jnp.concatenate(ys, axis=1) + xh * d_ref[...]
    zf = z_ref[...].astype(F32)
    y = y * (zf * _sigmoid(zf))
    y_ref[...] = jnp.concatenate(
        [_rms(y[:, g * GW:(g + 1) * GW], nw_ref[:, g * GW:(g + 1) * GW]) for g in range(SSM_GROUPS)],
        axis=1).astype(y_ref.dtype)


def ssd_branch(u3, dt_cols, acs_cols, acs_rows, conv_w, conv_b, d_full, norm_w):
    bsz, seq, _ = u3.shape
    nc = seq // CHUNK
    L = CHUNK
    ublk = lambda width, off: pl.BlockSpec((None, L, width), lambda b, c: (b, c, off // width))
    cols = pl.BlockSpec((None, L, SSM_HEADS), lambda b, c: (b, c, 0))
    const = lambda shape: pl.BlockSpec(shape, lambda b, c: (0, 0))
    cch = SSM_DINNER + SSM_BC
    return pl.pallas_call(
        _ssd_kernel,
        grid=(bsz, nc),
        in_specs=[ublk(1024, U_SSM_Z), ublk(1024, U_SSM_X), ublk(512, U_SSM_BC), cols, cols,
                  pl.BlockSpec((None, None, SSM_HEADS, L), lambda b, c: (b, c, 0, 0)),
                  const((SSM_CONV, cch)), const((1, cch)), const((1, SSM_DINNER)), const((1, SSM_DINNER))],
        out_specs=pl.BlockSpec((None, L, BRANCH_W), lambda b, c: (b, c, 0)),
        out_shape=jax.ShapeDtypeStruct((bsz, seq, BRANCH_W), BF),
        scratch_shapes=[pltpu.VMEM((L + 8, cch), F32),
                        pltpu.VMEM((SSM_GROUPS, SSM_STATE, SSM_DINNER // SSM_GROUPS), F32)],
        compiler_params=_cparams(("parallel", "arbitrary")),
    )(u3, u3, u3, dt_cols, acs_cols, acs_rows, conv_w.astype(F32), conv_b.reshape(1, cch).astype(F32),
      d_full.reshape(1, -1).astype(F32), norm_w.reshape(1, -1).astype(F32))


def _rope128(x, cs):
    t = x.astype(F32) * cs
    return t + pltpu.roll(t, MLA_ROPE, axis=1)


def _mla_kernel(q_ref, csq_ref, kv_ref, kr_ref, csk_ref, o_ref, qe_s, *stats, tq):
    qi = pl.program_id(1)
    ki = pl.program_id(2)
    qw = MLA_NOPE + LANES
    kw = MLA_NOPE + MLA_V
    m_s, l_s, acc_s = stats[0::3], stats[1::3], stats[2::3]
    nt = tq // LANES

    @pl.when(ki == 0)
    def _():
        for h in range(MLA_HEADS):
            qr = _rope128(q_ref[:, h * qw + MLA_NOPE:(h + 1) * qw], csq_ref[...]).astype(BF)
            qe_s[h] = jnp.concatenate([q_ref[:, h * qw:h * qw + MLA_NOPE], qr], axis=1)
            m_s[h][...] = jnp.full((tq, LANES), -jnp.inf, F32)
            l_s[h][...] = jnp.zeros((tq, LANES), F32)
            acc_s[h][...] = jnp.zeros((tq, MLA_V), F32)

    def step(masked):
        lane = lax.broadcasted_iota(jnp.int32, (tq, LANES), 1)
        kr = jnp.where(lane < MLA_ROPE, _rope128(kr_ref[...], csk_ref[...]), 0.0).astype(BF)
        if masked:
            causal = (lax.broadcasted_iota(jnp.int32, (tq, tq), 1) <= lax.broadcasted_iota(jnp.int32, (tq, tq), 0))
        for h in range(MLA_HEADS):
            ke = jnp.concatenate([kv_ref[:, h * kw:h * kw + MLA_NOPE], kr], axis=1)
            s = _dot_nt(qe_s[h], ke)
            if masked:
                s = jnp.where(causal, s, -jnp.inf)
            m_old = m_s[h][...]
            m_new = jnp.maximum(m_old, jnp.max(s, axis=1, keepdims=True))
            alpha = jnp.exp2(m_old - m_new)
            p = jnp.exp2(s - jnp.concatenate([m_new] * nt, axis=1))
            psum = p[:, 0:LANES]
            for t in range(1, nt):
                psum = psum + p[:, t * LANES:(t + 1) * LANES]
            l_s[h][...] = alpha * l_s[h][...] + psum
            acc_s[h][...] = alpha * acc_s[h][...] + _dot(p.astype(BF), kv_ref[:, h * kw + MLA_NOPE:(h + 1) * kw])
            m_s[h][...] = m_new

    @pl.when(ki < qi)
    def _():
        step(False)

    @pl.when(ki == qi)
    def _():
        step(True)
        for h in range(MLA_HEADS):
            l = jnp.sum(l_s[h][...], axis=1, keepdims=True)
            o_ref[:, h * MLA_V:(h + 1) * MLA_V] = (acc_s[h][...] / l).astype(o_ref.dtype)


def mla_attention(qf3, kvf3, u3, cs, tq=512):
    bsz, seq, _ = qf3.shape
    tq = min(tq, seq)
    nq = seq // tq
    kvi = lambda b, qi, ki: jnp.minimum(ki, qi)
    return pl.pallas_call(
        functools.partial(_mla_kernel, tq=tq),
        grid=(bsz, nq, nq),
        in_specs=[pl.BlockSpec((None, tq, qf3.shape[2]), lambda b, qi, ki: (b, qi, 0)),
                  pl.BlockSpec((tq, LANES), lambda b, qi, ki: (qi, 0)),
                  pl.BlockSpec((None, tq, kvf3.shape[2]), lambda b, qi, ki: (b, kvi(b, qi, ki), 0)),
                  pl.BlockSpec((None, tq, LANES), lambda b, qi, ki: (b, kvi(b, qi, ki), U_MLA_KR // LANES)),
                  pl.BlockSpec((tq, LANES), lambda b, qi, ki: (kvi(b, qi, ki), 0))],
        out_specs=pl.BlockSpec((None, tq, MLA_HEADS * MLA_V), lambda b, qi, ki: (b, qi, 0)),
        out_shape=jax.ShapeDtypeStruct((bsz, seq, MLA_HEADS * MLA_V), BF),
        scratch_shapes=[pltpu.VMEM((MLA_HEADS, tq, MLA_NOPE + LANES), BF)]
        + [pltpu.VMEM((tq, LANES), F32), pltpu.VMEM((tq, LANES), F32), pltpu.VMEM((tq, MLA_V), F32)] * MLA_HEADS,
        compiler_params=_cparams(("parallel", "parallel", "arbitrary")),
    )(qf3, cs, kvf3, u3, cs)


def _swa_kernel(q_ref, kc_ref, kp_ref, vc_ref, vp_ref, sink_ref, o_ref):
    W, d = SWA_WINDOW, SWA_HEAD_DIM
    n = pl.program_id(1)
    i = lax.broadcasted_iota(jnp.int32, (2 * W, 2 * W), 0) & (W - 1)
    j = lax.broadcasted_iota(jnp.int32, (2 * W, 2 * W), 1)
    valid = (j > i) & (j <= i + W) & ((n > 0) | (j >= W))
    first_tile = lax.broadcasted_iota(jnp.int32, (2 * W, LANES), 0) < W
    lower = lax.broadcasted_iota(jnp.int32, (2 * W, LANES), 1) < d
    ones = jnp.ones((2 * W, LANES), BF)
    for pair in range(SWA_KV_HEADS // 2):
        sl = slice(pair * LANES, (pair + 1) * LANES)
        kt = jnp.concatenate([kp_ref[:, sl], kc_ref[:, sl]], axis=0)
        vt = jnp.concatenate([vp_ref[:, sl], vc_ref[:, sl]], axis=0)
        kt_sw = pltpu.roll(kt.astype(F32), d, axis=1).astype(BF)
        vt_sw = pltpu.roll(vt.astype(F32), d, axis=1).astype(BF)
        for e in range(2):
            kh = 2 * pair + e
            k_lo = jnp.where(lower, kt if e == 0 else kt_sw, jnp.zeros_like(kt))
            k_hi = jnp.where(lower, jnp.zeros_like(kt), kt_sw if e == 0 else kt)
            v_lo, v_hi = (vt, vt_sw) if e == 0 else (vt_sw, vt)
            qs = jnp.concatenate([q_ref[:, 2 * kh * LANES:(2 * kh + 1) * LANES],
                                  q_ref[:, (2 * kh + 1) * LANES:(2 * kh + 2) * LANES]], axis=0)
            halves = []
            for half, (ke, ve) in enumerate(((k_lo, v_lo), (k_hi, v_hi))):
                ha, hb = 4 * kh + half, 4 * kh + 2 + half
                s = jnp.where(valid, _dot_nt(qs, ke), -jnp.inf)
                sink = jnp.where(first_tile, sink_ref[0:1, ha:ha + 1], sink_ref[0:1, hb:hb + 1]) * LOG2E
                m = jnp.maximum(jnp.max(s, axis=1, keepdims=True), sink)
                p = jnp.exp2(s - jnp.concatenate([m, m], axis=1)).astype(BF)
                den = _dot(p, ones) + jnp.exp2(sink - m)
                halves.append(_dot(p, ve) / den)
            ot = jnp.where(lower, halves[0], halves[1]).astype(o_ref.dtype)
            o_ref[:, 2 * kh * LANES:(2 * kh + 1) * LANES] = ot[:W]
            o_ref[:, (2 * kh + 1) * LANES:(2 * kh + 2) * LANES] = ot[W:]


def swa_branch(u3, sinks):
    bsz, seq, _ = u3.shape
    W = SWA_WINDOW
    kvw = SWA_KV_HEADS * SWA_HEAD_DIM
    cur = lambda width, off: pl.BlockSpec((None, W, width), lambda b, n: (b, n, off // width))
    prev = lambda width, off: pl.BlockSpec((None, W, width), lambda b, n: (b, jnp.maximum(n - 1, 0), off // width))
    return pl.pallas_call(
        _swa_kernel,
        grid=(bsz, seq // W),
        in_specs=[cur(1024, U_SWA_Q), cur(kvw, U_SWA_K), prev(kvw, U_SWA_K), cur(kvw, U_SWA_V), prev(kvw, U_SWA_V),
                  pl.BlockSpec((1, SWA_HEADS), lambda b, n: (0, 0))],
        out_specs=pl.BlockSpec((None, W, BRANCH_W), lambda b, n: (b, n, 0)),
        out_shape=jax.ShapeDtypeStruct((bsz, seq, BRANCH_W), BF),
        compiler_params=_cparams(("parallel", "parallel")),
    )(u3, u3, u3, u3, u3, sinks.reshape(1, -1).astype(F32))


def _merge_kernel(ya_ref, yb_ref, yc_ref, yd_ref, w_ref, g0_ref, g1_ref, g2_ref, g3_ref, o_ref):
    acc = None
    for n, (y_ref, g_ref) in enumerate(((ya_ref, g0_ref), (yb_ref, g1_ref), (yc_ref, g2_ref), (yd_ref, g3_ref))):
        t = _sigmoid(g_ref[...].astype(F32)) * _dot(y_ref[...], w_ref[n])
        acc = t if acc is None else acc + t
    o_ref[...] = acc.astype(o_ref.dtype)


def gated_merge(ya, yb, yc, yd, w_branch, u2, tm=1024, tn=512):
    m = ya.shape[0]
    tm = min(tm, m)
    ysp = pl.BlockSpec((tm, BRANCH_W), lambda i, j: (i, 0))
    gsp = lambda n: pl.BlockSpec((tm, tn), lambda i, j: (i, (U_GATE + n * D_MODEL) // tn + j))
    return pl.pallas_call(
        _merge_kernel,
        grid=(m // tm, D_MODEL // tn),
        in_specs=[ysp, ysp, ysp, ysp, pl.BlockSpec((N_BRANCH, BRANCH_W, tn), lambda i, j: (0, 0, j)),
                  gsp(0), gsp(1), gsp(2), gsp(3)],
        out_specs=pl.BlockSpec((tm, tn), lambda i, j: (i, j)),
        out_shape=jax.ShapeDtypeStruct((m, D_MODEL), BF),
        compiler_params=_cparams(("parallel", "parallel")),
    )(ya, yb, yc, yd, w_branch, u2, u2, u2, u2)


def _xattn_kernel(q_ref, kv_ref, o_ref):
    d = XA_HEAD_DIM
    outs = []
    for h in range(XA_HEADS):
        s = _dot_nt(q_ref[:, h * d:(h + 1) * d], kv_ref[:, h * d:(h + 1) * d]) * d ** -0.5
        p = jnp.exp(s - jnp.max(s, axis=1, keepdims=True))
        p = p / jnp.sum(p, axis=1, keepdims=True)
        outs.append(_dot(p.astype(BF), kv_ref[:, (XA_HEADS + h) * d:(XA_HEADS + h + 1) * d]))
    o_ref[...] = jnp.concatenate(outs, axis=1).astype(o_ref.dtype)


def cross_attention(q3, kv3, tq=512):
    bsz, seq, w = q3.shape
    mlen = kv3.shape[1]
    tq = min(tq, seq)
    return pl.pallas_call(
        _xattn_kernel,
        grid=(bsz, seq // tq),
        in_specs=[pl.BlockSpec((None, tq, w), lambda b, i: (b, i, 0)),
                  pl.BlockSpec((None, mlen, 2 * w), lambda b, i: (b, 0, 0))],
        out_specs=pl.BlockSpec((None, tq, w), lambda b, i: (b, i, 0)),
        out_shape=jax.ShapeDtypeStruct((bsz, seq, w), BF),
        compiler_params=_cparams(("parallel", "parallel")),
    )(q3, kv3)


def _ffn_kernel(h_ref, g_ref, w1_ref, w3_ref, w2_ref, o_ref, xn_ref, acc_ref):
    j = pl.program_id(1)

    @pl.when(j == 0)
    def _():
        hv = h_ref[...]
        xn_ref[...] = _rms(hv, g_ref[...]).astype(BF)
        acc_ref[...] = hv

    xn = xn_ref[...]
    h1 = _dot(xn, w1_ref[...])
    h3 = _dot(xn, w3_ref[...])
    act = (h1 * _sigmoid(h1) * h3).astype(BF)
    acc_ref[...] += _dot(act, w2_ref[...])

    @pl.when(j == pl.num_programs(1) - 1)
    def _():
        o_ref[...] = acc_ref[...]


def dense_ffn(h, norm_w, w13, w2, tm=512, tf=1024):
    m, d = h.shape
    f = w2.shape[0]
    tm = min(tm, m)
    nf = f // tf
    return pl.pallas_call(
        _ffn_kernel,
        grid=(m // tm, nf),
        in_specs=[pl.BlockSpec((tm, d), lambda i, j: (i, 0)), pl.BlockSpec((1, d), lambda i, j: (0, 0)),
                  pl.BlockSpec((d, tf), lambda i, j: (0, j)), pl.BlockSpec((d, tf), lambda i, j: (0, j + nf)),
                  pl.BlockSpec((tf, d), lambda i, j: (j, 0))],
        out_specs=pl.BlockSpec((tm, d), lambda i, j: (i, 0)),
        out_shape=jax.ShapeDtypeStruct((m, d), F32),
        scratch_shapes=[pltpu.VMEM((tm, d), BF), pltpu.VMEM((tm, d), F32)],
        compiler_params=_cparams(("parallel", "arbitrary")),
    )(h, norm_w.reshape(1, d).astype(F32), w13, w13, w2)


ROUTE_TM = 512
MOE_TM = 512


def _router_kernel(h_ref, g_ref, r_ref, hn_ref, route_ref, cnt_ref, carry_ref):
    tm = h_ref.shape[0]

    @pl.when(pl.program_id(0) == 0)
    def _():
        carry_ref[...] = jnp.zeros_like(carry_ref)

    hn = _rms(h_ref[...], g_ref[...])
    hn_ref[...] = hn.astype(BF)
    a1, a2, _ = _split3(hn)
    r1, r2, _ = _split3(r_ref[...])
    lane = lax.broadcasted_iota(jnp.int32, (tm, LANES), 1)
    logits = jnp.where(lane < N_EXPERTS, _dot(a1, r1) + (_dot(a1, r2) + _dot(a2, r1)), -jnp.inf)
    v0 = jnp.max(logits, axis=1, keepdims=True)
    i0 = jnp.min(jnp.where(logits == v0, lane, LANES), axis=1, keepdims=True)
    rest = jnp.where(lane == i0, -jnp.inf, logits)
    v1 = jnp.max(rest, axis=1, keepdims=True)
    i1 = jnp.min(jnp.where(rest == v1, lane, LANES), axis=1, keepdims=True)
    ex = jnp.exp(v1 - v0)
    g0 = 1.0 / (1.0 + ex)
    g1 = ex / (1.0 + ex)
    sel0, sel1 = lane == i0, lane == i1
    onehot = jnp.where(sel0 | sel1, 1.0, 0.0)
    below = (lax.broadcasted_iota(jnp.int32, (tm, tm), 1) < lax.broadcasted_iota(jnp.int32, (tm, tm), 0))
    before = carry_ref[0:1, :] + _dot(jnp.where(below, 1.0, 0.0).astype(BF), onehot.astype(BF))
    rank0 = jnp.sum(jnp.where(sel0, before, 0.0), axis=1, keepdims=True)
    rank1 = jnp.sum(jnp.where(sel1, before, 0.0), axis=1, keepdims=True)
    total = carry_ref[0:1, :] + jnp.sum(onehot, axis=0, keepdims=True)
    carry_ref[...] = jnp.broadcast_to(total, carry_ref.shape)
    cnt_ref[...] = jnp.broadcast_to(total, cnt_ref.shape)
    out = jnp.zeros((tm, LANES), F32)
    for pos, val in enumerate((i0.astype(F32), i1.astype(F32), g0, g1, rank0, rank1)):
        out = jnp.where(lane == pos, val, out)
    route_ref[...] = out


def moe_router(h, norm_w, router):
    m, d = h.shape
    tm = min(ROUTE_TM, m)
    rpad = jnp.zeros((d, LANES), F32).at[:, :N_EXPERTS].set(router.astype(F32))
    return pl.pallas_call(
        _router_kernel,
        grid=(m // tm,),
        in_specs=[pl.BlockSpec((tm, d), lambda i: (i, 0)), pl.BlockSpec((1, d), lambda i: (0, 0)),
                  pl.BlockSpec((d, LANES), lambda i: (0, 0))],
        out_specs=[pl.BlockSpec((tm, d), lambda i: (i, 0)), pl.BlockSpec((tm, LANES), lambda i: (i, 0)),
                   pl.BlockSpec((8, LANES), lambda i: (0, 0))],
        out_shape=[jax.ShapeDtypeStruct((m, d), BF), jax.ShapeDtypeStruct((m, LANES), F32),
                   jax.ShapeDtypeStruct((8, LANES), F32)],
        scratch_shapes=[pltpu.VMEM((8, LANES), F32)],
        compiler_params=_cparams(("arbitrary",)),
    )(h, norm_w.reshape(1, d).astype(F32), rpad)


def _gather_rows_kernel(idx_ref, src_ref, o_ref, sem):
    tg = o_ref.shape[0]
    base = pl.program_id(0) * tg

    def row_copy(r):
        return pltpu.make_async_copy(src_ref.at[idx_ref[base + r]], o_ref.at[r], sem)

    def start(r, c):
        row_copy(r).start()
        return c

    def wait(r, c):
        row_copy(r).wait()
        return c

    lax.fori_loop(0, tg, start, 0, unroll=8)
    lax.fori_loop(0, tg, wait, 0, unroll=8)


def gather_rows(src3, idx, tg=256):
    n = idx.shape[0]
    tg = min(tg, n)
    return pl.pallas_call(
        _gather_rows_kernel,
        grid_spec=pltpu.PrefetchScalarGridSpec(
            num_scalar_prefetch=1, grid=(n // tg,),
            in_specs=[pl.BlockSpec(memory_space=pl.ANY)],
            out_specs=pl.BlockSpec((tg,) + src3.shape[1:], lambda i, idx_ref: (i, 0, 0)),
            scratch_shapes=[pltpu.SemaphoreType.DMA(())]),
        out_shape=jax.ShapeDtypeStruct((n,) + src3.shape[1:], src3.dtype),
        compiler_params=_cparams(("arbitrary",)),
    )(idx, src3)


def _moe_up_kernel(te_ref, nt_ref, x_ref, w1_ref, w3_ref, o_ref, w1b_ref, w3b_ref):
    m = pl.program_id(1)
    used = m < nt_ref[0]
    new_expert = jnp.logical_or(m == 0, te_ref[m] != te_ref[jnp.maximum(m - 1, 0)])

    @pl.when(jnp.logical_and(used, new_expert))
    def _():
        w1b_ref[...] = w1_ref[...].astype(BF)
        w3b_ref[...] = w3_ref[...].astype(BF)

    @pl.when(used)
    def _():
        x = x_ref[...]
        h1 = _dot(x, w1b_ref[...])
        h3 = _dot(x, w3b_ref[...])
        o_ref[...] = (h1 * _sigmoid(h1) * h3).astype(o_ref.dtype)

    @pl.when(jnp.logical_not(used))
    def _():
        o_ref[...] = jnp.zeros_like(o_ref)


def _moe_down_kernel(te_ref, nt_ref, a_ref, w2_ref, g_ref, o_ref):
    used = pl.program_id(1) < nt_ref[0]

    @pl.when(used)
    def _():
        o_ref[...] = (_dot(a_ref[...], w2_ref[...]) * g_ref[...]).astype(o_ref.dtype)

    @pl.when(jnp.logical_not(used))
    def _():
        o_ref[...] = jnp.zeros_like(o_ref)


def moe_experts(xg, tile_e, ntiles, row_gate, w13, w2, tn_up=1024, tn_down=512):
    cap, d = xg.shape
    f = w2.shape[1]
    tm = MOE_TM
    nt_max = cap // tm
    nf = f // tn_up
    mt = lambda m, nt: jnp.minimum(m, nt[0] - 1)
    act = pl.pallas_call(
        _moe_up_kernel,
        grid_spec=pltpu.PrefetchScalarGridSpec(
            num_scalar_prefetch=2, grid=(nf, nt_max),
            in_specs=[pl.BlockSpec((tm, d), lambda j, m, te, nt: (mt(m, nt), 0)),
                      pl.BlockSpec((None, d, tn_up), lambda j, m, te, nt: (te[mt(m, nt)], 0, j)),
                      pl.BlockSpec((None, d, tn_up), lambda j, m, te, nt: (te[mt(m, nt)], 0, j + nf))],
            out_specs=pl.BlockSpec((tm, tn_up), lambda j, m, te, nt: (m, j)),
            scratch_shapes=[pltpu.VMEM((d, tn_up), BF), pltpu.VMEM((d, tn_up), BF)]),
        out_shape=jax.ShapeDtypeStruct((cap, f), BF),
        compiler_params=_cparams(("arbitrary", "arbitrary")),
    )(tile_e, ntiles, xg, w13, w13)
    return pl.pallas_call(
        _moe_down_kernel,
        grid_spec=pltpu.PrefetchScalarGridSpec(
            num_scalar_prefetch=2, grid=(d // tn_down, nt_max),
            in_specs=[pl.BlockSpec((tm, f), lambda j, m, te, nt: (mt(m, nt), 0)),
                      pl.BlockSpec((None, f, tn_down), lambda j, m, te, nt: (te[mt(m, nt)], 0, j)),
                      pl.BlockSpec((tm, 1), lambda j, m, te, nt: (mt(m, nt), 0))],
            out_specs=pl.BlockSpec((tm, tn_down), lambda j, m, te, nt: (m, j))),
        out_shape=jax.ShapeDtypeStruct((cap, d), BF),
        compiler_params=_cparams(("arbitrary", "arbitrary")),
    )(tile_e, ntiles, act, w2, row_gate)


def _combine_kernel(pos_ref, h_ref, y_ref, o_ref, buf, sem):
    tc = h_ref.shape[0]
    base = pl.program_id(0) * tc

    def row_copy(r, k):
        return pltpu.make_async_copy(y_ref.at[pos_ref[TOP_K * (base + r) + k]], buf.at[k, r], sem)

    def start(r, c):
        for k in range(TOP_K):
            row_copy(r, k).start()
        return c

    def wait(r, c):
        for k in range(TOP_K):
            row_copy(r, k).wait()
        return c

    lax.fori_loop(0, tc, start, 0, unroll=8)
    lax.fori_loop(0, tc, wait, 0, unroll=8)
    acc = h_ref[...]
    for k in range(TOP_K):
        acc = acc + buf[k].astype(F32)
    o_ref[...] = acc


def moe_combine(h3, yg3, pos, tc=256):
    n = h3.shape[0]
    tc = min(tc, n)
    blk = (tc,) + h3.shape[1:]
    return pl.pallas_call(
        _combine_kernel,
        grid_spec=pltpu.PrefetchScalarGridSpec(
            num_scalar_prefetch=1, grid=(n // tc,),
            in_specs=[pl.BlockSpec(blk, lambda i, pos_ref: (i, 0, 0)), pl.BlockSpec(memory_space=pl.ANY)],
            out_specs=pl.BlockSpec(blk, lambda i, pos_ref: (i, 0, 0)),
            scratch_shapes=[pltpu.VMEM((TOP_K,) + blk, yg3.dtype), pltpu.SemaphoreType.DMA(())]),
        out_shape=jax.ShapeDtypeStruct(h3.shape, F32),
        compiler_params=_cparams(("arbitrary",)),
    )(pos, h3, yg3)


def moe_ffn(h, norm_w, router, w13, w2):
    n, d = h.shape
    tm = MOE_TM
    hn, route, cnt = moe_router(h, norm_w, router)
    expert = route[:, 0:TOP_K].astype(jnp.int32)
    gate = route[:, TOP_K:2 * TOP_K]
    rank = route[:, 2 * TOP_K:3 * TOP_K].astype(jnp.int32)
    counts = cnt[0, :N_EXPERTS].astype(jnp.int32)
    tiles = (counts + tm - 1) // tm
    tile_end = jnp.cumsum(tiles)
    dest = ((tile_end - tiles) * tm)[expert] + rank
    nt_max = -(-n * TOP_K // tm) + N_EXPERTS
    cap = nt_max * tm
    flat = dest.reshape(-1)
    row_tok = jnp.zeros((cap,), jnp.int32).at[flat].set(jnp.arange(n * TOP_K, dtype=jnp.int32) // TOP_K)
    row_gate = jnp.zeros((cap,), F32).at[flat].set(gate.reshape(-1))
    tile_e = jnp.minimum(jnp.searchsorted(tile_end, jnp.arange(nt_max, dtype=jnp.int32), side="right"),
                         N_EXPERTS - 1).astype(jnp.int32)
    ntiles = tile_end[-1:].astype(jnp.int32)
    s = d // LANES
    xg = gather_rows(hn.reshape(n, s, LANES), row_tok).reshape(cap, d)
    yg = moe_experts(xg, tile_e, ntiles, row_gate.reshape(cap, 1), w13, w2)
    out = moe_combine(h.reshape(n, s, LANES), yg.reshape(cap, s, LANES), flat)
    return out.reshape(n, d)


def _final_kernel(x_ref, w_ref, o_ref):
    o_ref[...] = _rms(x_ref[...], w_ref[...])


def final_norm(h, w, tm=512):
    m, d = h.shape
    tm = min(tm, m)
    return pl.pallas_call(
        _final_kernel,
        grid=(m // tm,),
        in_specs=[pl.BlockSpec((tm, d), lambda i: (i, 0)), pl.BlockSpec((1, d), lambda i: (0, 0))],
        out_specs=pl.BlockSpec((tm, d), lambda i: (i, 0)),
        out_shape=jax.ShapeDtypeStruct((m, d), F32),
        compiler_params=_cparams(("parallel",)),
    )(h, w.reshape(1, d).astype(F32))


def _rot_cols(w):
    half = w.shape[-1] // 2
    return jnp.concatenate([-w[..., half:], w[..., :half]], axis=-1)


def _layout_w_in(w):
    seg = lambda off, width: w[:, off:off + width]
    kr = seg(O_MLA_KR, MLA_ROPE)
    xbc = seg(O_SSM_XBC, SSM_DINNER + SSM_BC)
    main = jnp.concatenate([
        seg(O_ML_Q, 512), seg(O_ML_K, 512), seg(O_ML_V, 1024), seg(O_ML_O, 1024), seg(O_SSM_Z, 1024),
        xbc[:, :SSM_DINNER], seg(O_SWA_Q, 1024) * (SWA_HEAD_DIM ** -0.5 * LOG2E), xbc[:, SSM_DINNER:],
        seg(O_MLA_CQ, 512), seg(O_MLA_CKV, 256),
        seg(O_SWA_K, 256), seg(O_SWA_V, 256), kr, _rot_cols(kr), jnp.zeros((w.shape[0], LANES), w.dtype),
        seg(O_GATE, N_BRANCH * D_MODEL)], axis=1)
    small = jnp.concatenate([seg(O_ML_I, ML_HEADS), seg(O_ML_F, ML_HEADS), seg(O_SSM_DT, SSM_HEADS),
                             jnp.zeros((w.shape[0], LANES - 2 * ML_HEADS - SSM_HEADS), w.dtype)], axis=1)
    return main.astype(BF), small.astype(BF)


def _layout_w_uq(w):
    k = w.shape[0]
    w = w.reshape(k, MLA_HEADS, MLA_NOPE + MLA_ROPE)
    rope = w[..., MLA_NOPE:]
    scale = (MLA_NOPE + MLA_ROPE) ** -0.5 * LOG2E
    return (jnp.concatenate([w, _rot_cols(rope)], axis=-1).reshape(k, -1) * scale).astype(BF)


def _rope_table(seq):
    inv_freq = 1.0 / (ROPE_THETA ** (jnp.arange(0, MLA_ROPE, 2, dtype=F32) / MLA_ROPE))
    ang = jnp.arange(seq, dtype=F32)[:, None] * inv_freq[None, :]
    c, s = jnp.cos(ang), jnp.sin(ang)
    return jnp.concatenate([c, c, s, s], axis=1)


def hybrid_mixer(h, bsz, seq, norm_w, w_main, w_small, ml_ib, ml_fb, ml_norm, conv_w, conv_b, dt_bias, a_log,
                 ssm_d, ssm_norm, q_norm, w_uq, kv_norm, w_ukv, sinks, w_branch, w_out, cs):
    n = bsz * seq
    nc = seq // CHUNK
    u2 = matmul(h, w_main, norm=norm_w, out_dtype=BF)
    small = matmul(h, w_small, norm=norm_w, out_dtype=F32, tn=LANES)
    u3 = u2.reshape(bsz, seq, U_TOTAL)

    def to_rows(cols, heads):
        return cols.reshape(bsz, seq, heads).transpose(0, 2, 1).reshape(bsz * heads * nc, CHUNK)

    def per_row(vec, heads):
        return jnp.broadcast_to(vec.astype(F32)[None, :, None], (bsz, heads, nc)).reshape(-1, 1)

    ig, bcum, dt, acs = recurrence_gates(
        to_rows(small[:, 0:ML_HEADS], ML_HEADS), to_rows(small[:, ML_HEADS:2 * ML_HEADS], ML_HEADS),
        per_row(ml_ib, ML_HEADS), per_row(ml_fb, ML_HEADS),
        to_rows(small[:, 2 * ML_HEADS:2 * ML_HEADS + SSM_HEADS], SSM_HEADS),
        per_row(dt_bias, SSM_HEADS), per_row(a_log, SSM_HEADS))

    def as_rows(x, heads):
        return x.reshape(bsz, heads, nc, CHUNK).transpose(0, 2, 1, 3)

    def as_cols(x, heads):
        return x.reshape(bsz, heads, seq).transpose(0, 2, 1)

    ya = mlstm_branch(u3, as_rows(ig, ML_HEADS), as_rows(bcum, ML_HEADS), as_cols(bcum, ML_HEADS), ml_norm)
    yb = ssd_branch(u3, as_cols(dt, SSM_HEADS), as_cols(acs, SSM_HEADS), as_rows(acs, SSM_HEADS), conv_w, conv_b,
                    jnp.repeat(ssm_d, SSM_HEADDIM), ssm_norm)
    qf = matmul(u2, w_uq, norm=q_norm, x_col_blk=U_MLA_CQ // MLA_Q_LORA)
    kvf = matmul(u2, w_ukv, norm=kv_norm, x_col_blk=U_MLA_CKV // MLA_KV_LORA)
    yc = mla_attention(qf.reshape(bsz, seq, -1), kvf.reshape(bsz, seq, -1), u3, cs)
    yd = swa_branch(u3, sinks)
    merged = gated_merge(ya.reshape(n, -1), yb.reshape(n, -1), yc.reshape(n, -1), yd.reshape(n, -1), w_branch, u2)
    return matmul(merged, w_out, residual=h, out_dtype=F32)


def kernel(x, mem, norm_mix, w_in, ml_igate_bias, ml_fgate_bias, ml_norm, ssm_conv_w, ssm_conv_b, ssm_dt_bias, ssm_a_log, ssm_d, ssm_norm, mla_q_norm, mla_w_uq, mla_kv_norm, mla_w_ukv, swa_sinks, w_branch, w_out, norm_cross, norm_mem, xa_wq, xa_wkv, xa_wo, norm_ffn, ffn_w13, ffn_w2, moe_router, moe_w13, moe_w2, norm_final):
    bsz, seq, d = x.shape
    depth = w_in.shape[0]
    n = bsz * seq
    mlen = mem.shape[1]
    cs = _rope_table(seq)
    h = x.reshape(n, d)
    mem2 = mem.reshape(bsz * mlen, d)
    for l in range(depth):
        w_main, w_small = _layout_w_in(w_in[l])
        h = hybrid_mixer(h, bsz, seq, norm_mix[l], w_main, w_small, ml_igate_bias[l], ml_fgate_bias[l], ml_norm[l],
                         ssm_conv_w[l], ssm_conv_b[l], ssm_dt_bias[l], ssm_a_log[l], ssm_d[l], ssm_norm[l],
                         mla_q_norm[l], _layout_w_uq(mla_w_uq[l]), mla_kv_norm[l], mla_w_ukv[l].astype(BF),
                         swa_sinks[l], w_branch[l].astype(BF), w_out[l].astype(BF), cs)
        q = matmul(h, xa_wq[l].astype(BF), norm=norm_cross[l])
        kv = matmul(mem2, xa_wkv[l].astype(BF), norm=norm_mem[l])
        o = cross_attention(q.reshape(bsz, seq, -1), kv.reshape(bsz, mlen, -1))
        h = matmul(o.reshape(n, -1), xa_wo[l].astype(BF), residual=h, out_dtype=F32)
        if l % 2 == 0:
            h = dense_ffn(h, norm_ffn[l], ffn_w13[l // 2].astype(BF), ffn_w2[l // 2].astype(BF))
        else:
            h = moe_ffn(h, norm_ffn[l], moe_router[l // 2], moe_w13[l // 2], moe_w2[l // 2].astype(BF))
    return final_norm(h, norm_final).reshape(bsz, seq, d)
```

```python
import functools
import math

import jax
import jax.numpy as jnp
from jax import lax
from jax.experimental import pallas as pl
from jax.experimental.pallas import tpu as pltpu

F32 = jnp.float32
BF = jnp.bfloat16

D_MODEL = 2048
RMS_EPS = 1e-6
ML_HEADS, ML_DQK, ML_DV = 4, 128, 256
SSM_HEADS, SSM_HEADDIM, SSM_GROUPS, SSM_STATE, SSM_CONV = 16, 64, 2, 128, 4
SSM_DINNER = SSM_HEADS * SSM_HEADDIM
SSM_BC = 2 * SSM_GROUPS * SSM_STATE
MLA_HEADS, MLA_Q_LORA, MLA_KV_LORA, MLA_NOPE, MLA_ROPE, MLA_V = 8, 512, 256, 128, 64, 128
ROPE_THETA = 10000.0
SWA_HEADS, SWA_KV_HEADS, SWA_HEAD_DIM, SWA_WINDOW = 16, 4, 64, 128
N_BRANCH, BRANCH_W = 4, 1024
XA_HEADS, XA_HEAD_DIM = 4, 128
FFN_DIM = 7168
N_EXPERTS, TOP_K = 8, 2

CHUNK = 128
LANES = 128
VMEM_LIMIT = 56 * 1024 * 1024
VMEM_LIMIT_BIG = 60 * 1024 * 1024
LOG2E = math.log2(math.e)

U_ML_Q, U_ML_K, U_ML_V, U_ML_O = 0, 512, 1024, 2048
U_SSM_Z, U_SSM_X, U_SWA_Q, U_SSM_BC = 3072, 4096, 5120, 6144
U_MLA_CQ, U_MLA_CKV, U_SWA_K, U_SWA_V, U_MLA_KR = 6656, 7168, 7424, 7680, 7936
U_GATE = 8192
U_TOTAL = U_GATE + N_BRANCH * D_MODEL
_SPLITS = (512, 512, 1024, 1024, 4, 4, 1024, 1536, 16, 512, 256, 64, 1024, 256, 256, 8192)
_OFF = [0]
for _s in _SPLITS:
    _OFF.append(_OFF[-1] + _s)
(O_ML_Q, O_ML_K, O_ML_V, O_ML_O, O_ML_I, O_ML_F, O_SSM_Z, O_SSM_XBC, O_SSM_DT, O_MLA_CQ, O_MLA_CKV,
 O_MLA_KR, O_SWA_Q, O_SWA_K, O_SWA_V, O_GATE, _O_END) = _OFF


def _cparams(sem, vmem_limit=VMEM_LIMIT):
    return pltpu.CompilerParams(dimension_semantics=sem, vmem_limit_bytes=vmem_limit)


def _dot(a, b):
    return jnp.dot(a, b, preferred_element_type=F32)


def _dot_nt(a, b):
    return lax.dot_general(a, b, (((1,), (1,)), ((), ())), preferred_element_type=F32)


def _dot_tn(a, b):
    return lax.dot_general(a, b, (((0,), (0,)), ((), ())), preferred_element_type=F32)


def _split3(a):
    a1 = a.astype(BF)
    r = a - a1.astype(F32)
    a2 = r.astype(BF)
    a3 = (r - a2.astype(F32)).astype(BF)
    return a1, a2, a3


def _dot_sel(a, sel):
    a1, a2, a3 = _split3(a)
    return _dot(a1, sel) + _dot(a2, sel) + _dot(a3, sel)


def _rms(x, w):
    return x * lax.rsqrt(jnp.mean(x * x, axis=-1, keepdims=True) + RMS_EPS) * w


def _sigmoid(x):
    return 1.0 / (1.0 + jnp.exp(-x))


def _mm_kernel(*refs, has_norm, has_res):
    it = iter(refs)
    x_ref = next(it)
    g_ref = next(it) if has_norm else None
    w_ref = next(it)
    r_ref = next(it) if has_res else None
    o_ref = next(it)
    if has_norm:
        xn_ref = next(it)

        @pl.when(pl.program_id(1) == 0)
        def _():
            xn_ref[...] = _rms(x_ref[...].astype(F32), g_ref[...]).astype(BF)

        xv = xn_ref[...]
    else:
        xv = x_ref[...]
    acc = _dot(xv, w_ref[...])
    if has_res:
        acc = acc + r_ref[...]
    o_ref[...] = acc.astype(o_ref.dtype)


def matmul(x, w, *, norm=None, residual=None, out_dtype=None, tm=1024, tn=1024, x_col_blk=0):
    out_dtype = out_dtype or BF
    m = x.shape[0]
    k, n = w.shape
    tm, tn = min(tm, m), min(tn, n)
    assert m % tm == 0 and n % tn == 0
    in_specs = [pl.BlockSpec((tm, k), lambda i, j: (i, x_col_blk))]
    args = [x]
    scratch = []
    if norm is not None:
        in_specs.append(pl.BlockSpec((1, k), lambda i, j: (0, 0)))
        args.append(norm.reshape(1, k).astype(F32))
        scratch.append(pltpu.VMEM((tm, k), BF))
    in_specs.append(pl.BlockSpec((k, tn), lambda i, j: (0, j)))
    args.append(w)
    if residual is not None:
        in_specs.append(pl.BlockSpec((tm, tn), lambda i, j: (i, j)))
        args.append(residual)
    return pl.pallas_call(
        functools.partial(_mm_kernel, has_norm=norm is not None, has_res=residual is not None),
        grid=(m // tm, n // tn),
        in_specs=in_specs,
        out_specs=pl.BlockSpec((tm, tn), lambda i, j: (i, j)),
        out_shape=jax.ShapeDtypeStruct((m, n), out_dtype),
        scratch_shapes=scratch,
        compiler_params=_cparams(("parallel", "arbitrary")),
    )(*args)


def _cumsum_lanes(x):
    lane = lax.broadcasted_iota(jnp.int32, x.shape, 1)
    s = 1
    while s < x.shape[1]:
        x = x + jnp.where(lane >= s, pltpu.roll(x, s, axis=1), 0.0)
        s *= 2
    return x


def _softplus(x):
    return jnp.maximum(x, 0.0) + jnp.log(1.0 + jnp.exp(-jnp.abs(x)))


def _gates_kernel(i_ref, f_ref, ib_ref, fb_ref, dt_ref, dtb_ref, alog_ref, ig_ref, b_ref, dto_ref, acs_ref):
    ig_ref[...] = i_ref[...] + ib_ref[...]
    b_ref[...] = _cumsum_lanes(-_softplus(-(f_ref[...] + fb_ref[...])))
    dt = _softplus(dt_ref[...] + dtb_ref[...])
    dto_ref[...] = dt
    acs_ref[...] = _cumsum_lanes(dt * (-jnp.exp(alog_ref[...])))


def recurrence_gates(i_rows, f_rows, ib, fb, dt_rows, dtb, alog):
    r1, r2 = i_rows.shape[0], dt_rows.shape[0]
    shp = lambda r: jax.ShapeDtypeStruct((r, CHUNK), F32)
    return pl.pallas_call(
        _gates_kernel,
        out_shape=(shp(r1), shp(r1), shp(r2), shp(r2)),
    )(i_rows, f_rows, ib, fb, dt_rows, dtb, alog)


def _mlstm_kernel(q_ref, k_ref, v_ref, o_ref, igr_ref, br_ref, bc_ref, nw_ref, y_ref, *state):
    ct_ref, n_ref, m_ref = state[0::3], state[1::3], state[2::3]
    L = CHUNK

    @pl.when(pl.program_id(1) == 0)
    def _():
        for ref in state:
            ref[...] = jnp.zeros_like(ref)

    row = lax.broadcasted_iota(jnp.int32, (L, L), 0)
    col = lax.broadcasted_iota(jnp.int32, (L, L), 1)
    causal = col <= row
    diag = col == row
    scale = ML_DQK ** -0.5
    outs = []
    for h in range(ML_HEADS):
        q = q_ref[:, h * ML_DQK:(h + 1) * ML_DQK]
        k = (k_ref[:, h * ML_DQK:(h + 1) * ML_DQK].astype(F32) * scale).astype(BF)
        v = v_ref[:, h * ML_DV:(h + 1) * ML_DV]
        bcol = bc_ref[:, h:h + 1]
        brow = br_ref[h:h + 1, :]
        igrow = igr_ref[h:h + 1, :]
        m_prev = m_ref[h][0:1, 0:1]
        n_prev = n_ref[h][0:1, :]
        ct_prev = ct_ref[h][...]

        dmat = jnp.where(causal, bcol - brow + igrow, -jnp.inf)
        m_intra = jnp.max(dmat, axis=1, keepdims=True)
        g = bcol + m_prev
        m_s = jnp.maximum(g, m_intra)
        p = jnp.exp(dmat - m_s) * _dot_nt(q, k)
        inter = jnp.exp(g - m_s)
        num = _dot(p.astype(BF), v) + inter * _dot(q, ct_prev.astype(BF))
        den = jnp.sum(p, axis=1, keepdims=True) + inter * jnp.sum(q.astype(F32) * n_prev, axis=1, keepdims=True)
        hh = num / jnp.maximum(jnp.abs(den), jnp.exp(-m_s))
        hn = _rms(hh, nw_ref[:, h * ML_DV:(h + 1) * ML_DV])
        outs.append(_sigmoid(o_ref[:, h * ML_DV:(h + 1) * ML_DV].astype(F32)) * hn)

        b_tot = brow[:, L - 1:L]
        a = b_tot - brow + igrow
        m_loc = jnp.max(a, axis=1, keepdims=True)
        wl = jnp.exp(a - m_loc)
        kw = _dot(jnp.where(diag, wl, 0.0).astype(BF), k)
        m_new = jnp.maximum(b_tot + m_prev, m_loc)
        da = jnp.exp(b_tot + m_prev - m_new)
        db = jnp.exp(m_loc - m_new)
        ct_ref[h][...] = da * ct_prev + db * _dot_tn(kw.astype(BF), v)
        n_ref[h][...] = jnp.broadcast_to(da * n_prev + db * jnp.sum(kw, axis=0, keepdims=True), n_ref[h].shape)
        m_ref[h][...] = jnp.broadcast_to(m_new, m_ref[h].shape)
    y_ref[...] = jnp.concatenate(outs, axis=1).astype(y_ref.dtype)


def mlstm_branch(u3, ig_rows, b_rows, b_cols, norm_w):
    bsz, seq, _ = u3.shape
    nc = seq // CHUNK
    L = CHUNK
    ublk = lambda width, off: pl.BlockSpec((None, L, width), lambda b, c: (b, c, off // width))
    rows = pl.BlockSpec((None, None, ML_HEADS, L), lambda b, c: (b, c, 0, 0))
    return pl.pallas_call(
        _mlstm_kernel,
        grid=(bsz, nc),
        in_specs=[ublk(512, U_ML_Q), ublk(512, U_ML_K), ublk(1024, U_ML_V), ublk(1024, U_ML_O), rows, rows,
                  pl.BlockSpec((None, L, ML_HEADS), lambda b, c: (b, c, 0)),
                  pl.BlockSpec((1, ML_HEADS * ML_DV), lambda b, c: (0, 0))],
        out_specs=pl.BlockSpec((None, L, BRANCH_W), lambda b, c: (b, c, 0)),
        out_shape=jax.ShapeDtypeStruct((bsz, seq, BRANCH_W), BF),
        scratch_shapes=[pltpu.VMEM((ML_DQK, ML_DV), F32), pltpu.VMEM((8, ML_DQK), F32),
                        pltpu.VMEM((8, LANES), F32)] * ML_HEADS,
        compiler_params=_cparams(("parallel", "arbitrary")),
    )(u3, u3, u3, u3, ig_rows, b_rows, b_cols, norm_w.reshape(1, -1).astype(F32))


def _ssd_kernel(z_ref, x_ref, bc_ref, dt_ref, ac_ref, ar_ref, cw_ref, cb_ref, d_ref, nw_ref, y_ref, xs_ref, st_ref):
    L = CHUNK
    P, R, NS = SSM_HEADDIM, SSM_HEADS // SSM_GROUPS, SSM_STATE
    GW = R * P

    @pl.when(pl.program_id(1) == 0)
    def _():
        xs_ref[0:8, :] = jnp.zeros((8, xs_ref.shape[1]), F32)
        st_ref[...] = jnp.zeros_like(st_ref)

    xs_ref[8:, :] = jnp.concatenate([x_ref[...], bc_ref[...]], axis=1).astype(F32)
    conv = cb_ref[...] + cw_ref[SSM_CONV - 1:SSM_CONV, :] * xs_ref[8:8 + L, :]
    for sft in range(1, SSM_CONV):
        conv = conv + cw_ref[SSM_CONV - 1 - sft:SSM_CONV - sft, :] * xs_ref[8 - sft:8 - sft + L, :]
    xs_ref[0:8, :] = xs_ref[L:L + 8, :]
    xbc = conv * _sigmoid(conv)
    xh = xbc[:, :SSM_DINNER]
    bmat = xbc[:, SSM_DINNER:SSM_DINNER + SSM_GROUPS * NS].astype(BF)
    cmat = xbc[:, SSM_DINNER + SSM_GROUPS * NS:].astype(BF)

    dtc = dt_ref[...]
    ac = ac_ref[...]
    ar = ar_ref[...]
    a_last = ac[L - 1:L, :]
    hsel = (lax.broadcasted_iota(jnp.int32, (SSM_HEADS, SSM_DINNER), 1) // P
            == lax.broadcasted_iota(jnp.int32, (SSM_HEADS, SSM_DINNER), 0))
    expand = jnp.where(hsel, 1.0, 0.0).astype(BF)
    stack = jnp.concatenate([dtc, jnp.exp(a_last - ac), jnp.exp(ac),
                             jnp.broadcast_to(jnp.exp(a_last), (8, SSM_HEADS))], axis=0)
    ex = _dot_sel(stack, expand)
    dt_full, dst_full, ind_full = ex[0:L], ex[L:2 * L], ex[2 * L:3 * L]
    cdec_full = ex[3 * L:3 * L + 1]
    xdt = xh * dt_full
    xdt_b = xdt.astype(BF)
    xw_b = (xdt * dst_full).astype(BF)

    row = lax.broadcasted_iota(jnp.int32, (L, L), 0)
    col = lax.broadcasted_iota(jnp.int32, (L, L), 1)
    causal = col <= row
    ys = []
    for g in range(SSM_GROUPS):
        bg = bmat[:, g * NS:(g + 1) * NS]
        cg = cmat[:, g * NS:(g + 1) * NS]
        cb = _dot_nt(cg, bg)
        st_prev = st_ref[g]
        yoff = _dot(cg, st_prev.astype(BF))
        st_ref[g] = cdec_full[:, g * GW:(g + 1) * GW] * st_prev + _dot_tn(bg, xw_b[:, g * GW:(g + 1) * GW])
        for r in range(R):
            h = g * R + r
            dec = jnp.exp(jnp.where(causal, ac[:, h:h + 1] - ar[h:h + 1, :], -jnp.inf))
            yd = _dot((dec * cb).astype(BF), xdt_b[:, h * P:(h + 1) * P])
            ys.append(yd + yoff[:, r * P:(r + 1) * P] * ind_full[:, h * P:(h + 1) * P])
    y = jnp.concatenate(ys, axis=1) + xh * d_ref[...]
    zf = z_ref[...].astype(F32)
    y = y * (zf * _sigmoid(zf))
    y_ref[...] = jnp.concatenate(
        [_rms(y[:, g * GW:(g + 1) * GW], nw_ref[:, g * GW:(g + 1) * GW]) for g in range(SSM_GROUPS)],
        axis=1).astype(y_ref.dtype)


def ssd_branch(u3, dt_cols, acs_cols, acs_rows, conv_w, conv_b, d_full, norm_w):
    bsz, seq, _ = u3.shape
    nc = seq // CHUNK
    L = CHUNK
    ublk = lambda width, off: pl.BlockSpec((None, L, width), lambda b, c: (b, c, off // width))
    cols = pl.BlockSpec((None, L, SSM_HEADS), lambda b, c: (b, c, 0))
    const = lambda shape: pl.BlockSpec(shape, lambda b, c: (0, 0))
    cch = SSM_DINNER + SSM_BC
    return pl.pallas_call(
        _ssd_kernel,
        grid=(bsz, nc),
        in_specs=[ublk(1024, U_SSM_Z), ublk(1024, U_SSM_X), ublk(512, U_SSM_BC), cols, cols,
                  pl.BlockSpec((None, None, SSM_HEADS, L), lambda b, c: (b, c, 0, 0)),
                  const((SSM_CONV, cch)), const((1, cch)), const((1, SSM_DINNER)), const((1, SSM_DINNER))],
        out_specs=pl.BlockSpec((None, L, BRANCH_W), lambda b, c: (b, c, 0)),
        out_shape=jax.ShapeDtypeStruct((bsz, seq, BRANCH_W), BF),
        scratch_shapes=[pltpu.VMEM((L + 8, cch), F32),
                        pltpu.VMEM((SSM_GROUPS, SSM_STATE, SSM_DINNER // SSM_GROUPS), F32)],
        compiler_params=_cparams(("parallel", "arbitrary")),
    )(u3, u3, u3, dt_cols, acs_cols, acs_rows, conv_w.astype(F32), conv_b.reshape(1, cch).astype(F32),
      d_full.reshape(1, -1).astype(F32), norm_w.reshape(1, -1).astype(F32))


def _rope128(x, cs):
    t = x.astype(F32) * cs
    return t + pltpu.roll(t, MLA_ROPE, axis=1)


def _mla_kernel(q_ref, csq_ref, kv_ref, kr_ref, csk_ref, o_ref, qe_s, *stats, tq):
    qi = pl.program_id(1)
    ki = pl.program_id(2)
    qw = MLA_NOPE + LANES
    kw = MLA_NOPE + MLA_V
    m_s, l_s, acc_s = stats[0::3], stats[1::3], stats[2::3]
    nt = tq // LANES

    @pl.when(ki == 0)
    def _():
        for h in range(MLA_HEADS):
            qr = _rope128(q_ref[:, h * qw + MLA_NOPE:(h + 1) * qw], csq_ref[...]).astype(BF)
            qe_s[h] = jnp.concatenate([q_ref[:, h * qw:h * qw + MLA_NOPE], qr], axis=1)
            m_s[h][...] = jnp.full((tq, LANES), -jnp.inf, F32)
            l_s[h][...] = jnp.zeros((tq, LANES), F32)
            acc_s[h][...] = jnp.zeros((tq, MLA_V), F32)

    def step(masked):
        lane = lax.broadcasted_iota(jnp.int32, (tq, LANES), 1)
        kr = jnp.where(lane < MLA_ROPE, _rope128(kr_ref[...], csk_ref[...]), 0.0).astype(BF)
        if masked:
            causal = (lax.broadcasted_iota(jnp.int32, (tq, tq), 1) <= lax.broadcasted_iota(jnp.int32, (tq, tq), 0))
        for h in range(MLA_HEADS):
            ke = jnp.concatenate([kv_ref[:, h * kw:h * kw + MLA_NOPE], kr], axis=1)
            s = _dot_nt(qe_s[h], ke)
            if masked:
                s = jnp.where(causal, s, -jnp.inf)
            m_old = m_s[h][...]
            m_new = jnp.maximum(m_old, jnp.max(s, axis=1, keepdims=True))
            alpha = jnp.exp2(m_old - m_new)
            p = jnp.exp2(s - jnp.concatenate([m_new] * nt, axis=1))
            psum = p[:, 0:LANES]
            for t in range(1, nt):
                psum = psum + p[:, t * LANES:(t + 1) * LANES]
            l_s[h][...] = alpha * l_s[h][...] + psum
            acc_s[h][...] = alpha * acc_s[h][...] + _dot(p.astype(BF), kv_ref[:, h * kw + MLA_NOPE:(h + 1) * kw])
            m_s[h][...] = m_new

    @pl.when(ki < qi)
    def _():
        step(False)

    @pl.when(ki == qi)
    def _():
        step(True)
        for h in range(MLA_HEADS):
            l = jnp.sum(l_s[h][...], axis=1, keepdims=True)
            o_ref[:, h * MLA_V:(h + 1) * MLA_V] = (acc_s[h][...] / l).astype(o_ref.dtype)


def mla_attention(qf3, kvf3, u3, cs, tq=512):
    bsz, seq, _ = qf3.shape
    tq = min(tq, seq)
    nq = seq // tq
    kvi = lambda b, qi, ki: jnp.minimum(ki, qi)
    return pl.pallas_call(
        functools.partial(_mla_kernel, tq=tq),
        grid=(bsz, nq, nq),
        in_specs=[pl.BlockSpec((None, tq, qf3.shape[2]), lambda b, qi, ki: (b, qi, 0)),
                  pl.BlockSpec((tq, LANES), lambda b, qi, ki: (qi, 0)),
                  pl.BlockSpec((None, tq, kvf3.shape[2]), lambda b, qi, ki: (b, kvi(b, qi, ki), 0)),
                  pl.BlockSpec((None, tq, LANES), lambda b, qi, ki: (b, kvi(b, qi, ki), U_MLA_KR // LANES)),
                  pl.BlockSpec((tq, LANES), lambda b, qi, ki: (kvi(b, qi, ki), 0))],
        out_specs=pl.BlockSpec((None, tq, MLA_HEADS * MLA_V), lambda b, qi, ki: (b, qi, 0)),
        out_shape=jax.ShapeDtypeStruct((bsz, seq, MLA_HEADS * MLA_V), BF),
        scratch_shapes=[pltpu.VMEM((MLA_HEADS, tq, MLA_NOPE + LANES), BF)]
        + [pltpu.VMEM((tq, LANES), F32), pltpu.VMEM((tq, LANES), F32), pltpu.VMEM((tq, MLA_V), F32)] * MLA_HEADS,
        compiler_params=_cparams(("parallel", "parallel", "arbitrary")),
    )(qf3, cs, kvf3, u3, cs)


def _swa_kernel(q_ref, kc_ref, kp_ref, vc_ref, vp_ref, sink_ref, o_ref):
    W, d = SWA_WINDOW, SWA_HEAD_DIM
    n = pl.program_id(1)
    i = lax.broadcasted_iota(jnp.int32, (2 * W, 2 * W), 0) & (W - 1)
    j = lax.broadcasted_iota(jnp.int32, (2 * W, 2 * W), 1)
    valid = (j > i) & (j <= i + W) & ((n > 0) | (j >= W))
    first_tile = lax.broadcasted_iota(jnp.int32, (2 * W, LANES), 0) < W
    lower = lax.broadcasted_iota(jnp.int32, (2 * W, LANES), 1) < d
    ones = jnp.ones((2 * W, LANES), BF)
    for pair in range(SWA_KV_HEADS // 2):
        sl = slice(pair * LANES, (pair + 1) * LANES)
        kt = jnp.concatenate([kp_ref[:, sl], kc_ref[:, sl]], axis=0)
        vt = jnp.concatenate([vp_ref[:, sl], vc_ref[:, sl]], axis=0)
        kt_sw = pltpu.roll(kt.astype(F32), d, axis=1).astype(BF)
        vt_sw = pltpu.roll(vt.astype(F32), d, axis=1).astype(BF)
        for e in range(2):
            kh = 2 * pair + e
            k_lo = jnp.where(lower, kt if e == 0 else kt_sw, jnp.zeros_like(kt))
            k_hi = jnp.where(lower, jnp.zeros_like(kt), kt_sw if e == 0 else kt)
            v_lo, v_hi = (vt, vt_sw) if e == 0 else (vt_sw, vt)
            qs = jnp.concatenate([q_ref[:, 2 * kh * LANES:(2 * kh + 1) * LANES],
                                  q_ref[:, (2 * kh + 1) * LANES:(2 * kh + 2) * LANES]], axis=0)
            halves = []
            for half, (ke, ve) in enumerate(((k_lo, v_lo), (k_hi, v_hi))):
                ha, hb = 4 * kh + half, 4 * kh + 2 + half
                s = jnp.where(valid, _dot_nt(qs, ke), -jnp.inf)
                sink = jnp.where(first_tile, sink_ref[0:1, ha:ha + 1], sink_ref[0:1, hb:hb + 1]) * LOG2E
                m = jnp.maximum(jnp.max(s, axis=1, keepdims=True), sink)
                p = jnp.exp2(s - jnp.concatenate([m, m], axis=1)).astype(BF)
                den = _dot(p, ones) + jnp.exp2(sink - m)
                halves.append(_dot(p, ve) / den)
            ot = jnp.where(lower, halves[0], halves[1]).astype(o_ref.dtype)
            o_ref[:, 2 * kh * LANES:(2 * kh + 1) * LANES] = ot[:W]
            o_ref[:, (2 * kh + 1) * LANES:(2 * kh + 2) * LANES] = ot[W:]


def swa_branch(u3, sinks):
    bsz, seq, _ = u3.shape
    W = SWA_WINDOW
    kvw = SWA_KV_HEADS * SWA_HEAD_DIM
    cur = lambda width, off: pl.BlockSpec((None, W, width), lambda b, n: (b, n, off // width))
    prev = lambda width, off: pl.BlockSpec((None, W, width), lambda b, n: (b, jnp.maximum(n - 1, 0), off // width))
    return pl.pallas_call(
        _swa_kernel,
        grid=(bsz, seq // W),
        in_specs=[cur(1024, U_SWA_Q), cur(kvw, U_SWA_K), prev(kvw, U_SWA_K), cur(kvw, U_SWA_V), prev(kvw, U_SWA_V),
                  pl.BlockSpec((1, SWA_HEADS), lambda b, n: (0, 0))],
        out_specs=pl.BlockSpec((None, W, BRANCH_W), lambda b, n: (b, n, 0)),
        out_shape=jax.ShapeDtypeStruct((bsz, seq, BRANCH_W), BF),
        compiler_params=_cparams(("parallel", "parallel")),
    )(u3, u3, u3, u3, u3, sinks.reshape(1, -1).astype(F32))


def _merge_kernel(ya_ref, yb_ref, yc_ref, yd_ref, w_ref, g0_ref, g1_ref, g2_ref, g3_ref, o_ref):
    acc = None
    for n, (y_ref, g_ref) in enumerate(((ya_ref, g0_ref), (yb_ref, g1_ref), (yc_ref, g2_ref), (yd_ref, g3_ref))):
        t = _sigmoid(g_ref[...].astype(F32)) * _dot(y_ref[...], w_ref[n])
        acc = t if acc is None else acc + t
    o_ref[...] = acc.astype(o_ref.dtype)


def gated_merge(ya, yb, yc, yd, w_branch, u2, tm=1024, tn=512):
    m = ya.shape[0]
    tm = min(tm, m)
    ysp = pl.BlockSpec((tm, BRANCH_W), lambda i, j: (i, 0))
    gsp = lambda n: pl.BlockSpec((tm, tn), lambda i, j: (i, (U_GATE + n * D_MODEL) // tn + j))
    return pl.pallas_call(
        _merge_kernel,
        grid=(m // tm, D_MODEL // tn),
        in_specs=[ysp, ysp, ysp, ysp, pl.BlockSpec((N_BRANCH, BRANCH_W, tn), lambda i, j: (0, 0, j)),
                  gsp(0), gsp(1), gsp(2), gsp(3)],
        out_specs=pl.BlockSpec((tm, tn), lambda i, j: (i, j)),
        out_shape=jax.ShapeDtypeStruct((m, D_MODEL), BF),
        compiler_params=_cparams(("parallel", "parallel")),
    )(ya, yb, yc, yd, w_branch, u2, u2, u2, u2)


def _xattn_kernel(h_ref, g_ref, wq_ref, kv_ref, wo_ref, o_ref):
    d = XA_HEAD_DIM
    h = h_ref[...]
    q = _dot(_rms(h, g_ref[...]).astype(BF), wq_ref[...]).astype(BF)
    ones = jnp.ones((kv_ref.shape[0], LANES), BF)
    outs = []
    for hd in range(XA_HEADS):
        s = _dot_nt(q[:, hd * d:(hd + 1) * d], kv_ref[:, hd * d:(hd + 1) * d])
        p = jnp.exp2(s - jnp.max(s, axis=1, keepdims=True)).astype(BF)
        den = _dot(p, ones)
        outs.append((_dot(p, kv_ref[:, (XA_HEADS + hd) * d:(XA_HEADS + hd + 1) * d]) / den).astype(BF))
    o_ref[...] = h + _dot(jnp.concatenate(outs, axis=1), wo_ref[...])


def cross_attention(h, seq, norm_w, wq, kv3, wo, tq=512):
    n, dm = h.shape
    _, mlen, kvw = kv3.shape
    tq = min(tq, seq)
    per_b = seq // tq
    const = lambda shape: pl.BlockSpec(shape, lambda i: (0, 0))
    return pl.pallas_call(
        _xattn_kernel,
        grid=(n // tq,),
        in_specs=[pl.BlockSpec((tq, dm), lambda i: (i, 0)), const((1, dm)), const(wq.shape),
                  pl.BlockSpec((None, mlen, kvw), lambda i: (i // per_b, 0, 0)), const(wo.shape)],
        out_specs=pl.BlockSpec((tq, dm), lambda i: (i, 0)),
        out_shape=jax.ShapeDtypeStruct((n, dm), F32),
        compiler_params=_cparams(("parallel",)),
    )(h, norm_w.reshape(1, dm).astype(F32), wq, kv3, wo)


def _ffn_kernel(h_ref, g_ref, w1_ref, w3_ref, w2_ref, o_ref, xn_ref, acc_ref):
    j = pl.program_id(1)

    @pl.when(j == 0)
    def _():
        hv = h_ref[...]
        xn_ref[...] = _rms(hv, g_ref[...]).astype(BF)
        acc_ref[...] = hv

    xn = xn_ref[...]
    h1 = _dot(xn, w1_ref[...])
    h3 = _dot(xn, w3_ref[...])
    act = (h1 * _sigmoid(h1) * h3).astype(BF)
    acc_ref[...] += _dot(act, w2_ref[...])

    @pl.when(j == pl.num_programs(1) - 1)
    def _():
        o_ref[...] = acc_ref[...]


def dense_ffn(h, norm_w, w13, w2, tm=512, tf=1024):
    m, d = h.shape
    f = w2.shape[0]
    tm = min(tm, m)
    nf = f // tf
    return pl.pallas_call(
        _ffn_kernel,
        grid=(m // tm, nf),
        in_specs=[pl.BlockSpec((tm, d), lambda i, j: (i, 0)), pl.BlockSpec((1, d), lambda i, j: (0, 0)),
                  pl.BlockSpec((d, tf), lambda i, j: (0, j)), pl.BlockSpec((d, tf), lambda i, j: (0, j + nf)),
                  pl.BlockSpec((tf, d), lambda i, j: (j, 0))],
        out_specs=pl.BlockSpec((tm, d), lambda i, j: (i, 0)),
        out_shape=jax.ShapeDtypeStruct((m, d), F32),
        scratch_shapes=[pltpu.VMEM((tm, d), BF), pltpu.VMEM((tm, d), F32)],
        compiler_params=_cparams(("parallel", "arbitrary")),
    )(h, norm_w.reshape(1, d).astype(F32), w13, w13, w2)


ROUTE_TM = 512
MOE_TM = 512


def _router_kernel(h_ref, g_ref, r_ref, hn_ref, route_ref, cnt_ref, carry_ref):
    tm = h_ref.shape[0]

    @pl.when(pl.program_id(0) == 0)
    def _():
        carry_ref[...] = jnp.zeros_like(carry_ref)

    hn = _rms(h_ref[...], g_ref[...])
    hn_ref[...] = hn
    a1, a2, _ = _split3(hn)
    r1, r2, _ = _split3(r_ref[...])
    lane = lax.broadcasted_iota(jnp.int32, (tm, LANES), 1)
    logits = jnp.where(lane < N_EXPERTS, _dot(a1, r1) + (_dot(a1, r2) + _dot(a2, r1)), -jnp.inf)
    v0 = jnp.max(logits, axis=1, keepdims=True)
    i0 = jnp.min(jnp.where(logits == v0, lane, LANES), axis=1, keepdims=True)
    rest = jnp.where(lane == i0, -jnp.inf, logits)
    v1 = jnp.max(rest, axis=1, keepdims=True)
    i1 = jnp.min(jnp.where(rest == v1, lane, LANES), axis=1, keepdims=True)
    ex = jnp.exp(v1 - v0)
    g0 = 1.0 / (1.0 + ex)
    g1 = ex / (1.0 + ex)
    sel0, sel1 = lane == i0, lane == i1
    onehot = jnp.where(sel0 | sel1, 1.0, 0.0)
    below = (lax.broadcasted_iota(jnp.int32, (tm, tm), 1) < lax.broadcasted_iota(jnp.int32, (tm, tm), 0))
    before = carry_ref[0:1, :] + _dot(jnp.where(below, 1.0, 0.0).astype(BF), onehot.astype(BF))
    rank0 = jnp.sum(jnp.where(sel0, before, 0.0), axis=1, keepdims=True)
    rank1 = jnp.sum(jnp.where(sel1, before, 0.0), axis=1, keepdims=True)
    total = carry_ref[0:1, :] + jnp.sum(onehot, axis=0, keepdims=True)
    carry_ref[...] = jnp.broadcast_to(total, carry_ref.shape)
    cnt_ref[...] = jnp.broadcast_to(total, cnt_ref.shape)
    out = jnp.zeros((tm, LANES), F32)
    for pos, val in enumerate((i0.astype(F32), i1.astype(F32), g0, g1, rank0, rank1)):
        out = jnp.where(lane == pos, val, out)
    route_ref[...] = out


def moe_router(h, norm_w, router):
    m, d = h.shape
    tm = min(ROUTE_TM, m)
    rpad = jnp.zeros((d, LANES), F32).at[:, :N_EXPERTS].set(router.astype(F32))
    return pl.pallas_call(
        _router_kernel,
        grid=(m // tm,),
        in_specs=[pl.BlockSpec((tm, d), lambda i: (i, 0)), pl.BlockSpec((1, d), lambda i: (0, 0)),
                  pl.BlockSpec((d, LANES), lambda i: (0, 0))],
        out_specs=[pl.BlockSpec((tm, d), lambda i: (i, 0)), pl.BlockSpec((tm, LANES), lambda i: (i, 0)),
                   pl.BlockSpec((8, LANES), lambda i: (0, 0))],
        out_shape=[jax.ShapeDtypeStruct((m, d), F32), jax.ShapeDtypeStruct((m, LANES), F32),
                   jax.ShapeDtypeStruct((8, LANES), F32)],
        scratch_shapes=[pltpu.VMEM((8, LANES), F32)],
        compiler_params=_cparams(("arbitrary",)),
    )(h, norm_w.reshape(1, d).astype(F32), rpad)


def _gather_rows_kernel(idx_ref, src_ref, o_ref, sem):
    tg = o_ref.shape[0]
    base = pl.program_id(0) * tg

    def row_copy(r):
        return pltpu.make_async_copy(src_ref.at[pl.ds(idx_ref[base + r], 1), :], o_ref.at[pl.ds(r, 1), :], sem)

    def start(r, c):
        row_copy(r).start()
        return c

    def wait(r, c):
        row_copy(r).wait()
        return c

    lax.fori_loop(0, tg, start, 0, unroll=8)
    lax.fori_loop(0, tg, wait, 0, unroll=8)


def gather_rows(src, idx, tg=256):
    n = idx.shape[0]
    d = src.shape[1]
    tg = min(tg, n)
    return pl.pallas_call(
        _gather_rows_kernel,
        grid_spec=pltpu.PrefetchScalarGridSpec(
            num_scalar_prefetch=1, grid=(n // tg,),
            in_specs=[pl.BlockSpec(memory_space=pl.ANY)],
            out_specs=pl.BlockSpec((tg, d), lambda i, idx_ref: (i, 0)),
            scratch_shapes=[pltpu.SemaphoreType.DMA(())]),
        out_shape=jax.ShapeDtypeStruct((n, d), src.dtype),
        compiler_params=_cparams(("arbitrary",)),
    )(idx, src)


def _new_expert(te_ref, m):
    return jnp.logical_or(m == 0, te_ref[m] != te_ref[jnp.maximum(m - 1, 0)])


def _moe_up_kernel(te_ref, nt_ref, x_ref, w1_ref, w3_ref, o_ref, w1b_ref, w3b_ref):
    m = pl.program_id(1)
    used = m < nt_ref[0]

    @pl.when(jnp.logical_and(used, _new_expert(te_ref, m)))
    def _():
        w1b_ref[...] = w1_ref[...].astype(BF)
        w3b_ref[...] = w3_ref[...].astype(BF)

    @pl.when(used)
    def _():
        x = x_ref[...].astype(BF)
        h1 = _dot(x, w1b_ref[...])
        h3 = _dot(x, w3b_ref[...])
        o_ref[...] = (h1 * _sigmoid(h1) * h3).astype(o_ref.dtype)

    @pl.when(jnp.logical_not(used))
    def _():
        o_ref[...] = jnp.zeros_like(o_ref)


def _moe_down_kernel(te_ref, nt_ref, a_ref, w2_ref, o_ref, wb_ref):
    m = pl.program_id(1)
    used = m < nt_ref[0]

    @pl.when(jnp.logical_and(used, _new_expert(te_ref, m)))
    def _():
        wb_ref[...] = w2_ref[...].astype(BF)

    @pl.when(used)
    def _():
        o_ref[...] = _dot(a_ref[...], wb_ref[...])

    @pl.when(jnp.logical_not(used))
    def _():
        o_ref[...] = jnp.zeros_like(o_ref)


def moe_experts(xg, tile_e, ntiles, w13, w2, tn_up=1024, tn_down=512):
    cap, d = xg.shape
    f = w2.shape[1]
    tm = MOE_TM
    nt_max = cap // tm
    nf = f // tn_up
    mt = lambda m, nt: jnp.minimum(m, nt[0] - 1)
    act = pl.pallas_call(
        _moe_up_kernel,
        grid_spec=pltpu.PrefetchScalarGridSpec(
            num_scalar_prefetch=2, grid=(nf, nt_max),
            in_specs=[pl.BlockSpec((tm, d), lambda j, m, te, nt: (mt(m, nt), 0)),
                      pl.BlockSpec((None, d, tn_up), lambda j, m, te, nt: (te[mt(m, nt)], 0, j)),
                      pl.BlockSpec((None, d, tn_up), lambda j, m, te, nt: (te[mt(m, nt)], 0, j + nf))],
            out_specs=pl.BlockSpec((tm, tn_up), lambda j, m, te, nt: (m, j)),
            scratch_shapes=[pltpu.VMEM((d, tn_up), BF), pltpu.VMEM((d, tn_up), BF)]),
        out_shape=jax.ShapeDtypeStruct((cap, f), BF),
        compiler_params=_cparams(("arbitrary", "arbitrary")),
    )(tile_e, ntiles, xg, w13, w13)
    return pl.pallas_call(
        _moe_down_kernel,
        grid_spec=pltpu.PrefetchScalarGridSpec(
            num_scalar_prefetch=2, grid=(d // tn_down, nt_max),
            in_specs=[pl.BlockSpec((tm, f), lambda j, m, te, nt: (mt(m, nt), 0)),
                      pl.BlockSpec((None, f, tn_down), lambda j, m, te, nt: (te[mt(m, nt)], 0, j))],
            out_specs=pl.BlockSpec((tm, tn_down), lambda j, m, te, nt: (m, j)),
            scratch_shapes=[pltpu.VMEM((f, tn_down), BF)]),
        out_shape=jax.ShapeDtypeStruct((cap, d), F32),
        compiler_params=_cparams(("arbitrary", "arbitrary"), VMEM_LIMIT_BIG),
    )(tile_e, ntiles, act, w2)


def _combine_kernel(pos_ref, h_ref, route_ref, y_ref, *rest, final_norm):
    if final_norm:
        nw_ref, o_ref, buf, sem = rest
    else:
        o_ref, buf, sem = rest
    tc = h_ref.shape[0]
    base = pl.program_id(0) * tc

    def row_copy(r, k):
        return pltpu.make_async_copy(y_ref.at[pl.ds(pos_ref[TOP_K * (base + r) + k], 1), :],
                                     buf.at[k, pl.ds(r, 1), :], sem)

    def start(r, c):
        for k in range(TOP_K):
            row_copy(r, k).start()
        return c

    def wait(r, c):
        for k in range(TOP_K):
            row_copy(r, k).wait()
        return c

    lax.fori_loop(0, tc, start, 0, unroll=8)
    lax.fori_loop(0, tc, wait, 0, unroll=8)
    acc = h_ref[...]
    for k in range(TOP_K):
        acc = acc + route_ref[:, TOP_K + k:TOP_K + k + 1] * buf[k]
    if final_norm:
        acc = _rms(acc, nw_ref[...])
    o_ref[...] = acc


def moe_combine(h, route, yg, pos, final_w=None, tc=256):
    n, d = h.shape
    tc = min(tc, n)
    in_specs = [pl.BlockSpec((tc, d), lambda i, pos_ref: (i, 0)), pl.BlockSpec((tc, LANES), lambda i, pos_ref: (i, 0)),
                pl.BlockSpec(memory_space=pl.ANY)]
    args = [pos, h, route, yg]
    if final_w is not None:
        in_specs.append(pl.BlockSpec((1, d), lambda i, pos_ref: (0, 0)))
        args.append(final_w.reshape(1, d).astype(F32))
    return pl.pallas_call(
        functools.partial(_combine_kernel, final_norm=final_w is not None),
        grid_spec=pltpu.PrefetchScalarGridSpec(
            num_scalar_prefetch=1, grid=(n // tc,),
            in_specs=in_specs,
            out_specs=pl.BlockSpec((tc, d), lambda i, pos_ref: (i, 0)),
            scratch_shapes=[pltpu.VMEM((TOP_K, tc, d), F32), pltpu.SemaphoreType.DMA(())]),
        out_shape=jax.ShapeDtypeStruct((n, d), F32),
        compiler_params=_cparams(("arbitrary",)),
    )(*args)


def moe_ffn(h, norm_w, router, w13, w2, final_w=None):
    n, d = h.shape
    tm = MOE_TM
    hn, route, cnt = moe_router(h, norm_w, router)
    expert = route[:, 0:TOP_K].astype(jnp.int32)
    rank = route[:, 2 * TOP_K:3 * TOP_K].astype(jnp.int32)
    counts = cnt[0, :N_EXPERTS].astype(jnp.int32)
    tiles = (counts + tm - 1) // tm
    tile_end = jnp.cumsum(tiles)
    dest = ((tile_end - tiles) * tm)[expert] + rank
    nt_max = -(-n * TOP_K // tm) + N_EXPERTS
    cap = nt_max * tm
    flat = dest.reshape(-1)
    row_tok = jnp.zeros((cap,), jnp.int32).at[flat].set(jnp.arange(n * TOP_K, dtype=jnp.int32) // TOP_K)
    tile_e = jnp.minimum(jnp.searchsorted(tile_end, jnp.arange(nt_max, dtype=jnp.int32), side="right"),
                         N_EXPERTS - 1).astype(jnp.int32)
    ntiles = tile_end[-1:].astype(jnp.int32)
    xg = gather_rows(hn, row_tok)
    yg = moe_experts(xg, tile_e, ntiles, w13, w2)
    return moe_combine(h, route, yg, flat, final_w)


def _final_kernel(x_ref, w_ref, o_ref):
    o_ref[...] = _rms(x_ref[...], w_ref[...])


def final_norm(h, w, tm=512):
    m, d = h.shape
    tm = min(tm, m)
    return pl.pallas_call(
        _final_kernel,
        grid=(m // tm,),
        in_specs=[pl.BlockSpec((tm, d), lambda i: (i, 0)), pl.BlockSpec((1, d), lambda i: (0, 0))],
        out_specs=pl.BlockSpec((tm, d), lambda i: (i, 0)),
        out_shape=jax.ShapeDtypeStruct((m, d), F32),
        compiler_params=_cparams(("parallel",)),
    )(h, w.reshape(1, d).astype(F32))


def _rot_cols(w):
    half = w.shape[-1] // 2
    return jnp.concatenate([-w[..., half:], w[..., :half]], axis=-1)


def _layout_w_in(w):
    seg = lambda off, width: w[:, off:off + width]
    kr = seg(O_MLA_KR, MLA_ROPE)
    xbc = seg(O_SSM_XBC, SSM_DINNER + SSM_BC)
    main = jnp.concatenate([
        seg(O_ML_Q, 512), seg(O_ML_K, 512), seg(O_ML_V, 1024), seg(O_ML_O, 1024), seg(O_SSM_Z, 1024),
        xbc[:, :SSM_DINNER], seg(O_SWA_Q, 1024) * (SWA_HEAD_DIM ** -0.5 * LOG2E), xbc[:, SSM_DINNER:],
        seg(O_MLA_CQ, 512), seg(O_MLA_CKV, 256),
        seg(O_SWA_K, 256), seg(O_SWA_V, 256), kr, _rot_cols(kr), jnp.zeros((w.shape[0], LANES), w.dtype),
        seg(O_GATE, N_BRANCH * D_MODEL)], axis=1)
    small = jnp.concatenate([seg(O_ML_I, ML_HEADS), seg(O_ML_F, ML_HEADS), seg(O_SSM_DT, SSM_HEADS),
                             jnp.zeros((w.shape[0], LANES - 2 * ML_HEADS - SSM_HEADS), w.dtype)], axis=1)
    return main.astype(BF), small.astype(BF)


def _layout_w_uq(w):
    k = w.shape[0]
    w = w.reshape(k, MLA_HEADS, MLA_NOPE + MLA_ROPE)
    rope = w[..., MLA_NOPE:]
    scale = (MLA_NOPE + MLA_ROPE) ** -0.5 * LOG2E
    return (jnp.concatenate([w, _rot_cols(rope)], axis=-1).reshape(k, -1) * scale).astype(BF)


def _rope_table(seq):
    inv_freq = 1.0 / (ROPE_THETA ** (jnp.arange(0, MLA_ROPE, 2, dtype=F32) / MLA_ROPE))
    ang = jnp.arange(seq, dtype=F32)[:, None] * inv_freq[None, :]
    c, s = jnp.cos(ang), jnp.sin(ang)
    return jnp.concatenate([c, c, s, s], axis=1)


def hybrid_mixer(h, bsz, seq, norm_w, w_main, w_small, ml_ib, ml_fb, ml_norm, conv_w, conv_b, dt_bias, a_log,
                 ssm_d, ssm_norm, q_norm, w_uq, kv_norm, w_ukv, sinks, w_branch, w_out, cs):
    n = bsz * seq
    nc = seq // CHUNK
    u2 = matmul(h, w_main, norm=norm_w, out_dtype=BF)
    small = matmul(h, w_small, norm=norm_w, out_dtype=F32, tn=LANES)
    u3 = u2.reshape(bsz, seq, U_TOTAL)

    def to_rows(cols, heads):
        return cols.reshape(bsz, seq, heads).transpose(0, 2, 1).reshape(bsz * heads * nc, CHUNK)

    def per_row(vec, heads):
        return jnp.broadcast_to(vec.astype(F32)[None, :, None], (bsz, heads, nc)).reshape(-1, 1)

    ig, bcum, dt, acs = recurrence_gates(
        to_rows(small[:, 0:ML_HEADS], ML_HEADS), to_rows(small[:, ML_HEADS:2 * ML_HEADS], ML_HEADS),
        per_row(ml_ib, ML_HEADS), per_row(ml_fb, ML_HEADS),
        to_rows(small[:, 2 * ML_HEADS:2 * ML_HEADS + SSM_HEADS], SSM_HEADS),
        per_row(dt_bias, SSM_HEADS), per_row(a_log, SSM_HEADS))

    def as_rows(x, heads):
        return x.reshape(bsz, heads, nc, CHUNK).transpose(0, 2, 1, 3)

    def as_cols(x, heads):
        return x.reshape(bsz, heads, seq).transpose(0, 2, 1)

    ya = mlstm_branch(u3, as_rows(ig, ML_HEADS), as_rows(bcum, ML_HEADS), as_cols(bcum, ML_HEADS), ml_norm)
    yb = ssd_branch(u3, as_cols(dt, SSM_HEADS), as_cols(acs, SSM_HEADS), as_rows(acs, SSM_HEADS), conv_w, conv_b,
                    jnp.repeat(ssm_d, SSM_HEADDIM), ssm_norm)
    qf = matmul(u2, w_uq, norm=q_norm, x_col_blk=U_MLA_CQ // MLA_Q_LORA)
    kvf = matmul(u2, w_ukv, norm=kv_norm, x_col_blk=U_MLA_CKV // MLA_KV_LORA)
    yc = mla_attention(qf.reshape(bsz, seq, -1), kvf.reshape(bsz, seq, -1), u3, cs)
    yd = swa_branch(u3, sinks)
    merged = gated_merge(ya.reshape(n, -1), yb.reshape(n, -1), yc.reshape(n, -1), yd.reshape(n, -1), w_branch, u2)
    return matmul(merged, w_out, residual=h, out_dtype=F32)


def kernel(x, mem, norm_mix, w_in, ml_igate_bias, ml_fgate_bias, ml_norm, ssm_conv_w, ssm_conv_b, ssm_dt_bias, ssm_a_log, ssm_d, ssm_norm, mla_q_norm, mla_w_uq, mla_kv_norm, mla_w_ukv, swa_sinks, w_branch, w_out, norm_cross, norm_mem, xa_wq, xa_wkv, xa_wo, norm_ffn, ffn_w13, ffn_w2, moe_router, moe_w13, moe_w2, norm_final):
    bsz, seq, d = x.shape
    depth = w_in.shape[0]
    n = bsz * seq
    mlen = mem.shape[1]
    cs = _rope_table(seq)
    h = x.reshape(n, d)
    mem2 = mem.reshape(bsz * mlen, d)
    for l in range(depth):
        w_main, w_small = _layout_w_in(w_in[l])
        h = hybrid_mixer(h, bsz, seq, norm_mix[l], w_main, w_small, ml_igate_bias[l], ml_fgate_bias[l], ml_norm[l],
                         ssm_conv_w[l], ssm_conv_b[l], ssm_dt_bias[l], ssm_a_log[l], ssm_d[l], ssm_norm[l],
                         mla_q_norm[l], _layout_w_uq(mla_w_uq[l]), mla_kv_norm[l], mla_w_ukv[l].astype(BF),
                         swa_sinks[l], w_branch[l].astype(BF), w_out[l].astype(BF), cs)
        kv = matmul(mem2, xa_wkv[l].astype(BF), norm=norm_mem[l])
        wq = (xa_wq[l] * (XA_HEAD_DIM ** -0.5 * LOG2E)).astype(BF)
        h = cross_attention(h, seq, norm_cross[l], wq, kv.reshape(bsz, mlen, -1), xa_wo[l].astype(BF))
        if l % 2 == 0:
            h = dense_ffn(h, norm_ffn[l], ffn_w13[l // 2].astype(BF), ffn_w2[l // 2].astype(BF))
        else:
            h = moe_ffn(h, norm_ffn[l], moe_router[l // 2], moe_w13[l // 2], moe_w2[l // 2],
                        final_w=norm_final if l == depth - 1 else None)
    if depth % 2 == 1:
        h = final_norm(h, norm_final)
    return h.reshape(bsz, seq, d)
```

```python
import functools
import math

import jax
import jax.numpy as jnp
from jax import lax
from jax.experimental import pallas as pl
from jax.experimental.pallas import tpu as pltpu

F32 = jnp.float32
BF = jnp.bfloat16

D_MODEL = 2048
RMS_EPS = 1e-6
ML_HEADS, ML_DQK, ML_DV = 4, 128, 256
SSM_HEADS, SSM_HEADDIM, SSM_GROUPS, SSM_STATE, SSM_CONV = 16, 64, 2, 128, 4
SSM_DINNER = SSM_HEADS * SSM_HEADDIM
SSM_BC = 2 * SSM_GROUPS * SSM_STATE
MLA_HEADS, MLA_Q_LORA, MLA_KV_LORA, MLA_NOPE, MLA_ROPE, MLA_V = 8, 512, 256, 128, 64, 128
ROPE_THETA = 10000.0
SWA_HEADS, SWA_KV_HEADS, SWA_HEAD_DIM, SWA_WINDOW = 16, 4, 64, 128
N_BRANCH, BRANCH_W = 4, 1024
XA_HEADS, XA_HEAD_DIM = 4, 128
FFN_DIM = 7168
N_EXPERTS, TOP_K = 8, 2

CHUNK = 128
LANES = 128
VMEM_LIMIT = 56 * 1024 * 1024
VMEM_LIMIT_BIG = 60 * 1024 * 1024
LOG2E = math.log2(math.e)

U_ML_Q, U_ML_K, U_ML_V, U_ML_O = 0, 512, 1024, 2048
U_SSM_Z, U_SSM_X, U_SWA_Q, U_SSM_BC = 3072, 4096, 5120, 6144
U_MLA_CQ, U_MLA_CKV, U_SWA_K, U_SWA_V, U_MLA_KR = 6656, 7168, 7424, 7680, 7936
U_GATE = 8192
U_TOTAL = U_GATE + N_BRANCH * D_MODEL
_SPLITS = (512, 512, 1024, 1024, 4, 4, 1024, 1536, 16, 512, 256, 64, 1024, 256, 256, 8192)
_OFF = [0]
for _s in _SPLITS:
    _OFF.append(_OFF[-1] + _s)
(O_ML_Q, O_ML_K, O_ML_V, O_ML_O, O_ML_I, O_ML_F, O_SSM_Z, O_SSM_XBC, O_SSM_DT, O_MLA_CQ, O_MLA_CKV,
 O_MLA_KR, O_SWA_Q, O_SWA_K, O_SWA_V, O_GATE, _O_END) = _OFF


def _cparams(sem, vmem_limit=VMEM_LIMIT):
    return pltpu.CompilerParams(dimension_semantics=sem, vmem_limit_bytes=vmem_limit)


def _dot(a, b):
    return jnp.dot(a, b, preferred_element_type=F32)


def _dot_nt(a, b):
    return lax.dot_general(a, b, (((1,), (1,)), ((), ())), preferred_element_type=F32)


def _dot_tn(a, b):
    return lax.dot_general(a, b, (((0,), (0,)), ((), ())), preferred_element_type=F32)


def _split3(a):
    a1 = a.astype(BF)
    r = a - a1.astype(F32)
    a2 = r.astype(BF)
    a3 = (r - a2.astype(F32)).astype(BF)
    return a1, a2, a3


def _dot_sel(a, sel):
    a1, a2, a3 = _split3(a)
    return _dot(a1, sel) + _dot(a2, sel) + _dot(a3, sel)


def _rms(x, w):
    return x * lax.rsqrt(jnp.mean(x * x, axis=-1, keepdims=True) + RMS_EPS) * w


def _sigmoid(x):
    return 1.0 / (1.0 + jnp.exp(-x))


def _mm_kernel(*refs, has_norm, has_res, has_side):
    it = iter(refs)
    x_ref = next(it)
    g_ref = next(it) if has_norm else None
    w_ref = next(it)
    r_ref = next(it) if has_res else None
    ws_ref = next(it) if has_side else None
    o_ref = next(it)
    os_ref = next(it) if has_side else None
    if has_norm:
        xn_ref = next(it)

        @pl.when(pl.program_id(1) == 0)
        def _():
            xn = _rms(x_ref[...].astype(F32), g_ref[...]).astype(BF)
            xn_ref[...] = xn
            if has_side:
                os_ref[...] = _dot(xn, ws_ref[...])

        xv = xn_ref[...]
    else:
        xv = x_ref[...]
    acc = _dot(xv, w_ref[...])
    if has_res:
        acc = acc + r_ref[...]
    o_ref[...] = acc.astype(o_ref.dtype)


def matmul(x, w, *, norm=None, residual=None, out_dtype=None, tm=1024, tn=1024, x_col_blk=0, w_layer=None,
           side_w=None):
    out_dtype = out_dtype or BF
    m = x.shape[0]
    k, n = w.shape[-2:]
    tm, tn = min(tm, m), min(tn, n)
    assert m % tm == 0 and n % tn == 0
    in_specs = [pl.BlockSpec((tm, k), lambda i, j: (i, x_col_blk))]
    args = [x]
    scratch = []
    if norm is not None:
        in_specs.append(pl.BlockSpec((1, k), lambda i, j: (0, 0)))
        args.append(norm.reshape(1, k).astype(F32))
        scratch.append(pltpu.VMEM((tm, k), BF))
    if w_layer is None:
        in_specs.append(pl.BlockSpec((k, tn), lambda i, j: (0, j)))
    else:
        in_specs.append(pl.BlockSpec((None, k, tn), lambda i, j: (w_layer, 0, j)))
    args.append(w)
    if residual is not None:
        in_specs.append(pl.BlockSpec((tm, tn), lambda i, j: (i, j)))
        args.append(residual)
    out_specs = pl.BlockSpec((tm, tn), lambda i, j: (i, j))
    out_shape = jax.ShapeDtypeStruct((m, n), out_dtype)
    if side_w is not None:
        assert norm is not None and w_layer is not None
        ns = side_w.shape[-1]
        in_specs.append(pl.BlockSpec((None, k, ns), lambda i, j: (w_layer, 0, 0)))
        args.append(side_w)
        out_specs = [out_specs, pl.BlockSpec((tm, ns), lambda i, j: (i, 0))]
        out_shape = [out_shape, jax.ShapeDtypeStruct((m, ns), F32)]
    return pl.pallas_call(
        functools.partial(_mm_kernel, has_norm=norm is not None, has_res=residual is not None,
                          has_side=side_w is not None),
        grid=(m // tm, n // tn),
        in_specs=in_specs,
        out_specs=out_specs,
        out_shape=out_shape,
        scratch_shapes=scratch,
        compiler_params=_cparams(("parallel", "arbitrary")),
    )(*args)


def _cumsum_lanes(x):
    lane = lax.broadcasted_iota(jnp.int32, x.shape, 1)
    s = 1
    while s < x.shape[1]:
        x = x + jnp.where(lane >= s, pltpu.roll(x, s, axis=1), 0.0)
        s *= 2
    return x


def _softplus(x):
    return jnp.maximum(x, 0.0) + jnp.log(1.0 + jnp.exp(-jnp.abs(x)))


def _gates_kernel(i_ref, f_ref, ib_ref, fb_ref, dt_ref, dtb_ref, alog_ref, ig_ref, b_ref, dto_ref, acs_ref):
    ig_ref[...] = i_ref[...] + ib_ref[...]
    b_ref[...] = _cumsum_lanes(-_softplus(-(f_ref[...] + fb_ref[...])))
    dt = _softplus(dt_ref[...] + dtb_ref[...])
    dto_ref[...] = dt
    acs_ref[...] = _cumsum_lanes(dt * (-jnp.exp(alog_ref[...])))


def recurrence_gates(i_rows, f_rows, ib, fb, dt_rows, dtb, alog):
    r1, r2 = i_rows.shape[0], dt_rows.shape[0]
    shp = lambda r: jax.ShapeDtypeStruct((r, CHUNK), F32)
    return pl.pallas_call(
        _gates_kernel,
        out_shape=(shp(r1), shp(r1), shp(r2), shp(r2)),
    )(i_rows, f_rows, ib, fb, dt_rows, dtb, alog)


def _mlstm_kernel(q_ref, k_ref, v_ref, o_ref, igr_ref, br_ref, bc_ref, nw_ref, y_ref, *state):
    ct_ref, n_ref, m_ref = state[0::3], state[1::3], state[2::3]
    L = CHUNK

    @pl.when(pl.program_id(1) == 0)
    def _():
        for ref in state:
            ref[...] = jnp.zeros_like(ref)

    row = lax.broadcasted_iota(jnp.int32, (L, L), 0)
    col = lax.broadcasted_iota(jnp.int32, (L, L), 1)
    causal = col <= row
    diag = col == row
    scale = ML_DQK ** -0.5
    outs = []
    for h in range(ML_HEADS):
        q = q_ref[:, h * ML_DQK:(h + 1) * ML_DQK]
        k = (k_ref[:, h * ML_DQK:(h + 1) * ML_DQK].astype(F32) * scale).astype(BF)
        v = v_ref[:, h * ML_DV:(h + 1) * ML_DV]
        bcol = bc_ref[:, h:h + 1]
        brow = br_ref[h:h + 1, :]
        igrow = igr_ref[h:h + 1, :]
        m_prev = m_ref[h][0:1, 0:1]
        n_prev = n_ref[h][0:1, :]
        ct_prev = ct_ref[h][...]

        dmat = jnp.where(causal, bcol - brow + igrow, -jnp.inf)
        m_intra = jnp.max(dmat, axis=1, keepdims=True)
        g = bcol + m_prev
        m_s = jnp.maximum(g, m_intra)
        p = jnp.exp(dmat - m_s) * _dot_nt(q, k)
        inter = jnp.exp(g - m_s)
        num = _dot(p.astype(BF), v) + inter * _dot(q, ct_prev.astype(BF))
        den = jnp.sum(p, axis=1, keepdims=True) + inter * jnp.sum(q.astype(F32) * n_prev, axis=1, keepdims=True)
        hh = num / jnp.maximum(jnp.abs(den), jnp.exp(-m_s))
        hn = _rms(hh, nw_ref[:, h * ML_DV:(h + 1) * ML_DV])
        outs.append(_sigmoid(o_ref[:, h * ML_DV:(h + 1) * ML_DV].astype(F32)) * hn)

        b_tot = brow[:, L - 1:L]
        a = b_tot - brow + igrow
        m_loc = jnp.max(a, axis=1, keepdims=True)
        wl = jnp.exp(a - m_loc)
        kw = _dot(jnp.where(diag, wl, 0.0).astype(BF), k)
        m_new = jnp.maximum(b_tot + m_prev, m_loc)
        da = jnp.exp(b_tot + m_prev - m_new)
        db = jnp.exp(m_loc - m_new)
        ct_ref[h][...] = da * ct_prev + db * _dot_tn(kw.astype(BF), v)
        n_ref[h][...] = jnp.broadcast_to(da * n_prev + db * jnp.sum(kw, axis=0, keepdims=True), n_ref[h].shape)
        m_ref[h][...] = jnp.broadcast_to(m_new, m_ref[h].shape)
    y_ref[...] = jnp.concatenate(outs, axis=1).astype(y_ref.dtype)


def mlstm_branch(u3, ig_rows, b_rows, b_cols, norm_w):
    bsz, seq, _ = u3.shape
    nc = seq // CHUNK
    L = CHUNK
    ublk = lambda width, off: pl.BlockSpec((None, L, width), lambda b, c: (b, c, off // width))
    rows = pl.BlockSpec((None, None, ML_HEADS, L), lambda b, c: (b, c, 0, 0))
    return pl.pallas_call(
        _mlstm_kernel,
        grid=(bsz, nc),
        in_specs=[ublk(512, U_ML_Q), ublk(512, U_ML_K), ublk(1024, U_ML_V), ublk(1024, U_ML_O), rows, rows,
                  pl.BlockSpec((None, L, ML_HEADS), lambda b, c: (b, c, 0)),
                  pl.BlockSpec((1, ML_HEADS * ML_DV), lambda b, c: (0, 0))],
        out_specs=pl.BlockSpec((None, L, BRANCH_W), lambda b, c: (b, c, 0)),
        out_shape=jax.ShapeDtypeStruct((bsz, seq, BRANCH_W), BF),
        scratch_shapes=[pltpu.VMEM((ML_DQK, ML_DV), F32), pltpu.VMEM((8, ML_DQK), F32),
                        pltpu.VMEM((8, LANES), F32)] * ML_HEADS,
        compiler_params=_cparams(("parallel", "arbitrary")),
    )(u3, u3, u3, u3, ig_rows, b_rows, b_cols, norm_w.reshape(1, -1).astype(F32))


def _ssd_kernel(z_ref, x_ref, bc_ref, dt_ref, ac_ref, ar_ref, cw_ref, cb_ref, d_ref, nw_ref, y_ref, xs_ref, st_ref):
    L = CHUNK
    P, R, NS = SSM_HEADDIM, SSM_HEADS // SSM_GROUPS, SSM_STATE
    GW = R * P

    @pl.when(pl.program_id(1) == 0)
    def _():
        xs_ref[0:8, :] = jnp.zeros((8, xs_ref.shape[1]), F32)
        st_ref[...] = jnp.zeros_like(st_ref)

    xs_ref[8:, :] = jnp.concatenate([x_ref[...], bc_ref[...]], axis=1).astype(F32)
    conv = cb_ref[...] + cw_ref[SSM_CONV - 1:SSM_CONV, :] * xs_ref[8:8 + L, :]
    for sft in range(1, SSM_CONV):
        conv = conv + cw_ref[SSM_CONV - 1 - sft:SSM_CONV - sft, :] * xs_ref[8 - sft:8 - sft + L, :]
    xs_ref[0:8, :] = xs_ref[L:L + 8, :]
    xbc = conv * _sigmoid(conv)
    xh = xbc[:, :SSM_DINNER]
    bmat = xbc[:, SSM_DINNER:SSM_DINNER + SSM_GROUPS * NS].astype(BF)
    cmat = xbc[:, SSM_DINNER + SSM_GROUPS * NS:].astype(BF)

    dtc = dt_ref[...]
    ac = ac_ref[...]
    ar = ar_ref[...]
    a_last = ac[L - 1:L, :]
    hsel = (lax.broadcasted_iota(jnp.int32, (SSM_HEADS, SSM_DINNER), 1) // P
            == lax.broadcasted_iota(jnp.int32, (SSM_HEADS, SSM_DINNER), 0))
    expand = jnp.where(hsel, 1.0, 0.0).astype(BF)
    stack = jnp.concatenate([dtc, jnp.exp(a_last - ac), jnp.exp(ac),
                             jnp.broadcast_to(jnp.exp(a_last), (8, SSM_HEADS))], axis=0)
    ex = _dot_sel(stack, expand)
    dt_full, dst_full, ind_full = ex[0:L], ex[L:2 * L], ex[2 * L:3 * L]
    cdec_full = ex[3 * L:3 * L + 1]
    xdt = xh * dt_full
    xdt_b = xdt.astype(BF)
    xw_b = (xdt * dst_full).astype(BF)

    row = lax.broadcasted_iota(jnp.int32, (L, L), 0)
    col = lax.broadcasted_iota(jnp.int32, (L, L), 1)
    causal = col <= row
    ys = []
    for g in range(SSM_GROUPS):
        bg = bmat[:, g * NS:(g + 1) * NS]
        cg = cmat[:, g * NS:(g + 1) * NS]
        cb = _dot_nt(cg, bg)
        st_prev = st_ref[g]
        yoff = _dot(cg, st_prev.astype(BF))
        st_ref[g] = cdec_full[:, g * GW:(g + 1) * GW] * st_prev + _dot_tn(bg, xw_b[:, g * GW:(g + 1) * GW])
        for r in range(R):
            h = g * R + r
            dec = jnp.exp(jnp.where(causal, ac[:, h:h + 1] - ar[h:h + 1, :], -jnp.inf))
            yd = _dot((dec * cb).astype(BF), xdt_b[:, h * P:(h + 1) * P])
            ys.append(yd + yoff[:, r * P:(r + 1) * P] * ind_full[:, h * P:(h + 1) * P])
    y = jnp.concatenate(ys, axis=1) + xh * d_ref[...]
    zf = z_ref[...].astype(F32)
    y = y * (zf * _sigmoid(zf))
    y_ref[...] = jnp.concatenate(
        [_rms(y[:, g * GW:(g + 1) * GW], nw_ref[:, g * GW:(g + 1) * GW]) for g in range(SSM_GROUPS)],
        axis=1).astype(y_ref.dtype)


def ssd_branch(u3, dt_cols, acs_cols, acs_rows, conv_w, conv_b, d_full, norm_w):
    bsz, seq, _ = u3.shape
    nc = seq // CHUNK
    L = CHUNK
    ublk = lambda width, off: pl.BlockSpec((None, L, width), lambda b, c: (b, c, off // width))
    cols = pl.BlockSpec((None, L, SSM_HEADS), lambda b, c: (b, c, 0))
    const = lambda shape: pl.BlockSpec(shape, lambda b, c: (0, 0))
    cch = SSM_DINNER + SSM_BC
    return pl.pallas_call(
        _ssd_kernel,
        grid=(bsz, nc),
        in_specs=[ublk(1024, U_SSM_Z), ublk(1024, U_SSM_X), ublk(512, U_SSM_BC), cols, cols,
                  pl.BlockSpec((None, None, SSM_HEADS, L), lambda b, c: (b, c, 0, 0)),
                  const((SSM_CONV, cch)), const((1, cch)), const((1, SSM_DINNER)), const((1, SSM_DINNER))],
        out_specs=pl.BlockSpec((None, L, BRANCH_W), lambda b, c: (b, c, 0)),
        out_shape=jax.ShapeDtypeStruct((bsz, seq, BRANCH_W), BF),
        scratch_shapes=[pltpu.VMEM((L + 8, cch), F32),
                        pltpu.VMEM((SSM_GROUPS, SSM_STATE, SSM_DINNER // SSM_GROUPS), F32)],
        compiler_params=_cparams(("parallel", "arbitrary")),
    )(u3, u3, u3, dt_cols, acs_cols, acs_rows, conv_w.astype(F32), conv_b.reshape(1, cch).astype(F32),
      d_full.reshape(1, -1).astype(F32), norm_w.reshape(1, -1).astype(F32))


def _rope128(x, cs):
    t = x.astype(F32) * cs
    return t + pltpu.roll(t, MLA_ROPE, axis=1)


def _mla_kernel(q_ref, csq_ref, kv_ref, kr_ref, csk_ref, o_ref, qe_s, *stats, tq):
    qi = pl.program_id(1)
    ki = pl.program_id(2)
    qw = MLA_NOPE + LANES
    kw = MLA_NOPE + MLA_V
    m_s, l_s, acc_s = stats[0::3], stats[1::3], stats[2::3]
    nt = tq // LANES

    @pl.when(ki == 0)
    def _():
        for h in range(MLA_HEADS):
            qr = _rope128(q_ref[:, h * qw + MLA_NOPE:(h + 1) * qw], csq_ref[...]).astype(BF)
            qe_s[h] = jnp.concatenate([q_ref[:, h * qw:h * qw + MLA_NOPE], qr], axis=1)
            m_s[h][...] = jnp.full((tq, LANES), -jnp.inf, F32)
            l_s[h][...] = jnp.zeros((tq, LANES), F32)
            acc_s[h][...] = jnp.zeros((tq, MLA_V), F32)

    def step(masked):
        lane = lax.broadcasted_iota(jnp.int32, (tq, LANES), 1)
        kr = jnp.where(lane < MLA_ROPE, _rope128(kr_ref[...], csk_ref[...]), 0.0).astype(BF)
        if masked:
            causal = (lax.broadcasted_iota(jnp.int32, (tq, tq), 1) <= lax.broadcasted_iota(jnp.int32, (tq, tq), 0))
        for h in range(MLA_HEADS):
            ke = jnp.concatenate([kv_ref[:, h * kw:h * kw + MLA_NOPE], kr], axis=1)
            s = _dot_nt(qe_s[h], ke)
            if masked:
                s = jnp.where(causal, s, -jnp.inf)
            m_old = m_s[h][...]
            m_new = jnp.maximum(m_old, jnp.max(s, axis=1, keepdims=True))
            alpha = jnp.exp2(m_old - m_new)
            p = jnp.exp2(s - jnp.concatenate([m_new] * nt, axis=1))
            psum = p[:, 0:LANES]
            for t in range(1, nt):
                psum = psum + p[:, t * LANES:(t + 1) * LANES]
            l_s[h][...] = alpha * l_s[h][...] + psum
            acc_s[h][...] = alpha * acc_s[h][...] + _dot(p.astype(BF), kv_ref[:, h * kw + MLA_NOPE:(h + 1) * kw])
            m_s[h][...] = m_new

    @pl.when(ki < qi)
    def _():
        step(False)

    @pl.when(ki == qi)
    def _():
        step(True)
        for h in range(MLA_HEADS):
            l = jnp.sum(l_s[h][...], axis=1, keepdims=True)
            o_ref[:, h * MLA_V:(h + 1) * MLA_V] = (acc_s[h][...] / l).astype(o_ref.dtype)


def mla_attention(qf3, kvf3, u3, cs, tq=512):
    bsz, seq, _ = qf3.shape
    tq = min(tq, seq)
    nq = seq // tq
    kvi = lambda b, qi, ki: jnp.minimum(ki, qi)
    return pl.pallas_call(
        functools.partial(_mla_kernel, tq=tq),
        grid=(bsz, nq, nq),
        in_specs=[pl.BlockSpec((None, tq, qf3.shape[2]), lambda b, qi, ki: (b, qi, 0)),
                  pl.BlockSpec((tq, LANES), lambda b, qi, ki: (qi, 0)),
                  pl.BlockSpec((None, tq, kvf3.shape[2]), lambda b, qi, ki: (b, kvi(b, qi, ki), 0)),
                  pl.BlockSpec((None, tq, LANES), lambda b, qi, ki: (b, kvi(b, qi, ki), U_MLA_KR // LANES)),
                  pl.BlockSpec((tq, LANES), lambda b, qi, ki: (kvi(b, qi, ki), 0))],
        out_specs=pl.BlockSpec((None, tq, MLA_HEADS * MLA_V), lambda b, qi, ki: (b, qi, 0)),
        out_shape=jax.ShapeDtypeStruct((bsz, seq, MLA_HEADS * MLA_V), BF),
        scratch_shapes=[pltpu.VMEM((MLA_HEADS, tq, MLA_NOPE + LANES), BF)]
        + [pltpu.VMEM((tq, LANES), F32), pltpu.VMEM((tq, LANES), F32), pltpu.VMEM((tq, MLA_V), F32)] * MLA_HEADS,
        compiler_params=_cparams(("parallel", "parallel", "arbitrary")),
    )(qf3, cs, kvf3, u3, cs)


def _swa_kernel(q_ref, kc_ref, kp_ref, vc_ref, vp_ref, sink_ref, o_ref):
    W, d = SWA_WINDOW, SWA_HEAD_DIM
    n = pl.program_id(1)
    i = lax.broadcasted_iota(jnp.int32, (2 * W, 2 * W), 0) & (W - 1)
    j = lax.broadcasted_iota(jnp.int32, (2 * W, 2 * W), 1)
    valid = (j > i) & (j <= i + W) & ((n > 0) | (j >= W))
    first_tile = lax.broadcasted_iota(jnp.int32, (2 * W, LANES), 0) < W
    lower = lax.broadcasted_iota(jnp.int32, (2 * W, LANES), 1) < d
    ones = jnp.ones((2 * W, LANES), BF)
    for pair in range(SWA_KV_HEADS // 2):
        sl = slice(pair * LANES, (pair + 1) * LANES)
        kt = jnp.concatenate([kp_ref[:, sl], kc_ref[:, sl]], axis=0)
        vt = jnp.concatenate([vp_ref[:, sl], vc_ref[:, sl]], axis=0)
        kt_sw = pltpu.roll(kt.astype(F32), d, axis=1).astype(BF)
        vt_sw = pltpu.roll(vt.astype(F32), d, axis=1).astype(BF)
        for e in range(2):
            kh = 2 * pair + e
            k_lo = jnp.where(lower, kt if e == 0 else kt_sw, jnp.zeros_like(kt))
            k_hi = jnp.where(lower, jnp.zeros_like(kt), kt_sw if e == 0 else kt)
            v_lo, v_hi = (vt, vt_sw) if e == 0 else (vt_sw, vt)
            qs = jnp.concatenate([q_ref[:, 2 * kh * LANES:(2 * kh + 1) * LANES],
                                  q_ref[:, (2 * kh + 1) * LANES:(2 * kh + 2) * LANES]], axis=0)
            halves = []
            for half, (ke, ve) in enumerate(((k_lo, v_lo), (k_hi, v_hi))):
                ha, hb = 4 * kh + half, 4 * kh + 2 + half
                s = jnp.where(valid, _dot_nt(qs, ke), -jnp.inf)
                sink = jnp.where(first_tile, sink_ref[0:1, ha:ha + 1], sink_ref[0:1, hb:hb + 1]) * LOG2E
                m = jnp.maximum(jnp.max(s, axis=1, keepdims=True), sink)
                p = jnp.exp2(s - jnp.concatenate([m, m], axis=1)).astype(BF)
                den = _dot(p, ones) + jnp.exp2(sink - m)
                halves.append(_dot(p, ve) / den)
            ot = jnp.where(lower, halves[0], halves[1]).astype(o_ref.dtype)
            o_ref[:, 2 * kh * LANES:(2 * kh + 1) * LANES] = ot[:W]
            o_ref[:, (2 * kh + 1) * LANES:(2 * kh + 2) * LANES] = ot[W:]


def swa_branch(u3, sinks):
    bsz, seq, _ = u3.shape
    W = SWA_WINDOW
    kvw = SWA_KV_HEADS * SWA_HEAD_DIM
    cur = lambda width, off: pl.BlockSpec((None, W, width), lambda b, n: (b, n, off // width))
    prev = lambda width, off: pl.BlockSpec((None, W, width), lambda b, n: (b, jnp.maximum(n - 1, 0), off // width))
    return pl.pallas_call(
        _swa_kernel,
        grid=(bsz, seq // W),
        in_specs=[cur(1024, U_SWA_Q), cur(kvw, U_SWA_K), prev(kvw, U_SWA_K), cur(kvw, U_SWA_V), prev(kvw, U_SWA_V),
                  pl.BlockSpec((1, SWA_HEADS), lambda b, n: (0, 0))],
        out_specs=pl.BlockSpec((None, W, BRANCH_W), lambda b, n: (b, n, 0)),
        out_shape=jax.ShapeDtypeStruct((bsz, seq, BRANCH_W), BF),
        compiler_params=_cparams(("parallel", "parallel")),
    )(u3, u3, u3, u3, u3, sinks.reshape(1, -1).astype(F32))


def _merge_kernel(ya_ref, yb_ref, yc_ref, yd_ref, w_ref, g0_ref, g1_ref, g2_ref, g3_ref, o_ref):
    acc = None
    for n, (y_ref, g_ref) in enumerate(((ya_ref, g0_ref), (yb_ref, g1_ref), (yc_ref, g2_ref), (yd_ref, g3_ref))):
        t = _sigmoid(g_ref[...].astype(F32)) * _dot(y_ref[...], w_ref[n])
        acc = t if acc is None else acc + t
    o_ref[...] = acc.astype(o_ref.dtype)


def gated_merge(ya, yb, yc, yd, w_branch, u2, tm=1024, tn=512):
    m = ya.shape[0]
    tm = min(tm, m)
    ysp = pl.BlockSpec((tm, BRANCH_W), lambda i, j: (i, 0))
    gsp = lambda n: pl.BlockSpec((tm, tn), lambda i, j: (i, (U_GATE + n * D_MODEL) // tn + j))
    return pl.pallas_call(
        _merge_kernel,
        grid=(m // tm, D_MODEL // tn),
        in_specs=[ysp, ysp, ysp, ysp, pl.BlockSpec((N_BRANCH, BRANCH_W, tn), lambda i, j: (0, 0, j)),
                  gsp(0), gsp(1), gsp(2), gsp(3)],
        out_specs=pl.BlockSpec((tm, tn), lambda i, j: (i, j)),
        out_shape=jax.ShapeDtypeStruct((m, D_MODEL), BF),
        compiler_params=_cparams(("parallel", "parallel")),
    )(ya, yb, yc, yd, w_branch, u2, u2, u2, u2)


def _xattn_kernel(h_ref, g_ref, wq_ref, kv_ref, wo_ref, o_ref):
    d = XA_HEAD_DIM
    h = h_ref[...]
    q = _dot(_rms(h, g_ref[...]).astype(BF), wq_ref[...]).astype(BF)
    ones = jnp.ones((kv_ref.shape[0], LANES), BF)
    outs = []
    for hd in range(XA_HEADS):
        s = _dot_nt(q[:, hd * d:(hd + 1) * d], kv_ref[:, hd * d:(hd + 1) * d])
        p = jnp.exp2(s - jnp.max(s, axis=1, keepdims=True)).astype(BF)
        den = _dot(p, ones)
        outs.append((_dot(p, kv_ref[:, (XA_HEADS + hd) * d:(XA_HEADS + hd + 1) * d]) / den).astype(BF))
    o_ref[...] = h + _dot(jnp.concatenate(outs, axis=1), wo_ref[...])


def cross_attention(h, seq, norm_w, wq, kv3, wo, tq=512):
    n, dm = h.shape
    _, mlen, kvw = kv3.shape
    tq = min(tq, seq)
    per_b = seq // tq
    const = lambda shape: pl.BlockSpec(shape, lambda i: (0, 0))
    return pl.pallas_call(
        _xattn_kernel,
        grid=(n // tq,),
        in_specs=[pl.BlockSpec((tq, dm), lambda i: (i, 0)), const((1, dm)), const(wq.shape),
                  pl.BlockSpec((None, mlen, kvw), lambda i: (i // per_b, 0, 0)), const(wo.shape)],
        out_specs=pl.BlockSpec((tq, dm), lambda i: (i, 0)),
        out_shape=jax.ShapeDtypeStruct((n, dm), F32),
        compiler_params=_cparams(("parallel",)),
    )(h, norm_w.reshape(1, dm).astype(F32), wq, kv3, wo)


def _ffn_kernel(h_ref, g_ref, w1_ref, w3_ref, w2_ref, o_ref, xn_ref, acc_ref):
    j = pl.program_id(1)

    @pl.when(j == 0)
    def _():
        hv = h_ref[...]
        xn_ref[...] = _rms(hv, g_ref[...]).astype(BF)
        acc_ref[...] = hv

    xn = xn_ref[...]
    h1 = _dot(xn, w1_ref[...])
    h3 = _dot(xn, w3_ref[...])
    act = (h1 * _sigmoid(h1) * h3).astype(BF)
    acc_ref[...] += _dot(act, w2_ref[...])

    @pl.when(j == pl.num_programs(1) - 1)
    def _():
        o_ref[...] = acc_ref[...]


def dense_ffn(h, norm_w, w13, w2, tm=512, tf=1024):
    m, d = h.shape
    f = w2.shape[0]
    tm = min(tm, m)
    nf = f // tf
    return pl.pallas_call(
        _ffn_kernel,
        grid=(m // tm, nf),
        in_specs=[pl.BlockSpec((tm, d), lambda i, j: (i, 0)), pl.BlockSpec((1, d), lambda i, j: (0, 0)),
                  pl.BlockSpec((d, tf), lambda i, j: (0, j)), pl.BlockSpec((d, tf), lambda i, j: (0, j + nf)),
                  pl.BlockSpec((tf, d), lambda i, j: (j, 0))],
        out_specs=pl.BlockSpec((tm, d), lambda i, j: (i, 0)),
        out_shape=jax.ShapeDtypeStruct((m, d), F32),
        scratch_shapes=[pltpu.VMEM((tm, d), BF), pltpu.VMEM((tm, d), F32)],
        compiler_params=_cparams(("parallel", "arbitrary")),
    )(h, norm_w.reshape(1, d).astype(F32), w13, w13, w2)


ROUTE_TM = 512
MOE_TM = 512


def _router_kernel(h_ref, g_ref, r_ref, hn_ref, route_ref, cnt_ref, carry_ref):
    tm = h_ref.shape[0]

    @pl.when(pl.program_id(0) == 0)
    def _():
        carry_ref[...] = jnp.zeros_like(carry_ref)

    hn = _rms(h_ref[...], g_ref[...])
    hn_ref[...] = hn
    a1, a2, _ = _split3(hn)
    r1, r2, _ = _split3(r_ref[...])
    lane = lax.broadcasted_iota(jnp.int32, (tm, LANES), 1)
    logits = jnp.where(lane < N_EXPERTS, _dot(a1, r1) + (_dot(a1, r2) + _dot(a2, r1)), -jnp.inf)
    v0 = jnp.max(logits, axis=1, keepdims=True)
    i0 = jnp.min(jnp.where(logits == v0, lane, LANES), axis=1, keepdims=True)
    rest = jnp.where(lane == i0, -jnp.inf, logits)
    v1 = jnp.max(rest, axis=1, keepdims=True)
    i1 = jnp.min(jnp.where(rest == v1, lane, LANES), axis=1, keepdims=True)
    ex = jnp.exp(v1 - v0)
    g0 = 1.0 / (1.0 + ex)
    g1 = ex / (1.0 + ex)
    sel0, sel1 = lane == i0, lane == i1
    onehot = jnp.where(sel0 | sel1, 1.0, 0.0)
    below = (lax.broadcasted_iota(jnp.int32, (tm, tm), 1) < lax.broadcasted_iota(jnp.int32, (tm, tm), 0))
    before = carry_ref[0:1, :] + _dot(jnp.where(below, 1.0, 0.0).astype(BF), onehot.astype(BF))
    rank0 = jnp.sum(jnp.where(sel0, before, 0.0), axis=1, keepdims=True)
    rank1 = jnp.sum(jnp.where(sel1, before, 0.0), axis=1, keepdims=True)
    total = carry_ref[0:1, :] + jnp.sum(onehot, axis=0, keepdims=True)
    carry_ref[...] = jnp.broadcast_to(total, carry_ref.shape)
    cnt_ref[...] = jnp.broadcast_to(total, cnt_ref.shape)
    out = jnp.zeros((tm, LANES), F32)
    for pos, val in enumerate((i0.astype(F32), i1.astype(F32), g0, g1, rank0, rank1)):
        out = jnp.where(lane == pos, val, out)
    route_ref[...] = out


def moe_router(h, norm_w, router):
    m, d = h.shape
    tm = min(ROUTE_TM, m)
    rpad = jnp.zeros((d, LANES), F32).at[:, :N_EXPERTS].set(router.astype(F32))
    return pl.pallas_call(
        _router_kernel,
        grid=(m // tm,),
        in_specs=[pl.BlockSpec((tm, d), lambda i: (i, 0)), pl.BlockSpec((1, d), lambda i: (0, 0)),
                  pl.BlockSpec((d, LANES), lambda i: (0, 0))],
        out_specs=[pl.BlockSpec((tm, d), lambda i: (i, 0)), pl.BlockSpec((tm, LANES), lambda i: (i, 0)),
                   pl.BlockSpec((8, LANES), lambda i: (0, 0))],
        out_shape=[jax.ShapeDtypeStruct((m, d), F32), jax.ShapeDtypeStruct((m, LANES), F32),
                   jax.ShapeDtypeStruct((8, LANES), F32)],
        scratch_shapes=[pltpu.VMEM((8, LANES), F32)],
        compiler_params=_cparams(("arbitrary",)),
    )(h, norm_w.reshape(1, d).astype(F32), rpad)


def _gather_rows_kernel(idx_ref, src_ref, o_ref, sem):
    tg = o_ref.shape[0]
    base = pl.program_id(0) * tg

    def row_copy(r):
        return pltpu.make_async_copy(src_ref.at[pl.ds(idx_ref[base + r], 1), :], o_ref.at[pl.ds(r, 1), :], sem)

    def start(r, c):
        row_copy(r).start()
        return c

    def wait(r, c):
        row_copy(r).wait()
        return c

    lax.fori_loop(0, tg, start, 0, unroll=8)
    lax.fori_loop(0, tg, wait, 0, unroll=8)


def gather_rows(src, idx, tg=256):
    n = idx.shape[0]
    d = src.shape[1]
    tg = min(tg, n)
    return pl.pallas_call(
        _gather_rows_kernel,
        grid_spec=pltpu.PrefetchScalarGridSpec(
            num_scalar_prefetch=1, grid=(n // tg,),
            in_specs=[pl.BlockSpec(memory_space=pl.ANY)],
            out_specs=pl.BlockSpec((tg, d), lambda i, idx_ref: (i, 0)),
            scratch_shapes=[pltpu.SemaphoreType.DMA(())]),
        out_shape=jax.ShapeDtypeStruct((n, d), src.dtype),
        compiler_params=_cparams(("arbitrary",)),
    )(idx, src)


def _new_expert(te_ref, m):
    return jnp.logical_or(m == 0, te_ref[m] != te_ref[jnp.maximum(m - 1, 0)])


def _moe_up_kernel(te_ref, nt_ref, x_ref, w1_ref, w3_ref, o_ref, w1b_ref, w3b_ref):
    m = pl.program_id(1)
    used = m < nt_ref[0]

    @pl.when(jnp.logical_and(used, _new_expert(te_ref, m)))
    def _():
        w1b_ref[...] = w1_ref[...].astype(BF)
        w3b_ref[...] = w3_ref[...].astype(BF)

    @pl.when(used)
    def _():
        x = x_ref[...].astype(BF)
        h1 = _dot(x, w1b_ref[...])
        h3 = _dot(x, w3b_ref[...])
        o_ref[...] = (h1 * _sigmoid(h1) * h3).astype(o_ref.dtype)

    @pl.when(jnp.logical_not(used))
    def _():
        o_ref[...] = jnp.zeros_like(o_ref)


def _moe_down_kernel(te_ref, nt_ref, a_ref, w2_ref, o_ref, wb_ref):
    m = pl.program_id(1)
    used = m < nt_ref[0]

    @pl.when(jnp.logical_and(used, _new_expert(te_ref, m)))
    def _():
        wb_ref[...] = w2_ref[...].astype(BF)

    @pl.when(used)
    def _():
        o_ref[...] = _dot(a_ref[...], wb_ref[...])

    @pl.when(jnp.logical_not(used))
    def _():
        o_ref[...] = jnp.zeros_like(o_ref)


def moe_experts(xg, tile_e, ntiles, w13, w2, tn_up=1024, tn_down=512):
    cap, d = xg.shape
    f = w2.shape[1]
    tm = MOE_TM
    nt_max = cap // tm
    nf = f // tn_up
    mt = lambda m, nt: jnp.minimum(m, nt[0] - 1)
    act = pl.pallas_call(
        _moe_up_kernel,
        grid_spec=pltpu.PrefetchScalarGridSpec(
            num_scalar_prefetch=2, grid=(nf, nt_max),
            in_specs=[pl.BlockSpec((tm, d), lambda j, m, te, nt: (mt(m, nt), 0)),
                      pl.BlockSpec((None, d, tn_up), lambda j, m, te, nt: (te[mt(m, nt)], 0, j)),
                      pl.BlockSpec((None, d, tn_up), lambda j, m, te, nt: (te[mt(m, nt)], 0, j + nf))],
            out_specs=pl.BlockSpec((tm, tn_up), lambda j, m, te, nt: (m, j)),
            scratch_shapes=[pltpu.VMEM((d, tn_up), BF), pltpu.VMEM((d, tn_up), BF)]),
        out_shape=jax.ShapeDtypeStruct((cap, f), BF),
        compiler_params=_cparams(("arbitrary", "arbitrary")),
    )(tile_e, ntiles, xg, w13, w13)
    return pl.pallas_call(
        _moe_down_kernel,
        grid_spec=pltpu.PrefetchScalarGridSpec(
            num_scalar_prefetch=2, grid=(d // tn_down, nt_max),
            in_specs=[pl.BlockSpec((tm, f), lambda j, m, te, nt: (mt(m, nt), 0)),
                      pl.BlockSpec((None, f, tn_down), lambda j, m, te, nt: (te[mt(m, nt)], 0, j))],
            out_specs=pl.BlockSpec((tm, tn_down), lambda j, m, te, nt: (m, j)),
            scratch_shapes=[pltpu.VMEM((f, tn_down), BF)]),
        out_shape=jax.ShapeDtypeStruct((cap, d), F32),
        compiler_params=_cparams(("arbitrary", "arbitrary"), VMEM_LIMIT_BIG),
    )(tile_e, ntiles, act, w2)


def _combine_kernel(pos_ref, h_ref, route_ref, y_ref, *rest, final_norm):
    if final_norm:
        nw_ref, o_ref, buf, sem = rest
    else:
        o_ref, buf, sem = rest
    tc = h_ref.shape[0]
    base = pl.program_id(0) * tc

    def row_copy(r, k):
        return pltpu.make_async_copy(y_ref.at[pl.ds(pos_ref[TOP_K * (base + r) + k], 1), :],
                                     buf.at[k, pl.ds(r, 1), :], sem)

    def start(r, c):
        for k in range(TOP_K):
            row_copy(r, k).start()
        return c

    def wait(r, c):
        for k in range(TOP_K):
            row_copy(r, k).wait()
        return c

    lax.fori_loop(0, tc, start, 0, unroll=8)
    lax.fori_loop(0, tc, wait, 0, unroll=8)
    acc = h_ref[...]
    for k in range(TOP_K):
        acc = acc + route_ref[:, TOP_K + k:TOP_K + k + 1] * buf[k]
    if final_norm:
        acc = _rms(acc, nw_ref[...])
    o_ref[...] = acc


def moe_combine(h, route, yg, pos, final_w=None, tc=256):
    n, d = h.shape
    tc = min(tc, n)
    in_specs = [pl.BlockSpec((tc, d), lambda i, pos_ref: (i, 0)), pl.BlockSpec((tc, LANES), lambda i, pos_ref: (i, 0)),
                pl.BlockSpec(memory_space=pl.ANY)]
    args = [pos, h, route, yg]
    if final_w is not None:
        in_specs.append(pl.BlockSpec((1, d), lambda i, pos_ref: (0, 0)))
        args.append(final_w.reshape(1, d).astype(F32))
    return pl.pallas_call(
        functools.partial(_combine_kernel, final_norm=final_w is not None),
        grid_spec=pltpu.PrefetchScalarGridSpec(
            num_scalar_prefetch=1, grid=(n // tc,),
            in_specs=in_specs,
            out_specs=pl.BlockSpec((tc, d), lambda i, pos_ref: (i, 0)),
            scratch_shapes=[pltpu.VMEM((TOP_K, tc, d), F32), pltpu.SemaphoreType.DMA(())]),
        out_shape=jax.ShapeDtypeStruct((n, d), F32),
        compiler_params=_cparams(("arbitrary",)),
    )(*args)


def moe_ffn(h, norm_w, router, w13, w2, final_w=None):
    n, d = h.shape
    tm = MOE_TM
    hn, route, cnt = moe_router(h, norm_w, router)
    expert = route[:, 0:TOP_K].astype(jnp.int32)
    rank = route[:, 2 * TOP_K:3 * TOP_K].astype(jnp.int32)
    counts = cnt[0, :N_EXPERTS].astype(jnp.int32)
    tiles = (counts + tm - 1) // tm
    tile_end = jnp.cumsum(tiles)
    dest = ((tile_end - tiles) * tm)[expert] + rank
    nt_max = -(-n * TOP_K // tm) + N_EXPERTS
    cap = nt_max * tm
    flat = dest.reshape(-1)
    row_tok = jnp.zeros((cap,), jnp.int32).at[flat].set(jnp.arange(n * TOP_K, dtype=jnp.int32) // TOP_K)
    tile_e = jnp.minimum(jnp.searchsorted(tile_end, jnp.arange(nt_max, dtype=jnp.int32), side="right"),
                         N_EXPERTS - 1).astype(jnp.int32)
    ntiles = tile_end[-1:].astype(jnp.int32)
    xg = gather_rows(hn, row_tok)
    yg = moe_experts(xg, tile_e, ntiles, w13, w2)
    return moe_combine(h, route, yg, flat, final_w)


def _final_kernel(x_ref, w_ref, o_ref):
    o_ref[...] = _rms(x_ref[...], w_ref[...])


def final_norm(h, w, tm=512):
    m, d = h.shape
    tm = min(tm, m)
    return pl.pallas_call(
        _final_kernel,
        grid=(m // tm,),
        in_specs=[pl.BlockSpec((tm, d), lambda i: (i, 0)), pl.BlockSpec((1, d), lambda i: (0, 0))],
        out_specs=pl.BlockSpec((tm, d), lambda i: (i, 0)),
        out_shape=jax.ShapeDtypeStruct((m, d), F32),
        compiler_params=_cparams(("parallel",)),
    )(h, w.reshape(1, d).astype(F32))


def _rot_cols(w):
    half = w.shape[-1] // 2
    return jnp.concatenate([-w[..., half:], w[..., :half]], axis=-1)


_W_IN_SEGMENTS = (
    (U_ML_Q, O_ML_Q, 512, 1.0), (U_ML_K, O_ML_K, 512, 1.0), (U_ML_V, O_ML_V, 1024, 1.0),
    (U_ML_O, O_ML_O, 1024, 1.0), (U_SSM_Z, O_SSM_Z, 1024, 1.0), (U_SSM_X, O_SSM_XBC, SSM_DINNER, 1.0),
    (U_SWA_Q, O_SWA_Q, 1024, SWA_HEAD_DIM ** -0.5 * LOG2E), (U_SSM_BC, O_SSM_XBC + SSM_DINNER, SSM_BC, 1.0),
    (U_MLA_CQ, O_MLA_CQ, MLA_Q_LORA, 1.0), (U_MLA_CKV, O_MLA_CKV, MLA_KV_LORA, 1.0),
    (U_SWA_K, O_SWA_K, 256, 1.0), (U_SWA_V, O_SWA_V, 256, 1.0), (U_GATE, O_GATE, N_BRANCH * D_MODEL, 1.0))


def _w_in_layout_kernel(w_ref, main_ref, small_ref):
    rows = w_ref.shape[0]
    for dst, src, width, scale in _W_IN_SEGMENTS:
        v = w_ref[:, src:src + width]
        main_ref[:, dst:dst + width] = (v if scale == 1.0 else v * scale).astype(BF)
    kr = w_ref[:, O_MLA_KR:O_MLA_KR + MLA_ROPE]
    main_ref[:, U_MLA_KR:U_MLA_KR + LANES] = jnp.concatenate([kr, _rot_cols(kr)], axis=1).astype(BF)
    main_ref[:, U_MLA_KR + LANES:U_GATE] = jnp.zeros((rows, U_GATE - U_MLA_KR - LANES), BF)
    small_ref[...] = jnp.concatenate(
        [w_ref[:, O_ML_I:O_ML_I + ML_HEADS], w_ref[:, O_ML_F:O_ML_F + ML_HEADS],
         w_ref[:, O_SSM_DT:O_SSM_DT + SSM_HEADS], jnp.zeros((rows, LANES - 2 * ML_HEADS - SSM_HEADS), F32)],
        axis=1).astype(BF)


def layout_w_in(w_in, rows=128):
    depth, d, n_in = w_in.shape
    return pl.pallas_call(
        _w_in_layout_kernel,
        grid=(depth, d // rows),
        in_specs=[pl.BlockSpec((None, rows, n_in), lambda l, i: (l, i, 0))],
        out_specs=[pl.BlockSpec((None, rows, U_TOTAL), lambda l, i: (l, i, 0)),
                   pl.BlockSpec((None, rows, LANES), lambda l, i: (l, i, 0))],
        out_shape=[jax.ShapeDtypeStruct((depth, d, U_TOTAL), BF), jax.ShapeDtypeStruct((depth, d, LANES), BF)],
        compiler_params=_cparams(("parallel", "parallel")),
    )(w_in)


def _layout_w_uq(w):
    k = w.shape[0]
    w = w.reshape(k, MLA_HEADS, MLA_NOPE + MLA_ROPE)
    rope = w[..., MLA_NOPE:]
    scale = (MLA_NOPE + MLA_ROPE) ** -0.5 * LOG2E
    return (jnp.concatenate([w, _rot_cols(rope)], axis=-1).reshape(k, -1) * scale).astype(BF)


def _rope_table(seq):
    inv_freq = 1.0 / (ROPE_THETA ** (jnp.arange(0, MLA_ROPE, 2, dtype=F32) / MLA_ROPE))
    ang = jnp.arange(seq, dtype=F32)[:, None] * inv_freq[None, :]
    c, s = jnp.cos(ang), jnp.sin(ang)
    return jnp.concatenate([c, c, s, s], axis=1)


def hybrid_mixer(h, bsz, seq, layer, norm_w, w_main, w_small, ml_ib, ml_fb, ml_norm, conv_w, conv_b, dt_bias, a_log,
                 ssm_d, ssm_norm, q_norm, w_uq, kv_norm, w_ukv, sinks, w_branch, w_out, cs):
    n = bsz * seq
    nc = seq // CHUNK
    u2, small = matmul(h, w_main, norm=norm_w, out_dtype=BF, w_layer=layer, side_w=w_small)
    u3 = u2.reshape(bsz, seq, U_TOTAL)

    def to_rows(cols, heads):
        return cols.reshape(bsz, seq, heads).transpose(0, 2, 1).reshape(bsz * heads * nc, CHUNK)

    def per_row(vec, heads):
        return jnp.broadcast_to(vec.astype(F32)[None, :, None], (bsz, heads, nc)).reshape(-1, 1)

    ig, bcum, dt, acs = recurrence_gates(
        to_rows(small[:, 0:ML_HEADS], ML_HEADS), to_rows(small[:, ML_HEADS:2 * ML_HEADS], ML_HEADS),
        per_row(ml_ib, ML_HEADS), per_row(ml_fb, ML_HEADS),
        to_rows(small[:, 2 * ML_HEADS:2 * ML_HEADS + SSM_HEADS], SSM_HEADS),
        per_row(dt_bias, SSM_HEADS), per_row(a_log, SSM_HEADS))

    def as_rows(x, heads):
        return x.reshape(bsz, heads, nc, CHUNK).transpose(0, 2, 1, 3)

    def as_cols(x, heads):
        return x.reshape(bsz, heads, seq).transpose(0, 2, 1)

    ya = mlstm_branch(u3, as_rows(ig, ML_HEADS), as_rows(bcum, ML_HEADS), as_cols(bcum, ML_HEADS), ml_norm)
    yb = ssd_branch(u3, as_cols(dt, SSM_HEADS), as_cols(acs, SSM_HEADS), as_rows(acs, SSM_HEADS), conv_w, conv_b,
                    jnp.repeat(ssm_d, SSM_HEADDIM), ssm_norm)
    qf = matmul(u2, w_uq, norm=q_norm, x_col_blk=U_MLA_CQ // MLA_Q_LORA)
    kvf = matmul(u2, w_ukv, norm=kv_norm, x_col_blk=U_MLA_CKV // MLA_KV_LORA)
    yc = mla_attention(qf.reshape(bsz, seq, -1), kvf.reshape(bsz, seq, -1), u3, cs)
    yd = swa_branch(u3, sinks)
    merged = gated_merge(ya.reshape(n, -1), yb.reshape(n, -1), yc.reshape(n, -1), yd.reshape(n, -1), w_branch, u2)
    return matmul(merged, w_out, residual=h, out_dtype=F32)


def kernel(x, mem, norm_mix, w_in, ml_igate_bias, ml_fgate_bias, ml_norm, ssm_conv_w, ssm_conv_b, ssm_dt_bias, ssm_a_log, ssm_d, ssm_norm, mla_q_norm, mla_w_uq, mla_kv_norm, mla_w_ukv, swa_sinks, w_branch, w_out, norm_cross, norm_mem, xa_wq, xa_wkv, xa_wo, norm_ffn, ffn_w13, ffn_w2, moe_router, moe_w13, moe_w2, norm_final):
    bsz, seq, d = x.shape
    depth = w_in.shape[0]
    n = bsz * seq
    mlen = mem.shape[1]
    cs = _rope_table(seq)
    h = x.reshape(n, d)
    mem2 = mem.reshape(bsz * mlen, d)
    w_main, w_small = layout_w_in(w_in)
    for l in range(depth):
        h = hybrid_mixer(h, bsz, seq, l, norm_mix[l], w_main, w_small, ml_igate_bias[l], ml_fgate_bias[l], ml_norm[l],
                         ssm_conv_w[l], ssm_conv_b[l], ssm_dt_bias[l], ssm_a_log[l], ssm_d[l], ssm_norm[l],
                         mla_q_norm[l], _layout_w_uq(mla_w_uq[l]), mla_kv_norm[l], mla_w_ukv[l].astype(BF),
                         swa_sinks[l], w_branch[l].astype(BF), w_out[l].astype(BF), cs)
        kv = matmul(mem2, xa_wkv[l].astype(BF), norm=norm_mem[l])
        wq = (xa_wq[l] * (XA_HEAD_DIM ** -0.5 * LOG2E)).astype(BF)
        h = cross_attention(h, seq, norm_cross[l], wq, kv.reshape(bsz, mlen, -1), xa_wo[l].astype(BF))
        if l % 2 == 0:
            h = dense_ffn(h, norm_ffn[l], ffn_w13[l // 2].astype(BF), ffn_w2[l // 2].astype(BF))
        else:
            h = moe_ffn(h, norm_ffn[l], moe_router[l // 2], moe_w13[l // 2], moe_w2[l // 2],
                        final_w=norm_final if l == depth - 1 else None)
    if depth % 2 == 1:
        h = final_norm(h, norm_final)
    return h.reshape(bsz, seq, d)
```

```python
import functools
import math

import jax
import jax.numpy as jnp
from jax import lax
from jax.experimental import pallas as pl
from jax.experimental.pallas import tpu as pltpu

F32 = jnp.float32
BF = jnp.bfloat16

D_MODEL = 2048
RMS_EPS = 1e-6
ML_HEADS, ML_DQK, ML_DV = 4, 128, 256
SSM_HEADS, SSM_HEADDIM, SSM_GROUPS, SSM_STATE, SSM_CONV = 16, 64, 2, 128, 4
SSM_DINNER = SSM_HEADS * SSM_HEADDIM
SSM_BC = 2 * SSM_GROUPS * SSM_STATE
MLA_HEADS, MLA_Q_LORA, MLA_KV_LORA, MLA_NOPE, MLA_ROPE, MLA_V = 8, 512, 256, 128, 64, 128
ROPE_THETA = 10000.0
SWA_HEADS, SWA_KV_HEADS, SWA_HEAD_DIM, SWA_WINDOW = 16, 4, 64, 128
N_BRANCH, BRANCH_W = 4, 1024
XA_HEADS, XA_HEAD_DIM = 4, 128
FFN_DIM = 7168
N_EXPERTS, TOP_K = 8, 2

CHUNK = 128
LANES = 128
SUBLANES = 8
VMEM_LIMIT = 56 * 1024 * 1024
VMEM_LIMIT_BIG = 60 * 1024 * 1024
LOG2E = math.log2(math.e)

U_ML_Q, U_ML_K, U_ML_V, U_ML_O = 0, 512, 1024, 2048
U_SSM_Z, U_SSM_X, U_SWA_Q, U_SSM_BC = 3072, 4096, 5120, 6144
U_MLA_CQ, U_MLA_CKV, U_SWA_K, U_SWA_V, U_MLA_KR = 6656, 7168, 7424, 7680, 7936
U_GATE = 8192
U_TOTAL = U_GATE + N_BRANCH * D_MODEL
_SPLITS = (512, 512, 1024, 1024, 4, 4, 1024, 1536, 16, 512, 256, 64, 1024, 256, 256, 8192)
_OFF = [0]
for _s in _SPLITS:
    _OFF.append(_OFF[-1] + _s)
(O_ML_Q, O_ML_K, O_ML_V, O_ML_O, O_ML_I, O_ML_F, O_SSM_Z, O_SSM_XBC, O_SSM_DT, O_MLA_CQ, O_MLA_CKV,
 O_MLA_KR, O_SWA_Q, O_SWA_K, O_SWA_V, O_GATE, _O_END) = _OFF


def _cparams(sem, vmem_limit=VMEM_LIMIT):
    return pltpu.CompilerParams(dimension_semantics=sem, vmem_limit_bytes=vmem_limit)


def _dot(a, b):
    return jnp.dot(a, b, preferred_element_type=F32)


def _dot_nt(a, b):
    return lax.dot_general(a, b, (((1,), (1,)), ((), ())), preferred_element_type=F32)


def _dot_tn(a, b):
    return lax.dot_general(a, b, (((0,), (0,)), ((), ())), preferred_element_type=F32)


def _split3(a):
    a1 = a.astype(BF)
    r = a - a1.astype(F32)
    a2 = r.astype(BF)
    a3 = (r - a2.astype(F32)).astype(BF)
    return a1, a2, a3


def _dot_sel(a, sel):
    a1, a2, a3 = _split3(a)
    return _dot(a1, sel) + _dot(a2, sel) + _dot(a3, sel)


def _rms(x, w):
    return x * lax.rsqrt(jnp.mean(x * x, axis=-1, keepdims=True) + RMS_EPS) * w


def _sigmoid(x):
    return 1.0 / (1.0 + jnp.exp(-x))


def _mm_kernel(*refs, has_norm, has_res, has_side, w_t):
    mm = _dot_nt if w_t else _dot
    it = iter(refs)
    x_ref = next(it)
    g_ref = next(it) if has_norm else None
    w_ref = next(it)
    r_ref = next(it) if has_res else None
    ws_ref = next(it) if has_side else None
    o_ref = next(it)
    os_ref = next(it) if has_side else None
    if has_norm:
        xn_ref = next(it)

        @pl.when(pl.program_id(1) == 0)
        def _():
            xn = _rms(x_ref[...].astype(F32), g_ref[...]).astype(BF)
            xn_ref[...] = xn
            if has_side:
                os_ref[...] = mm(xn, ws_ref[...])

        xv = xn_ref[...]
    else:
        xv = x_ref[...]
    acc = mm(xv, w_ref[...])
    if has_res:
        acc = acc + r_ref[...]
    o_ref[...] = acc.astype(o_ref.dtype)


def matmul(x, w, *, norm=None, residual=None, out_dtype=None, tm=1024, tn=1024, x_col_blk=0, w_layer=None,
           side_w=None, side_rows=None, w_t=False, n=None):
    out_dtype = out_dtype or BF
    m = x.shape[0]
    k, n_all = w.shape[-2:][::-1] if w_t else w.shape[-2:]
    n = n or n_all
    tm, tn = min(tm, m), min(tn, n)
    assert m % tm == 0 and n % tn == 0
    in_specs = [pl.BlockSpec((tm, k), lambda i, j: (i, x_col_blk))]
    args = [x]
    scratch = []
    if norm is not None:
        in_specs.append(pl.BlockSpec((1, k), lambda i, j: (0, 0)))
        args.append(norm.reshape(1, k).astype(F32))
        scratch.append(pltpu.VMEM((tm, k), BF))
    if w_layer is None:
        in_specs.append(pl.BlockSpec((k, tn), lambda i, j: (0, j)))
    else:
        in_specs.append(pl.BlockSpec((None, tn, k), lambda i, j: (w_layer, j, 0)) if w_t else
                        pl.BlockSpec((None, k, tn), lambda i, j: (w_layer, 0, j)))
    args.append(w)
    if residual is not None:
        in_specs.append(pl.BlockSpec((tm, tn), lambda i, j: (i, j)))
        args.append(residual)
    out_specs = pl.BlockSpec((tm, tn), lambda i, j: (i, j))
    out_shape = jax.ShapeDtypeStruct((m, n), out_dtype)
    if side_w is not None:
        assert norm is not None and w_layer is not None
        if side_rows is None:
            ns = side_w.shape[-2] if w_t else side_w.shape[-1]
            in_specs.append(pl.BlockSpec((None,) + side_w.shape[1:], lambda i, j: (w_layer, 0, 0)))
        else:
            blk, ns = side_rows
            in_specs.append(pl.BlockSpec((None, ns, k), lambda i, j: (w_layer, blk, 0)))
        args.append(side_w)
        out_specs = [out_specs, pl.BlockSpec((tm, ns), lambda i, j: (i, 0))]
        out_shape = [out_shape, jax.ShapeDtypeStruct((m, ns), F32)]
    return pl.pallas_call(
        functools.partial(_mm_kernel, has_norm=norm is not None, has_res=residual is not None,
                          has_side=side_w is not None, w_t=w_t),
        grid=(m // tm, n // tn),
        in_specs=in_specs,
        out_specs=out_specs,
        out_shape=out_shape,
        scratch_shapes=scratch,
        compiler_params=_cparams(("parallel", "arbitrary")),
    )(*args)


def _cumsum_lanes(x):
    lane = lax.broadcasted_iota(jnp.int32, x.shape, 1)
    s = 1
    while s < x.shape[1]:
        x = x + jnp.where(lane >= s, pltpu.roll(x, s, axis=1), 0.0)
        s *= 2
    return x


def _softplus(x):
    return jnp.maximum(x, 0.0) + jnp.log(1.0 + jnp.exp(-jnp.abs(x)))


def _gates_kernel(i_ref, f_ref, ib_ref, fb_ref, dt_ref, dtb_ref, alog_ref, ig_ref, b_ref, dto_ref, acs_ref):
    ig_ref[...] = i_ref[...] + ib_ref[...]
    b_ref[...] = _cumsum_lanes(-_softplus(-(f_ref[...] + fb_ref[...])))
    dt = _softplus(dt_ref[...] + dtb_ref[...])
    dto_ref[...] = dt
    acs_ref[...] = _cumsum_lanes(dt * (-jnp.exp(alog_ref[...])))


def recurrence_gates(i_rows, f_rows, ib, fb, dt_rows, dtb, alog):
    r1, r2 = i_rows.shape[0], dt_rows.shape[0]
    shp = lambda r: jax.ShapeDtypeStruct((r, CHUNK), F32)
    return pl.pallas_call(
        _gates_kernel,
        out_shape=(shp(r1), shp(r1), shp(r2), shp(r2)),
    )(i_rows, f_rows, ib, fb, dt_rows, dtb, alog)


def _mlstm_kernel(q_ref, k_ref, v_ref, o_ref, igr_ref, br_ref, bc_ref, nw_ref, y_ref, *state):
    ct_ref, n_ref, m_ref = state[0::3], state[1::3], state[2::3]
    L = CHUNK

    @pl.when(pl.program_id(1) == 0)
    def _():
        for ref in state:
            ref[...] = jnp.zeros_like(ref)

    row = lax.broadcasted_iota(jnp.int32, (L, L), 0)
    col = lax.broadcasted_iota(jnp.int32, (L, L), 1)
    causal = col <= row
    diag = col == row
    scale = ML_DQK ** -0.5
    outs = []
    for h in range(ML_HEADS):
        q = q_ref[:, h * ML_DQK:(h + 1) * ML_DQK]
        k = (k_ref[:, h * ML_DQK:(h + 1) * ML_DQK].astype(F32) * scale).astype(BF)
        v = v_ref[:, h * ML_DV:(h + 1) * ML_DV]
        bcol = bc_ref[:, h:h + 1]
        brow = br_ref[h:h + 1, :]
        igrow = igr_ref[h:h + 1, :]
        m_prev = m_ref[h][0:1, 0:1]
        n_prev = n_ref[h][0:1, :]
        ct_prev = ct_ref[h][...]

        dmat = jnp.where(causal, bcol - brow + igrow, -jnp.inf)
        m_intra = jnp.max(dmat, axis=1, keepdims=True)
        g = bcol + m_prev
        m_s = jnp.maximum(g, m_intra)
        p = jnp.exp(dmat - m_s) * _dot_nt(q, k)
        inter = jnp.exp(g - m_s)
        num = _dot(p.astype(BF), v) + inter * _dot(q, ct_prev.astype(BF))
        den = jnp.sum(p, axis=1, keepdims=True) + inter * jnp.sum(q.astype(F32) * n_prev, axis=1, keepdims=True)
        hh = num / jnp.maximum(jnp.abs(den), jnp.exp(-m_s))
        hn = _rms(hh, nw_ref[:, h * ML_DV:(h + 1) * ML_DV])
        outs.append(_sigmoid(o_ref[:, h * ML_DV:(h + 1) * ML_DV].astype(F32)) * hn)

        b_tot = brow[:, L - 1:L]
        a = b_tot - brow + igrow
        m_loc = jnp.max(a, axis=1, keepdims=True)
        wl = jnp.exp(a - m_loc)
        kw = _dot(jnp.where(diag, wl, 0.0).astype(BF), k)
        m_new = jnp.maximum(b_tot + m_prev, m_loc)
        da = jnp.exp(b_tot + m_prev - m_new)
        db = jnp.exp(m_loc - m_new)
        ct_ref[h][...] = da * ct_prev + db * _dot_tn(kw.astype(BF), v)
        n_ref[h][...] = jnp.broadcast_to(da * n_prev + db * jnp.sum(kw, axis=0, keepdims=True), n_ref[h].shape)
        m_ref[h][...] = jnp.broadcast_to(m_new, m_ref[h].shape)
    y_ref[...] = jnp.concatenate(outs, axis=1).astype(y_ref.dtype)


def mlstm_branch(u3, ig_rows, b_rows, b_cols, norm_w):
    bsz, seq, _ = u3.shape
    nc = seq // CHUNK
    L = CHUNK
    ublk = lambda width, off: pl.BlockSpec((None, L, width), lambda b, c: (b, c, off // width))
    rows = pl.BlockSpec((None, None, ML_HEADS, L), lambda b, c: (b, c, 0, 0))
    return pl.pallas_call(
        _mlstm_kernel,
        grid=(bsz, nc),
        in_specs=[ublk(512, U_ML_Q), ublk(512, U_ML_K), ublk(1024, U_ML_V), ublk(1024, U_ML_O), rows, rows,
                  pl.BlockSpec((None, L, ML_HEADS), lambda b, c: (b, c, 0)),
                  pl.BlockSpec((1, ML_HEADS * ML_DV), lambda b, c: (0, 0))],
        out_specs=pl.BlockSpec((None, L, BRANCH_W), lambda b, c: (b, c, 0)),
        out_shape=jax.ShapeDtypeStruct((bsz, seq, BRANCH_W), BF),
        scratch_shapes=[pltpu.VMEM((ML_DQK, ML_DV), F32), pltpu.VMEM((8, ML_DQK), F32),
                        pltpu.VMEM((8, LANES), F32)] * ML_HEADS,
        compiler_params=_cparams(("parallel", "arbitrary")),
    )(u3, u3, u3, u3, ig_rows, b_rows, b_cols, norm_w.reshape(1, -1).astype(F32))


def _ssd_kernel(z_ref, x_ref, bc_ref, dt_ref, ac_ref, ar_ref, cw_ref, cb_ref, d_ref, nw_ref, y_ref, xs_ref, st_ref):
    L = CHUNK
    P, R, NS = SSM_HEADDIM, SSM_HEADS // SSM_GROUPS, SSM_STATE
    GW = R * P

    @pl.when(pl.program_id(1) == 0)
    def _():
        xs_ref[0:8, :] = jnp.zeros((8, xs_ref.shape[1]), F32)
        st_ref[...] = jnp.zeros_like(st_ref)

    xs_ref[8:, :] = jnp.concatenate([x_ref[...], bc_ref[...]], axis=1).astype(F32)
    conv = cb_ref[...] + cw_ref[SSM_CONV - 1:SSM_CONV, :] * xs_ref[8:8 + L, :]
    for sft in range(1, SSM_CONV):
        conv = conv + cw_ref[SSM_CONV - 1 - sft:SSM_CONV - sft, :] * xs_ref[8 - sft:8 - sft + L, :]
    xs_ref[0:8, :] = xs_ref[L:L + 8, :]
    xbc = conv * _sigmoid(conv)
    xh = xbc[:, :SSM_DINNER]
    bmat = xbc[:, SSM_DINNER:SSM_DINNER + SSM_GROUPS * NS].astype(BF)
    cmat = xbc[:, SSM_DINNER + SSM_GROUPS * NS:].astype(BF)

    dtc = dt_ref[...]
    ac = ac_ref[...]
    ar = ar_ref[...]
    a_last = ac[L - 1:L, :]
    hsel = (lax.broadcasted_iota(jnp.int32, (SSM_HEADS, SSM_DINNER), 1) // P
            == lax.broadcasted_iota(jnp.int32, (SSM_HEADS, SSM_DINNER), 0))
    expand = jnp.where(hsel, 1.0, 0.0).astype(BF)
    stack = jnp.concatenate([dtc, jnp.exp(a_last - ac), jnp.exp(ac),
                             jnp.broadcast_to(jnp.exp(a_last), (8, SSM_HEADS))], axis=0)
    ex = _dot_sel(stack, expand)
    dt_full, dst_full, ind_full = ex[0:L], ex[L:2 * L], ex[2 * L:3 * L]
    cdec_full = ex[3 * L:3 * L + 1]
    xdt = xh * dt_full
    xdt_b = xdt.astype(BF)
    xw_b = (xdt * dst_full).astype(BF)

    row = lax.broadcasted_iota(jnp.int32, (L, L), 0)
    col = lax.broadcasted_iota(jnp.int32, (L, L), 1)
    causal = col <= row
    ys = []
    for g in range(SSM_GROUPS):
        bg = bmat[:, g * NS:(g + 1) * NS]
        cg = cmat[:, g * NS:(g + 1) * NS]
        cb = _dot_nt(cg, bg)
        st_prev = st_ref[g]
        yoff = _dot(cg, st_prev.astype(BF))
        st_ref[g] = cdec_full[:, g * GW:(g + 1) * GW] * st_prev + _dot_tn(bg, xw_b[:, g * GW:(g + 1) * GW])
        for r in range(R):
            h = g * R + r
            dec = jnp.exp(jnp.where(causal, ac[:, h:h + 1] - ar[h:h + 1, :], -jnp.inf))
            yd = _dot((dec * cb).astype(BF), xdt_b[:, h * P:(h + 1) * P])
            ys.append(yd + yoff[:, r * P:(r + 1) * P] * ind_full[:, h * P:(h + 1) * P])
    y = jnp.concatenate(ys, axis=1) + xh * d_ref[...]
    zf = z_ref[...].astype(F32)
    y = y * (zf * _sigmoid(zf))
    y_ref[...] = jnp.concatenate(
        [_rms(y[:, g * GW:(g + 1) * GW], nw_ref[:, g * GW:(g + 1) * GW]) for g in range(SSM_GROUPS)],
        axis=1).astype(y_ref.dtype)


def ssd_branch(u3, dt_cols, acs_cols, acs_rows, conv_w, conv_b, d_full, norm_w):
    bsz, seq, _ = u3.shape
    nc = seq // CHUNK
    L = CHUNK
    ublk = lambda width, off: pl.BlockSpec((None, L, width), lambda b, c: (b, c, off // width))
    cols = pl.BlockSpec((None, L, SSM_HEADS), lambda b, c: (b, c, 0))
    const = lambda shape: pl.BlockSpec(shape, lambda b, c: (0, 0))
    cch = SSM_DINNER + SSM_BC
    return pl.pallas_call(
        _ssd_kernel,
        grid=(bsz, nc),
        in_specs=[ublk(1024, U_SSM_Z), ublk(1024, U_SSM_X), ublk(512, U_SSM_BC), cols, cols,
                  pl.BlockSpec((None, None, SSM_HEADS, L), lambda b, c: (b, c, 0, 0)),
                  const((SSM_CONV, cch)), const((1, cch)), const((1, SSM_DINNER)), const((1, SSM_DINNER))],
        out_specs=pl.BlockSpec((None, L, BRANCH_W), lambda b, c: (b, c, 0)),
        out_shape=jax.ShapeDtypeStruct((bsz, seq, BRANCH_W), BF),
        scratch_shapes=[pltpu.VMEM((L + 8, cch), F32),
                        pltpu.VMEM((SSM_GROUPS, SSM_STATE, SSM_DINNER // SSM_GROUPS), F32)],
        compiler_params=_cparams(("parallel", "arbitrary")),
    )(u3, u3, u3, dt_cols, acs_cols, acs_rows, conv_w.astype(F32), conv_b.reshape(1, cch).astype(F32),
      d_full.reshape(1, -1).astype(F32), norm_w.reshape(1, -1).astype(F32))


def _rope128(x, cs):
    t = x.astype(F32) * cs
    return t + pltpu.roll(t, MLA_ROPE, axis=1)


def _mla_kernel(q_ref, csq_ref, kv_ref, kr_ref, csk_ref, o_ref, qe_s, *stats, tq):
    qi = pl.program_id(1)
    ki = pl.program_id(2)
    qw = MLA_NOPE + LANES
    kw = MLA_NOPE + MLA_V
    m_s, l_s, acc_s = stats[0::3], stats[1::3], stats[2::3]
    nt = tq // LANES

    @pl.when(ki == 0)
    def _():
        for h in range(MLA_HEADS):
            qr = _rope128(q_ref[:, h * qw + MLA_NOPE:(h + 1) * qw], csq_ref[...]).astype(BF)
            qe_s[h] = jnp.concatenate([q_ref[:, h * qw:h * qw + MLA_NOPE], qr], axis=1)
            m_s[h][...] = jnp.full((tq, LANES), -jnp.inf, F32)
            l_s[h][...] = jnp.zeros((tq, LANES), F32)
            acc_s[h][...] = jnp.zeros((tq, MLA_V), F32)

    def step(masked):
        lane = lax.broadcasted_iota(jnp.int32, (tq, LANES), 1)
        kr = jnp.where(lane < MLA_ROPE, _rope128(kr_ref[...], csk_ref[...]), 0.0).astype(BF)
        if masked:
            causal = (lax.broadcasted_iota(jnp.int32, (tq, tq), 1) <= lax.broadcasted_iota(jnp.int32, (tq, tq), 0))
        for h in range(MLA_HEADS):
            ke = jnp.concatenate([kv_ref[:, h * kw:h * kw + MLA_NOPE], kr], axis=1)
            s = _dot_nt(qe_s[h], ke)
            if masked:
                s = jnp.where(causal, s, -jnp.inf)
            m_old = m_s[h][...]
            m_new = jnp.maximum(m_old, jnp.max(s, axis=1, keepdims=True))
            alpha = jnp.exp2(m_old - m_new)
            p = jnp.exp2(s - jnp.concatenate([m_new] * nt, axis=1))
            psum = p[:, 0:LANES]
            for t in range(1, nt):
                psum = psum + p[:, t * LANES:(t + 1) * LANES]
            l_s[h][...] = alpha * l_s[h][...] + psum
            acc_s[h][...] = alpha * acc_s[h][...] + _dot(p.astype(BF), kv_ref[:, h * kw + MLA_NOPE:(h + 1) * kw])
            m_s[h][...] = m_new

    @pl.when(ki < qi)
    def _():
        step(False)

    @pl.when(ki == qi)
    def _():
        step(True)
        for h in range(MLA_HEADS):
            l = jnp.sum(l_s[h][...], axis=1, keepdims=True)
            o_ref[:, h * MLA_V:(h + 1) * MLA_V] = (acc_s[h][...] / l).astype(o_ref.dtype)


def mla_attention(qf3, kvf3, u3, cs, tq=512):
    bsz, seq, _ = qf3.shape
    tq = min(tq, seq)
    nq = seq // tq
    kvi = lambda b, qi, ki: jnp.minimum(ki, qi)
    return pl.pallas_call(
        functools.partial(_mla_kernel, tq=tq),
        grid=(bsz, nq, nq),
        in_specs=[pl.BlockSpec((None, tq, qf3.shape[2]), lambda b, qi, ki: (b, qi, 0)),
                  pl.BlockSpec((tq, LANES), lambda b, qi, ki: (qi, 0)),
                  pl.BlockSpec((None, tq, kvf3.shape[2]), lambda b, qi, ki: (b, kvi(b, qi, ki), 0)),
                  pl.BlockSpec((None, tq, LANES), lambda b, qi, ki: (b, kvi(b, qi, ki), U_MLA_KR // LANES)),
                  pl.BlockSpec((tq, LANES), lambda b, qi, ki: (kvi(b, qi, ki), 0))],
        out_specs=pl.BlockSpec((None, tq, MLA_HEADS * MLA_V), lambda b, qi, ki: (b, qi, 0)),
        out_shape=jax.ShapeDtypeStruct((bsz, seq, MLA_HEADS * MLA_V), BF),
        scratch_shapes=[pltpu.VMEM((MLA_HEADS, tq, MLA_NOPE + LANES), BF)]
        + [pltpu.VMEM((tq, LANES), F32), pltpu.VMEM((tq, LANES), F32), pltpu.VMEM((tq, MLA_V), F32)] * MLA_HEADS,
        compiler_params=_cparams(("parallel", "parallel", "arbitrary")),
    )(qf3, cs, kvf3, u3, cs)


def _swa_kernel(q_ref, kc_ref, kp_ref, vc_ref, vp_ref, sink_ref, o_ref):
    W, d = SWA_WINDOW, SWA_HEAD_DIM
    n = pl.program_id(1)
    i = lax.broadcasted_iota(jnp.int32, (2 * W, 2 * W), 0) & (W - 1)
    j = lax.broadcasted_iota(jnp.int32, (2 * W, 2 * W), 1)
    valid = (j > i) & (j <= i + W) & ((n > 0) | (j >= W))
    first_tile = lax.broadcasted_iota(jnp.int32, (2 * W, LANES), 0) < W
    lower = lax.broadcasted_iota(jnp.int32, (2 * W, LANES), 1) < d
    ones = jnp.ones((2 * W, LANES), BF)
    for pair in range(SWA_KV_HEADS // 2):
        sl = slice(pair * LANES, (pair + 1) * LANES)
        kt = jnp.concatenate([kp_ref[:, sl], kc_ref[:, sl]], axis=0)
        vt = jnp.concatenate([vp_ref[:, sl], vc_ref[:, sl]], axis=0)
        kt_sw = pltpu.roll(kt.astype(F32), d, axis=1).astype(BF)
        vt_sw = pltpu.roll(vt.astype(F32), d, axis=1).astype(BF)
        for e in range(2):
            kh = 2 * pair + e
            k_lo = jnp.where(lower, kt if e == 0 else kt_sw, jnp.zeros_like(kt))
            k_hi = jnp.where(lower, jnp.zeros_like(kt), kt_sw if e == 0 else kt)
            v_lo, v_hi = (vt, vt_sw) if e == 0 else (vt_sw, vt)
            qs = jnp.concatenate([q_ref[:, 2 * kh * LANES:(2 * kh + 1) * LANES],
                                  q_ref[:, (2 * kh + 1) * LANES:(2 * kh + 2) * LANES]], axis=0)
            halves = []
            for half, (ke, ve) in enumerate(((k_lo, v_lo), (k_hi, v_hi))):
                ha, hb = 4 * kh + half, 4 * kh + 2 + half
                s = jnp.where(valid, _dot_nt(qs, ke), -jnp.inf)
                sink = jnp.where(first_tile, sink_ref[0:1, ha:ha + 1], sink_ref[0:1, hb:hb + 1]) * LOG2E
                m = jnp.maximum(jnp.max(s, axis=1, keepdims=True), sink)
                p = jnp.exp2(s - jnp.concatenate([m, m], axis=1)).astype(BF)
                den = _dot(p, ones) + jnp.exp2(sink - m)
                halves.append(_dot(p, ve) / den)
            ot = jnp.where(lower, halves[0], halves[1]).astype(o_ref.dtype)
            o_ref[:, 2 * kh * LANES:(2 * kh + 1) * LANES] = ot[:W]
            o_ref[:, (2 * kh + 1) * LANES:(2 * kh + 2) * LANES] = ot[W:]


def swa_branch(u3, sinks):
    bsz, seq, _ = u3.shape
    W = SWA_WINDOW
    kvw = SWA_KV_HEADS * SWA_HEAD_DIM
    cur = lambda width, off: pl.BlockSpec((None, W, width), lambda b, n: (b, n, off // width))
    prev = lambda width, off: pl.BlockSpec((None, W, width), lambda b, n: (b, jnp.maximum(n - 1, 0), off // width))
    return pl.pallas_call(
        _swa_kernel,
        grid=(bsz, seq // W),
        in_specs=[cur(1024, U_SWA_Q), cur(kvw, U_SWA_K), prev(kvw, U_SWA_K), cur(kvw, U_SWA_V), prev(kvw, U_SWA_V),
                  pl.BlockSpec((1, SWA_HEADS), lambda b, n: (0, 0))],
        out_specs=pl.BlockSpec((None, W, BRANCH_W), lambda b, n: (b, n, 0)),
        out_shape=jax.ShapeDtypeStruct((bsz, seq, BRANCH_W), BF),
        compiler_params=_cparams(("parallel", "parallel")),
    )(u3, u3, u3, u3, u3, sinks.reshape(1, -1).astype(F32))


def _merge_kernel(ya_ref, yb_ref, yc_ref, yd_ref, w_ref, g0_ref, g1_ref, g2_ref, g3_ref, o_ref):
    acc = None
    for n, (y_ref, g_ref) in enumerate(((ya_ref, g0_ref), (yb_ref, g1_ref), (yc_ref, g2_ref), (yd_ref, g3_ref))):
        t = _sigmoid(g_ref[...].astype(F32)) * _dot(y_ref[...], w_ref[n])
        acc = t if acc is None else acc + t
    o_ref[...] = acc.astype(o_ref.dtype)


def gated_merge(ya, yb, yc, yd, w_branch, u2, tm=1024, tn=512):
    m = ya.shape[0]
    tm = min(tm, m)
    ysp = pl.BlockSpec((tm, BRANCH_W), lambda i, j: (i, 0))
    gsp = lambda n: pl.BlockSpec((tm, tn), lambda i, j: (i, (U_GATE + n * D_MODEL) // tn + j))
    return pl.pallas_call(
        _merge_kernel,
        grid=(m // tm, D_MODEL // tn),
        in_specs=[ysp, ysp, ysp, ysp, pl.BlockSpec((N_BRANCH, BRANCH_W, tn), lambda i, j: (0, 0, j)),
                  gsp(0), gsp(1), gsp(2), gsp(3)],
        out_specs=pl.BlockSpec((tm, tn), lambda i, j: (i, j)),
        out_shape=jax.ShapeDtypeStruct((m, D_MODEL), BF),
        compiler_params=_cparams(("parallel", "parallel")),
    )(ya, yb, yc, yd, w_branch, u2, u2, u2, u2)


def _xattn_kernel(h_ref, g_ref, wq_ref, kv_ref, wo_ref, o_ref):
    d = XA_HEAD_DIM
    h = h_ref[...]
    q = _dot(_rms(h, g_ref[...]).astype(BF), wq_ref[...]).astype(BF)
    ones = jnp.ones((kv_ref.shape[0], LANES), BF)
    outs = []
    for hd in range(XA_HEADS):
        s = _dot_nt(q[:, hd * d:(hd + 1) * d], kv_ref[:, hd * d:(hd + 1) * d])
        p = jnp.exp2(s - jnp.max(s, axis=1, keepdims=True)).astype(BF)
        den = _dot(p, ones)
        outs.append((_dot(p, kv_ref[:, (XA_HEADS + hd) * d:(XA_HEADS + hd + 1) * d]) / den).astype(BF))
    o_ref[...] = h + _dot(jnp.concatenate(outs, axis=1), wo_ref[...])


def cross_attention(h, seq, norm_w, wq, kv3, wo, tq=512):
    n, dm = h.shape
    _, mlen, kvw = kv3.shape
    tq = min(tq, seq)
    per_b = seq // tq
    const = lambda shape: pl.BlockSpec(shape, lambda i: (0, 0))
    return pl.pallas_call(
        _xattn_kernel,
        grid=(n // tq,),
        in_specs=[pl.BlockSpec((tq, dm), lambda i: (i, 0)), const((1, dm)), const(wq.shape),
                  pl.BlockSpec((None, mlen, kvw), lambda i: (i // per_b, 0, 0)), const(wo.shape)],
        out_specs=pl.BlockSpec((tq, dm), lambda i: (i, 0)),
        out_shape=jax.ShapeDtypeStruct((n, dm), F32),
        compiler_params=_cparams(("parallel",)),
    )(h, norm_w.reshape(1, dm).astype(F32), wq, kv3, wo)


def _ffn_kernel(h_ref, g_ref, w1_ref, w3_ref, w2_ref, o_ref, xn_ref, acc_ref):
    j = pl.program_id(1)

    @pl.when(j == 0)
    def _():
        hv = h_ref[...]
        xn_ref[...] = _rms(hv, g_ref[...]).astype(BF)
        acc_ref[...] = hv

    xn = xn_ref[...]
    h1 = _dot(xn, w1_ref[...])
    h3 = _dot(xn, w3_ref[...])
    act = (h1 * _sigmoid(h1) * h3).astype(BF)
    acc_ref[...] += _dot(act, w2_ref[...])

    @pl.when(j == pl.num_programs(1) - 1)
    def _():
        o_ref[...] = acc_ref[...]


def dense_ffn(h, norm_w, w13, w2, tm=512, tf=1024):
    m, d = h.shape
    f = w2.shape[0]
    tm = min(tm, m)
    nf = f // tf
    return pl.pallas_call(
        _ffn_kernel,
        grid=(m // tm, nf),
        in_specs=[pl.BlockSpec((tm, d), lambda i, j: (i, 0)), pl.BlockSpec((1, d), lambda i, j: (0, 0)),
                  pl.BlockSpec((d, tf), lambda i, j: (0, j)), pl.BlockSpec((d, tf), lambda i, j: (0, j + nf)),
                  pl.BlockSpec((tf, d), lambda i, j: (j, 0))],
        out_specs=pl.BlockSpec((tm, d), lambda i, j: (i, 0)),
        out_shape=jax.ShapeDtypeStruct((m, d), F32),
        scratch_shapes=[pltpu.VMEM((tm, d), BF), pltpu.VMEM((tm, d), F32)],
        compiler_params=_cparams(("parallel", "arbitrary")),
    )(h, norm_w.reshape(1, d).astype(F32), w13, w13, w2)


ROUTE_TM = 512
MOE_TM = 512


def _router_kernel(h_ref, g_ref, r_ref, hn_ref, route_ref, cnt_ref, carry_ref):
    tm = h_ref.shape[0]

    @pl.when(pl.program_id(0) == 0)
    def _():
        carry_ref[...] = jnp.zeros_like(carry_ref)

    hn = _rms(h_ref[...], g_ref[...])
    hn_ref[...] = hn
    a1, a2, _ = _split3(hn)
    r1, r2, _ = _split3(r_ref[...])
    lane = lax.broadcasted_iota(jnp.int32, (tm, LANES), 1)
    logits = jnp.where(lane < N_EXPERTS, _dot(a1, r1) + (_dot(a1, r2) + _dot(a2, r1)), -jnp.inf)
    v0 = jnp.max(logits, axis=1, keepdims=True)
    i0 = jnp.min(jnp.where(logits == v0, lane, LANES), axis=1, keepdims=True)
    rest = jnp.where(lane == i0, -jnp.inf, logits)
    v1 = jnp.max(rest, axis=1, keepdims=True)
    i1 = jnp.min(jnp.where(rest == v1, lane, LANES), axis=1, keepdims=True)
    ex = jnp.exp(v1 - v0)
    g0 = 1.0 / (1.0 + ex)
    g1 = ex / (1.0 + ex)
    sel0, sel1 = lane == i0, lane == i1
    onehot = jnp.where(sel0 | sel1, 1.0, 0.0)
    below = (lax.broadcasted_iota(jnp.int32, (tm, tm), 1) < lax.broadcasted_iota(jnp.int32, (tm, tm), 0))
    before = carry_ref[0:1, :] + _dot(jnp.where(below, 1.0, 0.0).astype(BF), onehot.astype(BF))
    rank0 = jnp.sum(jnp.where(sel0, before, 0.0), axis=1, keepdims=True)
    rank1 = jnp.sum(jnp.where(sel1, before, 0.0), axis=1, keepdims=True)
    total = carry_ref[0:1, :] + jnp.sum(onehot, axis=0, keepdims=True)
    carry_ref[...] = jnp.broadcast_to(total, carry_ref.shape)
    cnt_ref[...] = jnp.broadcast_to(total, cnt_ref.shape)
    out = jnp.zeros((tm, LANES), F32)
    for pos, val in enumerate((i0.astype(F32), i1.astype(F32), g0, g1, rank0, rank1)):
        out = jnp.where(lane == pos, val, out)
    route_ref[...] = out


def moe_router(h, norm_w, router):
    m, d = h.shape
    tm = min(ROUTE_TM, m)
    rpad = jnp.zeros((d, LANES), F32).at[:, :N_EXPERTS].set(router.astype(F32))
    return pl.pallas_call(
        _router_kernel,
        grid=(m // tm,),
        in_specs=[pl.BlockSpec((tm, d), lambda i: (i, 0)), pl.BlockSpec((1, d), lambda i: (0, 0)),
                  pl.BlockSpec((d, LANES), lambda i: (0, 0))],
        out_specs=[pl.BlockSpec((tm, d), lambda i: (i, 0)), pl.BlockSpec((tm, LANES), lambda i: (i, 0)),
                   pl.BlockSpec((8, LANES), lambda i: (0, 0))],
        out_shape=[jax.ShapeDtypeStruct((m, d), F32), jax.ShapeDtypeStruct((m, LANES), F32),
                   jax.ShapeDtypeStruct((8, LANES), F32)],
        scratch_shapes=[pltpu.VMEM((8, LANES), F32)],
        compiler_params=_cparams(("arbitrary",)),
    )(h, norm_w.reshape(1, d).astype(F32), rpad)


def _gather_rows_kernel(idx_ref, src_ref, o_ref, sem):
    tg = o_ref.shape[0]
    base = pl.program_id(0) * tg

    def row_copy(r):
        return pltpu.make_async_copy(src_ref.at[pl.ds(idx_ref[base + r], 1), :], o_ref.at[pl.ds(r, 1), :], sem)

    def start(r, c):
        row_copy(r).start()
        return c

    def wait(r, c):
        row_copy(r).wait()
        return c

    lax.fori_loop(0, tg, start, 0, unroll=8)
    lax.fori_loop(0, tg, wait, 0, unroll=8)


def gather_rows(src, idx, tg=256):
    n = idx.shape[0]
    d = src.shape[1]
    tg = min(tg, n)
    return pl.pallas_call(
        _gather_rows_kernel,
        grid_spec=pltpu.PrefetchScalarGridSpec(
            num_scalar_prefetch=1, grid=(n // tg,),
            in_specs=[pl.BlockSpec(memory_space=pl.ANY)],
            out_specs=pl.BlockSpec((tg, d), lambda i, idx_ref: (i, 0)),
            scratch_shapes=[pltpu.SemaphoreType.DMA(())]),
        out_shape=jax.ShapeDtypeStruct((n, d), src.dtype),
        compiler_params=_cparams(("arbitrary",)),
    )(idx, src)


def _new_expert(te_ref, m):
    return jnp.logical_or(m == 0, te_ref[m] != te_ref[jnp.maximum(m - 1, 0)])


def _moe_up_kernel(te_ref, nt_ref, x_ref, w1_ref, w3_ref, o_ref, w1b_ref, w3b_ref):
    m = pl.program_id(1)
    used = m < nt_ref[0]

    @pl.when(jnp.logical_and(used, _new_expert(te_ref, m)))
    def _():
        w1b_ref[...] = w1_ref[...].astype(BF)
        w3b_ref[...] = w3_ref[...].astype(BF)

    @pl.when(used)
    def _():
        x = x_ref[...].astype(BF)
        h1 = _dot(x, w1b_ref[...])
        h3 = _dot(x, w3b_ref[...])
        o_ref[...] = (h1 * _sigmoid(h1) * h3).astype(o_ref.dtype)

    @pl.when(jnp.logical_not(used))
    def _():
        o_ref[...] = jnp.zeros_like(o_ref)


def _moe_down_kernel(te_ref, nt_ref, a_ref, w2_ref, o_ref, wb_ref):
    m = pl.program_id(1)
    used = m < nt_ref[0]

    @pl.when(jnp.logical_and(used, _new_expert(te_ref, m)))
    def _():
        wb_ref[...] = w2_ref[...].astype(BF)

    @pl.when(used)
    def _():
        o_ref[...] = _dot(a_ref[...], wb_ref[...])

    @pl.when(jnp.logical_not(used))
    def _():
        o_ref[...] = jnp.zeros_like(o_ref)


def moe_experts(xg, tile_e, ntiles, w13, w2, tn_up=1024, tn_down=512):
    cap, d = xg.shape
    f = w2.shape[1]
    tm = MOE_TM
    nt_max = cap // tm
    nf = f // tn_up
    mt = lambda m, nt: jnp.minimum(m, nt[0] - 1)
    act = pl.pallas_call(
        _moe_up_kernel,
        grid_spec=pltpu.PrefetchScalarGridSpec(
            num_scalar_prefetch=2, grid=(nf, nt_max),
            in_specs=[pl.BlockSpec((tm, d), lambda j, m, te, nt: (mt(m, nt), 0)),
                      pl.BlockSpec((None, d, tn_up), lambda j, m, te, nt: (te[mt(m, nt)], 0, j)),
                      pl.BlockSpec((None, d, tn_up), lambda j, m, te, nt: (te[mt(m, nt)], 0, j + nf))],
            out_specs=pl.BlockSpec((tm, tn_up), lambda j, m, te, nt: (m, j)),
            scratch_shapes=[pltpu.VMEM((d, tn_up), BF), pltpu.VMEM((d, tn_up), BF)]),
        out_shape=jax.ShapeDtypeStruct((cap, f), BF),
        compiler_params=_cparams(("arbitrary", "arbitrary")),
    )(tile_e, ntiles, xg, w13, w13)
    return pl.pallas_call(
        _moe_down_kernel,
        grid_spec=pltpu.PrefetchScalarGridSpec(
            num_scalar_prefetch=2, grid=(d // tn_down, nt_max),
            in_specs=[pl.BlockSpec((tm, f), lambda j, m, te, nt: (mt(m, nt), 0)),
                      pl.BlockSpec((None, f, tn_down), lambda j, m, te, nt: (te[mt(m, nt)], 0, j))],
            out_specs=pl.BlockSpec((tm, tn_down), lambda j, m, te, nt: (m, j)),
            scratch_shapes=[pltpu.VMEM((f, tn_down), BF)]),
        out_shape=jax.ShapeDtypeStruct((cap, d), F32),
        compiler_params=_cparams(("arbitrary", "arbitrary"), VMEM_LIMIT_BIG),
    )(tile_e, ntiles, act, w2)


def _combine_kernel(pos_ref, h_ref, route_ref, y_ref, *rest, final_norm):
    if final_norm:
        nw_ref, o_ref, buf, sem = rest
    else:
        o_ref, buf, sem = rest
    tc = h_ref.shape[0]
    base = pl.program_id(0) * tc

    def row_copy(r, k):
        return pltpu.make_async_copy(y_ref.at[pl.ds(pos_ref[TOP_K * (base + r) + k], 1), :],
                                     buf.at[k, pl.ds(r, 1), :], sem)

    def start(r, c):
        for k in range(TOP_K):
            row_copy(r, k).start()
        return c

    def wait(r, c):
        for k in range(TOP_K):
            row_copy(r, k).wait()
        return c

    lax.fori_loop(0, tc, start, 0, unroll=8)
    lax.fori_loop(0, tc, wait, 0, unroll=8)
    acc = h_ref[...]
    for k in range(TOP_K):
        acc = acc + route_ref[:, TOP_K + k:TOP_K + k + 1] * buf[k]
    if final_norm:
        acc = _rms(acc, nw_ref[...])
    o_ref[...] = acc


def moe_combine(h, route, yg, pos, final_w=None, tc=256):
    n, d = h.shape
    tc = min(tc, n)
    in_specs = [pl.BlockSpec((tc, d), lambda i, pos_ref: (i, 0)), pl.BlockSpec((tc, LANES), lambda i, pos_ref: (i, 0)),
                pl.BlockSpec(memory_space=pl.ANY)]
    args = [pos, h, route, yg]
    if final_w is not None:
        in_specs.append(pl.BlockSpec((1, d), lambda i, pos_ref: (0, 0)))
        args.append(final_w.reshape(1, d).astype(F32))
    return pl.pallas_call(
        functools.partial(_combine_kernel, final_norm=final_w is not None),
        grid_spec=pltpu.PrefetchScalarGridSpec(
            num_scalar_prefetch=1, grid=(n // tc,),
            in_specs=in_specs,
            out_specs=pl.BlockSpec((tc, d), lambda i, pos_ref: (i, 0)),
            scratch_shapes=[pltpu.VMEM((TOP_K, tc, d), F32), pltpu.SemaphoreType.DMA(())]),
        out_shape=jax.ShapeDtypeStruct((n, d), F32),
        compiler_params=_cparams(("arbitrary",)),
    )(*args)


def moe_ffn(h, norm_w, router, w13, w2, final_w=None):
    n, d = h.shape
    tm = MOE_TM
    hn, route, cnt = moe_router(h, norm_w, router)
    expert = route[:, 0:TOP_K].astype(jnp.int32)
    rank = route[:, 2 * TOP_K:3 * TOP_K].astype(jnp.int32)
    counts = cnt[0, :N_EXPERTS].astype(jnp.int32)
    tiles = (counts + tm - 1) // tm
    tile_end = jnp.cumsum(tiles)
    dest = ((tile_end - tiles) * tm)[expert] + rank
    nt_max = -(-n * TOP_K // tm) + N_EXPERTS
    cap = nt_max * tm
    flat = dest.reshape(-1)
    row_tok = jnp.zeros((cap,), jnp.int32).at[flat].set(jnp.arange(n * TOP_K, dtype=jnp.int32) // TOP_K)
    tile_e = jnp.minimum(jnp.searchsorted(tile_end, jnp.arange(nt_max, dtype=jnp.int32), side="right"),
                         N_EXPERTS - 1).astype(jnp.int32)
    ntiles = tile_end[-1:].astype(jnp.int32)
    xg = gather_rows(hn, row_tok)
    yg = moe_experts(xg, tile_e, ntiles, w13, w2)
    return moe_combine(h, route, yg, flat, final_w)


def _final_kernel(x_ref, w_ref, o_ref):
    o_ref[...] = _rms(x_ref[...], w_ref[...])


def final_norm(h, w, tm=512):
    m, d = h.shape
    tm = min(tm, m)
    return pl.pallas_call(
        _final_kernel,
        grid=(m // tm,),
        in_specs=[pl.BlockSpec((tm, d), lambda i: (i, 0)), pl.BlockSpec((1, d), lambda i: (0, 0))],
        out_specs=pl.BlockSpec((tm, d), lambda i: (i, 0)),
        out_shape=jax.ShapeDtypeStruct((m, d), F32),
        compiler_params=_cparams(("parallel",)),
    )(h, w.reshape(1, d).astype(F32))


def _rot_cols(w):
    half = w.shape[-1] // 2
    return jnp.concatenate([-w[..., half:], w[..., :half]], axis=-1)


_W_IN_SEGMENTS = (
    (U_ML_Q, O_ML_Q, 512, 1.0), (U_ML_K, O_ML_K, 512, 1.0), (U_ML_V, O_ML_V, 1024, 1.0),
    (U_ML_O, O_ML_O, 1024, 1.0), (U_SSM_Z, O_SSM_Z, 1024, 1.0), (U_SSM_X, O_SSM_XBC, SSM_DINNER, 1.0),
    (U_SWA_Q, O_SWA_Q, 1024, SWA_HEAD_DIM ** -0.5 * LOG2E), (U_SSM_BC, O_SSM_XBC + SSM_DINNER, SSM_BC, 1.0),
    (U_MLA_CQ, O_MLA_CQ, MLA_Q_LORA, 1.0), (U_MLA_CKV, O_MLA_CKV, MLA_KV_LORA, 1.0),
    (U_SWA_K, O_SWA_K, 256, 1.0), (U_SWA_V, O_SWA_V, 256, 1.0), (U_GATE, O_GATE, N_BRANCH * D_MODEL, 1.0))


W_IN_TILE = 256
_ROPE_TILE = U_MLA_KR // W_IN_TILE
_GATE_TILE = U_TOTAL // W_IN_TILE


def _w_in_tile_sources():
    src = [0] * (U_TOTAL // W_IN_TILE)
    for dst, s, width, _ in _W_IN_SEGMENTS:
        for off in range(0, width, W_IN_TILE):
            src[(dst + off) // W_IN_TILE] = s + off
    src[_ROPE_TILE] = O_MLA_KR
    return src


def _w_in_layout_kernel(src_ref, w_ref, dt_ref, o_ref):
    t = pl.program_id(1)

    @pl.when(t == _GATE_TILE)
    def _():
        ngate = 2 * ML_HEADS
        o_ref[...] = jnp.concatenate(
            [w_ref[0, 0:ngate, :], dt_ref[0],
             jnp.zeros((W_IN_TILE - ngate - SSM_HEADS, o_ref.shape[1]), F32)], axis=0).astype(BF)

    @pl.when(jnp.logical_and(t != _ROPE_TILE, t != _GATE_TILE))
    def _():
        swa_q = jnp.logical_and(t >= U_SWA_Q // W_IN_TILE, t < U_SSM_BC // W_IN_TILE)
        scale = jnp.where(swa_q, SWA_HEAD_DIM ** -0.5 * LOG2E, 1.0)
        o_ref[...] = (w_ref[0] * scale).astype(BF)

    @pl.when(t == _ROPE_TILE)
    def _():
        half = MLA_ROPE // 2
        o_ref[0:MLA_ROPE, :] = w_ref[0, 0:MLA_ROPE, :].astype(BF)
        o_ref[MLA_ROPE:MLA_ROPE + half, :] = (-w_ref[0, half:MLA_ROPE, :]).astype(BF)
        o_ref[MLA_ROPE + half:2 * MLA_ROPE, :] = w_ref[0, 0:half, :].astype(BF)
        o_ref[2 * MLA_ROPE:, :] = jnp.zeros((W_IN_TILE - 2 * MLA_ROPE, o_ref.shape[1]), BF)


def layout_w_in(w_in):
    depth, d, _ = w_in.shape
    wt = jnp.swapaxes(w_in, 1, 2)
    src = jnp.asarray([s // SUBLANES for s in _w_in_tile_sources() + [O_ML_I]], jnp.int32)
    return pl.pallas_call(
        _w_in_layout_kernel,
        grid_spec=pltpu.PrefetchScalarGridSpec(
            num_scalar_prefetch=1, grid=(depth, _GATE_TILE + 1),
            in_specs=[pl.BlockSpec((pl.Element(1), pl.Element(W_IN_TILE), pl.Element(d)),
                                   lambda l, t, src_ref: (l, src_ref[t] * SUBLANES, 0)),
                      pl.BlockSpec((pl.Element(1), pl.Element(SSM_HEADS), pl.Element(d)),
                                   lambda l, t, src_ref: (l, O_SSM_DT, 0))],
            out_specs=pl.BlockSpec((None, W_IN_TILE, d), lambda l, t, src_ref: (l, t, 0))),
        out_shape=jax.ShapeDtypeStruct((depth, U_TOTAL + W_IN_TILE, d), BF),
        compiler_params=_cparams(("parallel", "parallel")),
    )(src, wt, wt)


def _layout_w_uq(w):
    k = w.shape[0]
    w = w.reshape(k, MLA_HEADS, MLA_NOPE + MLA_ROPE)
    rope = w[..., MLA_NOPE:]
    scale = (MLA_NOPE + MLA_ROPE) ** -0.5 * LOG2E
    return (jnp.concatenate([w, _rot_cols(rope)], axis=-1).reshape(k, -1) * scale).astype(BF)


def _rope_table(seq):
    inv_freq = 1.0 / (ROPE_THETA ** (jnp.arange(0, MLA_ROPE, 2, dtype=F32) / MLA_ROPE))
    ang = jnp.arange(seq, dtype=F32)[:, None] * inv_freq[None, :]
    c, s = jnp.cos(ang), jnp.sin(ang)
    return jnp.concatenate([c, c, s, s], axis=1)


def hybrid_mixer(h, bsz, seq, layer, norm_w, w_main, ml_ib, ml_fb, ml_norm, conv_w, conv_b, dt_bias, a_log,
                 ssm_d, ssm_norm, q_norm, w_uq, kv_norm, w_ukv, sinks, w_branch, w_out, cs):
    n = bsz * seq
    nc = seq // CHUNK
    u2, small = matmul(h, w_main, norm=norm_w, out_dtype=BF, w_layer=layer, w_t=True, n=U_TOTAL,
                       side_w=w_main, side_rows=(U_TOTAL // LANES, LANES))
    u3 = u2.reshape(bsz, seq, U_TOTAL)

    def to_rows(cols, heads):
        return cols.reshape(bsz, seq, heads).transpose(0, 2, 1).reshape(bsz * heads * nc, CHUNK)

    def per_row(vec, heads):
        return jnp.broadcast_to(vec.astype(F32)[None, :, None], (bsz, heads, nc)).reshape(-1, 1)

    ig, bcum, dt, acs = recurrence_gates(
        to_rows(small[:, 0:ML_HEADS], ML_HEADS), to_rows(small[:, ML_HEADS:2 * ML_HEADS], ML_HEADS),
        per_row(ml_ib, ML_HEADS), per_row(ml_fb, ML_HEADS),
        to_rows(small[:, 2 * ML_HEADS:2 * ML_HEADS + SSM_HEADS], SSM_HEADS),
        per_row(dt_bias, SSM_HEADS), per_row(a_log, SSM_HEADS))

    def as_rows(x, heads):
        return x.reshape(bsz, heads, nc, CHUNK).transpose(0, 2, 1, 3)

    def as_cols(x, heads):
        return x.reshape(bsz, heads, seq).transpose(0, 2, 1)

    ya = mlstm_branch(u3, as_rows(ig, ML_HEADS), as_rows(bcum, ML_HEADS), as_cols(bcum, ML_HEADS), ml_norm)
    yb = ssd_branch(u3, as_cols(dt, SSM_HEADS), as_cols(acs, SSM_HEADS), as_rows(acs, SSM_HEADS), conv_w, conv_b,
                    jnp.repeat(ssm_d, SSM_HEADDIM), ssm_norm)
    qf = matmul(u2, w_uq, norm=q_norm, x_col_blk=U_MLA_CQ // MLA_Q_LORA)
    kvf = matmul(u2, w_ukv, norm=kv_norm, x_col_blk=U_MLA_CKV // MLA_KV_LORA)
    yc = mla_attention(qf.reshape(bsz, seq, -1), kvf.reshape(bsz, seq, -1), u3, cs)
    yd = swa_branch(u3, sinks)
    merged = gated_merge(ya.reshape(n, -1), yb.reshape(n, -1), yc.reshape(n, -1), yd.reshape(n, -1), w_branch, u2)
    return matmul(merged, w_out, residual=h, out_dtype=F32)


def kernel(x, mem, norm_mix, w_in, ml_igate_bias, ml_fgate_bias, ml_norm, ssm_conv_w, ssm_conv_b, ssm_dt_bias, ssm_a_log, ssm_d, ssm_norm, mla_q_norm, mla_w_uq, mla_kv_norm, mla_w_ukv, swa_sinks, w_branch, w_out, norm_cross, norm_mem, xa_wq, xa_wkv, xa_wo, norm_ffn, ffn_w13, ffn_w2, moe_router, moe_w13, moe_w2, norm_final):
    bsz, seq, d = x.shape
    depth = w_in.shape[0]
    n = bsz * seq
    mlen = mem.shape[1]
    cs = _rope_table(seq)
    h = x.reshape(n, d)
    mem2 = mem.reshape(bsz * mlen, d)
    w_main = layout_w_in(w_in)
    for l in range(depth):
        h = hybrid_mixer(h, bsz, seq, l, norm_mix[l], w_main, ml_igate_bias[l], ml_fgate_bias[l], ml_norm[l],
                         ssm_conv_w[l], ssm_conv_b[l], ssm_dt_bias[l], ssm_a_log[l], ssm_d[l], ssm_norm[l],
                         mla_q_norm[l], _layout_w_uq(mla_w_uq[l]), mla_kv_norm[l], mla_w_ukv[l].astype(BF),
                         swa_sinks[l], w_branch[l].astype(BF), w_out[l].astype(BF), cs)
        kv = matmul(mem2, xa_wkv[l].astype(BF), norm=norm_mem[l])
        wq = (xa_wq[l] * (XA_HEAD_DIM ** -0.5 * LOG2E)).astype(BF)
        h = cross_attention(h, seq, norm_cross[l], wq, kv.reshape(bsz, mlen, -1), xa_wo[l].astype(BF))
        if l % 2 == 0:
            h = dense_ffn(h, norm_ffn[l], ffn_w13[l // 2].astype(BF), ffn_w2[l // 2].astype(BF))
        else:
            h = moe_ffn(h, norm_ffn[l], moe_router[l // 2], moe_w13[l // 2], moe_w2[l // 2],
                        final_w=norm_final if l == depth - 1 else None)
    if depth % 2 == 1:
        h = final_norm(h, norm_final)
    return h.reshape(bsz, seq, d)
```

```python
import functools
import math

import jax
import jax.numpy as jnp
from jax import lax
from jax.experimental import pallas as pl
from jax.experimental.pallas import tpu as pltpu

F32 = jnp.float32
BF = jnp.bfloat16

D_MODEL = 2048
RMS_EPS = 1e-6
ML_HEADS, ML_DQK, ML_DV = 4, 128, 256
SSM_HEADS, SSM_HEADDIM, SSM_GROUPS, SSM_STATE, SSM_CONV = 16, 64, 2, 128, 4
SSM_DINNER = SSM_HEADS * SSM_HEADDIM
SSM_BC = 2 * SSM_GROUPS * SSM_STATE
MLA_HEADS, MLA_Q_LORA, MLA_KV_LORA, MLA_NOPE, MLA_ROPE, MLA_V = 8, 512, 256, 128, 64, 128
ROPE_THETA = 10000.0
SWA_HEADS, SWA_KV_HEADS, SWA_HEAD_DIM, SWA_WINDOW = 16, 4, 64, 128
N_BRANCH, BRANCH_W = 4, 1024
XA_HEADS, XA_HEAD_DIM = 4, 128
FFN_DIM = 7168
N_EXPERTS, TOP_K = 8, 2

CHUNK = 128
LANES = 128
SUBLANES = 8
VMEM_LIMIT = 56 * 1024 * 1024
VMEM_LIMIT_BIG = 60 * 1024 * 1024
LOG2E = math.log2(math.e)

U_ML_Q, U_ML_K, U_ML_V, U_ML_O = 0, 512, 1024, 2048
U_SSM_Z, U_SSM_X, U_SWA_Q, U_SSM_BC = 3072, 4096, 5120, 6144
U_MLA_CQ, U_MLA_CKV, U_SWA_K, U_SWA_V, U_MLA_KR = 6656, 7168, 7424, 7680, 7936
U_GATE = 8192
U_TOTAL = U_GATE + N_BRANCH * D_MODEL
_SPLITS = (512, 512, 1024, 1024, 4, 4, 1024, 1536, 16, 512, 256, 64, 1024, 256, 256, 8192)
_OFF = [0]
for _s in _SPLITS:
    _OFF.append(_OFF[-1] + _s)
(O_ML_Q, O_ML_K, O_ML_V, O_ML_O, O_ML_I, O_ML_F, O_SSM_Z, O_SSM_XBC, O_SSM_DT, O_MLA_CQ, O_MLA_CKV,
 O_MLA_KR, O_SWA_Q, O_SWA_K, O_SWA_V, O_GATE, _O_END) = _OFF


def _cparams(sem, vmem_limit=VMEM_LIMIT):
    return pltpu.CompilerParams(dimension_semantics=sem, vmem_limit_bytes=vmem_limit)


def _dot(a, b):
    return jnp.dot(a, b, preferred_element_type=F32)


def _dot_nt(a, b):
    return lax.dot_general(a, b, (((1,), (1,)), ((), ())), preferred_element_type=F32)


def _dot_tn(a, b):
    return lax.dot_general(a, b, (((0,), (0,)), ((), ())), preferred_element_type=F32)


def _split3(a):
    a1 = a.astype(BF)
    r = a - a1.astype(F32)
    a2 = r.astype(BF)
    a3 = (r - a2.astype(F32)).astype(BF)
    return a1, a2, a3


def _dot_sel(a, sel):
    a1, a2, a3 = _split3(a)
    return _dot(a1, sel) + _dot(a2, sel) + _dot(a3, sel)


def _rms(x, w):
    return x * lax.rsqrt(jnp.mean(x * x, axis=-1, keepdims=True) + RMS_EPS) * w


def _sigmoid(x):
    return 1.0 / (1.0 + jnp.exp(-x))


def _mm_kernel(*refs, has_norm, has_res, has_side, w_t):
    mm = _dot_nt if w_t else _dot
    it = iter(refs)
    x_ref = next(it)
    g_ref = next(it) if has_norm else None
    w_ref = next(it)
    r_ref = next(it) if has_res else None
    ws_ref = next(it) if has_side else None
    o_ref = next(it)
    os_ref = next(it) if has_side else None
    if has_norm:
        xn_ref = next(it)

        @pl.when(pl.program_id(1) == 0)
        def _():
            xn = _rms(x_ref[...].astype(F32), g_ref[...]).astype(BF)
            xn_ref[...] = xn
            if has_side:
                os_ref[...] = mm(xn, ws_ref[...])

        xv = xn_ref[...]
    else:
        xv = x_ref[...]
    acc = mm(xv, w_ref[...])
    if has_res:
        acc = acc + r_ref[...]
    o_ref[...] = acc.astype(o_ref.dtype)


def matmul(x, w, *, norm=None, residual=None, out_dtype=None, tm=1024, tn=1024, x_col_blk=0, w_layer=None,
           side_w=None, side_rows=None, w_t=False, n=None):
    out_dtype = out_dtype or BF
    m = x.shape[0]
    k, n_all = w.shape[-2:][::-1] if w_t else w.shape[-2:]
    n = n or n_all
    tm, tn = min(tm, m), min(tn, n)
    assert m % tm == 0 and n % tn == 0
    in_specs = [pl.BlockSpec((tm, k), lambda i, j: (i, x_col_blk))]
    args = [x]
    scratch = []
    if norm is not None:
        in_specs.append(pl.BlockSpec((1, k), lambda i, j: (0, 0)))
        args.append(norm.reshape(1, k).astype(F32))
        scratch.append(pltpu.VMEM((tm, k), BF))
    if w_layer is None:
        in_specs.append(pl.BlockSpec((k, tn), lambda i, j: (0, j)))
    else:
        in_specs.append(pl.BlockSpec((None, tn, k), lambda i, j: (w_layer, j, 0)) if w_t else
                        pl.BlockSpec((None, k, tn), lambda i, j: (w_layer, 0, j)))
    args.append(w)
    if residual is not None:
        in_specs.append(pl.BlockSpec((tm, tn), lambda i, j: (i, j)))
        args.append(residual)
    out_specs = pl.BlockSpec((tm, tn), lambda i, j: (i, j))
    out_shape = jax.ShapeDtypeStruct((m, n), out_dtype)
    if side_w is not None:
        assert norm is not None and w_layer is not None
        if side_rows is None:
            ns = side_w.shape[-2] if w_t else side_w.shape[-1]
            in_specs.append(pl.BlockSpec((None,) + side_w.shape[1:], lambda i, j: (w_layer, 0, 0)))
        else:
            blk, ns = side_rows
            in_specs.append(pl.BlockSpec((None, ns, k), lambda i, j: (w_layer, blk, 0)))
        args.append(side_w)
        out_specs = [out_specs, pl.BlockSpec((tm, ns), lambda i, j: (i, 0))]
        out_shape = [out_shape, jax.ShapeDtypeStruct((m, ns), F32)]
    return pl.pallas_call(
        functools.partial(_mm_kernel, has_norm=norm is not None, has_res=residual is not None,
                          has_side=side_w is not None, w_t=w_t),
        grid=(m // tm, n // tn),
        in_specs=in_specs,
        out_specs=out_specs,
        out_shape=out_shape,
        scratch_shapes=scratch,
        compiler_params=_cparams(("parallel", "arbitrary")),
    )(*args)


def _cumsum_lanes(x):
    lane = lax.broadcasted_iota(jnp.int32, x.shape, 1)
    s = 1
    while s < x.shape[1]:
        x = x + jnp.where(lane >= s, pltpu.roll(x, s, axis=1), 0.0)
        s *= 2
    return x


def _softplus(x):
    return jnp.maximum(x, 0.0) + jnp.log(1.0 + jnp.exp(-jnp.abs(x)))


def _gates_kernel(i_ref, f_ref, ib_ref, fb_ref, dt_ref, dtb_ref, alog_ref, ig_ref, b_ref, dto_ref, acs_ref):
    ig_ref[...] = i_ref[...] + ib_ref[...]
    b_ref[...] = _cumsum_lanes(-_softplus(-(f_ref[...] + fb_ref[...])))
    dt = _softplus(dt_ref[...] + dtb_ref[...])
    dto_ref[...] = dt
    acs_ref[...] = _cumsum_lanes(dt * (-jnp.exp(alog_ref[...])))


def recurrence_gates(i_rows, f_rows, ib, fb, dt_rows, dtb, alog):
    r1, r2 = i_rows.shape[0], dt_rows.shape[0]
    shp = lambda r: jax.ShapeDtypeStruct((r, CHUNK), F32)
    return pl.pallas_call(
        _gates_kernel,
        out_shape=(shp(r1), shp(r1), shp(r2), shp(r2)),
    )(i_rows, f_rows, ib, fb, dt_rows, dtb, alog)


def _mlstm_kernel(q_ref, k_ref, v_ref, o_ref, igr_ref, br_ref, bc_ref, nw_ref, y_ref, *state):
    ct_ref, n_ref, m_ref = state[0::3], state[1::3], state[2::3]
    L = CHUNK

    @pl.when(pl.program_id(1) == 0)
    def _():
        for ref in state:
            ref[...] = jnp.zeros_like(ref)

    row = lax.broadcasted_iota(jnp.int32, (L, L), 0)
    col = lax.broadcasted_iota(jnp.int32, (L, L), 1)
    causal = col <= row
    diag = col == row
    scale = ML_DQK ** -0.5
    outs = []
    for h in range(ML_HEADS):
        q = q_ref[:, h * ML_DQK:(h + 1) * ML_DQK]
        k = (k_ref[:, h * ML_DQK:(h + 1) * ML_DQK].astype(F32) * scale).astype(BF)
        v = v_ref[:, h * ML_DV:(h + 1) * ML_DV]
        bcol = bc_ref[:, h:h + 1]
        brow = br_ref[h:h + 1, :]
        igrow = igr_ref[h:h + 1, :]
        m_prev = m_ref[h][0:1, 0:1]
        n_prev = n_ref[h][0:1, :]
        ct_prev = ct_ref[h][...]

        dmat = jnp.where(causal, bcol - brow + igrow, -jnp.inf)
        m_intra = jnp.max(dmat, axis=1, keepdims=True)
        g = bcol + m_prev
        m_s = jnp.maximum(g, m_intra)
        p = jnp.exp(dmat - m_s) * _dot_nt(q, k)
        inter = jnp.exp(g - m_s)
        num = _dot(p.astype(BF), v) + inter * _dot(q, ct_prev.astype(BF))
        den = jnp.sum(p, axis=1, keepdims=True) + inter * jnp.sum(q.astype(F32) * n_prev, axis=1, keepdims=True)
        hh = num / jnp.maximum(jnp.abs(den), jnp.exp(-m_s))
        hn = _rms(hh, nw_ref[:, h * ML_DV:(h + 1) * ML_DV])
        outs.append(_sigmoid(o_ref[:, h * ML_DV:(h + 1) * ML_DV].astype(F32)) * hn)

        b_tot = brow[:, L - 1:L]
        a = b_tot - brow + igrow
        m_loc = jnp.max(a, axis=1, keepdims=True)
        wl = jnp.exp(a - m_loc)
        kw = _dot(jnp.where(diag, wl, 0.0).astype(BF), k)
        m_new = jnp.maximum(b_tot + m_prev, m_loc)
        da = jnp.exp(b_tot + m_prev - m_new)
        db = jnp.exp(m_loc - m_new)
        ct_ref[h][...] = da * ct_prev + db * _dot_tn(kw.astype(BF), v)
        n_ref[h][...] = jnp.broadcast_to(da * n_prev + db * jnp.sum(kw, axis=0, keepdims=True), n_ref[h].shape)
        m_ref[h][...] = jnp.broadcast_to(m_new, m_ref[h].shape)
    y_ref[...] = jnp.concatenate(outs, axis=1).astype(y_ref.dtype)


def mlstm_branch(u3, ig_rows, b_rows, b_cols, norm_w):
    bsz, seq, _ = u3.shape
    nc = seq // CHUNK
    L = CHUNK
    ublk = lambda width, off: pl.BlockSpec((None, L, width), lambda b, c: (b, c, off // width))
    rows = pl.BlockSpec((None, None, ML_HEADS, L), lambda b, c: (b, c, 0, 0))
    return pl.pallas_call(
        _mlstm_kernel,
        grid=(bsz, nc),
        in_specs=[ublk(512, U_ML_Q), ublk(512, U_ML_K), ublk(1024, U_ML_V), ublk(1024, U_ML_O), rows, rows,
                  pl.BlockSpec((None, L, ML_HEADS), lambda b, c: (b, c, 0)),
                  pl.BlockSpec((1, ML_HEADS * ML_DV), lambda b, c: (0, 0))],
        out_specs=pl.BlockSpec((None, L, BRANCH_W), lambda b, c: (b, c, 0)),
        out_shape=jax.ShapeDtypeStruct((bsz, seq, BRANCH_W), BF),
        scratch_shapes=[pltpu.VMEM((ML_DQK, ML_DV), F32), pltpu.VMEM((8, ML_DQK), F32),
                        pltpu.VMEM((8, LANES), F32)] * ML_HEADS,
        compiler_params=_cparams(("parallel", "arbitrary")),
    )(u3, u3, u3, u3, ig_rows, b_rows, b_cols, norm_w.reshape(1, -1).astype(F32))


def _ssd_kernel(z_ref, x_ref, bc_ref, dt_ref, ac_ref, ar_ref, cw_ref, cb_ref, d_ref, nw_ref, y_ref, xs_ref, st_ref):
    L = CHUNK
    P, R, NS = SSM_HEADDIM, SSM_HEADS // SSM_GROUPS, SSM_STATE
    GW = R * P

    @pl.when(pl.program_id(1) == 0)
    def _():
        xs_ref[0:8, :] = jnp.zeros((8, xs_ref.shape[1]), F32)
        st_ref[...] = jnp.zeros_like(st_ref)

    xs_ref[8:, :] = jnp.concatenate([x_ref[...], bc_ref[...]], axis=1).astype(F32)
    conv = cb_ref[...] + cw_ref[SSM_CONV - 1:SSM_CONV, :] * xs_ref[8:8 + L, :]
    for sft in range(1, SSM_CONV):
        conv = conv + cw_ref[SSM_CONV - 1 - sft:SSM_CONV - sft, :] * xs_ref[8 - sft:8 - sft + L, :]
    xs_ref[0:8, :] = xs_ref[L:L + 8, :]
    xbc = conv * _sigmoid(conv)
    xh = xbc[:, :SSM_DINNER]
    bmat = xbc[:, SSM_DINNER:SSM_DINNER + SSM_GROUPS * NS].astype(BF)
    cmat = xbc[:, SSM_DINNER + SSM_GROUPS * NS:].astype(BF)

    dtc = dt_ref[...]
    ac = ac_ref[...]
    ar = ar_ref[...]
    a_last = ac[L - 1:L, :]
    hsel = (lax.broadcasted_iota(jnp.int32, (SSM_HEADS, SSM_DINNER), 1) // P
            == lax.broadcasted_iota(jnp.int32, (SSM_HEADS, SSM_DINNER), 0))
    expand = jnp.where(hsel, 1.0, 0.0).astype(BF)
    stack = jnp.concatenate([dtc, jnp.exp(a_last - ac), jnp.exp(ac),
                             jnp.broadcast_to(jnp.exp(a_last), (8, SSM_HEADS))], axis=0)
    ex = _dot_sel(stack, expand)
    dt_full, dst_full, ind_full = ex[0:L], ex[L:2 * L], ex[2 * L:3 * L]
    cdec_full = ex[3 * L:3 * L + 1]
    xdt = xh * dt_full
    xdt_b = xdt.astype(BF)
    xw_b = (xdt * dst_full).astype(BF)

    row = lax.broadcasted_iota(jnp.int32, (L, L), 0)
    col = lax.broadcasted_iota(jnp.int32, (L, L), 1)
    causal = col <= row
    ys = []
    for g in range(SSM_GROUPS):
        bg = bmat[:, g * NS:(g + 1) * NS]
        cg = cmat[:, g * NS:(g + 1) * NS]
        cb = _dot_nt(cg, bg)
        st_prev = st_ref[g]
        yoff = _dot(cg, st_prev.astype(BF))
        st_ref[g] = cdec_full[:, g * GW:(g + 1) * GW] * st_prev + _dot_tn(bg, xw_b[:, g * GW:(g + 1) * GW])
        for r in range(R):
            h = g * R + r
            dec = jnp.exp(jnp.where(causal, ac[:, h:h + 1] - ar[h:h + 1, :], -jnp.inf))
            yd = _dot((dec * cb).astype(BF), xdt_b[:, h * P:(h + 1) * P])
            ys.append(yd + yoff[:, r * P:(r + 1) * P] * ind_full[:, h * P:(h + 1) * P])
    y = jnp.concatenate(ys, axis=1) + xh * d_ref[...]
    zf = z_ref[...].astype(F32)
    y = y * (zf * _sigmoid(zf))
    y_ref[...] = jnp.concatenate(
        [_rms(y[:, g * GW:(g + 1) * GW], nw_ref[:, g * GW:(g + 1) * GW]) for g in range(SSM_GROUPS)],
        axis=1).astype(y_ref.dtype)


def ssd_branch(u3, dt_cols, acs_cols, acs_rows, conv_w, conv_b, d_full, norm_w):
    bsz, seq, _ = u3.shape
    nc = seq // CHUNK
    L = CHUNK
    ublk = lambda width, off: pl.BlockSpec((None, L, width), lambda b, c: (b, c, off // width))
    cols = pl.BlockSpec((None, L, SSM_HEADS), lambda b, c: (b, c, 0))
    const = lambda shape: pl.BlockSpec(shape, lambda b, c: (0, 0))
    cch = SSM_DINNER + SSM_BC
    return pl.pallas_call(
        _ssd_kernel,
        grid=(bsz, nc),
        in_specs=[ublk(1024, U_SSM_Z), ublk(1024, U_SSM_X), ublk(512, U_SSM_BC), cols, cols,
                  pl.BlockSpec((None, None, SSM_HEADS, L), lambda b, c: (b, c, 0, 0)),
                  const((SSM_CONV, cch)), const((1, cch)), const((1, SSM_DINNER)), const((1, SSM_DINNER))],
        out_specs=pl.BlockSpec((None, L, BRANCH_W), lambda b, c: (b, c, 0)),
        out_shape=jax.ShapeDtypeStruct((bsz, seq, BRANCH_W), BF),
        scratch_shapes=[pltpu.VMEM((L + 8, cch), F32),
                        pltpu.VMEM((SSM_GROUPS, SSM_STATE, SSM_DINNER // SSM_GROUPS), F32)],
        compiler_params=_cparams(("parallel", "arbitrary")),
    )(u3, u3, u3, dt_cols, acs_cols, acs_rows, conv_w.astype(F32), conv_b.reshape(1, cch).astype(F32),
      d_full.reshape(1, -1).astype(F32), norm_w.reshape(1, -1).astype(F32))


def _rope128(x, cs):
    t = x.astype(F32) * cs
    return t + pltpu.roll(t, MLA_ROPE, axis=1)


def _mla_kernel(q_ref, csq_ref, kv_ref, kr_ref, csk_ref, o_ref, qe_s, *stats, tq):
    qi = pl.program_id(1)
    ki = pl.program_id(2)
    qw = MLA_NOPE + LANES
    kw = MLA_NOPE + MLA_V
    m_s, l_s, acc_s = stats[0::3], stats[1::3], stats[2::3]
    nt = tq // LANES

    @pl.when(ki == 0)
    def _():
        for h in range(MLA_HEADS):
            qr = _rope128(q_ref[:, h * qw + MLA_NOPE:(h + 1) * qw], csq_ref[...]).astype(BF)
            qe_s[h] = jnp.concatenate([q_ref[:, h * qw:h * qw + MLA_NOPE], qr], axis=1)
            m_s[h][...] = jnp.full((tq, LANES), -jnp.inf, F32)
            l_s[h][...] = jnp.zeros((tq, LANES), F32)
            acc_s[h][...] = jnp.zeros((tq, MLA_V), F32)

    def step(masked):
        lane = lax.broadcasted_iota(jnp.int32, (tq, LANES), 1)
        kr = jnp.where(lane < MLA_ROPE, _rope128(kr_ref[...], csk_ref[...]), 0.0).astype(BF)
        if masked:
            causal = (lax.broadcasted_iota(jnp.int32, (tq, tq), 1) <= lax.broadcasted_iota(jnp.int32, (tq, tq), 0))
        for h in range(MLA_HEADS):
            ke = jnp.concatenate([kv_ref[:, h * kw:h * kw + MLA_NOPE], kr], axis=1)
            s = _dot_nt(qe_s[h], ke)
            if masked:
                s = jnp.where(causal, s, -jnp.inf)
            m_old = m_s[h][...]
            m_new = jnp.maximum(m_old, jnp.max(s, axis=1, keepdims=True))
            alpha = jnp.exp2(m_old - m_new)
            p = jnp.exp2(s - jnp.concatenate([m_new] * nt, axis=1))
            psum = p[:, 0:LANES]
            for t in range(1, nt):
                psum = psum + p[:, t * LANES:(t + 1) * LANES]
            l_s[h][...] = alpha * l_s[h][...] + psum
            acc_s[h][...] = alpha * acc_s[h][...] + _dot(p.astype(BF), kv_ref[:, h * kw + MLA_NOPE:(h + 1) * kw])
            m_s[h][...] = m_new

    @pl.when(ki < qi)
    def _():
        step(False)

    @pl.when(ki == qi)
    def _():
        step(True)
        for h in range(MLA_HEADS):
            l = jnp.sum(l_s[h][...], axis=1, keepdims=True)
            o_ref[:, h * MLA_V:(h + 1) * MLA_V] = (acc_s[h][...] / l).astype(o_ref.dtype)


def mla_attention(qf3, kvf3, u3, cs, tq=512):
    bsz, seq, _ = qf3.shape
    tq = min(tq, seq)
    nq = seq // tq
    kvi = lambda b, qi, ki: jnp.minimum(ki, qi)
    return pl.pallas_call(
        functools.partial(_mla_kernel, tq=tq),
        grid=(bsz, nq, nq),
        in_specs=[pl.BlockSpec((None, tq, qf3.shape[2]), lambda b, qi, ki: (b, qi, 0)),
                  pl.BlockSpec((tq, LANES), lambda b, qi, ki: (qi, 0)),
                  pl.BlockSpec((None, tq, kvf3.shape[2]), lambda b, qi, ki: (b, kvi(b, qi, ki), 0)),
                  pl.BlockSpec((None, tq, LANES), lambda b, qi, ki: (b, kvi(b, qi, ki), U_MLA_KR // LANES)),
                  pl.BlockSpec((tq, LANES), lambda b, qi, ki: (kvi(b, qi, ki), 0))],
        out_specs=pl.BlockSpec((None, tq, MLA_HEADS * MLA_V), lambda b, qi, ki: (b, qi, 0)),
        out_shape=jax.ShapeDtypeStruct((bsz, seq, MLA_HEADS * MLA_V), BF),
        scratch_shapes=[pltpu.VMEM((MLA_HEADS, tq, MLA_NOPE + LANES), BF)]
        + [pltpu.VMEM((tq, LANES), F32), pltpu.VMEM((tq, LANES), F32), pltpu.VMEM((tq, MLA_V), F32)] * MLA_HEADS,
        compiler_params=_cparams(("parallel", "parallel", "arbitrary")),
    )(qf3, cs, kvf3, u3, cs)


def _swa_kernel(q_ref, kc_ref, kp_ref, vc_ref, vp_ref, sink_ref, o_ref):
    W, d = SWA_WINDOW, SWA_HEAD_DIM
    n = pl.program_id(1)
    i = lax.broadcasted_iota(jnp.int32, (2 * W, 2 * W), 0) & (W - 1)
    j = lax.broadcasted_iota(jnp.int32, (2 * W, 2 * W), 1)
    valid = (j > i) & (j <= i + W) & ((n > 0) | (j >= W))
    first_tile = lax.broadcasted_iota(jnp.int32, (2 * W, LANES), 0) < W
    lower = lax.broadcasted_iota(jnp.int32, (2 * W, LANES), 1) < d
    ones = jnp.ones((2 * W, LANES), BF)
    for pair in range(SWA_KV_HEADS // 2):
        sl = slice(pair * LANES, (pair + 1) * LANES)
        kt = jnp.concatenate([kp_ref[:, sl], kc_ref[:, sl]], axis=0)
        vt = jnp.concatenate([vp_ref[:, sl], vc_ref[:, sl]], axis=0)
        kt_sw = pltpu.roll(kt.astype(F32), d, axis=1).astype(BF)
        vt_sw = pltpu.roll(vt.astype(F32), d, axis=1).astype(BF)
        for e in range(2):
            kh = 2 * pair + e
            k_lo = jnp.where(lower, kt if e == 0 else kt_sw, jnp.zeros_like(kt))
            k_hi = jnp.where(lower, jnp.zeros_like(kt), kt_sw if e == 0 else kt)
            v_lo, v_hi = (vt, vt_sw) if e == 0 else (vt_sw, vt)
            qs = jnp.concatenate([q_ref[:, 2 * kh * LANES:(2 * kh + 1) * LANES],
                                  q_ref[:, (2 * kh + 1) * LANES:(2 * kh + 2) * LANES]], axis=0)
            halves = []
            for half, (ke, ve) in enumerate(((k_lo, v_lo), (k_hi, v_hi))):
                ha, hb = 4 * kh + half, 4 * kh + 2 + half
                s = jnp.where(valid, _dot_nt(qs, ke), -jnp.inf)
                sink = jnp.where(first_tile, sink_ref[0:1, ha:ha + 1], sink_ref[0:1, hb:hb + 1]) * LOG2E
                m = jnp.maximum(jnp.max(s, axis=1, keepdims=True), sink)
                p = jnp.exp2(s - jnp.concatenate([m, m], axis=1)).astype(BF)
                den = _dot(p, ones) + jnp.exp2(sink - m)
                halves.append(_dot(p, ve) / den)
            ot = jnp.where(lower, halves[0], halves[1]).astype(o_ref.dtype)
            o_ref[:, 2 * kh * LANES:(2 * kh + 1) * LANES] = ot[:W]
            o_ref[:, (2 * kh + 1) * LANES:(2 * kh + 2) * LANES] = ot[W:]


def swa_branch(u3, sinks):
    bsz, seq, _ = u3.shape
    W = SWA_WINDOW
    kvw = SWA_KV_HEADS * SWA_HEAD_DIM
    cur = lambda width, off: pl.BlockSpec((None, W, width), lambda b, n: (b, n, off // width))
    prev = lambda width, off: pl.BlockSpec((None, W, width), lambda b, n: (b, jnp.maximum(n - 1, 0), off // width))
    return pl.pallas_call(
        _swa_kernel,
        grid=(bsz, seq // W),
        in_specs=[cur(1024, U_SWA_Q), cur(kvw, U_SWA_K), prev(kvw, U_SWA_K), cur(kvw, U_SWA_V), prev(kvw, U_SWA_V),
                  pl.BlockSpec((1, SWA_HEADS), lambda b, n: (0, 0))],
        out_specs=pl.BlockSpec((None, W, BRANCH_W), lambda b, n: (b, n, 0)),
        out_shape=jax.ShapeDtypeStruct((bsz, seq, BRANCH_W), BF),
        compiler_params=_cparams(("parallel", "parallel")),
    )(u3, u3, u3, u3, u3, sinks.reshape(1, -1).astype(F32))


def _merge_kernel(ya_ref, yb_ref, yc_ref, yd_ref, w_ref, g0_ref, g1_ref, g2_ref, g3_ref, o_ref):
    acc = None
    for n, (y_ref, g_ref) in enumerate(((ya_ref, g0_ref), (yb_ref, g1_ref), (yc_ref, g2_ref), (yd_ref, g3_ref))):
        t = _sigmoid(g_ref[...].astype(F32)) * _dot(y_ref[...], w_ref[n])
        acc = t if acc is None else acc + t
    o_ref[...] = acc.astype(o_ref.dtype)


def gated_merge(ya, yb, yc, yd, w_branch, u2, tm=1024, tn=512):
    m = ya.shape[0]
    tm = min(tm, m)
    ysp = pl.BlockSpec((tm, BRANCH_W), lambda i, j: (i, 0))
    gsp = lambda n: pl.BlockSpec((tm, tn), lambda i, j: (i, (U_GATE + n * D_MODEL) // tn + j))
    return pl.pallas_call(
        _merge_kernel,
        grid=(m // tm, D_MODEL // tn),
        in_specs=[ysp, ysp, ysp, ysp, pl.BlockSpec((N_BRANCH, BRANCH_W, tn), lambda i, j: (0, 0, j)),
                  gsp(0), gsp(1), gsp(2), gsp(3)],
        out_specs=pl.BlockSpec((tm, tn), lambda i, j: (i, j)),
        out_shape=jax.ShapeDtypeStruct((m, D_MODEL), BF),
        compiler_params=_cparams(("parallel", "parallel")),
    )(ya, yb, yc, yd, w_branch, u2, u2, u2, u2)


def _xattn_kernel(h_ref, g_ref, wq_ref, kv_ref, wo_ref, o_ref):
    d = XA_HEAD_DIM
    h = h_ref[...]
    q = _dot(_rms(h, g_ref[...]).astype(BF), wq_ref[...]).astype(BF)
    ones = jnp.ones((kv_ref.shape[0], LANES), BF)
    outs = []
    for hd in range(XA_HEADS):
        s = _dot_nt(q[:, hd * d:(hd + 1) * d], kv_ref[:, hd * d:(hd + 1) * d])
        p = jnp.exp2(s - jnp.max(s, axis=1, keepdims=True)).astype(BF)
        den = _dot(p, ones)
        outs.append((_dot(p, kv_ref[:, (XA_HEADS + hd) * d:(XA_HEADS + hd + 1) * d]) / den).astype(BF))
    o_ref[...] = h + _dot(jnp.concatenate(outs, axis=1), wo_ref[...])


def cross_attention(h, seq, norm_w, wq, kv3, wo, tq=512):
    n, dm = h.shape
    _, mlen, kvw = kv3.shape
    tq = min(tq, seq)
    per_b = seq // tq
    const = lambda shape: pl.BlockSpec(shape, lambda i: (0, 0))
    return pl.pallas_call(
        _xattn_kernel,
        grid=(n // tq,),
        in_specs=[pl.BlockSpec((tq, dm), lambda i: (i, 0)), const((1, dm)), const(wq.shape),
                  pl.BlockSpec((None, mlen, kvw), lambda i: (i // per_b, 0, 0)), const(wo.shape)],
        out_specs=pl.BlockSpec((tq, dm), lambda i: (i, 0)),
        out_shape=jax.ShapeDtypeStruct((n, dm), F32),
        compiler_params=_cparams(("parallel",)),
    )(h, norm_w.reshape(1, dm).astype(F32), wq, kv3, wo)


def _ffn_kernel(h_ref, g_ref, w1_ref, w3_ref, w2_ref, o_ref, xn_ref, acc_ref):
    j = pl.program_id(1)

    @pl.when(j == 0)
    def _():
        hv = h_ref[...]
        xn_ref[...] = _rms(hv, g_ref[...]).astype(BF)
        acc_ref[...] = hv

    xn = xn_ref[...]
    h1 = _dot(xn, w1_ref[...])
    h3 = _dot(xn, w3_ref[...])
    act = (h1 * _sigmoid(h1) * h3).astype(BF)
    acc_ref[...] += _dot(act, w2_ref[...])

    @pl.when(j == pl.num_programs(1) - 1)
    def _():
        o_ref[...] = acc_ref[...]


def dense_ffn(h, norm_w, w13, w2, tm=512, tf=1024):
    m, d = h.shape
    f = w2.shape[0]
    tm = min(tm, m)
    nf = f // tf
    return pl.pallas_call(
        _ffn_kernel,
        grid=(m // tm, nf),
        in_specs=[pl.BlockSpec((tm, d), lambda i, j: (i, 0)), pl.BlockSpec((1, d), lambda i, j: (0, 0)),
                  pl.BlockSpec((d, tf), lambda i, j: (0, j)), pl.BlockSpec((d, tf), lambda i, j: (0, j + nf)),
                  pl.BlockSpec((tf, d), lambda i, j: (j, 0))],
        out_specs=pl.BlockSpec((tm, d), lambda i, j: (i, 0)),
        out_shape=jax.ShapeDtypeStruct((m, d), F32),
        scratch_shapes=[pltpu.VMEM((tm, d), BF), pltpu.VMEM((tm, d), F32)],
        compiler_params=_cparams(("parallel", "arbitrary")),
    )(h, norm_w.reshape(1, d).astype(F32), w13, w13, w2)


ROUTE_TM = 512
MOE_TM = 512


def _router_kernel(h_ref, g_ref, r_ref, hn_ref, route_ref, cnt_ref, carry_ref):
    tm = h_ref.shape[0]

    @pl.when(pl.program_id(0) == 0)
    def _():
        carry_ref[...] = jnp.zeros_like(carry_ref)

    hn = _rms(h_ref[...], g_ref[...])
    hn_ref[...] = hn
    a1, a2, _ = _split3(hn)
    r1, r2, _ = _split3(r_ref[...])
    lane = lax.broadcasted_iota(jnp.int32, (tm, LANES), 1)
    logits = jnp.where(lane < N_EXPERTS, _dot(a1, r1) + (_dot(a1, r2) + _dot(a2, r1)), -jnp.inf)
    v0 = jnp.max(logits, axis=1, keepdims=True)
    i0 = jnp.min(jnp.where(logits == v0, lane, LANES), axis=1, keepdims=True)
    rest = jnp.where(lane == i0, -jnp.inf, logits)
    v1 = jnp.max(rest, axis=1, keepdims=True)
    i1 = jnp.min(jnp.where(rest == v1, lane, LANES), axis=1, keepdims=True)
    ex = jnp.exp(v1 - v0)
    g0 = 1.0 / (1.0 + ex)
    g1 = ex / (1.0 + ex)
    sel0, sel1 = lane == i0, lane == i1
    onehot = jnp.where(sel0 | sel1, 1.0, 0.0)
    below = (lax.broadcasted_iota(jnp.int32, (tm, tm), 1) < lax.broadcasted_iota(jnp.int32, (tm, tm), 0))
    before = carry_ref[0:1, :] + _dot(jnp.where(below, 1.0, 0.0).astype(BF), onehot.astype(BF))
    rank0 = jnp.sum(jnp.where(sel0, before, 0.0), axis=1, keepdims=True)
    rank1 = jnp.sum(jnp.where(sel1, before, 0.0), axis=1, keepdims=True)
    total = carry_ref[0:1, :] + jnp.sum(onehot, axis=0, keepdims=True)
    carry_ref[...] = jnp.broadcast_to(total, carry_ref.shape)
    cnt_ref[...] = jnp.broadcast_to(total, cnt_ref.shape)
    out = jnp.zeros((tm, LANES), F32)
    for pos, val in enumerate((i0.astype(F32), i1.astype(F32), g0, g1, rank0, rank1)):
        out = jnp.where(lane == pos, val, out)
    route_ref[...] = out


def moe_router(h, norm_w, router):
    m, d = h.shape
    tm = min(ROUTE_TM, m)
    rpad = jnp.zeros((d, LANES), F32).at[:, :N_EXPERTS].set(router.astype(F32))
    return pl.pallas_call(
        _router_kernel,
        grid=(m // tm,),
        in_specs=[pl.BlockSpec((tm, d), lambda i: (i, 0)), pl.BlockSpec((1, d), lambda i: (0, 0)),
                  pl.BlockSpec((d, LANES), lambda i: (0, 0))],
        out_specs=[pl.BlockSpec((tm, d), lambda i: (i, 0)), pl.BlockSpec((tm, LANES), lambda i: (i, 0)),
                   pl.BlockSpec((8, LANES), lambda i: (0, 0))],
        out_shape=[jax.ShapeDtypeStruct((m, d), F32), jax.ShapeDtypeStruct((m, LANES), F32),
                   jax.ShapeDtypeStruct((8, LANES), F32)],
        scratch_shapes=[pltpu.VMEM((8, LANES), F32)],
        compiler_params=_cparams(("arbitrary",)),
    )(h, norm_w.reshape(1, d).astype(F32), rpad)


def _gather_rows_kernel(idx_ref, src_ref, o_ref, sem):
    tg = o_ref.shape[0]
    base = pl.program_id(0) * tg

    def row_copy(r):
        return pltpu.make_async_copy(src_ref.at[pl.ds(idx_ref[base + r], 1), :], o_ref.at[pl.ds(r, 1), :], sem)

    def start(r, c):
        row_copy(r).start()
        return c

    def wait(r, c):
        row_copy(r).wait()
        return c

    lax.fori_loop(0, tg, start, 0, unroll=8)
    lax.fori_loop(0, tg, wait, 0, unroll=8)


def gather_rows(src, idx, tg=256):
    n = idx.shape[0]
    d = src.shape[1]
    tg = min(tg, n)
    return pl.pallas_call(
        _gather_rows_kernel,
        grid_spec=pltpu.PrefetchScalarGridSpec(
            num_scalar_prefetch=1, grid=(n // tg,),
            in_specs=[pl.BlockSpec(memory_space=pl.ANY)],
            out_specs=pl.BlockSpec((tg, d), lambda i, idx_ref: (i, 0)),
            scratch_shapes=[pltpu.SemaphoreType.DMA(())]),
        out_shape=jax.ShapeDtypeStruct((n, d), src.dtype),
        compiler_params=_cparams(("arbitrary",)),
    )(idx, src)


def _new_expert(te_ref, m):
    return jnp.logical_or(m == 0, te_ref[m] != te_ref[jnp.maximum(m - 1, 0)])


def _moe_up_kernel(te_ref, nt_ref, x_ref, w1_ref, w3_ref, o_ref, w1b_ref, w3b_ref):
    m = pl.program_id(1)
    used = m < nt_ref[0]

    @pl.when(jnp.logical_and(used, _new_expert(te_ref, m)))
    def _():
        w1b_ref[...] = w1_ref[...].astype(BF)
        w3b_ref[...] = w3_ref[...].astype(BF)

    @pl.when(used)
    def _():
        x = x_ref[...].astype(BF)
        h1 = _dot(x, w1b_ref[...])
        h3 = _dot(x, w3b_ref[...])
        o_ref[...] = (h1 * _sigmoid(h1) * h3).astype(o_ref.dtype)

    @pl.when(jnp.logical_not(used))
    def _():
        o_ref[...] = jnp.zeros_like(o_ref)


def _moe_down_kernel(te_ref, nt_ref, a_ref, w2_ref, o_ref, wb_ref):
    m = pl.program_id(1)
    used = m < nt_ref[0]

    @pl.when(jnp.logical_and(used, _new_expert(te_ref, m)))
    def _():
        wb_ref[...] = w2_ref[...].astype(BF)

    @pl.when(used)
    def _():
        o_ref[...] = _dot(a_ref[...], wb_ref[...])

    @pl.when(jnp.logical_not(used))
    def _():
        o_ref[...] = jnp.zeros_like(o_ref)


def moe_experts(xg, tile_e, ntiles, w13, w2, tn_up=1024, tn_down=512):
    cap, d = xg.shape
    f = w2.shape[1]
    tm = MOE_TM
    nt_max = cap // tm
    nf = f // tn_up
    mt = lambda m, nt: jnp.minimum(m, nt[0] - 1)
    act = pl.pallas_call(
        _moe_up_kernel,
        grid_spec=pltpu.PrefetchScalarGridSpec(
            num_scalar_prefetch=2, grid=(nf, nt_max),
            in_specs=[pl.BlockSpec((tm, d), lambda j, m, te, nt: (mt(m, nt), 0)),
                      pl.BlockSpec((None, d, tn_up), lambda j, m, te, nt: (te[mt(m, nt)], 0, j)),
                      pl.BlockSpec((None, d, tn_up), lambda j, m, te, nt: (te[mt(m, nt)], 0, j + nf))],
            out_specs=pl.BlockSpec((tm, tn_up), lambda j, m, te, nt: (m, j)),
            scratch_shapes=[pltpu.VMEM((d, tn_up), BF), pltpu.VMEM((d, tn_up), BF)]),
        out_shape=jax.ShapeDtypeStruct((cap, f), BF),
        compiler_params=_cparams(("arbitrary", "arbitrary")),
    )(tile_e, ntiles, xg, w13, w13)
    return pl.pallas_call(
        _moe_down_kernel,
        grid_spec=pltpu.PrefetchScalarGridSpec(
            num_scalar_prefetch=2, grid=(d // tn_down, nt_max),
            in_specs=[pl.BlockSpec((tm, f), lambda j, m, te, nt: (mt(m, nt), 0)),
                      pl.BlockSpec((None, f, tn_down), lambda j, m, te, nt: (te[mt(m, nt)], 0, j))],
            out_specs=pl.BlockSpec((tm, tn_down), lambda j, m, te, nt: (m, j)),
            scratch_shapes=[pltpu.VMEM((f, tn_down), BF)]),
        out_shape=jax.ShapeDtypeStruct((cap, d), F32),
        compiler_params=_cparams(("arbitrary", "arbitrary"), VMEM_LIMIT_BIG),
    )(tile_e, ntiles, act, w2)


def _combine_kernel(pos_ref, h_ref, route_ref, y_ref, *rest, final_norm):
    if final_norm:
        nw_ref, o_ref, buf, sem = rest
    else:
        o_ref, buf, sem = rest
    tc = h_ref.shape[0]
    base = pl.program_id(0) * tc

    def row_copy(r, k):
        return pltpu.make_async_copy(y_ref.at[pl.ds(pos_ref[TOP_K * (base + r) + k], 1), :],
                                     buf.at[k, pl.ds(r, 1), :], sem)

    def start(r, c):
        for k in range(TOP_K):
            row_copy(r, k).start()
        return c

    def wait(r, c):
        for k in range(TOP_K):
            row_copy(r, k).wait()
        return c

    lax.fori_loop(0, tc, start, 0, unroll=8)
    lax.fori_loop(0, tc, wait, 0, unroll=8)
    acc = h_ref[...]
    for k in range(TOP_K):
        acc = acc + route_ref[:, TOP_K + k:TOP_K + k + 1] * buf[k]
    if final_norm:
        acc = _rms(acc, nw_ref[...])
    o_ref[...] = acc


def moe_combine(h, route, yg, pos, final_w=None, tc=256):
    n, d = h.shape
    tc = min(tc, n)
    in_specs = [pl.BlockSpec((tc, d), lambda i, pos_ref: (i, 0)), pl.BlockSpec((tc, LANES), lambda i, pos_ref: (i, 0)),
                pl.BlockSpec(memory_space=pl.ANY)]
    args = [pos, h, route, yg]
    if final_w is not None:
        in_specs.append(pl.BlockSpec((1, d), lambda i, pos_ref: (0, 0)))
        args.append(final_w.reshape(1, d).astype(F32))
    return pl.pallas_call(
        functools.partial(_combine_kernel, final_norm=final_w is not None),
        grid_spec=pltpu.PrefetchScalarGridSpec(
            num_scalar_prefetch=1, grid=(n // tc,),
            in_specs=in_specs,
            out_specs=pl.BlockSpec((tc, d), lambda i, pos_ref: (i, 0)),
            scratch_shapes=[pltpu.VMEM((TOP_K, tc, d), F32), pltpu.SemaphoreType.DMA(())]),
        out_shape=jax.ShapeDtypeStruct((n, d), F32),
        compiler_params=_cparams(("arbitrary",)),
    )(*args)


def moe_ffn(h, norm_w, router, w13, w2, final_w=None):
    n, d = h.shape
    tm = MOE_TM
    hn, route, cnt = moe_router(h, norm_w, router)
    expert = route[:, 0:TOP_K].astype(jnp.int32)
    rank = route[:, 2 * TOP_K:3 * TOP_K].astype(jnp.int32)
    counts = cnt[0, :N_EXPERTS].astype(jnp.int32)
    tiles = (counts + tm - 1) // tm
    tile_end = jnp.cumsum(tiles)
    dest = ((tile_end - tiles) * tm)[expert] + rank
    nt_max = -(-n * TOP_K // tm) + N_EXPERTS
    cap = nt_max * tm
    flat = dest.reshape(-1)
    row_tok = jnp.zeros((cap,), jnp.int32).at[flat].set(jnp.arange(n * TOP_K, dtype=jnp.int32) // TOP_K)
    tile_e = jnp.minimum(jnp.searchsorted(tile_end, jnp.arange(nt_max, dtype=jnp.int32), side="right"),
                         N_EXPERTS - 1).astype(jnp.int32)
    ntiles = tile_end[-1:].astype(jnp.int32)
    xg = gather_rows(hn, row_tok)
    yg = moe_experts(xg, tile_e, ntiles, w13, w2)
    return moe_combine(h, route, yg, flat, final_w)


def _final_kernel(x_ref, w_ref, o_ref):
    o_ref[...] = _rms(x_ref[...], w_ref[...])


def final_norm(h, w, tm=512):
    m, d = h.shape
    tm = min(tm, m)
    return pl.pallas_call(
        _final_kernel,
        grid=(m // tm,),
        in_specs=[pl.BlockSpec((tm, d), lambda i: (i, 0)), pl.BlockSpec((1, d), lambda i: (0, 0))],
        out_specs=pl.BlockSpec((tm, d), lambda i: (i, 0)),
        out_shape=jax.ShapeDtypeStruct((m, d), F32),
        compiler_params=_cparams(("parallel",)),
    )(h, w.reshape(1, d).astype(F32))


def _rot_cols(w):
    half = w.shape[-1] // 2
    return jnp.concatenate([-w[..., half:], w[..., :half]], axis=-1)


_W_IN_SEGMENTS = (
    (U_ML_Q, O_ML_Q, 512, 1.0), (U_ML_K, O_ML_K, 512, 1.0), (U_ML_V, O_ML_V, 1024, 1.0),
    (U_ML_O, O_ML_O, 1024, 1.0), (U_SSM_Z, O_SSM_Z, 1024, 1.0), (U_SSM_X, O_SSM_XBC, SSM_DINNER, 1.0),
    (U_SWA_Q, O_SWA_Q, 1024, SWA_HEAD_DIM ** -0.5 * LOG2E), (U_SSM_BC, O_SSM_XBC + SSM_DINNER, SSM_BC, 1.0),
    (U_MLA_CQ, O_MLA_CQ, MLA_Q_LORA, 1.0), (U_MLA_CKV, O_MLA_CKV, MLA_KV_LORA, 1.0),
    (U_SWA_K, O_SWA_K, 256, 1.0), (U_SWA_V, O_SWA_V, 256, 1.0), (U_GATE, O_GATE, N_BRANCH * D_MODEL, 1.0))


W_IN_TILE = 256
_ROPE_TILE = U_MLA_KR // W_IN_TILE
_GATE_TILE = U_TOTAL // W_IN_TILE


def _w_in_tile_sources():
    src = [0] * (U_TOTAL // W_IN_TILE)
    for dst, s, width, _ in _W_IN_SEGMENTS:
        for off in range(0, width, W_IN_TILE):
            src[(dst + off) // W_IN_TILE] = s + off
    src[_ROPE_TILE] = O_MLA_KR
    return src


def _w_in_layout_kernel(src_ref, w_ref, dt_ref, o_ref):
    t = pl.program_id(1)

    @pl.when(t == _GATE_TILE)
    def _():
        ngate = 2 * ML_HEADS
        o_ref[...] = jnp.concatenate(
            [w_ref[0, 0:ngate, :], dt_ref[0],
             jnp.zeros((W_IN_TILE - ngate - SSM_HEADS, o_ref.shape[1]), F32)], axis=0).astype(BF)

    @pl.when(jnp.logical_and(t != _ROPE_TILE, t != _GATE_TILE))
    def _():
        swa_q = jnp.logical_and(t >= U_SWA_Q // W_IN_TILE, t < U_SSM_BC // W_IN_TILE)
        scale = jnp.where(swa_q, SWA_HEAD_DIM ** -0.5 * LOG2E, 1.0)
        o_ref[...] = (w_ref[0] * scale).astype(BF)

    @pl.when(t == _ROPE_TILE)
    def _():
        half = MLA_ROPE // 2
        o_ref[0:MLA_ROPE, :] = w_ref[0, 0:MLA_ROPE, :].astype(BF)
        o_ref[MLA_ROPE:MLA_ROPE + half, :] = (-w_ref[0, half:MLA_ROPE, :]).astype(BF)
        o_ref[MLA_ROPE + half:2 * MLA_ROPE, :] = w_ref[0, 0:half, :].astype(BF)
        o_ref[2 * MLA_ROPE:, :] = jnp.zeros((W_IN_TILE - 2 * MLA_ROPE, o_ref.shape[1]), BF)


def layout_w_in(w_in):
    depth, d, _ = w_in.shape
    wt = jnp.swapaxes(w_in, 1, 2)
    src = jnp.asarray([s // SUBLANES for s in _w_in_tile_sources() + [O_ML_I]], jnp.int32)
    return pl.pallas_call(
        _w_in_layout_kernel,
        grid_spec=pltpu.PrefetchScalarGridSpec(
            num_scalar_prefetch=1, grid=(depth, _GATE_TILE + 1),
            in_specs=[pl.BlockSpec((pl.Element(1), pl.Element(W_IN_TILE), pl.Element(d)),
                                   lambda l, t, src_ref: (l, src_ref[t] * SUBLANES, 0)),
                      pl.BlockSpec((pl.Element(1), pl.Element(SSM_HEADS), pl.Element(d)),
                                   lambda l, t, src_ref: (l, O_SSM_DT, 0))],
            out_specs=pl.BlockSpec((None, W_IN_TILE, d), lambda l, t, src_ref: (l, t, 0))),
        out_shape=jax.ShapeDtypeStruct((depth, U_TOTAL + W_IN_TILE, d), BF),
        compiler_params=_cparams(("parallel", "parallel")),
    )(src, wt, wt)


def _layout_w_uq(w):
    k = w.shape[0]
    w = w.reshape(k, MLA_HEADS, MLA_NOPE + MLA_ROPE)
    rope = w[..., MLA_NOPE:]
    scale = (MLA_NOPE + MLA_ROPE) ** -0.5 * LOG2E
    return (jnp.concatenate([w, _rot_cols(rope)], axis=-1).reshape(k, -1) * scale).astype(BF)


def _rope_table(seq):
    inv_freq = 1.0 / (ROPE_THETA ** (jnp.arange(0, MLA_ROPE, 2, dtype=F32) / MLA_ROPE))
    ang = jnp.arange(seq, dtype=F32)[:, None] * inv_freq[None, :]
    c, s = jnp.cos(ang), jnp.sin(ang)
    return jnp.concatenate([c, c, s, s], axis=1)


def hybrid_mixer(h, bsz, seq, layer, norm_w, w_main, ml_ib, ml_fb, ml_norm, conv_w, conv_b, dt_bias, a_log,
                 ssm_d, ssm_norm, q_norm, w_uq, kv_norm, w_ukv, sinks, w_branch, w_out, cs):
    n = bsz * seq
    nc = seq // CHUNK
    u2, small = matmul(h, w_main, norm=norm_w, out_dtype=BF, w_layer=layer, w_t=True, n=U_TOTAL, tn=2048,
                       side_w=w_main, side_rows=(U_TOTAL // LANES, LANES))
    u3 = u2.reshape(bsz, seq, U_TOTAL)

    def to_rows(cols, heads):
        return cols.reshape(bsz, seq, heads).transpose(0, 2, 1).reshape(bsz * heads * nc, CHUNK)

    def per_row(vec, heads):
        return jnp.broadcast_to(vec.astype(F32)[None, :, None], (bsz, heads, nc)).reshape(-1, 1)

    ig, bcum, dt, acs = recurrence_gates(
        to_rows(small[:, 0:ML_HEADS], ML_HEADS), to_rows(small[:, ML_HEADS:2 * ML_HEADS], ML_HEADS),
        per_row(ml_ib, ML_HEADS), per_row(ml_fb, ML_HEADS),
        to_rows(small[:, 2 * ML_HEADS:2 * ML_HEADS + SSM_HEADS], SSM_HEADS),
        per_row(dt_bias, SSM_HEADS), per_row(a_log, SSM_HEADS))

    def as_rows(x, heads):
        return x.reshape(bsz, heads, nc, CHUNK).transpose(0, 2, 1, 3)

    def as_cols(x, heads):
        return x.reshape(bsz, heads, seq).transpose(0, 2, 1)

    ya = mlstm_branch(u3, as_rows(ig, ML_HEADS), as_rows(bcum, ML_HEADS), as_cols(bcum, ML_HEADS), ml_norm)
    yb = ssd_branch(u3, as_cols(dt, SSM_HEADS), as_cols(acs, SSM_HEADS), as_rows(acs, SSM_HEADS), conv_w, conv_b,
                    jnp.repeat(ssm_d, SSM_HEADDIM), ssm_norm)
    qf = matmul(u2, w_uq, norm=q_norm, x_col_blk=U_MLA_CQ // MLA_Q_LORA)
    kvf = matmul(u2, w_ukv, norm=kv_norm, x_col_blk=U_MLA_CKV // MLA_KV_LORA)
    yc = mla_attention(qf.reshape(bsz, seq, -1), kvf.reshape(bsz, seq, -1), u3, cs)
    yd = swa_branch(u3, sinks)
    merged = gated_merge(ya.reshape(n, -1), yb.reshape(n, -1), yc.reshape(n, -1), yd.reshape(n, -1), w_branch, u2)
    return matmul(merged, w_out, residual=h, out_dtype=F32)


def kernel(x, mem, norm_mix, w_in, ml_igate_bias, ml_fgate_bias, ml_norm, ssm_conv_w, ssm_conv_b, ssm_dt_bias, ssm_a_log, ssm_d, ssm_norm, mla_q_norm, mla_w_uq, mla_kv_norm, mla_w_ukv, swa_sinks, w_branch, w_out, norm_cross, norm_mem, xa_wq, xa_wkv, xa_wo, norm_ffn, ffn_w13, ffn_w2, moe_router, moe_w13, moe_w2, norm_final):
    bsz, seq, d = x.shape
    depth = w_in.shape[0]
    n = bsz * seq
    mlen = mem.shape[1]
    cs = _rope_table(seq)
    h = x.reshape(n, d)
    mem2 = mem.reshape(bsz * mlen, d)
    w_main = layout_w_in(w_in)
    for l in range(depth):
        h = hybrid_mixer(h, bsz, seq, l, norm_mix[l], w_main, ml_igate_bias[l], ml_fgate_bias[l], ml_norm[l],
                         ssm_conv_w[l], ssm_conv_b[l], ssm_dt_bias[l], ssm_a_log[l], ssm_d[l], ssm_norm[l],
                         mla_q_norm[l], _layout_w_uq(mla_w_uq[l]), mla_kv_norm[l], mla_w_ukv[l].astype(BF),
                         swa_sinks[l], w_branch[l].astype(BF), w_out[l].astype(BF), cs)
        kv = matmul(mem2, xa_wkv[l].astype(BF), norm=norm_mem[l])
        wq = (xa_wq[l] * (XA_HEAD_DIM ** -0.5 * LOG2E)).astype(BF)
        h = cross_attention(h, seq, norm_cross[l], wq, kv.reshape(bsz, mlen, -1), xa_wo[l].astype(BF))
        if l % 2 == 0:
            h = dense_ffn(h, norm_ffn[l], ffn_w13[l // 2].astype(BF), ffn_w2[l // 2].astype(BF))
        else:
            h = moe_ffn(h, norm_ffn[l], moe_router[l // 2], moe_w13[l // 2], moe_w2[l // 2],
                        final_w=norm_final if l == depth - 1 else None)
    if depth % 2 == 1:
        h = final_norm(h, norm_final)
    return h.reshape(bsz, seq, d)
```

```python
import functools
import math

import jax
import jax.numpy as jnp
from jax import lax
from jax.experimental import pallas as pl
from jax.experimental.pallas import tpu as pltpu

F32 = jnp.float32
BF = jnp.bfloat16

D_MODEL = 2048
RMS_EPS = 1e-6
ML_HEADS, ML_DQK, ML_DV = 4, 128, 256
SSM_HEADS, SSM_HEADDIM, SSM_GROUPS, SSM_STATE, SSM_CONV = 16, 64, 2, 128, 4
SSM_DINNER = SSM_HEADS * SSM_HEADDIM
SSM_BC = 2 * SSM_GROUPS * SSM_STATE
MLA_HEADS, MLA_Q_LORA, MLA_KV_LORA, MLA_NOPE, MLA_ROPE, MLA_V = 8, 512, 256, 128, 64, 128
ROPE_THETA = 10000.0
SWA_HEADS, SWA_KV_HEADS, SWA_HEAD_DIM, SWA_WINDOW = 16, 4, 64, 128
N_BRANCH, BRANCH_W = 4, 1024
XA_HEADS, XA_HEAD_DIM = 4, 128
FFN_DIM = 7168
N_EXPERTS, TOP_K = 8, 2

CHUNK = 128
LANES = 128
SUBLANES = 8
VMEM_LIMIT = 56 * 1024 * 1024
VMEM_LIMIT_BIG = 60 * 1024 * 1024
LOG2E = math.log2(math.e)

U_ML_Q, U_ML_K, U_ML_V, U_ML_O = 0, 512, 1024, 2048
U_SSM_Z, U_SSM_X, U_SWA_Q, U_SSM_BC = 3072, 4096, 5120, 6144
U_MLA_CQ, U_MLA_CKV, U_SWA_K, U_SWA_V, U_MLA_KR = 6656, 7168, 7424, 7680, 7936
U_GATE = 8192
U_TOTAL = U_GATE + N_BRANCH * D_MODEL
_SPLITS = (512, 512, 1024, 1024, 4, 4, 1024, 1536, 16, 512, 256, 64, 1024, 256, 256, 8192)
_OFF = [0]
for _s in _SPLITS:
    _OFF.append(_OFF[-1] + _s)
(O_ML_Q, O_ML_K, O_ML_V, O_ML_O, O_ML_I, O_ML_F, O_SSM_Z, O_SSM_XBC, O_SSM_DT, O_MLA_CQ, O_MLA_CKV,
 O_MLA_KR, O_SWA_Q, O_SWA_K, O_SWA_V, O_GATE, _O_END) = _OFF


def _cparams(sem, vmem_limit=VMEM_LIMIT):
    return pltpu.CompilerParams(dimension_semantics=sem, vmem_limit_bytes=vmem_limit)


def _dot(a, b):
    return jnp.dot(a, b, preferred_element_type=F32)


def _dot_nt(a, b):
    return lax.dot_general(a, b, (((1,), (1,)), ((), ())), preferred_element_type=F32)


def _dot_tn(a, b):
    return lax.dot_general(a, b, (((0,), (0,)), ((), ())), preferred_element_type=F32)


def _split3(a):
    a1 = a.astype(BF)
    r = a - a1.astype(F32)
    a2 = r.astype(BF)
    a3 = (r - a2.astype(F32)).astype(BF)
    return a1, a2, a3


def _dot_sel(a, sel):
    a1, a2, a3 = _split3(a)
    return _dot(a1, sel) + _dot(a2, sel) + _dot(a3, sel)


def _rms(x, w):
    return x * lax.rsqrt(jnp.mean(x * x, axis=-1, keepdims=True) + RMS_EPS) * w


def _sigmoid(x):
    return 1.0 / (1.0 + jnp.exp(-x))


def _mm_kernel(*refs, has_norm, has_res, has_side, w_t):
    mm = _dot_nt if w_t else _dot
    it = iter(refs)
    x_ref = next(it)
    g_ref = next(it) if has_norm else None
    w_ref = next(it)
    r_ref = next(it) if has_res else None
    ws_ref = next(it) if has_side else None
    o_ref = next(it)
    os_ref = next(it) if has_side else None
    if has_norm:
        xn_ref = next(it)

        @pl.when(pl.program_id(1) == 0)
        def _():
            xn = _rms(x_ref[...].astype(F32), g_ref[...]).astype(BF)
            xn_ref[...] = xn
            if has_side:
                os_ref[...] = mm(xn, ws_ref[...])

        xv = xn_ref[...]
    else:
        xv = x_ref[...]
    acc = mm(xv, w_ref[...])
    if has_res:
        acc = acc + r_ref[...]
    o_ref[...] = acc.astype(o_ref.dtype)


def matmul(x, w, *, norm=None, residual=None, out_dtype=None, tm=1024, tn=1024, x_col_blk=0, w_layer=None,
           side_w=None, side_rows=None, w_t=False, n=None):
    out_dtype = out_dtype or BF
    m = x.shape[0]
    k, n_all = w.shape[-2:][::-1] if w_t else w.shape[-2:]
    n = n or n_all
    tm, tn = min(tm, m), min(tn, n)
    assert m % tm == 0 and n % tn == 0
    in_specs = [pl.BlockSpec((tm, k), lambda i, j: (i, x_col_blk))]
    args = [x]
    scratch = []
    if norm is not None:
        in_specs.append(pl.BlockSpec((1, k), lambda i, j: (0, 0)))
        args.append(norm.reshape(1, k).astype(F32))
        scratch.append(pltpu.VMEM((tm, k), BF))
    if w_layer is None:
        in_specs.append(pl.BlockSpec((k, tn), lambda i, j: (0, j)))
    else:
        in_specs.append(pl.BlockSpec((None, tn, k), lambda i, j: (w_layer, j, 0)) if w_t else
                        pl.BlockSpec((None, k, tn), lambda i, j: (w_layer, 0, j)))
    args.append(w)
    if residual is not None:
        in_specs.append(pl.BlockSpec((tm, tn), lambda i, j: (i, j)))
        args.append(residual)
    out_specs = pl.BlockSpec((tm, tn), lambda i, j: (i, j))
    out_shape = jax.ShapeDtypeStruct((m, n), out_dtype)
    if side_w is not None:
        assert norm is not None and w_layer is not None
        if side_rows is None:
            ns = side_w.shape[-2] if w_t else side_w.shape[-1]
            in_specs.append(pl.BlockSpec((None,) + side_w.shape[1:], lambda i, j: (w_layer, 0, 0)))
        else:
            blk, ns = side_rows
            in_specs.append(pl.BlockSpec((None, ns, k), lambda i, j: (w_layer, blk, 0)))
        args.append(side_w)
        out_specs = [out_specs, pl.BlockSpec((tm, ns), lambda i, j: (i, 0))]
        out_shape = [out_shape, jax.ShapeDtypeStruct((m, ns), F32)]
    return pl.pallas_call(
        functools.partial(_mm_kernel, has_norm=norm is not None, has_res=residual is not None,
                          has_side=side_w is not None, w_t=w_t),
        grid=(m // tm, n // tn),
        in_specs=in_specs,
        out_specs=out_specs,
        out_shape=out_shape,
        scratch_shapes=scratch,
        compiler_params=_cparams(("parallel", "arbitrary")),
    )(*args)


def _cumsum_lanes(x):
    lane = lax.broadcasted_iota(jnp.int32, x.shape, 1)
    s = 1
    while s < x.shape[1]:
        x = x + jnp.where(lane >= s, pltpu.roll(x, s, axis=1), 0.0)
        s *= 2
    return x


def _softplus(x):
    return jnp.maximum(x, 0.0) + jnp.log(1.0 + jnp.exp(-jnp.abs(x)))


def _gates_kernel(i_ref, f_ref, ib_ref, fb_ref, dt_ref, dtb_ref, alog_ref, ig_ref, b_ref, dto_ref, acs_ref):
    ig_ref[...] = i_ref[...] + ib_ref[...]
    b_ref[...] = _cumsum_lanes(-_softplus(-(f_ref[...] + fb_ref[...])))
    dt = _softplus(dt_ref[...] + dtb_ref[...])
    dto_ref[...] = dt
    acs_ref[...] = _cumsum_lanes(dt * (-jnp.exp(alog_ref[...])))


def recurrence_gates(i_rows, f_rows, ib, fb, dt_rows, dtb, alog):
    r1, r2 = i_rows.shape[0], dt_rows.shape[0]
    shp = lambda r: jax.ShapeDtypeStruct((r, CHUNK), F32)
    return pl.pallas_call(
        _gates_kernel,
        out_shape=(shp(r1), shp(r1), shp(r2), shp(r2)),
    )(i_rows, f_rows, ib, fb, dt_rows, dtb, alog)


def _mlstm_kernel(q_ref, k_ref, v_ref, o_ref, igr_ref, br_ref, bc_ref, nw_ref, y_ref, *state):
    ct_ref, n_ref, m_ref = state[0::3], state[1::3], state[2::3]
    L = CHUNK

    @pl.when(pl.program_id(1) == 0)
    def _():
        for ref in state:
            ref[...] = jnp.zeros_like(ref)

    row = lax.broadcasted_iota(jnp.int32, (L, L), 0)
    col = lax.broadcasted_iota(jnp.int32, (L, L), 1)
    causal = col <= row
    diag = col == row
    scale = ML_DQK ** -0.5
    outs = []
    for h in range(ML_HEADS):
        q = q_ref[:, h * ML_DQK:(h + 1) * ML_DQK]
        k = (k_ref[:, h * ML_DQK:(h + 1) * ML_DQK].astype(F32) * scale).astype(BF)
        v = v_ref[:, h * ML_DV:(h + 1) * ML_DV]
        bcol = bc_ref[:, h:h + 1]
        brow = br_ref[h:h + 1, :]
        igrow = igr_ref[h:h + 1, :]
        m_prev = m_ref[h][0:1, 0:1]
        n_prev = n_ref[h][0:1, :]
        ct_prev = ct_ref[h][...]

        dmat = jnp.where(causal, bcol - brow + igrow, -jnp.inf)
        m_intra = jnp.max(dmat, axis=1, keepdims=True)
        g = bcol + m_prev
        m_s = jnp.maximum(g, m_intra)
        p = jnp.exp(dmat - m_s) * _dot_nt(q, k)
        inter = jnp.exp(g - m_s)
        num = _dot(p.astype(BF), v) + inter * _dot(q, ct_prev.astype(BF))
        den = jnp.sum(p, axis=1, keepdims=True) + inter * jnp.sum(q.astype(F32) * n_prev, axis=1, keepdims=True)
        hh = num / jnp.maximum(jnp.abs(den), jnp.exp(-m_s))
        hn = _rms(hh, nw_ref[:, h * ML_DV:(h + 1) * ML_DV])
        outs.append(_sigmoid(o_ref[:, h * ML_DV:(h + 1) * ML_DV].astype(F32)) * hn)

        b_tot = brow[:, L - 1:L]
        a = b_tot - brow + igrow
        m_loc = jnp.max(a, axis=1, keepdims=True)
        wl = jnp.exp(a - m_loc)
        kw = _dot(jnp.where(diag, wl, 0.0).astype(BF), k)
        m_new = jnp.maximum(b_tot + m_prev, m_loc)
        da = jnp.exp(b_tot + m_prev - m_new)
        db = jnp.exp(m_loc - m_new)
        ct_ref[h][...] = da * ct_prev + db * _dot_tn(kw.astype(BF), v)
        n_ref[h][...] = jnp.broadcast_to(da * n_prev + db * jnp.sum(kw, axis=0, keepdims=True), n_ref[h].shape)
        m_ref[h][...] = jnp.broadcast_to(m_new, m_ref[h].shape)
    y_ref[...] = jnp.concatenate(outs, axis=1).astype(y_ref.dtype)


def mlstm_branch(u3, ig_rows, b_rows, b_cols, norm_w):
    bsz, seq, _ = u3.shape
    nc = seq // CHUNK
    L = CHUNK
    ublk = lambda width, off: pl.BlockSpec((None, L, width), lambda b, c: (b, c, off // width))
    rows = pl.BlockSpec((None, None, ML_HEADS, L), lambda b, c: (b, c, 0, 0))
    return pl.pallas_call(
        _mlstm_kernel,
        grid=(bsz, nc),
        in_specs=[ublk(512, U_ML_Q), ublk(512, U_ML_K), ublk(1024, U_ML_V), ublk(1024, U_ML_O), rows, rows,
                  pl.BlockSpec((None, L, ML_HEADS), lambda b, c: (b, c, 0)),
                  pl.BlockSpec((1, ML_HEADS * ML_DV), lambda b, c: (0, 0))],
        out_specs=pl.BlockSpec((None, L, BRANCH_W), lambda b, c: (b, c, 0)),
        out_shape=jax.ShapeDtypeStruct((bsz, seq, BRANCH_W), BF),
        scratch_shapes=[pltpu.VMEM((ML_DQK, ML_DV), F32), pltpu.VMEM((8, ML_DQK), F32),
                        pltpu.VMEM((8, LANES), F32)] * ML_HEADS,
        compiler_params=_cparams(("parallel", "arbitrary")),
    )(u3, u3, u3, u3, ig_rows, b_rows, b_cols, norm_w.reshape(1, -1).astype(F32))


def _ssd_kernel(z_ref, x_ref, bc_ref, dt_ref, ac_ref, ar_ref, cw_ref, cb_ref, d_ref, nw_ref, y_ref, xs_ref, st_ref):
    L = CHUNK
    P, R, NS = SSM_HEADDIM, SSM_HEADS // SSM_GROUPS, SSM_STATE
    GW = R * P

    @pl.when(pl.program_id(1) == 0)
    def _():
        xs_ref[0:8, :] = jnp.zeros((8, xs_ref.shape[1]), F32)
        st_ref[...] = jnp.zeros_like(st_ref)

    xs_ref[8:, :] = jnp.concatenate([x_ref[...], bc_ref[...]], axis=1).astype(F32)
    conv = cb_ref[...] + cw_ref[SSM_CONV - 1:SSM_CONV, :] * xs_ref[8:8 + L, :]
    for sft in range(1, SSM_CONV):
        conv = conv + cw_ref[SSM_CONV - 1 - sft:SSM_CONV - sft, :] * xs_ref[8 - sft:8 - sft + L, :]
    xs_ref[0:8, :] = xs_ref[L:L + 8, :]
    xbc = conv * _sigmoid(conv)
    xh = xbc[:, :SSM_DINNER]
    bmat = xbc[:, SSM_DINNER:SSM_DINNER + SSM_GROUPS * NS].astype(BF)
    cmat = xbc[:, SSM_DINNER + SSM_GROUPS * NS:].astype(BF)

    dtc = dt_ref[...]
    ac = ac_ref[...]
    ar = ar_ref[...]
    a_last = ac[L - 1:L, :]
    hsel = (lax.broadcasted_iota(jnp.int32, (SSM_HEADS, SSM_DINNER), 1) // P
            == lax.broadcasted_iota(jnp.int32, (SSM_HEADS, SSM_DINNER), 0))
    expand = jnp.where(hsel, 1.0, 0.0).astype(BF)
    stack = jnp.concatenate([dtc, jnp.exp(a_last - ac), jnp.exp(ac),
                             jnp.broadcast_to(jnp.exp(a_last), (8, SSM_HEADS))], axis=0)
    ex = _dot_sel(stack, expand)
    dt_full, dst_full, ind_full = ex[0:L], ex[L:2 * L], ex[2 * L:3 * L]
    cdec_full = ex[3 * L:3 * L + 1]
    xdt = xh * dt_full
    xdt_b = xdt.astype(BF)
    xw_b = (xdt * dst_full).astype(BF)

    row = lax.broadcasted_iota(jnp.int32, (L, L), 0)
    col = lax.broadcasted_iota(jnp.int32, (L, L), 1)
    causal = col <= row
    ys = []
    for g in range(SSM_GROUPS):
        bg = bmat[:, g * NS:(g + 1) * NS]
        cg = cmat[:, g * NS:(g + 1) * NS]
        cb = _dot_nt(cg, bg)
        st_prev = st_ref[g]
        yoff = _dot(cg, st_prev.astype(BF))
        st_ref[g] = cdec_full[:, g * GW:(g + 1) * GW] * st_prev + _dot_tn(bg, xw_b[:, g * GW:(g + 1) * GW])
        for r in range(R):
            h = g * R + r
            dec = jnp.exp(jnp.where(causal, ac[:, h:h + 1] - ar[h:h + 1, :], -jnp.inf))
            yd = _dot((dec * cb).astype(BF), xdt_b[:, h * P:(h + 1) * P])
            ys.append(yd + yoff[:, r * P:(r + 1) * P] * ind_full[:, h * P:(h + 1) * P])
    y = jnp.concatenate(ys, axis=1) + xh * d_ref[...]
    zf = z_ref[...].astype(F32)
    y = y * (zf * _sigmoid(zf))
    y_ref[...] = jnp.concatenate(
        [_rms(y[:, g * GW:(g + 1) * GW], nw_ref[:, g * GW:(g + 1) * GW]) for g in range(SSM_GROUPS)],
        axis=1).astype(y_ref.dtype)


def ssd_branch(u3, dt_cols, acs_cols, acs_rows, conv_w, conv_b, d_full, norm_w):
    bsz, seq, _ = u3.shape
    nc = seq // CHUNK
    L = CHUNK
    ublk = lambda width, off: pl.BlockSpec((None, L, width), lambda b, c: (b, c, off // width))
    cols = pl.BlockSpec((None, L, SSM_HEADS), lambda b, c: (b, c, 0))
    const = lambda shape: pl.BlockSpec(shape, lambda b, c: (0, 0))
    cch = SSM_DINNER + SSM_BC
    return pl.pallas_call(
        _ssd_kernel,
        grid=(bsz, nc),
        in_specs=[ublk(1024, U_SSM_Z), ublk(1024, U_SSM_X), ublk(512, U_SSM_BC), cols, cols,
                  pl.BlockSpec((None, None, SSM_HEADS, L), lambda b, c: (b, c, 0, 0)),
                  const((SSM_CONV, cch)), const((1, cch)), const((1, SSM_DINNER)), const((1, SSM_DINNER))],
        out_specs=pl.BlockSpec((None, L, BRANCH_W), lambda b, c: (b, c, 0)),
        out_shape=jax.ShapeDtypeStruct((bsz, seq, BRANCH_W), BF),
        scratch_shapes=[pltpu.VMEM((L + 8, cch), F32),
                        pltpu.VMEM((SSM_GROUPS, SSM_STATE, SSM_DINNER // SSM_GROUPS), F32)],
        compiler_params=_cparams(("parallel", "arbitrary")),
    )(u3, u3, u3, dt_cols, acs_cols, acs_rows, conv_w.astype(F32), conv_b.reshape(1, cch).astype(F32),
      d_full.reshape(1, -1).astype(F32), norm_w.reshape(1, -1).astype(F32))


def _rope128(x, cs):
    t = x.astype(F32) * cs
    return t + pltpu.roll(t, MLA_ROPE, axis=1)


def _mla_kernel(q_ref, csq_ref, kv_ref, kr_ref, csk_ref, o_ref, qe_s, *stats, tq):
    qi = pl.program_id(1)
    ki = pl.program_id(2)
    qw = MLA_NOPE + LANES
    kw = MLA_NOPE + MLA_V
    m_s, l_s, acc_s = stats[0::3], stats[1::3], stats[2::3]
    nt = tq // LANES

    @pl.when(ki == 0)
    def _():
        for h in range(MLA_HEADS):
            qr = _rope128(q_ref[:, h * qw + MLA_NOPE:(h + 1) * qw], csq_ref[...]).astype(BF)
            qe_s[h] = jnp.concatenate([q_ref[:, h * qw:h * qw + MLA_NOPE], qr], axis=1)
            m_s[h][...] = jnp.full((tq, LANES), -jnp.inf, F32)
            l_s[h][...] = jnp.zeros((tq, LANES), F32)
            acc_s[h][...] = jnp.zeros((tq, MLA_V), F32)

    def step(masked):
        lane = lax.broadcasted_iota(jnp.int32, (tq, LANES), 1)
        kr = jnp.where(lane < MLA_ROPE, _rope128(kr_ref[...], csk_ref[...]), 0.0).astype(BF)
        if masked:
            causal = (lax.broadcasted_iota(jnp.int32, (tq, tq), 1) <= lax.broadcasted_iota(jnp.int32, (tq, tq), 0))
        for h in range(MLA_HEADS):
            ke = jnp.concatenate([kv_ref[:, h * kw:h * kw + MLA_NOPE], kr], axis=1)
            s = _dot_nt(qe_s[h], ke)
            if masked:
                s = jnp.where(causal, s, -jnp.inf)
            m_old = m_s[h][...]
            m_new = jnp.maximum(m_old, jnp.max(s, axis=1, keepdims=True))
            alpha = jnp.exp2(m_old - m_new)
            p = jnp.exp2(s - jnp.concatenate([m_new] * nt, axis=1))
            psum = p[:, 0:LANES]
            for t in range(1, nt):
                psum = psum + p[:, t * LANES:(t + 1) * LANES]
            l_s[h][...] = alpha * l_s[h][...] + psum
            acc_s[h][...] = alpha * acc_s[h][...] + _dot(p.astype(BF), kv_ref[:, h * kw + MLA_NOPE:(h + 1) * kw])
            m_s[h][...] = m_new

    @pl.when(ki < qi)
    def _():
        step(False)

    @pl.when(ki == qi)
    def _():
        step(True)
        for h in range(MLA_HEADS):
            l = jnp.sum(l_s[h][...], axis=1, keepdims=True)
            o_ref[:, h * MLA_V:(h + 1) * MLA_V] = (acc_s[h][...] / l).astype(o_ref.dtype)


def mla_attention(qf3, kvf3, u3, cs, tq=1024):
    bsz, seq, _ = qf3.shape
    tq = min(tq, seq)
    nq = seq // tq
    kvi = lambda b, qi, ki: jnp.minimum(ki, qi)
    return pl.pallas_call(
        functools.partial(_mla_kernel, tq=tq),
        grid=(bsz, nq, nq),
        in_specs=[pl.BlockSpec((None, tq, qf3.shape[2]), lambda b, qi, ki: (b, qi, 0)),
                  pl.BlockSpec((tq, LANES), lambda b, qi, ki: (qi, 0)),
                  pl.BlockSpec((None, tq, kvf3.shape[2]), lambda b, qi, ki: (b, kvi(b, qi, ki), 0)),
                  pl.BlockSpec((None, tq, LANES), lambda b, qi, ki: (b, kvi(b, qi, ki), U_MLA_KR // LANES)),
                  pl.BlockSpec((tq, LANES), lambda b, qi, ki: (kvi(b, qi, ki), 0))],
        out_specs=pl.BlockSpec((None, tq, MLA_HEADS * MLA_V), lambda b, qi, ki: (b, qi, 0)),
        out_shape=jax.ShapeDtypeStruct((bsz, seq, MLA_HEADS * MLA_V), BF),
        scratch_shapes=[pltpu.VMEM((MLA_HEADS, tq, MLA_NOPE + LANES), BF)]
        + [pltpu.VMEM((tq, LANES), F32), pltpu.VMEM((tq, LANES), F32), pltpu.VMEM((tq, MLA_V), F32)] * MLA_HEADS,
        compiler_params=_cparams(("parallel", "parallel", "arbitrary")),
    )(qf3, cs, kvf3, u3, cs)


def _swa_kernel(q_ref, kc_ref, kp_ref, vc_ref, vp_ref, sink_ref, o_ref):
    W, d = SWA_WINDOW, SWA_HEAD_DIM
    n = pl.program_id(1)
    i = lax.broadcasted_iota(jnp.int32, (2 * W, 2 * W), 0) & (W - 1)
    j = lax.broadcasted_iota(jnp.int32, (2 * W, 2 * W), 1)
    valid = (j > i) & (j <= i + W) & ((n > 0) | (j >= W))
    first_tile = lax.broadcasted_iota(jnp.int32, (2 * W, LANES), 0) < W
    lower = lax.broadcasted_iota(jnp.int32, (2 * W, LANES), 1) < d
    ones = jnp.ones((2 * W, LANES), BF)
    for pair in range(SWA_KV_HEADS // 2):
        sl = slice(pair * LANES, (pair + 1) * LANES)
        kt = jnp.concatenate([kp_ref[:, sl], kc_ref[:, sl]], axis=0)
        vt = jnp.concatenate([vp_ref[:, sl], vc_ref[:, sl]], axis=0)
        kt_sw = pltpu.roll(kt.astype(F32), d, axis=1).astype(BF)
        vt_sw = pltpu.roll(vt.astype(F32), d, axis=1).astype(BF)
        for e in range(2):
            kh = 2 * pair + e
            k_lo = jnp.where(lower, kt if e == 0 else kt_sw, jnp.zeros_like(kt))
            k_hi = jnp.where(lower, jnp.zeros_like(kt), kt_sw if e == 0 else kt)
            v_lo, v_hi = (vt, vt_sw) if e == 0 else (vt_sw, vt)
            qs = jnp.concatenate([q_ref[:, 2 * kh * LANES:(2 * kh + 1) * LANES],
                                  q_ref[:, (2 * kh + 1) * LANES:(2 * kh + 2) * LANES]], axis=0)
            halves = []
            for half, (ke, ve) in enumerate(((k_lo, v_lo), (k_hi, v_hi))):
                ha, hb = 4 * kh + half, 4 * kh + 2 + half
                s = jnp.where(valid, _dot_nt(qs, ke), -jnp.inf)
                sink = jnp.where(first_tile, sink_ref[0:1, ha:ha + 1], sink_ref[0:1, hb:hb + 1]) * LOG2E
                m = jnp.maximum(jnp.max(s, axis=1, keepdims=True), sink)
                p = jnp.exp2(s - jnp.concatenate([m, m], axis=1)).astype(BF)
                den = _dot(p, ones) + jnp.exp2(sink - m)
                halves.append(_dot(p, ve) / den)
            ot = jnp.where(lower, halves[0], halves[1]).astype(o_ref.dtype)
            o_ref[:, 2 * kh * LANES:(2 * kh + 1) * LANES] = ot[:W]
            o_ref[:, (2 * kh + 1) * LANES:(2 * kh + 2) * LANES] = ot[W:]


def swa_branch(u3, sinks):
    bsz, seq, _ = u3.shape
    W = SWA_WINDOW
    kvw = SWA_KV_HEADS * SWA_HEAD_DIM
    cur = lambda width, off: pl.BlockSpec((None, W, width), lambda b, n: (b, n, off // width))
    prev = lambda width, off: pl.BlockSpec((None, W, width), lambda b, n: (b, jnp.maximum(n - 1, 0), off // width))
    return pl.pallas_call(
        _swa_kernel,
        grid=(bsz, seq // W),
        in_specs=[cur(1024, U_SWA_Q), cur(kvw, U_SWA_K), prev(kvw, U_SWA_K), cur(kvw, U_SWA_V), prev(kvw, U_SWA_V),
                  pl.BlockSpec((1, SWA_HEADS), lambda b, n: (0, 0))],
        out_specs=pl.BlockSpec((None, W, BRANCH_W), lambda b, n: (b, n, 0)),
        out_shape=jax.ShapeDtypeStruct((bsz, seq, BRANCH_W), BF),
        compiler_params=_cparams(("parallel", "parallel")),
    )(u3, u3, u3, u3, u3, sinks.reshape(1, -1).astype(F32))


def _merge_kernel(ya_ref, yb_ref, yc_ref, yd_ref, w_ref, g0_ref, g1_ref, g2_ref, g3_ref, o_ref):
    acc = None
    for n, (y_ref, g_ref) in enumerate(((ya_ref, g0_ref), (yb_ref, g1_ref), (yc_ref, g2_ref), (yd_ref, g3_ref))):
        t = _sigmoid(g_ref[...].astype(F32)) * _dot(y_ref[...], w_ref[n])
        acc = t if acc is None else acc + t
    o_ref[...] = acc.astype(o_ref.dtype)


def gated_merge(ya, yb, yc, yd, w_branch, u2, tm=1024, tn=512):
    m = ya.shape[0]
    tm = min(tm, m)
    ysp = pl.BlockSpec((tm, BRANCH_W), lambda i, j: (i, 0))
    gsp = lambda n: pl.BlockSpec((tm, tn), lambda i, j: (i, (U_GATE + n * D_MODEL) // tn + j))
    return pl.pallas_call(
        _merge_kernel,
        grid=(m // tm, D_MODEL // tn),
        in_specs=[ysp, ysp, ysp, ysp, pl.BlockSpec((N_BRANCH, BRANCH_W, tn), lambda i, j: (0, 0, j)),
                  gsp(0), gsp(1), gsp(2), gsp(3)],
        out_specs=pl.BlockSpec((tm, tn), lambda i, j: (i, j)),
        out_shape=jax.ShapeDtypeStruct((m, D_MODEL), BF),
        compiler_params=_cparams(("parallel", "parallel")),
    )(ya, yb, yc, yd, w_branch, u2, u2, u2, u2)


def _xattn_kernel(h_ref, g_ref, wq_ref, kv_ref, wo_ref, o_ref):
    d = XA_HEAD_DIM
    h = h_ref[...]
    q = _dot(_rms(h, g_ref[...]).astype(BF), wq_ref[...]).astype(BF)
    ones = jnp.ones((kv_ref.shape[0], LANES), BF)
    outs = []
    for hd in range(XA_HEADS):
        s = _dot_nt(q[:, hd * d:(hd + 1) * d], kv_ref[:, hd * d:(hd + 1) * d])
        p = jnp.exp2(s - jnp.max(s, axis=1, keepdims=True)).astype(BF)
        den = _dot(p, ones)
        outs.append((_dot(p, kv_ref[:, (XA_HEADS + hd) * d:(XA_HEADS + hd + 1) * d]) / den).astype(BF))
    o_ref[...] = h + _dot(jnp.concatenate(outs, axis=1), wo_ref[...])


def cross_attention(h, seq, norm_w, wq, kv3, wo, tq=512):
    n, dm = h.shape
    _, mlen, kvw = kv3.shape
    tq = min(tq, seq)
    per_b = seq // tq
    const = lambda shape: pl.BlockSpec(shape, lambda i: (0, 0))
    return pl.pallas_call(
        _xattn_kernel,
        grid=(n // tq,),
        in_specs=[pl.BlockSpec((tq, dm), lambda i: (i, 0)), const((1, dm)), const(wq.shape),
                  pl.BlockSpec((None, mlen, kvw), lambda i: (i // per_b, 0, 0)), const(wo.shape)],
        out_specs=pl.BlockSpec((tq, dm), lambda i: (i, 0)),
        out_shape=jax.ShapeDtypeStruct((n, dm), F32),
        compiler_params=_cparams(("parallel",)),
    )(h, norm_w.reshape(1, dm).astype(F32), wq, kv3, wo)


def _ffn_kernel(h_ref, g_ref, w1_ref, w3_ref, w2_ref, o_ref, xn_ref, acc_ref):
    j = pl.program_id(1)

    @pl.when(j == 0)
    def _():
        hv = h_ref[...]
        xn_ref[...] = _rms(hv, g_ref[...]).astype(BF)
        acc_ref[...] = hv

    xn = xn_ref[...]
    h1 = _dot(xn, w1_ref[...])
    h3 = _dot(xn, w3_ref[...])
    act = (h1 * _sigmoid(h1) * h3).astype(BF)
    acc_ref[...] += _dot(act, w2_ref[...])

    @pl.when(j == pl.num_programs(1) - 1)
    def _():
        o_ref[...] = acc_ref[...]


def dense_ffn(h, norm_w, w13, w2, tm=512, tf=1024):
    m, d = h.shape
    f = w2.shape[0]
    tm = min(tm, m)
    nf = f // tf
    return pl.pallas_call(
        _ffn_kernel,
        grid=(m // tm, nf),
        in_specs=[pl.BlockSpec((tm, d), lambda i, j: (i, 0)), pl.BlockSpec((1, d), lambda i, j: (0, 0)),
                  pl.BlockSpec((d, tf), lambda i, j: (0, j)), pl.BlockSpec((d, tf), lambda i, j: (0, j + nf)),
                  pl.BlockSpec((tf, d), lambda i, j: (j, 0))],
        out_specs=pl.BlockSpec((tm, d), lambda i, j: (i, 0)),
        out_shape=jax.ShapeDtypeStruct((m, d), F32),
        scratch_shapes=[pltpu.VMEM((tm, d), BF), pltpu.VMEM((tm, d), F32)],
        compiler_params=_cparams(("parallel", "arbitrary")),
    )(h, norm_w.reshape(1, d).astype(F32), w13, w13, w2)


ROUTE_TM = 512
MOE_TM = 512


def _router_kernel(h_ref, g_ref, r_ref, hn_ref, route_ref, cnt_ref, carry_ref):
    tm = h_ref.shape[0]

    @pl.when(pl.program_id(0) == 0)
    def _():
        carry_ref[...] = jnp.zeros_like(carry_ref)

    hn = _rms(h_ref[...], g_ref[...])
    hn_ref[...] = hn
    a1, a2, _ = _split3(hn)
    r1, r2, _ = _split3(r_ref[...])
    lane = lax.broadcasted_iota(jnp.int32, (tm, LANES), 1)
    logits = jnp.where(lane < N_EXPERTS, _dot(a1, r1) + (_dot(a1, r2) + _dot(a2, r1)), -jnp.inf)
    v0 = jnp.max(logits, axis=1, keepdims=True)
    i0 = jnp.min(jnp.where(logits == v0, lane, LANES), axis=1, keepdims=True)
    rest = jnp.where(lane == i0, -jnp.inf, logits)
    v1 = jnp.max(rest, axis=1, keepdims=True)
    i1 = jnp.min(jnp.where(rest == v1, lane, LANES), axis=1, keepdims=True)
    ex = jnp.exp(v1 - v0)
    g0 = 1.0 / (1.0 + ex)
    g1 = ex / (1.0 + ex)
    sel0, sel1 = lane == i0, lane == i1
    onehot = jnp.where(sel0 | sel1, 1.0, 0.0)
    below = (lax.broadcasted_iota(jnp.int32, (tm, tm), 1) < lax.broadcasted_iota(jnp.int32, (tm, tm), 0))
    before = carry_ref[0:1, :] + _dot(jnp.where(below, 1.0, 0.0).astype(BF), onehot.astype(BF))
    rank0 = jnp.sum(jnp.where(sel0, before, 0.0), axis=1, keepdims=True)
    rank1 = jnp.sum(jnp.where(sel1, before, 0.0), axis=1, keepdims=True)
    total = carry_ref[0:1, :] + jnp.sum(onehot, axis=0, keepdims=True)
    carry_ref[...] = jnp.broadcast_to(total, carry_ref.shape)
    cnt_ref[...] = jnp.broadcast_to(total, cnt_ref.shape)
    out = jnp.zeros((tm, LANES), F32)
    for pos, val in enumerate((i0.astype(F32), i1.astype(F32), g0, g1, rank0, rank1)):
        out = jnp.where(lane == pos, val, out)
    route_ref[...] = out


def moe_router(h, norm_w, router):
    m, d = h.shape
    tm = min(ROUTE_TM, m)
    rpad = jnp.zeros((d, LANES), F32).at[:, :N_EXPERTS].set(router.astype(F32))
    return pl.pallas_call(
        _router_kernel,
        grid=(m // tm,),
        in_specs=[pl.BlockSpec((tm, d), lambda i: (i, 0)), pl.BlockSpec((1, d), lambda i: (0, 0)),
                  pl.BlockSpec((d, LANES), lambda i: (0, 0))],
        out_specs=[pl.BlockSpec((tm, d), lambda i: (i, 0)), pl.BlockSpec((tm, LANES), lambda i: (i, 0)),
                   pl.BlockSpec((8, LANES), lambda i: (0, 0))],
        out_shape=[jax.ShapeDtypeStruct((m, d), F32), jax.ShapeDtypeStruct((m, LANES), F32),
                   jax.ShapeDtypeStruct((8, LANES), F32)],
        scratch_shapes=[pltpu.VMEM((8, LANES), F32)],
        compiler_params=_cparams(("arbitrary",)),
    )(h, norm_w.reshape(1, d).astype(F32), rpad)


def _gather_rows_kernel(idx_ref, src_ref, o_ref, sem):
    tg = o_ref.shape[0]
    base = pl.program_id(0) * tg

    def row_copy(r):
        return pltpu.make_async_copy(src_ref.at[pl.ds(idx_ref[base + r], 1), :], o_ref.at[pl.ds(r, 1), :], sem)

    def start(r, c):
        row_copy(r).start()
        return c

    def wait(r, c):
        row_copy(r).wait()
        return c

    lax.fori_loop(0, tg, start, 0, unroll=8)
    lax.fori_loop(0, tg, wait, 0, unroll=8)


def gather_rows(src, idx, tg=256):
    n = idx.shape[0]
    d = src.shape[1]
    tg = min(tg, n)
    return pl.pallas_call(
        _gather_rows_kernel,
        grid_spec=pltpu.PrefetchScalarGridSpec(
            num_scalar_prefetch=1, grid=(n // tg,),
            in_specs=[pl.BlockSpec(memory_space=pl.ANY)],
            out_specs=pl.BlockSpec((tg, d), lambda i, idx_ref: (i, 0)),
            scratch_shapes=[pltpu.SemaphoreType.DMA(())]),
        out_shape=jax.ShapeDtypeStruct((n, d), src.dtype),
        compiler_params=_cparams(("arbitrary",)),
    )(idx, src)


def _new_expert(te_ref, m):
    return jnp.logical_or(m == 0, te_ref[m] != te_ref[jnp.maximum(m - 1, 0)])


def _moe_up_kernel(te_ref, nt_ref, x_ref, w1_ref, w3_ref, o_ref, w1b_ref, w3b_ref):
    m = pl.program_id(1)
    used = m < nt_ref[0]

    @pl.when(jnp.logical_and(used, _new_expert(te_ref, m)))
    def _():
        w1b_ref[...] = w1_ref[...].astype(BF)
        w3b_ref[...] = w3_ref[...].astype(BF)

    @pl.when(used)
    def _():
        x = x_ref[...].astype(BF)
        h1 = _dot(x, w1b_ref[...])
        h3 = _dot(x, w3b_ref[...])
        o_ref[...] = (h1 * _sigmoid(h1) * h3).astype(o_ref.dtype)

    @pl.when(jnp.logical_not(used))
    def _():
        o_ref[...] = jnp.zeros_like(o_ref)


def _moe_down_kernel(te_ref, nt_ref, a_ref, w2_ref, o_ref, wb_ref):
    m = pl.program_id(1)
    used = m < nt_ref[0]

    @pl.when(jnp.logical_and(used, _new_expert(te_ref, m)))
    def _():
        wb_ref[...] = w2_ref[...].astype(BF)

    @pl.when(used)
    def _():
        o_ref[...] = _dot(a_ref[...], wb_ref[...])

    @pl.when(jnp.logical_not(used))
    def _():
        o_ref[...] = jnp.zeros_like(o_ref)


def moe_experts(xg, tile_e, ntiles, w13, w2, tn_up=1024, tn_down=512):
    cap, d = xg.shape
    f = w2.shape[1]
    tm = MOE_TM
    nt_max = cap // tm
    nf = f // tn_up
    mt = lambda m, nt: jnp.minimum(m, nt[0] - 1)
    act = pl.pallas_call(
        _moe_up_kernel,
        grid_spec=pltpu.PrefetchScalarGridSpec(
            num_scalar_prefetch=2, grid=(nf, nt_max),
            in_specs=[pl.BlockSpec((tm, d), lambda j, m, te, nt: (mt(m, nt), 0)),
                      pl.BlockSpec((None, d, tn_up), lambda j, m, te, nt: (te[mt(m, nt)], 0, j)),
                      pl.BlockSpec((None, d, tn_up), lambda j, m, te, nt: (te[mt(m, nt)], 0, j + nf))],
            out_specs=pl.BlockSpec((tm, tn_up), lambda j, m, te, nt: (m, j)),
            scratch_shapes=[pltpu.VMEM((d, tn_up), BF), pltpu.VMEM((d, tn_up), BF)]),
        out_shape=jax.ShapeDtypeStruct((cap, f), BF),
        compiler_params=_cparams(("arbitrary", "arbitrary")),
    )(tile_e, ntiles, xg, w13, w13)
    return pl.pallas_call(
        _moe_down_kernel,
        grid_spec=pltpu.PrefetchScalarGridSpec(
            num_scalar_prefetch=2, grid=(d // tn_down, nt_max),
            in_specs=[pl.BlockSpec((tm, f), lambda j, m, te, nt: (mt(m, nt), 0)),
                      pl.BlockSpec((None, f, tn_down), lambda j, m, te, nt: (te[mt(m, nt)], 0, j))],
            out_specs=pl.BlockSpec((tm, tn_down), lambda j, m, te, nt: (m, j)),
            scratch_shapes=[pltpu.VMEM((f, tn_down), BF)]),
        out_shape=jax.ShapeDtypeStruct((cap, d), F32),
        compiler_params=_cparams(("arbitrary", "arbitrary"), VMEM_LIMIT_BIG),
    )(tile_e, ntiles, act, w2)


def _combine_kernel(pos_ref, h_ref, route_ref, y_ref, *rest, final_norm):
    if final_norm:
        nw_ref, o_ref, buf, sem = rest
    else:
        o_ref, buf, sem = rest
    tc = h_ref.shape[0]
    base = pl.program_id(0) * tc

    def row_copy(r, k):
        return pltpu.make_async_copy(y_ref.at[pl.ds(pos_ref[TOP_K * (base + r) + k], 1), :],
                                     buf.at[k, pl.ds(r, 1), :], sem)

    def start(r, c):
        for k in range(TOP_K):
            row_copy(r, k).start()
        return c

    def wait(r, c):
        for k in range(TOP_K):
            row_copy(r, k).wait()
        return c

    lax.fori_loop(0, tc, start, 0, unroll=8)
    lax.fori_loop(0, tc, wait, 0, unroll=8)
    acc = h_ref[...]
    for k in range(TOP_K):
        acc = acc + route_ref[:, TOP_K + k:TOP_K + k + 1] * buf[k]
    if final_norm:
        acc = _rms(acc, nw_ref[...])
    o_ref[...] = acc


def moe_combine(h, route, yg, pos, final_w=None, tc=256):
    n, d = h.shape
    tc = min(tc, n)
    in_specs = [pl.BlockSpec((tc, d), lambda i, pos_ref: (i, 0)), pl.BlockSpec((tc, LANES), lambda i, pos_ref: (i, 0)),
                pl.BlockSpec(memory_space=pl.ANY)]
    args = [pos, h, route, yg]
    if final_w is not None:
        in_specs.append(pl.BlockSpec((1, d), lambda i, pos_ref: (0, 0)))
        args.append(final_w.reshape(1, d).astype(F32))
    return pl.pallas_call(
        functools.partial(_combine_kernel, final_norm=final_w is not None),
        grid_spec=pltpu.PrefetchScalarGridSpec(
            num_scalar_prefetch=1, grid=(n // tc,),
            in_specs=in_specs,
            out_specs=pl.BlockSpec((tc, d), lambda i, pos_ref: (i, 0)),
            scratch_shapes=[pltpu.VMEM((TOP_K, tc, d), F32), pltpu.SemaphoreType.DMA(())]),
        out_shape=jax.ShapeDtypeStruct((n, d), F32),
        compiler_params=_cparams(("arbitrary",)),
    )(*args)


def moe_ffn(h, norm_w, router, w13, w2, final_w=None):
    n, d = h.shape
    tm = MOE_TM
    hn, route, cnt = moe_router(h, norm_w, router)
    expert = route[:, 0:TOP_K].astype(jnp.int32)
    rank = route[:, 2 * TOP_K:3 * TOP_K].astype(jnp.int32)
    counts = cnt[0, :N_EXPERTS].astype(jnp.int32)
    tiles = (counts + tm - 1) // tm
    tile_end = jnp.cumsum(tiles)
    dest = ((tile_end - tiles) * tm)[expert] + rank
    nt_max = -(-n * TOP_K // tm) + N_EXPERTS
    cap = nt_max * tm
    flat = dest.reshape(-1)
    row_tok = jnp.zeros((cap,), jnp.int32).at[flat].set(jnp.arange(n * TOP_K, dtype=jnp.int32) // TOP_K)
    tile_e = jnp.minimum(jnp.searchsorted(tile_end, jnp.arange(nt_max, dtype=jnp.int32), side="right"),
                         N_EXPERTS - 1).astype(jnp.int32)
    ntiles = tile_end[-1:].astype(jnp.int32)
    xg = gather_rows(hn, row_tok)
    yg = moe_experts(xg, tile_e, ntiles, w13, w2)
    return moe_combine(h, route, yg, flat, final_w)


def _final_kernel(x_ref, w_ref, o_ref):
    o_ref[...] = _rms(x_ref[...], w_ref[...])


def final_norm(h, w, tm=512):
    m, d = h.shape
    tm = min(tm, m)
    return pl.pallas_call(
        _final_kernel,
        grid=(m // tm,),
        in_specs=[pl.BlockSpec((tm, d), lambda i: (i, 0)), pl.BlockSpec((1, d), lambda i: (0, 0))],
        out_specs=pl.BlockSpec((tm, d), lambda i: (i, 0)),
        out_shape=jax.ShapeDtypeStruct((m, d), F32),
        compiler_params=_cparams(("parallel",)),
    )(h, w.reshape(1, d).astype(F32))


def _rot_cols(w):
    half = w.shape[-1] // 2
    return jnp.concatenate([-w[..., half:], w[..., :half]], axis=-1)


_W_IN_SEGMENTS = (
    (U_ML_Q, O_ML_Q, 512, 1.0), (U_ML_K, O_ML_K, 512, 1.0), (U_ML_V, O_ML_V, 1024, 1.0),
    (U_ML_O, O_ML_O, 1024, 1.0), (U_SSM_Z, O_SSM_Z, 1024, 1.0), (U_SSM_X, O_SSM_XBC, SSM_DINNER, 1.0),
    (U_SWA_Q, O_SWA_Q, 1024, SWA_HEAD_DIM ** -0.5 * LOG2E), (U_SSM_BC, O_SSM_XBC + SSM_DINNER, SSM_BC, 1.0),
    (U_MLA_CQ, O_MLA_CQ, MLA_Q_LORA, 1.0), (U_MLA_CKV, O_MLA_CKV, MLA_KV_LORA, 1.0),
    (U_SWA_K, O_SWA_K, 256, 1.0), (U_SWA_V, O_SWA_V, 256, 1.0), (U_GATE, O_GATE, N_BRANCH * D_MODEL, 1.0))


W_IN_TILE = 256
_ROPE_TILE = U_MLA_KR // W_IN_TILE
_GATE_TILE = U_TOTAL // W_IN_TILE


def _w_in_tile_sources():
    src = [0] * (U_TOTAL // W_IN_TILE)
    for dst, s, width, _ in _W_IN_SEGMENTS:
        for off in range(0, width, W_IN_TILE):
            src[(dst + off) // W_IN_TILE] = s + off
    src[_ROPE_TILE] = O_MLA_KR
    return src


def _w_in_layout_kernel(src_ref, w_ref, dt_ref, o_ref):
    t = pl.program_id(1)

    @pl.when(t == _GATE_TILE)
    def _():
        ngate = 2 * ML_HEADS
        o_ref[...] = jnp.concatenate(
            [w_ref[0, 0:ngate, :], dt_ref[0],
             jnp.zeros((W_IN_TILE - ngate - SSM_HEADS, o_ref.shape[1]), F32)], axis=0).astype(BF)

    @pl.when(jnp.logical_and(t != _ROPE_TILE, t != _GATE_TILE))
    def _():
        swa_q = jnp.logical_and(t >= U_SWA_Q // W_IN_TILE, t < U_SSM_BC // W_IN_TILE)
        scale = jnp.where(swa_q, SWA_HEAD_DIM ** -0.5 * LOG2E, 1.0)
        o_ref[...] = (w_ref[0] * scale).astype(BF)

    @pl.when(t == _ROPE_TILE)
    def _():
        half = MLA_ROPE // 2
        o_ref[0:MLA_ROPE, :] = w_ref[0, 0:MLA_ROPE, :].astype(BF)
        o_ref[MLA_ROPE:MLA_ROPE + half, :] = (-w_ref[0, half:MLA_ROPE, :]).astype(BF)
        o_ref[MLA_ROPE + half:2 * MLA_ROPE, :] = w_ref[0, 0:half, :].astype(BF)
        o_ref[2 * MLA_ROPE:, :] = jnp.zeros((W_IN_TILE - 2 * MLA_ROPE, o_ref.shape[1]), BF)


def layout_w_in(w_in):
    depth, d, _ = w_in.shape
    wt = jnp.swapaxes(w_in, 1, 2)
    src = jnp.asarray([s // SUBLANES for s in _w_in_tile_sources() + [O_ML_I]], jnp.int32)
    return pl.pallas_call(
        _w_in_layout_kernel,
        grid_spec=pltpu.PrefetchScalarGridSpec(
            num_scalar_prefetch=1, grid=(depth, _GATE_TILE + 1),
            in_specs=[pl.BlockSpec((pl.Element(1), pl.Element(W_IN_TILE), pl.Element(d)),
                                   lambda l, t, src_ref: (l, src_ref[t] * SUBLANES, 0)),
                      pl.BlockSpec((pl.Element(1), pl.Element(SSM_HEADS), pl.Element(d)),
                                   lambda l, t, src_ref: (l, O_SSM_DT, 0))],
            out_specs=pl.BlockSpec((None, W_IN_TILE, d), lambda l, t, src_ref: (l, t, 0))),
        out_shape=jax.ShapeDtypeStruct((depth, U_TOTAL + W_IN_TILE, d), BF),
        compiler_params=_cparams(("parallel", "parallel")),
    )(src, wt, wt)


def _layout_w_uq(w):
    k = w.shape[0]
    w = w.reshape(k, MLA_HEADS, MLA_NOPE + MLA_ROPE)
    rope = w[..., MLA_NOPE:]
    scale = (MLA_NOPE + MLA_ROPE) ** -0.5 * LOG2E
    return (jnp.concatenate([w, _rot_cols(rope)], axis=-1).reshape(k, -1) * scale).astype(BF)


def _rope_table(seq):
    inv_freq = 1.0 / (ROPE_THETA ** (jnp.arange(0, MLA_ROPE, 2, dtype=F32) / MLA_ROPE))
    ang = jnp.arange(seq, dtype=F32)[:, None] * inv_freq[None, :]
    c, s = jnp.cos(ang), jnp.sin(ang)
    return jnp.concatenate([c, c, s, s], axis=1)


def hybrid_mixer(h, bsz, seq, layer, norm_w, w_main, ml_ib, ml_fb, ml_norm, conv_w, conv_b, dt_bias, a_log,
                 ssm_d, ssm_norm, q_norm, w_uq, kv_norm, w_ukv, sinks, w_branch, w_out, cs):
    n = bsz * seq
    nc = seq // CHUNK
    u2, small = matmul(h, w_main, norm=norm_w, out_dtype=BF, w_layer=layer, w_t=True, n=U_TOTAL, tn=2048,
                       side_w=w_main, side_rows=(U_TOTAL // LANES, LANES))
    u3 = u2.reshape(bsz, seq, U_TOTAL)

    def to_rows(cols, heads):
        return cols.reshape(bsz, seq, heads).transpose(0, 2, 1).reshape(bsz * heads * nc, CHUNK)

    def per_row(vec, heads):
        return jnp.broadcast_to(vec.astype(F32)[None, :, None], (bsz, heads, nc)).reshape(-1, 1)

    ig, bcum, dt, acs = recurrence_gates(
        to_rows(small[:, 0:ML_HEADS], ML_HEADS), to_rows(small[:, ML_HEADS:2 * ML_HEADS], ML_HEADS),
        per_row(ml_ib, ML_HEADS), per_row(ml_fb, ML_HEADS),
        to_rows(small[:, 2 * ML_HEADS:2 * ML_HEADS + SSM_HEADS], SSM_HEADS),
        per_row(dt_bias, SSM_HEADS), per_row(a_log, SSM_HEADS))

    def as_rows(x, heads):
        return x.reshape(bsz, heads, nc, CHUNK).transpose(0, 2, 1, 3)

    def as_cols(x, heads):
        return x.reshape(bsz, heads, seq).transpose(0, 2, 1)

    ya = mlstm_branch(u3, as_rows(ig, ML_HEADS), as_rows(bcum, ML_HEADS), as_cols(bcum, ML_HEADS), ml_norm)
    yb = ssd_branch(u3, as_cols(dt, SSM_HEADS), as_cols(acs, SSM_HEADS), as_rows(acs, SSM_HEADS), conv_w, conv_b,
                    jnp.repeat(ssm_d, SSM_HEADDIM), ssm_norm)
    qf = matmul(u2, w_uq, norm=q_norm, x_col_blk=U_MLA_CQ // MLA_Q_LORA)
    kvf = matmul(u2, w_ukv, norm=kv_norm, x_col_blk=U_MLA_CKV // MLA_KV_LORA)
    yc = mla_attention(qf.reshape(bsz, seq, -1), kvf.reshape(bsz, seq, -1), u3, cs)
    yd = swa_branch(u3, sinks)
    merged = gated_merge(ya.reshape(n, -1), yb.reshape(n, -1), yc.reshape(n, -1), yd.reshape(n, -1), w_branch, u2)
    return matmul(merged, w_out, residual=h, out_dtype=F32)


def kernel(x, mem, norm_mix, w_in, ml_igate_bias, ml_fgate_bias, ml_norm, ssm_conv_w, ssm_conv_b, ssm_dt_bias, ssm_a_log, ssm_d, ssm_norm, mla_q_norm, mla_w_uq, mla_kv_norm, mla_w_ukv, swa_sinks, w_branch, w_out, norm_cross, norm_mem, xa_wq, xa_wkv, xa_wo, norm_ffn, ffn_w13, ffn_w2, moe_router, moe_w13, moe_w2, norm_final):
    bsz, seq, d = x.shape
    depth = w_in.shape[0]
    n = bsz * seq
    mlen = mem.shape[1]
    cs = _rope_table(seq)
    h = x.reshape(n, d)
    mem2 = mem.reshape(bsz * mlen, d)
    w_main = layout_w_in(w_in)
    for l in range(depth):
        h = hybrid_mixer(h, bsz, seq, l, norm_mix[l], w_main, ml_igate_bias[l], ml_fgate_bias[l], ml_norm[l],
                         ssm_conv_w[l], ssm_conv_b[l], ssm_dt_bias[l], ssm_a_log[l], ssm_d[l], ssm_norm[l],
                         mla_q_norm[l], _layout_w_uq(mla_w_uq[l]), mla_kv_norm[l], mla_w_ukv[l].astype(BF),
                         swa_sinks[l], w_branch[l].astype(BF), w_out[l].astype(BF), cs)
        kv = matmul(mem2, xa_wkv[l].astype(BF), norm=norm_mem[l])
        wq = (xa_wq[l] * (XA_HEAD_DIM ** -0.5 * LOG2E)).astype(BF)
        h = cross_attention(h, seq, norm_cross[l], wq, kv.reshape(bsz, mlen, -1), xa_wo[l].astype(BF))
        if l % 2 == 0:
            h = dense_ffn(h, norm_ffn[l], ffn_w13[l // 2].astype(BF), ffn_w2[l // 2].astype(BF))
        else:
            h = moe_ffn(h, norm_ffn[l], moe_router[l // 2], moe_w13[l // 2], moe_w2[l // 2],
                        final_w=norm_final if l == depth - 1 else None)
    if depth % 2 == 1:
        h = final_norm(h, norm_final)
    return h.reshape(bsz, seq, d)
```

```python
import functools
import math

import jax
import jax.numpy as jnp
from jax import lax
from jax.experimental import pallas as pl
from jax.experimental.pallas import tpu as pltpu

F32 = jnp.float32
BF = jnp.bfloat16

D_MODEL = 2048
RMS_EPS = 1e-6
ML_HEADS, ML_DQK, ML_DV = 4, 128, 256
SSM_HEADS, SSM_HEADDIM, SSM_GROUPS, SSM_STATE, SSM_CONV = 16, 64, 2, 128, 4
SSM_DINNER = SSM_HEADS * SSM_HEADDIM
SSM_BC = 2 * SSM_GROUPS * SSM_STATE
MLA_HEADS, MLA_Q_LORA, MLA_KV_LORA, MLA_NOPE, MLA_ROPE, MLA_V = 8, 512, 256, 128, 64, 128
ROPE_THETA = 10000.0
SWA_HEADS, SWA_KV_HEADS, SWA_HEAD_DIM, SWA_WINDOW = 16, 4, 64, 128
N_BRANCH, BRANCH_W = 4, 1024
XA_HEADS, XA_HEAD_DIM = 4, 128
FFN_DIM = 7168
N_EXPERTS, TOP_K = 8, 2

CHUNK = 128
LANES = 128
SUBLANES = 8
VMEM_LIMIT = 56 * 1024 * 1024
VMEM_LIMIT_BIG = 60 * 1024 * 1024
LOG2E = math.log2(math.e)

U_ML_Q, U_ML_K, U_ML_V, U_ML_O = 0, 512, 1024, 2048
U_SSM_Z, U_SSM_X, U_SWA_Q, U_SSM_BC = 3072, 4096, 5120, 6144
U_MLA_CQ, U_MLA_CKV, U_SWA_K, U_SWA_V, U_MLA_KR = 6656, 7168, 7424, 7680, 7936
U_GATE = 8192
U_TOTAL = U_GATE + N_BRANCH * D_MODEL
_SPLITS = (512, 512, 1024, 1024, 4, 4, 1024, 1536, 16, 512, 256, 64, 1024, 256, 256, 8192)
_OFF = [0]
for _s in _SPLITS:
    _OFF.append(_OFF[-1] + _s)
(O_ML_Q, O_ML_K, O_ML_V, O_ML_O, O_ML_I, O_ML_F, O_SSM_Z, O_SSM_XBC, O_SSM_DT, O_MLA_CQ, O_MLA_CKV,
 O_MLA_KR, O_SWA_Q, O_SWA_K, O_SWA_V, O_GATE, _O_END) = _OFF


def _cparams(sem, vmem_limit=VMEM_LIMIT):
    return pltpu.CompilerParams(dimension_semantics=sem, vmem_limit_bytes=vmem_limit)


def _dot(a, b):
    return jnp.dot(a, b, preferred_element_type=F32)


def _dot_nt(a, b):
    return lax.dot_general(a, b, (((1,), (1,)), ((), ())), preferred_element_type=F32)


def _dot_tn(a, b):
    return lax.dot_general(a, b, (((0,), (0,)), ((), ())), preferred_element_type=F32)


def _split3(a):
    a1 = a.astype(BF)
    r = a - a1.astype(F32)
    a2 = r.astype(BF)
    a3 = (r - a2.astype(F32)).astype(BF)
    return a1, a2, a3


def _dot_sel(a, sel):
    a1, a2, a3 = _split3(a)
    return _dot(a1, sel) + _dot(a2, sel) + _dot(a3, sel)


def _rms(x, w):
    return x * lax.rsqrt(jnp.mean(x * x, axis=-1, keepdims=True) + RMS_EPS) * w


def _sigmoid(x):
    return 1.0 / (1.0 + jnp.exp(-x))


def _mm_kernel(*refs, has_norm, has_res, has_side, w_t):
    mm = _dot_nt if w_t else _dot
    it = iter(refs)
    x_ref = next(it)
    g_ref = next(it) if has_norm else None
    w_ref = next(it)
    r_ref = next(it) if has_res else None
    ws_ref = next(it) if has_side else None
    o_ref = next(it)
    os_ref = next(it) if has_side else None
    if has_norm:
        xn_ref = next(it)

        @pl.when(pl.program_id(1) == 0)
        def _():
            xn = _rms(x_ref[...].astype(F32), g_ref[...]).astype(BF)
            xn_ref[...] = xn
            if has_side:
                os_ref[...] = mm(xn, ws_ref[...])

        xv = xn_ref[...]
    else:
        xv = x_ref[...]
    acc = mm(xv, w_ref[...])
    if has_res:
        acc = acc + r_ref[...]
    o_ref[...] = acc.astype(o_ref.dtype)


def matmul(x, w, *, norm=None, residual=None, out_dtype=None, tm=1024, tn=1024, x_col_blk=0, w_layer=None,
           side_w=None, side_rows=None, w_t=False, n=None):
    out_dtype = out_dtype or BF
    m = x.shape[0]
    k, n_all = w.shape[-2:][::-1] if w_t else w.shape[-2:]
    n = n or n_all
    tm, tn = min(tm, m), min(tn, n)
    assert m % tm == 0 and n % tn == 0
    in_specs = [pl.BlockSpec((tm, k), lambda i, j: (i, x_col_blk))]
    args = [x]
    scratch = []
    if norm is not None:
        in_specs.append(pl.BlockSpec((1, k), lambda i, j: (0, 0)))
        args.append(norm.reshape(1, k).astype(F32))
        scratch.append(pltpu.VMEM((tm, k), BF))
    if w_layer is None:
        in_specs.append(pl.BlockSpec((k, tn), lambda i, j: (0, j)))
    else:
        in_specs.append(pl.BlockSpec((None, tn, k), lambda i, j: (w_layer, j, 0)) if w_t else
                        pl.BlockSpec((None, k, tn), lambda i, j: (w_layer, 0, j)))
    args.append(w)
    if residual is not None:
        in_specs.append(pl.BlockSpec((tm, tn), lambda i, j: (i, j)))
        args.append(residual)
    out_specs = pl.BlockSpec((tm, tn), lambda i, j: (i, j))
    out_shape = jax.ShapeDtypeStruct((m, n), out_dtype)
    if side_w is not None:
        assert norm is not None and w_layer is not None
        if side_rows is None:
            ns = side_w.shape[-2] if w_t else side_w.shape[-1]
            in_specs.append(pl.BlockSpec((None,) + side_w.shape[1:], lambda i, j: (w_layer, 0, 0)))
        else:
            blk, ns = side_rows
            in_specs.append(pl.BlockSpec((None, ns, k), lambda i, j: (w_layer, blk, 0)))
        args.append(side_w)
        out_specs = [out_specs, pl.BlockSpec((tm, ns), lambda i, j: (i, 0))]
        out_shape = [out_shape, jax.ShapeDtypeStruct((m, ns), F32)]
    return pl.pallas_call(
        functools.partial(_mm_kernel, has_norm=norm is not None, has_res=residual is not None,
                          has_side=side_w is not None, w_t=w_t),
        grid=(m // tm, n // tn),
        in_specs=in_specs,
        out_specs=out_specs,
        out_shape=out_shape,
        scratch_shapes=scratch,
        compiler_params=_cparams(("parallel", "arbitrary")),
    )(*args)


def _cumsum_lanes(x):
    lane = lax.broadcasted_iota(jnp.int32, x.shape, 1)
    s = 1
    while s < x.shape[1]:
        x = x + jnp.where(lane >= s, pltpu.roll(x, s, axis=1), 0.0)
        s *= 2
    return x


def _softplus(x):
    return jnp.maximum(x, 0.0) + jnp.log(1.0 + jnp.exp(-jnp.abs(x)))


def _gates_kernel(i_ref, f_ref, ib_ref, fb_ref, dt_ref, dtb_ref, alog_ref, ig_ref, b_ref, dto_ref, acs_ref):
    ig_ref[...] = i_ref[...] + ib_ref[...]
    b_ref[...] = _cumsum_lanes(-_softplus(-(f_ref[...] + fb_ref[...])))
    dt = _softplus(dt_ref[...] + dtb_ref[...])
    dto_ref[...] = dt
    acs_ref[...] = _cumsum_lanes(dt * (-jnp.exp(alog_ref[...])))


def recurrence_gates(i_rows, f_rows, ib, fb, dt_rows, dtb, alog):
    r1, r2 = i_rows.shape[0], dt_rows.shape[0]
    shp = lambda r: jax.ShapeDtypeStruct((r, CHUNK), F32)
    return pl.pallas_call(
        _gates_kernel,
        out_shape=(shp(r1), shp(r1), shp(r2), shp(r2)),
    )(i_rows, f_rows, ib, fb, dt_rows, dtb, alog)


def _mlstm_kernel(q_ref, k_ref, v_ref, o_ref, igr_ref, br_ref, bc_ref, nw_ref, y_ref, *state):
    ct_ref, n_ref, m_ref = state[0::3], state[1::3], state[2::3]
    L = CHUNK

    @pl.when(pl.program_id(1) == 0)
    def _():
        for ref in state:
            ref[...] = jnp.zeros_like(ref)

    row = lax.broadcasted_iota(jnp.int32, (L, L), 0)
    col = lax.broadcasted_iota(jnp.int32, (L, L), 1)
    causal = col <= row
    diag = col == row
    scale = ML_DQK ** -0.5
    outs = []
    for h in range(ML_HEADS):
        q = q_ref[:, h * ML_DQK:(h + 1) * ML_DQK]
        k = (k_ref[:, h * ML_DQK:(h + 1) * ML_DQK].astype(F32) * scale).astype(BF)
        v = v_ref[:, h * ML_DV:(h + 1) * ML_DV]
        bcol = bc_ref[:, h:h + 1]
        brow = br_ref[h:h + 1, :]
        igrow = igr_ref[h:h + 1, :]
        m_prev = m_ref[h][0:1, 0:1]
        n_prev = n_ref[h][0:1, :]
        ct_prev = ct_ref[h][...]

        dmat = jnp.where(causal, bcol - brow + igrow, -jnp.inf)
        m_intra = jnp.max(dmat, axis=1, keepdims=True)
        g = bcol + m_prev
        m_s = jnp.maximum(g, m_intra)
        p = jnp.exp(dmat - m_s) * _dot_nt(q, k)
        inter = jnp.exp(g - m_s)
        num = _dot(p.astype(BF), v) + inter * _dot(q, ct_prev.astype(BF))
        den = jnp.sum(p, axis=1, keepdims=True) + inter * jnp.sum(q.astype(F32) * n_prev, axis=1, keepdims=True)
        hh = num / jnp.maximum(jnp.abs(den), jnp.exp(-m_s))
        hn = _rms(hh, nw_ref[:, h * ML_DV:(h + 1) * ML_DV])
        outs.append(_sigmoid(o_ref[:, h * ML_DV:(h + 1) * ML_DV].astype(F32)) * hn)

        b_tot = brow[:, L - 1:L]
        a = b_tot - brow + igrow
        m_loc = jnp.max(a, axis=1, keepdims=True)
        wl = jnp.exp(a - m_loc)
        kw = _dot(jnp.where(diag, wl, 0.0).astype(BF), k)
        m_new = jnp.maximum(b_tot + m_prev, m_loc)
        da = jnp.exp(b_tot + m_prev - m_new)
        db = jnp.exp(m_loc - m_new)
        ct_ref[h][...] = da * ct_prev + db * _dot_tn(kw.astype(BF), v)
        n_ref[h][...] = jnp.broadcast_to(da * n_prev + db * jnp.sum(kw, axis=0, keepdims=True), n_ref[h].shape)
        m_ref[h][...] = jnp.broadcast_to(m_new, m_ref[h].shape)
    y_ref[...] = jnp.concatenate(outs, axis=1).astype(y_ref.dtype)


def mlstm_branch(u3, ig_rows, b_rows, b_cols, norm_w):
    bsz, seq, _ = u3.shape
    nc = seq // CHUNK
    L = CHUNK
    ublk = lambda width, off: pl.BlockSpec((None, L, width), lambda b, c: (b, c, off // width))
    rows = pl.BlockSpec((None, None, ML_HEADS, L), lambda b, c: (b, c, 0, 0))
    return pl.pallas_call(
        _mlstm_kernel,
        grid=(bsz, nc),
        in_specs=[ublk(512, U_ML_Q), ublk(512, U_ML_K), ublk(1024, U_ML_V), ublk(1024, U_ML_O), rows, rows,
                  pl.BlockSpec((None, L, ML_HEADS), lambda b, c: (b, c, 0)),
                  pl.BlockSpec((1, ML_HEADS * ML_DV), lambda b, c: (0, 0))],
        out_specs=pl.BlockSpec((None, L, BRANCH_W), lambda b, c: (b, c, 0)),
        out_shape=jax.ShapeDtypeStruct((bsz, seq, BRANCH_W), BF),
        scratch_shapes=[pltpu.VMEM((ML_DQK, ML_DV), F32), pltpu.VMEM((8, ML_DQK), F32),
                        pltpu.VMEM((8, LANES), F32)] * ML_HEADS,
        compiler_params=_cparams(("parallel", "arbitrary")),
    )(u3, u3, u3, u3, ig_rows, b_rows, b_cols, norm_w.reshape(1, -1).astype(F32))


def _ssd_kernel(z_ref, x_ref, bc_ref, dt_ref, ac_ref, ar_ref, cw_ref, cb_ref, d_ref, nw_ref, y_ref, xs_ref, st_ref):
    L = CHUNK
    P, R, NS = SSM_HEADDIM, SSM_HEADS // SSM_GROUPS, SSM_STATE
    GW = R * P

    @pl.when(pl.program_id(1) == 0)
    def _():
        xs_ref[0:8, :] = jnp.zeros((8, xs_ref.shape[1]), F32)
        st_ref[...] = jnp.zeros_like(st_ref)

    xs_ref[8:, :] = jnp.concatenate([x_ref[...], bc_ref[...]], axis=1).astype(F32)
    conv = cb_ref[...] + cw_ref[SSM_CONV - 1:SSM_CONV, :] * xs_ref[8:8 + L, :]
    for sft in range(1, SSM_CONV):
        conv = conv + cw_ref[SSM_CONV - 1 - sft:SSM_CONV - sft, :] * xs_ref[8 - sft:8 - sft + L, :]
    xs_ref[0:8, :] = xs_ref[L:L + 8, :]
    xbc = conv * _sigmoid(conv)
    xh = xbc[:, :SSM_DINNER]
    bmat = xbc[:, SSM_DINNER:SSM_DINNER + SSM_GROUPS * NS].astype(BF)
    cmat = xbc[:, SSM_DINNER + SSM_GROUPS * NS:].astype(BF)

    dtc = dt_ref[...]
    ac = ac_ref[...]
    ar = ar_ref[...]
    a_last = ac[L - 1:L, :]
    hsel = (lax.broadcasted_iota(jnp.int32, (SSM_HEADS, SSM_DINNER), 1) // P
            == lax.broadcasted_iota(jnp.int32, (SSM_HEADS, SSM_DINNER), 0))
    expand = jnp.where(hsel, 1.0, 0.0).astype(BF)
    stack = jnp.concatenate([dtc, jnp.exp(a_last - ac), jnp.exp(ac),
                             jnp.broadcast_to(jnp.exp(a_last), (8, SSM_HEADS))], axis=0)
    ex = _dot_sel(stack, expand)
    dt_full, dst_full, ind_full = ex[0:L], ex[L:2 * L], ex[2 * L:3 * L]
    cdec_full = ex[3 * L:3 * L + 1]
    xdt = xh * dt_full
    xdt_b = xdt.astype(BF)
    xw_b = (xdt * dst_full).astype(BF)

    row = lax.broadcasted_iota(jnp.int32, (L, L), 0)
    col = lax.broadcasted_iota(jnp.int32, (L, L), 1)
    causal = col <= row
    ys = []
    for g in range(SSM_GROUPS):
        bg = bmat[:, g * NS:(g + 1) * NS]
        cg = cmat[:, g * NS:(g + 1) * NS]
        cb = _dot_nt(cg, bg)
        st_prev = st_ref[g]
        yoff = _dot(cg, st_prev.astype(BF))
        st_ref[g] = cdec_full[:, g * GW:(g + 1) * GW] * st_prev + _dot_tn(bg, xw_b[:, g * GW:(g + 1) * GW])
        for r in range(R):
            h = g * R + r
            dec = jnp.exp(jnp.where(causal, ac[:, h:h + 1] - ar[h:h + 1, :], -jnp.inf))
            yd = _dot((dec * cb).astype(BF), xdt_b[:, h * P:(h + 1) * P])
            ys.append(yd + yoff[:, r * P:(r + 1) * P] * ind_full[:, h * P:(h + 1) * P])
    y = jnp.concatenate(ys, axis=1) + xh * d_ref[...]
    zf = z_ref[...].astype(F32)
    y = y * (zf * _sigmoid(zf))
    y_ref[...] = jnp.concatenate(
        [_rms(y[:, g * GW:(g + 1) * GW], nw_ref[:, g * GW:(g + 1) * GW]) for g in range(SSM_GROUPS)],
        axis=1).astype(y_ref.dtype)


def ssd_branch(u3, dt_cols, acs_cols, acs_rows, conv_w, conv_b, d_full, norm_w):
    bsz, seq, _ = u3.shape
    nc = seq // CHUNK
    L = CHUNK
    ublk = lambda width, off: pl.BlockSpec((None, L, width), lambda b, c: (b, c, off // width))
    cols = pl.BlockSpec((None, L, SSM_HEADS), lambda b, c: (b, c, 0))
    const = lambda shape: pl.BlockSpec(shape, lambda b, c: (0, 0))
    cch = SSM_DINNER + SSM_BC
    return pl.pallas_call(
        _ssd_kernel,
        grid=(bsz, nc),
        in_specs=[ublk(1024, U_SSM_Z), ublk(1024, U_SSM_X), ublk(512, U_SSM_BC), cols, cols,
                  pl.BlockSpec((None, None, SSM_HEADS, L), lambda b, c: (b, c, 0, 0)),
                  const((SSM_CONV, cch)), const((1, cch)), const((1, SSM_DINNER)), const((1, SSM_DINNER))],
        out_specs=pl.BlockSpec((None, L, BRANCH_W), lambda b, c: (b, c, 0)),
        out_shape=jax.ShapeDtypeStruct((bsz, seq, BRANCH_W), BF),
        scratch_shapes=[pltpu.VMEM((L + 8, cch), F32),
                        pltpu.VMEM((SSM_GROUPS, SSM_STATE, SSM_DINNER // SSM_GROUPS), F32)],
        compiler_params=_cparams(("parallel", "arbitrary")),
    )(u3, u3, u3, dt_cols, acs_cols, acs_rows, conv_w.astype(F32), conv_b.reshape(1, cch).astype(F32),
      d_full.reshape(1, -1).astype(F32), norm_w.reshape(1, -1).astype(F32))


def _rope128(x, cs):
    t = x.astype(F32) * cs
    return t + pltpu.roll(t, MLA_ROPE, axis=1)


def _mla_kernel(q_ref, csq_ref, kv_ref, kr_ref, csk_ref, o_ref, qe_s, *stats, tq):
    qi = pl.program_id(1)
    ki = pl.program_id(2)
    qw = MLA_NOPE + LANES
    kw = MLA_NOPE + MLA_V
    m_s, l_s, acc_s = stats[0::3], stats[1::3], stats[2::3]
    nt = tq // LANES

    @pl.when(ki == 0)
    def _():
        for h in range(MLA_HEADS):
            qr = _rope128(q_ref[:, h * qw + MLA_NOPE:(h + 1) * qw], csq_ref[...]).astype(BF)
            qe_s[h] = jnp.concatenate([q_ref[:, h * qw:h * qw + MLA_NOPE], qr], axis=1)
            m_s[h][...] = jnp.full((tq, LANES), -jnp.inf, F32)
            l_s[h][...] = jnp.zeros((tq, LANES), F32)
            acc_s[h][...] = jnp.zeros((tq, MLA_V), F32)

    def step(masked):
        lane = lax.broadcasted_iota(jnp.int32, (tq, LANES), 1)
        kr = jnp.where(lane < MLA_ROPE, _rope128(kr_ref[...], csk_ref[...]), 0.0).astype(BF)
        if masked:
            causal = (lax.broadcasted_iota(jnp.int32, (tq, tq), 1) <= lax.broadcasted_iota(jnp.int32, (tq, tq), 0))
        for h in range(MLA_HEADS):
            ke = jnp.concatenate([kv_ref[:, h * kw:h * kw + MLA_NOPE], kr], axis=1)
            s = _dot_nt(qe_s[h], ke)
            if masked:
                s = jnp.where(causal, s, -jnp.inf)
            m_old = m_s[h][...]
            m_new = jnp.maximum(m_old, jnp.max(s, axis=1, keepdims=True))
            alpha = jnp.exp2(m_old - m_new)
            p = jnp.exp2(s - jnp.concatenate([m_new] * nt, axis=1))
            psum = p[:, 0:LANES]
            for t in range(1, nt):
                psum = psum + p[:, t * LANES:(t + 1) * LANES]
            l_s[h][...] = alpha * l_s[h][...] + psum
            acc_s[h][...] = alpha * acc_s[h][...] + _dot(p.astype(BF), kv_ref[:, h * kw + MLA_NOPE:(h + 1) * kw])
            m_s[h][...] = m_new

    @pl.when(ki < qi)
    def _():
        step(False)

    @pl.when(ki == qi)
    def _():
        step(True)
        for h in range(MLA_HEADS):
            l = jnp.sum(l_s[h][...], axis=1, keepdims=True)
            o_ref[:, h * MLA_V:(h + 1) * MLA_V] = (acc_s[h][...] / l).astype(o_ref.dtype)


def mla_attention(qf3, kvf3, u3, cs, tq=1024):
    bsz, seq, _ = qf3.shape
    tq = min(tq, seq)
    nq = seq // tq
    kvi = lambda b, qi, ki: jnp.minimum(ki, qi)
    return pl.pallas_call(
        functools.partial(_mla_kernel, tq=tq),
        grid=(bsz, nq, nq),
        in_specs=[pl.BlockSpec((None, tq, qf3.shape[2]), lambda b, qi, ki: (b, qi, 0)),
                  pl.BlockSpec((tq, LANES), lambda b, qi, ki: (qi, 0)),
                  pl.BlockSpec((None, tq, kvf3.shape[2]), lambda b, qi, ki: (b, kvi(b, qi, ki), 0)),
                  pl.BlockSpec((None, tq, LANES), lambda b, qi, ki: (b, kvi(b, qi, ki), U_MLA_KR // LANES)),
                  pl.BlockSpec((tq, LANES), lambda b, qi, ki: (kvi(b, qi, ki), 0))],
        out_specs=pl.BlockSpec((None, tq, MLA_HEADS * MLA_V), lambda b, qi, ki: (b, qi, 0)),
        out_shape=jax.ShapeDtypeStruct((bsz, seq, MLA_HEADS * MLA_V), BF),
        scratch_shapes=[pltpu.VMEM((MLA_HEADS, tq, MLA_NOPE + LANES), BF)]
        + [pltpu.VMEM((tq, LANES), F32), pltpu.VMEM((tq, LANES), F32), pltpu.VMEM((tq, MLA_V), F32)] * MLA_HEADS,
        compiler_params=_cparams(("parallel", "parallel", "arbitrary")),
    )(qf3, cs, kvf3, u3, cs)


SWA_WPS = 2


def _swa_kernel(q_ref, kc_ref, kp_ref, vc_ref, vp_ref, sink_ref, o_ref):
    W = SWA_WINDOW
    first_step = pl.program_id(1) == 0
    for w in range(SWA_WPS):
        rows = slice(w * W, (w + 1) * W)
        if w == 0:
            k_prev, v_prev, has_prev = kp_ref[...], vp_ref[...], jnp.logical_not(first_step)
        else:
            before = slice((w - 1) * W, w * W)
            k_prev, v_prev, has_prev = kc_ref[before, :], vc_ref[before, :], True
        _swa_window(q_ref[rows, :], k_prev, kc_ref[rows, :], v_prev, vc_ref[rows, :], has_prev, sink_ref, o_ref, rows)


def _swa_window(q, k_prev, k_cur, v_prev, v_cur, has_prev, sink_ref, o_ref, rows):
    W, d = SWA_WINDOW, SWA_HEAD_DIM
    i = lax.broadcasted_iota(jnp.int32, (2 * W, 2 * W), 0) & (W - 1)
    j = lax.broadcasted_iota(jnp.int32, (2 * W, 2 * W), 1)
    valid = (j > i) & (j <= i + W) & (has_prev | (j >= W))
    first_tile = lax.broadcasted_iota(jnp.int32, (2 * W, LANES), 0) < W
    lower = lax.broadcasted_iota(jnp.int32, (2 * W, LANES), 1) < d
    ones = jnp.ones((2 * W, LANES), BF)
    for pair in range(SWA_KV_HEADS // 2):
        sl = slice(pair * LANES, (pair + 1) * LANES)
        kt = jnp.concatenate([k_prev[:, sl], k_cur[:, sl]], axis=0)
        vt = jnp.concatenate([v_prev[:, sl], v_cur[:, sl]], axis=0)
        kt_sw = pltpu.roll(kt.astype(F32), d, axis=1).astype(BF)
        vt_sw = pltpu.roll(vt.astype(F32), d, axis=1).astype(BF)
        for e in range(2):
            kh = 2 * pair + e
            k_lo = jnp.where(lower, kt if e == 0 else kt_sw, jnp.zeros_like(kt))
            k_hi = jnp.where(lower, jnp.zeros_like(kt), kt_sw if e == 0 else kt)
            v_lo, v_hi = (vt, vt_sw) if e == 0 else (vt_sw, vt)
            qs = jnp.concatenate([q[:, 2 * kh * LANES:(2 * kh + 1) * LANES],
                                  q[:, (2 * kh + 1) * LANES:(2 * kh + 2) * LANES]], axis=0)
            halves = []
            for half, (ke, ve) in enumerate(((k_lo, v_lo), (k_hi, v_hi))):
                ha, hb = 4 * kh + half, 4 * kh + 2 + half
                s = jnp.where(valid, _dot_nt(qs, ke), -jnp.inf)
                sink = jnp.where(first_tile, sink_ref[0:1, ha:ha + 1], sink_ref[0:1, hb:hb + 1]) * LOG2E
                m = jnp.maximum(jnp.max(s, axis=1, keepdims=True), sink)
                p = jnp.exp2(s - jnp.concatenate([m, m], axis=1)).astype(BF)
                den = _dot(p, ones) + jnp.exp2(sink - m)
                halves.append(_dot(p, ve) / den)
            ot = jnp.where(lower, halves[0], halves[1]).astype(o_ref.dtype)
            o_ref[rows, 2 * kh * LANES:(2 * kh + 1) * LANES] = ot[:W]
            o_ref[rows, (2 * kh + 1) * LANES:(2 * kh + 2) * LANES] = ot[W:]


def swa_branch(u3, sinks):
    bsz, seq, _ = u3.shape
    W = SWA_WINDOW
    kvw = SWA_KV_HEADS * SWA_HEAD_DIM
    rows = SWA_WPS * W
    assert seq % rows == 0
    cur = lambda width, off: pl.BlockSpec((None, rows, width), lambda b, n: (b, n, off // width))
    prev = lambda width, off: pl.BlockSpec((None, W, width),
                                           lambda b, n: (b, jnp.maximum(n * SWA_WPS - 1, 0), off // width))
    return pl.pallas_call(
        _swa_kernel,
        grid=(bsz, seq // rows),
        in_specs=[cur(1024, U_SWA_Q), cur(kvw, U_SWA_K), prev(kvw, U_SWA_K), cur(kvw, U_SWA_V), prev(kvw, U_SWA_V),
                  pl.BlockSpec((1, SWA_HEADS), lambda b, n: (0, 0))],
        out_specs=pl.BlockSpec((None, rows, BRANCH_W), lambda b, n: (b, n, 0)),
        out_shape=jax.ShapeDtypeStruct((bsz, seq, BRANCH_W), BF),
        compiler_params=_cparams(("parallel", "parallel")),
    )(u3, u3, u3, u3, u3, sinks.reshape(1, -1).astype(F32))


def _merge_kernel(ya_ref, yb_ref, yc_ref, yd_ref, w_ref, g0_ref, g1_ref, g2_ref, g3_ref, o_ref):
    acc = None
    for n, (y_ref, g_ref) in enumerate(((ya_ref, g0_ref), (yb_ref, g1_ref), (yc_ref, g2_ref), (yd_ref, g3_ref))):
        t = _sigmoid(g_ref[...].astype(F32)) * _dot(y_ref[...], w_ref[n])
        acc = t if acc is None else acc + t
    o_ref[...] = acc.astype(o_ref.dtype)


def gated_merge(ya, yb, yc, yd, w_branch, u2, tm=1024, tn=512):
    m = ya.shape[0]
    tm = min(tm, m)
    ysp = pl.BlockSpec((tm, BRANCH_W), lambda i, j: (i, 0))
    gsp = lambda n: pl.BlockSpec((tm, tn), lambda i, j: (i, (U_GATE + n * D_MODEL) // tn + j))
    return pl.pallas_call(
        _merge_kernel,
        grid=(m // tm, D_MODEL // tn),
        in_specs=[ysp, ysp, ysp, ysp, pl.BlockSpec((N_BRANCH, BRANCH_W, tn), lambda i, j: (0, 0, j)),
                  gsp(0), gsp(1), gsp(2), gsp(3)],
        out_specs=pl.BlockSpec((tm, tn), lambda i, j: (i, j)),
        out_shape=jax.ShapeDtypeStruct((m, D_MODEL), BF),
        compiler_params=_cparams(("parallel", "parallel")),
    )(ya, yb, yc, yd, w_branch, u2, u2, u2, u2)


def _xattn_kernel(h_ref, g_ref, wq_ref, kv_ref, wo_ref, o_ref):
    d = XA_HEAD_DIM
    h = h_ref[...]
    q = _dot(_rms(h, g_ref[...]).astype(BF), wq_ref[...]).astype(BF)
    ones = jnp.ones((kv_ref.shape[0], LANES), BF)
    outs = []
    for hd in range(XA_HEADS):
        s = _dot_nt(q[:, hd * d:(hd + 1) * d], kv_ref[:, hd * d:(hd + 1) * d])
        p = jnp.exp2(s - jnp.max(s, axis=1, keepdims=True)).astype(BF)
        den = _dot(p, ones)
        outs.append((_dot(p, kv_ref[:, (XA_HEADS + hd) * d:(XA_HEADS + hd + 1) * d]) / den).astype(BF))
    o_ref[...] = h + _dot(jnp.concatenate(outs, axis=1), wo_ref[...])


def cross_attention(h, seq, norm_w, wq, kv3, wo, tq=512):
    n, dm = h.shape
    _, mlen, kvw = kv3.shape
    tq = min(tq, seq)
    per_b = seq // tq
    const = lambda shape: pl.BlockSpec(shape, lambda i: (0, 0))
    return pl.pallas_call(
        _xattn_kernel,
        grid=(n // tq,),
        in_specs=[pl.BlockSpec((tq, dm), lambda i: (i, 0)), const((1, dm)), const(wq.shape),
                  pl.BlockSpec((None, mlen, kvw), lambda i: (i // per_b, 0, 0)), const(wo.shape)],
        out_specs=pl.BlockSpec((tq, dm), lambda i: (i, 0)),
        out_shape=jax.ShapeDtypeStruct((n, dm), F32),
        compiler_params=_cparams(("parallel",)),
    )(h, norm_w.reshape(1, dm).astype(F32), wq, kv3, wo)


def _ffn_kernel(h_ref, g_ref, w1_ref, w3_ref, w2_ref, o_ref, xn_ref, acc_ref):
    j = pl.program_id(1)

    @pl.when(j == 0)
    def _():
        hv = h_ref[...]
        xn_ref[...] = _rms(hv, g_ref[...]).astype(BF)
        acc_ref[...] = hv

    xn = xn_ref[...]
    h1 = _dot(xn, w1_ref[...])
    h3 = _dot(xn, w3_ref[...])
    act = (h1 * _sigmoid(h1) * h3).astype(BF)
    acc_ref[...] += _dot(act, w2_ref[...])

    @pl.when(j == pl.num_programs(1) - 1)
    def _():
        o_ref[...] = acc_ref[...]


def dense_ffn(h, norm_w, w13, w2, tm=512, tf=1024):
    m, d = h.shape
    f = w2.shape[0]
    tm = min(tm, m)
    nf = f // tf
    return pl.pallas_call(
        _ffn_kernel,
        grid=(m // tm, nf),
        in_specs=[pl.BlockSpec((tm, d), lambda i, j: (i, 0)), pl.BlockSpec((1, d), lambda i, j: (0, 0)),
                  pl.BlockSpec((d, tf), lambda i, j: (0, j)), pl.BlockSpec((d, tf), lambda i, j: (0, j + nf)),
                  pl.BlockSpec((tf, d), lambda i, j: (j, 0))],
        out_specs=pl.BlockSpec((tm, d), lambda i, j: (i, 0)),
        out_shape=jax.ShapeDtypeStruct((m, d), F32),
        scratch_shapes=[pltpu.VMEM((tm, d), BF), pltpu.VMEM((tm, d), F32)],
        compiler_params=_cparams(("parallel", "arbitrary")),
    )(h, norm_w.reshape(1, d).astype(F32), w13, w13, w2)


ROUTE_TM = 512
MOE_TM = 512


def _router_kernel(h_ref, g_ref, r_ref, hn_ref, route_ref, cnt_ref, carry_ref):
    tm = h_ref.shape[0]

    @pl.when(pl.program_id(0) == 0)
    def _():
        carry_ref[...] = jnp.zeros_like(carry_ref)

    hn = _rms(h_ref[...], g_ref[...])
    hn_ref[...] = hn
    a1, a2, _ = _split3(hn)
    r1, r2, _ = _split3(r_ref[...])
    lane = lax.broadcasted_iota(jnp.int32, (tm, LANES), 1)
    logits = jnp.where(lane < N_EXPERTS, _dot(a1, r1) + (_dot(a1, r2) + _dot(a2, r1)), -jnp.inf)
    v0 = jnp.max(logits, axis=1, keepdims=True)
    i0 = jnp.min(jnp.where(logits == v0, lane, LANES), axis=1, keepdims=True)
    rest = jnp.where(lane == i0, -jnp.inf, logits)
    v1 = jnp.max(rest, axis=1, keepdims=True)
    i1 = jnp.min(jnp.where(rest == v1, lane, LANES), axis=1, keepdims=True)
    ex = jnp.exp(v1 - v0)
    g0 = 1.0 / (1.0 + ex)
    g1 = ex / (1.0 + ex)
    sel0, sel1 = lane == i0, lane == i1
    onehot = jnp.where(sel0 | sel1, 1.0, 0.0)
    below = (lax.broadcasted_iota(jnp.int32, (tm, tm), 1) < lax.broadcasted_iota(jnp.int32, (tm, tm), 0))
    before = carry_ref[0:1, :] + _dot(jnp.where(below, 1.0, 0.0).astype(BF), onehot.astype(BF))
    rank0 = jnp.sum(jnp.where(sel0, before, 0.0), axis=1, keepdims=True)
    rank1 = jnp.sum(jnp.where(sel1, before, 0.0), axis=1, keepdims=True)
    total = carry_ref[0:1, :] + jnp.sum(onehot, axis=0, keepdims=True)
    carry_ref[...] = jnp.broadcast_to(total, carry_ref.shape)
    cnt_ref[...] = jnp.broadcast_to(total, cnt_ref.shape)
    out = jnp.zeros((tm, LANES), F32)
    for pos, val in enumerate((i0.astype(F32), i1.astype(F32), g0, g1, rank0, rank1)):
        out = jnp.where(lane == pos, val, out)
    route_ref[...] = out


def moe_router(h, norm_w, router):
    m, d = h.shape
    tm = min(ROUTE_TM, m)
    rpad = jnp.zeros((d, LANES), F32).at[:, :N_EXPERTS].set(router.astype(F32))
    return pl.pallas_call(
        _router_kernel,
        grid=(m // tm,),
        in_specs=[pl.BlockSpec((tm, d), lambda i: (i, 0)), pl.BlockSpec((1, d), lambda i: (0, 0)),
                  pl.BlockSpec((d, LANES), lambda i: (0, 0))],
        out_specs=[pl.BlockSpec((tm, d), lambda i: (i, 0)), pl.BlockSpec((tm, LANES), lambda i: (i, 0)),
                   pl.BlockSpec((8, LANES), lambda i: (0, 0))],
        out_shape=[jax.ShapeDtypeStruct((m, d), F32), jax.ShapeDtypeStruct((m, LANES), F32),
                   jax.ShapeDtypeStruct((8, LANES), F32)],
        scratch_shapes=[pltpu.VMEM((8, LANES), F32)],
        compiler_params=_cparams(("arbitrary",)),
    )(h, norm_w.reshape(1, d).astype(F32), rpad)


def _gather_rows_kernel(idx_ref, src_ref, o_ref, sem):
    tg = o_ref.shape[0]
    base = pl.program_id(0) * tg

    def row_copy(r):
        return pltpu.make_async_copy(src_ref.at[pl.ds(idx_ref[base + r], 1), :], o_ref.at[pl.ds(r, 1), :], sem)

    def start(r, c):
        row_copy(r).start()
        return c

    def wait(r, c):
        row_copy(r).wait()
        return c

    lax.fori_loop(0, tg, start, 0, unroll=8)
    lax.fori_loop(0, tg, wait, 0, unroll=8)


def gather_rows(src, idx, tg=256):
    n = idx.shape[0]
    d = src.shape[1]
    tg = min(tg, n)
    return pl.pallas_call(
        _gather_rows_kernel,
        grid_spec=pltpu.PrefetchScalarGridSpec(
            num_scalar_prefetch=1, grid=(n // tg,),
            in_specs=[pl.BlockSpec(memory_space=pl.ANY)],
            out_specs=pl.BlockSpec((tg, d), lambda i, idx_ref: (i, 0)),
            scratch_shapes=[pltpu.SemaphoreType.DMA(())]),
        out_shape=jax.ShapeDtypeStruct((n, d), src.dtype),
        compiler_params=_cparams(("arbitrary",)),
    )(idx, src)


def _new_expert(te_ref, m):
    return jnp.logical_or(m == 0, te_ref[m] != te_ref[jnp.maximum(m - 1, 0)])


def _moe_up_kernel(te_ref, nt_ref, x_ref, w1_ref, w3_ref, o_ref, w1b_ref, w3b_ref):
    m = pl.program_id(1)
    used = m < nt_ref[0]

    @pl.when(jnp.logical_and(used, _new_expert(te_ref, m)))
    def _():
        w1b_ref[...] = w1_ref[...].astype(BF)
        w3b_ref[...] = w3_ref[...].astype(BF)

    @pl.when(used)
    def _():
        x = x_ref[...].astype(BF)
        h1 = _dot(x, w1b_ref[...])
        h3 = _dot(x, w3b_ref[...])
        o_ref[...] = (h1 * _sigmoid(h1) * h3).astype(o_ref.dtype)

    @pl.when(jnp.logical_not(used))
    def _():
        o_ref[...] = jnp.zeros_like(o_ref)


def _moe_down_kernel(te_ref, nt_ref, a_ref, w2_ref, o_ref, wb_ref):
    m = pl.program_id(1)
    used = m < nt_ref[0]

    @pl.when(jnp.logical_and(used, _new_expert(te_ref, m)))
    def _():
        wb_ref[...] = w2_ref[...].astype(BF)

    @pl.when(used)
    def _():
        o_ref[...] = _dot(a_ref[...], wb_ref[...])

    @pl.when(jnp.logical_not(used))
    def _():
        o_ref[...] = jnp.zeros_like(o_ref)


def moe_experts(xg, tile_e, ntiles, w13, w2, tn_up=1024, tn_down=512):
    cap, d = xg.shape
    f = w2.shape[1]
    tm = MOE_TM
    nt_max = cap // tm
    nf = f // tn_up
    mt = lambda m, nt: jnp.minimum(m, nt[0] - 1)
    act = pl.pallas_call(
        _moe_up_kernel,
        grid_spec=pltpu.PrefetchScalarGridSpec(
            num_scalar_prefetch=2, grid=(nf, nt_max),
            in_specs=[pl.BlockSpec((tm, d), lambda j, m, te, nt: (mt(m, nt), 0)),
                      pl.BlockSpec((None, d, tn_up), lambda j, m, te, nt: (te[mt(m, nt)], 0, j)),
                      pl.BlockSpec((None, d, tn_up), lambda j, m, te, nt: (te[mt(m, nt)], 0, j + nf))],
            out_specs=pl.BlockSpec((tm, tn_up), lambda j, m, te, nt: (m, j)),
            scratch_shapes=[pltpu.VMEM((d, tn_up), BF), pltpu.VMEM((d, tn_up), BF)]),
        out_shape=jax.ShapeDtypeStruct((cap, f), BF),
        compiler_params=_cparams(("arbitrary", "arbitrary")),
    )(tile_e, ntiles, xg, w13, w13)
    return pl.pallas_call(
        _moe_down_kernel,
        grid_spec=pltpu.PrefetchScalarGridSpec(
            num_scalar_prefetch=2, grid=(d // tn_down, nt_max),
            in_specs=[pl.BlockSpec((tm, f), lambda j, m, te, nt: (mt(m, nt), 0)),
                      pl.BlockSpec((None, f, tn_down), lambda j, m, te, nt: (te[mt(m, nt)], 0, j))],
            out_specs=pl.BlockSpec((tm, tn_down), lambda j, m, te, nt: (m, j)),
            scratch_shapes=[pltpu.VMEM((f, tn_down), BF)]),
        out_shape=jax.ShapeDtypeStruct((cap, d), F32),
        compiler_params=_cparams(("arbitrary", "arbitrary"), VMEM_LIMIT_BIG),
    )(tile_e, ntiles, act, w2)


def _combine_kernel(pos_ref, h_ref, route_ref, y_ref, *rest, final_norm):
    if final_norm:
        nw_ref, o_ref, buf, sem = rest
    else:
        o_ref, buf, sem = rest
    tc = h_ref.shape[0]
    base = pl.program_id(0) * tc

    def row_copy(r, k):
        return pltpu.make_async_copy(y_ref.at[pl.ds(pos_ref[TOP_K * (base + r) + k], 1), :],
                                     buf.at[k, pl.ds(r, 1), :], sem)

    def start(r, c):
        for k in range(TOP_K):
            row_copy(r, k).start()
        return c

    def wait(r, c):
        for k in range(TOP_K):
            row_copy(r, k).wait()
        return c

    lax.fori_loop(0, tc, start, 0, unroll=8)
    lax.fori_loop(0, tc, wait, 0, unroll=8)
    acc = h_ref[...]
    for k in range(TOP_K):
        acc = acc + route_ref[:, TOP_K + k:TOP_K + k + 1] * buf[k]
    if final_norm:
        acc = _rms(acc, nw_ref[...])
    o_ref[...] = acc


def moe_combine(h, route, yg, pos, final_w=None, tc=256):
    n, d = h.shape
    tc = min(tc, n)
    in_specs = [pl.BlockSpec((tc, d), lambda i, pos_ref: (i, 0)), pl.BlockSpec((tc, LANES), lambda i, pos_ref: (i, 0)),
                pl.BlockSpec(memory_space=pl.ANY)]
    args = [pos, h, route, yg]
    if final_w is not None:
        in_specs.append(pl.BlockSpec((1, d), lambda i, pos_ref: (0, 0)))
        args.append(final_w.reshape(1, d).astype(F32))
    return pl.pallas_call(
        functools.partial(_combine_kernel, final_norm=final_w is not None),
        grid_spec=pltpu.PrefetchScalarGridSpec(
            num_scalar_prefetch=1, grid=(n // tc,),
            in_specs=in_specs,
            out_specs=pl.BlockSpec((tc, d), lambda i, pos_ref: (i, 0)),
            scratch_shapes=[pltpu.VMEM((TOP_K, tc, d), F32), pltpu.SemaphoreType.DMA(())]),
        out_shape=jax.ShapeDtypeStruct((n, d), F32),
        compiler_params=_cparams(("arbitrary",)),
    )(*args)


def moe_ffn(h, norm_w, router, w13, w2, final_w=None):
    n, d = h.shape
    tm = MOE_TM
    hn, route, cnt = moe_router(h, norm_w, router)
    expert = route[:, 0:TOP_K].astype(jnp.int32)
    rank = route[:, 2 * TOP_K:3 * TOP_K].astype(jnp.int32)
    counts = cnt[0, :N_EXPERTS].astype(jnp.int32)
    tiles = (counts + tm - 1) // tm
    tile_end = jnp.cumsum(tiles)
    dest = ((tile_end - tiles) * tm)[expert] + rank
    nt_max = -(-n * TOP_K // tm) + N_EXPERTS
    cap = nt_max * tm
    flat = dest.reshape(-1)
    row_tok = jnp.zeros((cap,), jnp.int32).at[flat].set(jnp.arange(n * TOP_K, dtype=jnp.int32) // TOP_K)
    tile_e = jnp.minimum(jnp.searchsorted(tile_end, jnp.arange(nt_max, dtype=jnp.int32), side="right"),
                         N_EXPERTS - 1).astype(jnp.int32)
    ntiles = tile_end[-1:].astype(jnp.int32)
    xg = gather_rows(hn, row_tok)
    yg = moe_experts(xg, tile_e, ntiles, w13, w2)
    return moe_combine(h, route, yg, flat, final_w)


def _final_kernel(x_ref, w_ref, o_ref):
    o_ref[...] = _rms(x_ref[...], w_ref[...])


def final_norm(h, w, tm=512):
    m, d = h.shape
    tm = min(tm, m)
    return pl.pallas_call(
        _final_kernel,
        grid=(m // tm,),
        in_specs=[pl.BlockSpec((tm, d), lambda i: (i, 0)), pl.BlockSpec((1, d), lambda i: (0, 0))],
        out_specs=pl.BlockSpec((tm, d), lambda i: (i, 0)),
        out_shape=jax.ShapeDtypeStruct((m, d), F32),
        compiler_params=_cparams(("parallel",)),
    )(h, w.reshape(1, d).astype(F32))


def _rot_cols(w):
    half = w.shape[-1] // 2
    return jnp.concatenate([-w[..., half:], w[..., :half]], axis=-1)


_W_IN_SEGMENTS = (
    (U_ML_Q, O_ML_Q, 512, 1.0), (U_ML_K, O_ML_K, 512, 1.0), (U_ML_V, O_ML_V, 1024, 1.0),
    (U_ML_O, O_ML_O, 1024, 1.0), (U_SSM_Z, O_SSM_Z, 1024, 1.0), (U_SSM_X, O_SSM_XBC, SSM_DINNER, 1.0),
    (U_SWA_Q, O_SWA_Q, 1024, SWA_HEAD_DIM ** -0.5 * LOG2E), (U_SSM_BC, O_SSM_XBC + SSM_DINNER, SSM_BC, 1.0),
    (U_MLA_CQ, O_MLA_CQ, MLA_Q_LORA, 1.0), (U_MLA_CKV, O_MLA_CKV, MLA_KV_LORA, 1.0),
    (U_SWA_K, O_SWA_K, 256, 1.0), (U_SWA_V, O_SWA_V, 256, 1.0), (U_GATE, O_GATE, N_BRANCH * D_MODEL, 1.0))


W_IN_TILE = 256
_ROPE_TILE = U_MLA_KR // W_IN_TILE
_GATE_TILE = U_TOTAL // W_IN_TILE


def _w_in_tile_sources():
    src = [0] * (U_TOTAL // W_IN_TILE)
    for dst, s, width, _ in _W_IN_SEGMENTS:
        for off in range(0, width, W_IN_TILE):
            src[(dst + off) // W_IN_TILE] = s + off
    src[_ROPE_TILE] = O_MLA_KR
    return src


def _w_in_layout_kernel(src_ref, w_ref, dt_ref, o_ref):
    t = pl.program_id(1)

    @pl.when(t == _GATE_TILE)
    def _():
        ngate = 2 * ML_HEADS
        o_ref[...] = jnp.concatenate(
            [w_ref[0, 0:ngate, :], dt_ref[0],
             jnp.zeros((W_IN_TILE - ngate - SSM_HEADS, o_ref.shape[1]), F32)], axis=0).astype(BF)

    @pl.when(jnp.logical_and(t != _ROPE_TILE, t != _GATE_TILE))
    def _():
        swa_q = jnp.logical_and(t >= U_SWA_Q // W_IN_TILE, t < U_SSM_BC // W_IN_TILE)
        scale = jnp.where(swa_q, SWA_HEAD_DIM ** -0.5 * LOG2E, 1.0)
        o_ref[...] = (w_ref[0] * scale).astype(BF)

    @pl.when(t == _ROPE_TILE)
    def _():
        half = MLA_ROPE // 2
        o_ref[0:MLA_ROPE, :] = w_ref[0, 0:MLA_ROPE, :].astype(BF)
        o_ref[MLA_ROPE:MLA_ROPE + half, :] = (-w_ref[0, half:MLA_ROPE, :]).astype(BF)
        o_ref[MLA_ROPE + half:2 * MLA_ROPE, :] = w_ref[0, 0:half, :].astype(BF)
        o_ref[2 * MLA_ROPE:, :] = jnp.zeros((W_IN_TILE - 2 * MLA_ROPE, o_ref.shape[1]), BF)


def layout_w_in(w_in):
    depth, d, _ = w_in.shape
    wt = jnp.swapaxes(w_in, 1, 2)
    src = jnp.asarray([s // SUBLANES for s in _w_in_tile_sources() + [O_ML_I]], jnp.int32)
    return pl.pallas_call(
        _w_in_layout_kernel,
        grid_spec=pltpu.PrefetchScalarGridSpec(
            num_scalar_prefetch=1, grid=(depth, _GATE_TILE + 1),
            in_specs=[pl.BlockSpec((pl.Element(1), pl.Element(W_IN_TILE), pl.Element(d)),
                                   lambda l, t, src_ref: (l, src_ref[t] * SUBLANES, 0)),
                      pl.BlockSpec((pl.Element(1), pl.Element(SSM_HEADS), pl.Element(d)),
                                   lambda l, t, src_ref: (l, O_SSM_DT, 0))],
            out_specs=pl.BlockSpec((None, W_IN_TILE, d), lambda l, t, src_ref: (l, t, 0))),
        out_shape=jax.ShapeDtypeStruct((depth, U_TOTAL + W_IN_TILE, d), BF),
        compiler_params=_cparams(("parallel", "parallel")),
    )(src, wt, wt)


def _layout_w_uq(w):
    k = w.shape[0]
    w = w.reshape(k, MLA_HEADS, MLA_NOPE + MLA_ROPE)
    rope = w[..., MLA_NOPE:]
    scale = (MLA_NOPE + MLA_ROPE) ** -0.5 * LOG2E
    return (jnp.concatenate([w, _rot_cols(rope)], axis=-1).reshape(k, -1) * scale).astype(BF)


def _rope_table(seq):
    inv_freq = 1.0 / (ROPE_THETA ** (jnp.arange(0, MLA_ROPE, 2, dtype=F32) / MLA_ROPE))
    ang = jnp.arange(seq, dtype=F32)[:, None] * inv_freq[None, :]
    c, s = jnp.cos(ang), jnp.sin(ang)
    return jnp.concatenate([c, c, s, s], axis=1)


def hybrid_mixer(h, bsz, seq, layer, norm_w, w_main, ml_ib, ml_fb, ml_norm, conv_w, conv_b, dt_bias, a_log,
                 ssm_d, ssm_norm, q_norm, w_uq, kv_norm, w_ukv, sinks, w_branch, w_out, cs):
    n = bsz * seq
    nc = seq // CHUNK
    u2, small = matmul(h, w_main, norm=norm_w, out_dtype=BF, w_layer=layer, w_t=True, n=U_TOTAL, tn=2048,
                       side_w=w_main, side_rows=(U_TOTAL // LANES, LANES))
    u3 = u2.reshape(bsz, seq, U_TOTAL)

    def to_rows(cols, heads):
        return cols.reshape(bsz, seq, heads).transpose(0, 2, 1).reshape(bsz * heads * nc, CHUNK)

    def per_row(vec, heads):
        return jnp.broadcast_to(vec.astype(F32)[None, :, None], (bsz, heads, nc)).reshape(-1, 1)

    ig, bcum, dt, acs = recurrence_gates(
        to_rows(small[:, 0:ML_HEADS], ML_HEADS), to_rows(small[:, ML_HEADS:2 * ML_HEADS], ML_HEADS),
        per_row(ml_ib, ML_HEADS), per_row(ml_fb, ML_HEADS),
        to_rows(small[:, 2 * ML_HEADS:2 * ML_HEADS + SSM_HEADS], SSM_HEADS),
        per_row(dt_bias, SSM_HEADS), per_row(a_log, SSM_HEADS))

    def as_rows(x, heads):
        return x.reshape(bsz, heads, nc, CHUNK).transpose(0, 2, 1, 3)

    def as_cols(x, heads):
        return x.reshape(bsz, heads, seq).transpose(0, 2, 1)

    ya = mlstm_branch(u3, as_rows(ig, ML_HEADS), as_rows(bcum, ML_HEADS), as_cols(bcum, ML_HEADS), ml_norm)
    yb = ssd_branch(u3, as_cols(dt, SSM_HEADS), as_cols(acs, SSM_HEADS), as_rows(acs, SSM_HEADS), conv_w, conv_b,
                    jnp.repeat(ssm_d, SSM_HEADDIM), ssm_norm)
    qf = matmul(u2, w_uq, norm=q_norm, x_col_blk=U_MLA_CQ // MLA_Q_LORA, tn=2048)
    kvf = matmul(u2, w_ukv, norm=kv_norm, x_col_blk=U_MLA_CKV // MLA_KV_LORA, tn=2048)
    yc = mla_attention(qf.reshape(bsz, seq, -1), kvf.reshape(bsz, seq, -1), u3, cs)
    yd = swa_branch(u3, sinks)
    merged = gated_merge(ya.reshape(n, -1), yb.reshape(n, -1), yc.reshape(n, -1), yd.reshape(n, -1), w_branch, u2)
    return matmul(merged, w_out, residual=h, out_dtype=F32)


def kernel(x, mem, norm_mix, w_in, ml_igate_bias, ml_fgate_bias, ml_norm, ssm_conv_w, ssm_conv_b, ssm_dt_bias, ssm_a_log, ssm_d, ssm_norm, mla_q_norm, mla_w_uq, mla_kv_norm, mla_w_ukv, swa_sinks, w_branch, w_out, norm_cross, norm_mem, xa_wq, xa_wkv, xa_wo, norm_ffn, ffn_w13, ffn_w2, moe_router, moe_w13, moe_w2, norm_final):
    bsz, seq, d = x.shape
    depth = w_in.shape[0]
    n = bsz * seq
    mlen = mem.shape[1]
    cs = _rope_table(seq)
    h = x.reshape(n, d)
    mem2 = mem.reshape(bsz * mlen, d)
    w_main = layout_w_in(w_in)
    for l in range(depth):
        h = hybrid_mixer(h, bsz, seq, l, norm_mix[l], w_main, ml_igate_bias[l], ml_fgate_bias[l], ml_norm[l],
                         ssm_conv_w[l], ssm_conv_b[l], ssm_dt_bias[l], ssm_a_log[l], ssm_d[l], ssm_norm[l],
                         mla_q_norm[l], _layout_w_uq(mla_w_uq[l]), mla_kv_norm[l], mla_w_ukv[l].astype(BF),
                         swa_sinks[l], w_branch[l].astype(BF), w_out[l].astype(BF), cs)
        kv = matmul(mem2, xa_wkv[l].astype(BF), norm=norm_mem[l])
        wq = (xa_wq[l] * (XA_HEAD_DIM ** -0.5 * LOG2E)).astype(BF)
        h = cross_attention(h, seq, norm_cross[l], wq, kv.reshape(bsz, mlen, -1), xa_wo[l].astype(BF))
        if l % 2 == 0:
            h = dense_ffn(h, norm_ffn[l], ffn_w13[l // 2].astype(BF), ffn_w2[l // 2].astype(BF))
        else:
            h = moe_ffn(h, norm_ffn[l], moe_router[l // 2], moe_w13[l // 2], moe_w2[l // 2],
                        final_w=norm_final if l == depth - 1 else None)
    if depth % 2 == 1:
        h = final_norm(h, norm_final)
    return h.reshape(bsz, seq, d)
```

```python
import functools
import math

import jax
import jax.numpy as jnp
from jax import lax
from jax.experimental import pallas as pl
from jax.experimental.pallas import tpu as pltpu

F32 = jnp.float32
BF = jnp.bfloat16

D_MODEL = 2048
RMS_EPS = 1e-6
ML_HEADS, ML_DQK, ML_DV = 4, 128, 256
SSM_HEADS, SSM_HEADDIM, SSM_GROUPS, SSM_STATE, SSM_CONV = 16, 64, 2, 128, 4
SSM_DINNER = SSM_HEADS * SSM_HEADDIM
SSM_BC = 2 * SSM_GROUPS * SSM_STATE
MLA_HEADS, MLA_Q_LORA, MLA_KV_LORA, MLA_NOPE, MLA_ROPE, MLA_V = 8, 512, 256, 128, 64, 128
ROPE_THETA = 10000.0
SWA_HEADS, SWA_KV_HEADS, SWA_HEAD_DIM, SWA_WINDOW = 16, 4, 64, 128
N_BRANCH, BRANCH_W = 4, 1024
XA_HEADS, XA_HEAD_DIM = 4, 128
FFN_DIM = 7168
N_EXPERTS, TOP_K = 8, 2

CHUNK = 128
LANES = 128
SUBLANES = 8
VMEM_LIMIT = 56 * 1024 * 1024
VMEM_LIMIT_BIG = 60 * 1024 * 1024
LOG2E = math.log2(math.e)

U_ML_Q, U_ML_K, U_ML_V, U_ML_O = 0, 512, 1024, 2048
U_SSM_Z, U_SSM_X, U_SWA_Q, U_SSM_BC = 3072, 4096, 5120, 6144
U_MLA_CQ, U_MLA_CKV, U_SWA_K, U_SWA_V, U_MLA_KR = 6656, 7168, 7424, 7680, 7936
U_GATE = 8192
U_TOTAL = U_GATE + N_BRANCH * D_MODEL
_SPLITS = (512, 512, 1024, 1024, 4, 4, 1024, 1536, 16, 512, 256, 64, 1024, 256, 256, 8192)
_OFF = [0]
for _s in _SPLITS:
    _OFF.append(_OFF[-1] + _s)
(O_ML_Q, O_ML_K, O_ML_V, O_ML_O, O_ML_I, O_ML_F, O_SSM_Z, O_SSM_XBC, O_SSM_DT, O_MLA_CQ, O_MLA_CKV,
 O_MLA_KR, O_SWA_Q, O_SWA_K, O_SWA_V, O_GATE, _O_END) = _OFF


def _cparams(sem, vmem_limit=VMEM_LIMIT):
    return pltpu.CompilerParams(dimension_semantics=sem, vmem_limit_bytes=vmem_limit)


def _dot(a, b):
    return jnp.dot(a, b, preferred_element_type=F32)


def _dot_nt(a, b):
    return lax.dot_general(a, b, (((1,), (1,)), ((), ())), preferred_element_type=F32)


def _dot_tn(a, b):
    return lax.dot_general(a, b, (((0,), (0,)), ((), ())), preferred_element_type=F32)


def _split3(a):
    a1 = a.astype(BF)
    r = a - a1.astype(F32)
    a2 = r.astype(BF)
    a3 = (r - a2.astype(F32)).astype(BF)
    return a1, a2, a3


def _dot_sel(a, sel):
    a1, a2, a3 = _split3(a)
    return _dot(a1, sel) + _dot(a2, sel) + _dot(a3, sel)


def _rms(x, w):
    return x * lax.rsqrt(jnp.mean(x * x, axis=-1, keepdims=True) + RMS_EPS) * w


def _sigmoid(x):
    return 1.0 / (1.0 + jnp.exp(-x))


def _mm_kernel(*refs, has_norm, has_res, has_side, w_t):
    mm = _dot_nt if w_t else _dot
    it = iter(refs)
    x_ref = next(it)
    g_ref = next(it) if has_norm else None
    w_ref = next(it)
    r_ref = next(it) if has_res else None
    ws_ref = next(it) if has_side else None
    o_ref = next(it)
    os_ref = next(it) if has_side else None
    if has_norm:
        xn_ref = next(it)

        @pl.when(pl.program_id(1) == 0)
        def _():
            xn = _rms(x_ref[...].astype(F32), g_ref[...]).astype(BF)
            xn_ref[...] = xn
            if has_side:
                os_ref[...] = mm(xn, ws_ref[...])

        xv = xn_ref[...]
    else:
        xv = x_ref[...]
    acc = mm(xv, w_ref[...])
    if has_res:
        acc = acc + r_ref[...]
    o_ref[...] = acc.astype(o_ref.dtype)


def matmul(x, w, *, norm=None, residual=None, out_dtype=None, tm=1024, tn=1024, x_col_blk=0, w_layer=None,
           side_w=None, side_rows=None, w_t=False, n=None):
    out_dtype = out_dtype or BF
    m = x.shape[0]
    k, n_all = w.shape[-2:][::-1] if w_t else w.shape[-2:]
    n = n or n_all
    tm, tn = min(tm, m), min(tn, n)
    assert m % tm == 0 and n % tn == 0
    in_specs = [pl.BlockSpec((tm, k), lambda i, j: (i, x_col_blk))]
    args = [x]
    scratch = []
    if norm is not None:
        in_specs.append(pl.BlockSpec((1, k), lambda i, j: (0, 0)))
        args.append(norm.reshape(1, k).astype(F32))
        scratch.append(pltpu.VMEM((tm, k), BF))
    if w_layer is None:
        in_specs.append(pl.BlockSpec((k, tn), lambda i, j: (0, j)))
    else:
        in_specs.append(pl.BlockSpec((None, tn, k), lambda i, j: (w_layer, j, 0)) if w_t else
                        pl.BlockSpec((None, k, tn), lambda i, j: (w_layer, 0, j)))
    args.append(w)
    if residual is not None:
        in_specs.append(pl.BlockSpec((tm, tn), lambda i, j: (i, j)))
        args.append(residual)
    out_specs = pl.BlockSpec((tm, tn), lambda i, j: (i, j))
    out_shape = jax.ShapeDtypeStruct((m, n), out_dtype)
    if side_w is not None:
        assert norm is not None and w_layer is not None
        if side_rows is None:
            ns = side_w.shape[-2] if w_t else side_w.shape[-1]
            in_specs.append(pl.BlockSpec((None,) + side_w.shape[1:], lambda i, j: (w_layer, 0, 0)))
        else:
            blk, ns = side_rows
            in_specs.append(pl.BlockSpec((None, ns, k), lambda i, j: (w_layer, blk, 0)))
        args.append(side_w)
        out_specs = [out_specs, pl.BlockSpec((tm, ns), lambda i, j: (i, 0))]
        out_shape = [out_shape, jax.ShapeDtypeStruct((m, ns), F32)]
    return pl.pallas_call(
        functools.partial(_mm_kernel, has_norm=norm is not None, has_res=residual is not None,
                          has_side=side_w is not None, w_t=w_t),
        grid=(m // tm, n // tn),
        in_specs=in_specs,
        out_specs=out_specs,
        out_shape=out_shape,
        scratch_shapes=scratch,
        compiler_params=_cparams(("parallel", "arbitrary")),
    )(*args)


def _cumsum_lanes(x):
    lane = lax.broadcasted_iota(jnp.int32, x.shape, 1)
    s = 1
    while s < x.shape[1]:
        x = x + jnp.where(lane >= s, pltpu.roll(x, s, axis=1), 0.0)
        s *= 2
    return x


def _softplus(x):
    return jnp.maximum(x, 0.0) + jnp.log(1.0 + jnp.exp(-jnp.abs(x)))


def _gates_kernel(i_ref, f_ref, ib_ref, fb_ref, dt_ref, dtb_ref, alog_ref, ig_ref, b_ref, dto_ref, acs_ref):
    ig_ref[...] = i_ref[...] + ib_ref[...]
    b_ref[...] = _cumsum_lanes(-_softplus(-(f_ref[...] + fb_ref[...])))
    dt = _softplus(dt_ref[...] + dtb_ref[...])
    dto_ref[...] = dt
    acs_ref[...] = _cumsum_lanes(dt * (-jnp.exp(alog_ref[...])))


def recurrence_gates(i_rows, f_rows, ib, fb, dt_rows, dtb, alog):
    r1, r2 = i_rows.shape[0], dt_rows.shape[0]
    shp = lambda r: jax.ShapeDtypeStruct((r, CHUNK), F32)
    return pl.pallas_call(
        _gates_kernel,
        out_shape=(shp(r1), shp(r1), shp(r2), shp(r2)),
    )(i_rows, f_rows, ib, fb, dt_rows, dtb, alog)


def _mlstm_kernel(q_ref, k_ref, v_ref, o_ref, igr_ref, br_ref, bc_ref, nw_ref, y_ref, *state):
    ct_ref, n_ref, m_ref = state[0::3], state[1::3], state[2::3]
    L = CHUNK

    @pl.when(pl.program_id(1) == 0)
    def _():
        for ref in state:
            ref[...] = jnp.zeros_like(ref)

    row = lax.broadcasted_iota(jnp.int32, (L, L), 0)
    col = lax.broadcasted_iota(jnp.int32, (L, L), 1)
    causal = col <= row
    diag = col == row
    scale = ML_DQK ** -0.5
    outs = []
    for h in range(ML_HEADS):
        q = q_ref[:, h * ML_DQK:(h + 1) * ML_DQK]
        k = (k_ref[:, h * ML_DQK:(h + 1) * ML_DQK].astype(F32) * scale).astype(BF)
        v = v_ref[:, h * ML_DV:(h + 1) * ML_DV]
        bcol = bc_ref[:, h:h + 1]
        brow = br_ref[h:h + 1, :]
        igrow = igr_ref[h:h + 1, :]
        m_prev = m_ref[h][0:1, 0:1]
        n_prev = n_ref[h][0:1, :]
        ct_prev = ct_ref[h][...]

        dmat = jnp.where(causal, bcol - brow + igrow, -jnp.inf)
        m_intra = jnp.max(dmat, axis=1, keepdims=True)
        g = bcol + m_prev
        m_s = jnp.maximum(g, m_intra)
        p = jnp.exp(dmat - m_s) * _dot_nt(q, k)
        inter = jnp.exp(g - m_s)
        num = _dot(p.astype(BF), v) + inter * _dot(q, ct_prev.astype(BF))
        den = jnp.sum(p, axis=1, keepdims=True) + inter * jnp.sum(q.astype(F32) * n_prev, axis=1, keepdims=True)
        hh = num / jnp.maximum(jnp.abs(den), jnp.exp(-m_s))
        hn = _rms(hh, nw_ref[:, h * ML_DV:(h + 1) * ML_DV])
        outs.append(_sigmoid(o_ref[:, h * ML_DV:(h + 1) * ML_DV].astype(F32)) * hn)

        b_tot = brow[:, L - 1:L]
        a = b_tot - brow + igrow
        m_loc = jnp.max(a, axis=1, keepdims=True)
        wl = jnp.exp(a - m_loc)
        kw = _dot(jnp.where(diag, wl, 0.0).astype(BF), k)
        m_new = jnp.maximum(b_tot + m_prev, m_loc)
        da = jnp.exp(b_tot + m_prev - m_new)
        db = jnp.exp(m_loc - m_new)
        ct_ref[h][...] = da * ct_prev + db * _dot_tn(kw.astype(BF), v)
        n_ref[h][...] = jnp.broadcast_to(da * n_prev + db * jnp.sum(kw, axis=0, keepdims=True), n_ref[h].shape)
        m_ref[h][...] = jnp.broadcast_to(m_new, m_ref[h].shape)
    y_ref[...] = jnp.concatenate(outs, axis=1).astype(y_ref.dtype)


def mlstm_branch(u3, ig_rows, b_rows, b_cols, norm_w):
    bsz, seq, _ = u3.shape
    nc = seq // CHUNK
    L = CHUNK
    ublk = lambda width, off: pl.BlockSpec((None, L, width), lambda b, c: (b, c, off // width))
    rows = pl.BlockSpec((None, None, ML_HEADS, L), lambda b, c: (b, c, 0, 0))
    return pl.pallas_call(
        _mlstm_kernel,
        grid=(bsz, nc),
        in_specs=[ublk(512, U_ML_Q), ublk(512, U_ML_K), ublk(1024, U_ML_V), ublk(1024, U_ML_O), rows, rows,
                  pl.BlockSpec((None, L, ML_HEADS), lambda b, c: (b, c, 0)),
                  pl.BlockSpec((1, ML_HEADS * ML_DV), lambda b, c: (0, 0))],
        out_specs=pl.BlockSpec((None, L, BRANCH_W), lambda b, c: (b, c, 0)),
        out_shape=jax.ShapeDtypeStruct((bsz, seq, BRANCH_W), BF),
        scratch_shapes=[pltpu.VMEM((ML_DQK, ML_DV), F32), pltpu.VMEM((8, ML_DQK), F32),
                        pltpu.VMEM((8, LANES), F32)] * ML_HEADS,
        compiler_params=_cparams(("parallel", "arbitrary")),
    )(u3, u3, u3, u3, ig_rows, b_rows, b_cols, norm_w.reshape(1, -1).astype(F32))


def _ssd_kernel(z_ref, x_ref, bc_ref, dt_ref, ac_ref, ar_ref, cw_ref, cb_ref, d_ref, nw_ref, y_ref, xs_ref, st_ref):
    L = CHUNK
    P, R, NS = SSM_HEADDIM, SSM_HEADS // SSM_GROUPS, SSM_STATE
    GW = R * P

    @pl.when(pl.program_id(1) == 0)
    def _():
        xs_ref[0:8, :] = jnp.zeros((8, xs_ref.shape[1]), F32)
        st_ref[...] = jnp.zeros_like(st_ref)

    xs_ref[8:, :] = jnp.concatenate([x_ref[...], bc_ref[...]], axis=1).astype(F32)
    conv = cb_ref[...] + cw_ref[SSM_CONV - 1:SSM_CONV, :] * xs_ref[8:8 + L, :]
    for sft in range(1, SSM_CONV):
        conv = conv + cw_ref[SSM_CONV - 1 - sft:SSM_CONV - sft, :] * xs_ref[8 - sft:8 - sft + L, :]
    xs_ref[0:8, :] = xs_ref[L:L + 8, :]
    xbc = conv * _sigmoid(conv)
    xh = xbc[:, :SSM_DINNER]
    bmat = xbc[:, SSM_DINNER:SSM_DINNER + SSM_GROUPS * NS].astype(BF)
    cmat = xbc[:, SSM_DINNER + SSM_GROUPS * NS:].astype(BF)

    dtc = dt_ref[...]
    ac = ac_ref[...]
    ar = ar_ref[...]
    a_last = ac[L - 1:L, :]
    hsel = (lax.broadcasted_iota(jnp.int32, (SSM_HEADS, SSM_DINNER), 1) // P
            == lax.broadcasted_iota(jnp.int32, (SSM_HEADS, SSM_DINNER), 0))
    expand = jnp.where(hsel, 1.0, 0.0).astype(BF)
    stack = jnp.concatenate([dtc, jnp.exp(a_last - ac), jnp.exp(ac),
                             jnp.broadcast_to(jnp.exp(a_last), (8, SSM_HEADS))], axis=0)
    ex = _dot_sel(stack, expand)
    dt_full, dst_full, ind_full = ex[0:L], ex[L:2 * L], ex[2 * L:3 * L]
    cdec_full = ex[3 * L:3 * L + 1]
    xdt = xh * dt_full
    xdt_b = xdt.astype(BF)
    xw_b = (xdt * dst_full).astype(BF)

    row = lax.broadcasted_iota(jnp.int32, (L, L), 0)
    col = lax.broadcasted_iota(jnp.int32, (L, L), 1)
    causal = col <= row
    ys = []
    for g in range(SSM_GROUPS):
        bg = bmat[:, g * NS:(g + 1) * NS]
        cg = cmat[:, g * NS:(g + 1) * NS]
        cb = _dot_nt(cg, bg)
        st_prev = st_ref[g]
        yoff = _dot(cg, st_prev.astype(BF))
        st_ref[g] = cdec_full[:, g * GW:(g + 1) * GW] * st_prev + _dot_tn(bg, xw_b[:, g * GW:(g + 1) * GW])
        for r in range(R):
            h = g * R + r
            dec = jnp.exp(jnp.where(causal, ac[:, h:h + 1] - ar[h:h + 1, :], -jnp.inf))
            yd = _dot((dec * cb).astype(BF), xdt_b[:, h * P:(h + 1) * P])
            ys.append(yd + yoff[:, r * P:(r + 1) * P] * ind_full[:, h * P:(h + 1) * P])
    y = jnp.concatenate(ys, axis=1) + xh * d_ref[...]
    zf = z_ref[...].astype(F32)
    y = y * (zf * _sigmoid(zf))
    y_ref[...] = jnp.concatenate(
        [_rms(y[:, g * GW:(g + 1) * GW], nw_ref[:, g * GW:(g + 1) * GW]) for g in range(SSM_GROUPS)],
        axis=1).astype(y_ref.dtype)


def ssd_branch(u3, dt_cols, acs_cols, acs_rows, conv_w, conv_b, d_full, norm_w):
    bsz, seq, _ = u3.shape
    nc = seq // CHUNK
    L = CHUNK
    ublk = lambda width, off: pl.BlockSpec((None, L, width), lambda b, c: (b, c, off // width))
    cols = pl.BlockSpec((None, L, SSM_HEADS), lambda b, c: (b, c, 0))
    const = lambda shape: pl.BlockSpec(shape, lambda b, c: (0, 0))
    cch = SSM_DINNER + SSM_BC
    return pl.pallas_call(
        _ssd_kernel,
        grid=(bsz, nc),
        in_specs=[ublk(1024, U_SSM_Z), ublk(1024, U_SSM_X), ublk(512, U_SSM_BC), cols, cols,
                  pl.BlockSpec((None, None, SSM_HEADS, L), lambda b, c: (b, c, 0, 0)),
                  const((SSM_CONV, cch)), const((1, cch)), const((1, SSM_DINNER)), const((1, SSM_DINNER))],
        out_specs=pl.BlockSpec((None, L, BRANCH_W), lambda b, c: (b, c, 0)),
        out_shape=jax.ShapeDtypeStruct((bsz, seq, BRANCH_W), BF),
        scratch_shapes=[pltpu.VMEM((L + 8, cch), F32),
                        pltpu.VMEM((SSM_GROUPS, SSM_STATE, SSM_DINNER // SSM_GROUPS), F32)],
        compiler_params=_cparams(("parallel", "arbitrary")),
    )(u3, u3, u3, dt_cols, acs_cols, acs_rows, conv_w.astype(F32), conv_b.reshape(1, cch).astype(F32),
      d_full.reshape(1, -1).astype(F32), norm_w.reshape(1, -1).astype(F32))


def _rope128(x, cs):
    t = x.astype(F32) * cs
    return t + pltpu.roll(t, MLA_ROPE, axis=1)


def _mla_kernel(q_ref, csq_ref, kv_ref, kr_ref, csk_ref, o_ref, qe_s, *stats, tq):
    qi = pl.program_id(1)
    ki = pl.program_id(2)
    qw = MLA_NOPE + LANES
    kw = MLA_NOPE + MLA_V
    m_s, l_s, acc_s = stats[0::3], stats[1::3], stats[2::3]
    nt = tq // LANES

    @pl.when(ki == 0)
    def _():
        for h in range(MLA_HEADS):
            qr = _rope128(q_ref[:, h * qw + MLA_NOPE:(h + 1) * qw], csq_ref[...]).astype(BF)
            qe_s[h] = jnp.concatenate([q_ref[:, h * qw:h * qw + MLA_NOPE], qr], axis=1)
            m_s[h][...] = jnp.full((tq, LANES), -jnp.inf, F32)
            l_s[h][...] = jnp.zeros((tq, LANES), F32)
            acc_s[h][...] = jnp.zeros((tq, MLA_V), F32)

    def step(masked):
        lane = lax.broadcasted_iota(jnp.int32, (tq, LANES), 1)
        kr = jnp.where(lane < MLA_ROPE, _rope128(kr_ref[...], csk_ref[...]), 0.0).astype(BF)
        if masked:
            causal = (lax.broadcasted_iota(jnp.int32, (tq, tq), 1) <= lax.broadcasted_iota(jnp.int32, (tq, tq), 0))
        for h in range(MLA_HEADS):
            ke = jnp.concatenate([kv_ref[:, h * kw:h * kw + MLA_NOPE], kr], axis=1)
            s = _dot_nt(qe_s[h], ke)
            if masked:
                s = jnp.where(causal, s, -jnp.inf)
            m_old = m_s[h][...]
            m_new = jnp.maximum(m_old, jnp.max(s, axis=1, keepdims=True))
            alpha = jnp.exp2(m_old - m_new)
            p = jnp.exp2(s - jnp.concatenate([m_new] * nt, axis=1))
            psum = p[:, 0:LANES]
            for t in range(1, nt):
                psum = psum + p[:, t * LANES:(t + 1) * LANES]
            l_s[h][...] = alpha * l_s[h][...] + psum
            acc_s[h][...] = alpha * acc_s[h][...] + _dot(p.astype(BF), kv_ref[:, h * kw + MLA_NOPE:(h + 1) * kw])
            m_s[h][...] = m_new

    @pl.when(ki < qi)
    def _():
        step(False)

    @pl.when(ki == qi)
    def _():
        step(True)
        for h in range(MLA_HEADS):
            l = jnp.sum(l_s[h][...], axis=1, keepdims=True)
            o_ref[:, h * MLA_V:(h + 1) * MLA_V] = (acc_s[h][...] / l).astype(o_ref.dtype)


def mla_attention(qf3, kvf3, u3, cs, tq=1024):
    bsz, seq, _ = qf3.shape
    tq = min(tq, seq)
    nq = seq // tq
    kvi = lambda b, qi, ki: jnp.minimum(ki, qi)
    return pl.pallas_call(
        functools.partial(_mla_kernel, tq=tq),
        grid=(bsz, nq, nq),
        in_specs=[pl.BlockSpec((None, tq, qf3.shape[2]), lambda b, qi, ki: (b, qi, 0)),
                  pl.BlockSpec((tq, LANES), lambda b, qi, ki: (qi, 0)),
                  pl.BlockSpec((None, tq, kvf3.shape[2]), lambda b, qi, ki: (b, kvi(b, qi, ki), 0)),
                  pl.BlockSpec((None, tq, LANES), lambda b, qi, ki: (b, kvi(b, qi, ki), U_MLA_KR // LANES)),
                  pl.BlockSpec((tq, LANES), lambda b, qi, ki: (kvi(b, qi, ki), 0))],
        out_specs=pl.BlockSpec((None, tq, MLA_HEADS * MLA_V), lambda b, qi, ki: (b, qi, 0)),
        out_shape=jax.ShapeDtypeStruct((bsz, seq, MLA_HEADS * MLA_V), BF),
        scratch_shapes=[pltpu.VMEM((MLA_HEADS, tq, MLA_NOPE + LANES), BF)]
        + [pltpu.VMEM((tq, LANES), F32), pltpu.VMEM((tq, LANES), F32), pltpu.VMEM((tq, MLA_V), F32)] * MLA_HEADS,
        compiler_params=_cparams(("parallel", "parallel", "arbitrary")),
    )(qf3, cs, kvf3, u3, cs)


def _swa_kernel(q_ref, kc_ref, kp_ref, vc_ref, vp_ref, sink_ref, o_ref):
    W, d = SWA_WINDOW, SWA_HEAD_DIM
    n = pl.program_id(1)
    i = lax.broadcasted_iota(jnp.int32, (2 * W, 2 * W), 0) & (W - 1)
    j = lax.broadcasted_iota(jnp.int32, (2 * W, 2 * W), 1)
    valid = (j > i) & (j <= i + W) & ((n > 0) | (j >= W))
    first_tile = lax.broadcasted_iota(jnp.int32, (2 * W, LANES), 0) < W
    lower = lax.broadcasted_iota(jnp.int32, (2 * W, LANES), 1) < d
    ones = jnp.ones((2 * W, LANES), BF)
    for pair in range(SWA_KV_HEADS // 2):
        sl = slice(pair * LANES, (pair + 1) * LANES)
        kt = jnp.concatenate([kp_ref[:, sl], kc_ref[:, sl]], axis=0)
        vt = jnp.concatenate([vp_ref[:, sl], vc_ref[:, sl]], axis=0)
        kt_sw = pltpu.roll(kt.astype(F32), d, axis=1).astype(BF)
        vt_sw = pltpu.roll(vt.astype(F32), d, axis=1).astype(BF)
        for e in range(2):
            kh = 2 * pair + e
            k_lo = jnp.where(lower, kt if e == 0 else kt_sw, jnp.zeros_like(kt))
            k_hi = jnp.where(lower, jnp.zeros_like(kt), kt_sw if e == 0 else kt)
            v_lo, v_hi = (vt, vt_sw) if e == 0 else (vt_sw, vt)
            qs = jnp.concatenate([q_ref[:, 2 * kh * LANES:(2 * kh + 1) * LANES],
                                  q_ref[:, (2 * kh + 1) * LANES:(2 * kh + 2) * LANES]], axis=0)
            halves = []
            for half, (ke, ve) in enumerate(((k_lo, v_lo), (k_hi, v_hi))):
                ha, hb = 4 * kh + half, 4 * kh + 2 + half
                s = jnp.where(valid, _dot_nt(qs, ke), -jnp.inf)
                sink = jnp.where(first_tile, sink_ref[0:1, ha:ha + 1], sink_ref[0:1, hb:hb + 1]) * LOG2E
                m = jnp.maximum(jnp.max(s, axis=1, keepdims=True), sink)
                p = jnp.exp2(s - jnp.concatenate([m, m], axis=1)).astype(BF)
                den = _dot(p, ones) + jnp.exp2(sink - m)
                halves.append(_dot(p, ve) / den)
            ot = jnp.where(lower, halves[0], halves[1]).astype(o_ref.dtype)
            o_ref[:, 2 * kh * LANES:(2 * kh + 1) * LANES] = ot[:W]
            o_ref[:, (2 * kh + 1) * LANES:(2 * kh + 2) * LANES] = ot[W:]


def swa_branch(u3, sinks):
    bsz, seq, _ = u3.shape
    W = SWA_WINDOW
    kvw = SWA_KV_HEADS * SWA_HEAD_DIM
    cur = lambda width, off: pl.BlockSpec((None, W, width), lambda b, n: (b, n, off // width))
    prev = lambda width, off: pl.BlockSpec((None, W, width), lambda b, n: (b, jnp.maximum(n - 1, 0), off // width))
    return pl.pallas_call(
        _swa_kernel,
        grid=(bsz, seq // W),
        in_specs=[cur(1024, U_SWA_Q), cur(kvw, U_SWA_K), prev(kvw, U_SWA_K), cur(kvw, U_SWA_V), prev(kvw, U_SWA_V),
                  pl.BlockSpec((1, SWA_HEADS), lambda b, n: (0, 0))],
        out_specs=pl.BlockSpec((None, W, BRANCH_W), lambda b, n: (b, n, 0)),
        out_shape=jax.ShapeDtypeStruct((bsz, seq, BRANCH_W), BF),
        compiler_params=_cparams(("parallel", "parallel")),
    )(u3, u3, u3, u3, u3, sinks.reshape(1, -1).astype(F32))


def _merge_kernel(ya_ref, yb_ref, yc_ref, yd_ref, w_ref, g0_ref, g1_ref, g2_ref, g3_ref, o_ref):
    acc = None
    for n, (y_ref, g_ref) in enumerate(((ya_ref, g0_ref), (yb_ref, g1_ref), (yc_ref, g2_ref), (yd_ref, g3_ref))):
        t = _sigmoid(g_ref[...].astype(F32)) * _dot(y_ref[...], w_ref[n])
        acc = t if acc is None else acc + t
    o_ref[...] = acc.astype(o_ref.dtype)


def gated_merge(ya, yb, yc, yd, w_branch, u2, tm=1024, tn=512):
    m = ya.shape[0]
    tm = min(tm, m)
    ysp = pl.BlockSpec((tm, BRANCH_W), lambda i, j: (i, 0))
    gsp = lambda n: pl.BlockSpec((tm, tn), lambda i, j: (i, (U_GATE + n * D_MODEL) // tn + j))
    return pl.pallas_call(
        _merge_kernel,
        grid=(m // tm, D_MODEL // tn),
        in_specs=[ysp, ysp, ysp, ysp, pl.BlockSpec((N_BRANCH, BRANCH_W, tn), lambda i, j: (0, 0, j)),
                  gsp(0), gsp(1), gsp(2), gsp(3)],
        out_specs=pl.BlockSpec((tm, tn), lambda i, j: (i, j)),
        out_shape=jax.ShapeDtypeStruct((m, D_MODEL), BF),
        compiler_params=_cparams(("parallel", "parallel")),
    )(ya, yb, yc, yd, w_branch, u2, u2, u2, u2)


def _xattn_kernel(h_ref, g_ref, wq_ref, kv_ref, wo_ref, o_ref):
    d = XA_HEAD_DIM
    h = h_ref[...]
    q = _dot(_rms(h, g_ref[...]).astype(BF), wq_ref[...]).astype(BF)
    ones = jnp.ones((kv_ref.shape[0], LANES), BF)
    outs = []
    for hd in range(XA_HEADS):
        s = _dot_nt(q[:, hd * d:(hd + 1) * d], kv_ref[:, hd * d:(hd + 1) * d])
        p = jnp.exp2(s - jnp.max(s, axis=1, keepdims=True)).astype(BF)
        den = _dot(p, ones)
        outs.append((_dot(p, kv_ref[:, (XA_HEADS + hd) * d:(XA_HEADS + hd + 1) * d]) / den).astype(BF))
    o_ref[...] = h + _dot(jnp.concatenate(outs, axis=1), wo_ref[...])


def cross_attention(h, seq, norm_w, wq, kv3, wo, tq=1024):
    n, dm = h.shape
    _, mlen, kvw = kv3.shape
    tq = min(tq, seq)
    per_b = seq // tq
    const = lambda shape: pl.BlockSpec(shape, lambda i: (0, 0))
    return pl.pallas_call(
        _xattn_kernel,
        grid=(n // tq,),
        in_specs=[pl.BlockSpec((tq, dm), lambda i: (i, 0)), const((1, dm)), const(wq.shape),
                  pl.BlockSpec((None, mlen, kvw), lambda i: (i // per_b, 0, 0)), const(wo.shape)],
        out_specs=pl.BlockSpec((tq, dm), lambda i: (i, 0)),
        out_shape=jax.ShapeDtypeStruct((n, dm), F32),
        compiler_params=_cparams(("parallel",)),
    )(h, norm_w.reshape(1, dm).astype(F32), wq, kv3, wo)


def _ffn_kernel(h_ref, g_ref, w1_ref, w3_ref, w2_ref, o_ref, xn_ref, acc_ref):
    j = pl.program_id(1)

    @pl.when(j == 0)
    def _():
        hv = h_ref[...]
        xn_ref[...] = _rms(hv, g_ref[...]).astype(BF)
        acc_ref[...] = hv

    xn = xn_ref[...]
    h1 = _dot(xn, w1_ref[...])
    h3 = _dot(xn, w3_ref[...])
    act = (h1 * _sigmoid(h1) * h3).astype(BF)
    acc_ref[...] += _dot(act, w2_ref[...])

    @pl.when(j == pl.num_programs(1) - 1)
    def _():
        o_ref[...] = acc_ref[...]


def dense_ffn(h, norm_w, w13, w2, tm=512, tf=1024):
    m, d = h.shape
    f = w2.shape[0]
    tm = min(tm, m)
    nf = f // tf
    return pl.pallas_call(
        _ffn_kernel,
        grid=(m // tm, nf),
        in_specs=[pl.BlockSpec((tm, d), lambda i, j: (i, 0)), pl.BlockSpec((1, d), lambda i, j: (0, 0)),
                  pl.BlockSpec((d, tf), lambda i, j: (0, j)), pl.BlockSpec((d, tf), lambda i, j: (0, j + nf)),
                  pl.BlockSpec((tf, d), lambda i, j: (j, 0))],
        out_specs=pl.BlockSpec((tm, d), lambda i, j: (i, 0)),
        out_shape=jax.ShapeDtypeStruct((m, d), F32),
        scratch_shapes=[pltpu.VMEM((tm, d), BF), pltpu.VMEM((tm, d), F32)],
        compiler_params=_cparams(("parallel", "arbitrary")),
    )(h, norm_w.reshape(1, d).astype(F32), w13, w13, w2)


ROUTE_TM = 512
MOE_TM = 512


def _router_kernel(h_ref, g_ref, r_ref, hn_ref, route_ref, cnt_ref, carry_ref):
    tm = h_ref.shape[0]

    @pl.when(pl.program_id(0) == 0)
    def _():
        carry_ref[...] = jnp.zeros_like(carry_ref)

    hn = _rms(h_ref[...], g_ref[...])
    hn_ref[...] = hn
    a1, a2, _ = _split3(hn)
    r1, r2, _ = _split3(r_ref[...])
    lane = lax.broadcasted_iota(jnp.int32, (tm, LANES), 1)
    logits = jnp.where(lane < N_EXPERTS, _dot(a1, r1) + (_dot(a1, r2) + _dot(a2, r1)), -jnp.inf)
    v0 = jnp.max(logits, axis=1, keepdims=True)
    i0 = jnp.min(jnp.where(logits == v0, lane, LANES), axis=1, keepdims=True)
    rest = jnp.where(lane == i0, -jnp.inf, logits)
    v1 = jnp.max(rest, axis=1, keepdims=True)
    i1 = jnp.min(jnp.where(rest == v1, lane, LANES), axis=1, keepdims=True)
    ex = jnp.exp(v1 - v0)
    g0 = 1.0 / (1.0 + ex)
    g1 = ex / (1.0 + ex)
    sel0, sel1 = lane == i0, lane == i1
    onehot = jnp.where(sel0 | sel1, 1.0, 0.0)
    below = (lax.broadcasted_iota(jnp.int32, (tm, tm), 1) < lax.broadcasted_iota(jnp.int32, (tm, tm), 0))
    before = carry_ref[0:1, :] + _dot(jnp.where(below, 1.0, 0.0).astype(BF), onehot.astype(BF))
    rank0 = jnp.sum(jnp.where(sel0, before, 0.0), axis=1, keepdims=True)
    rank1 = jnp.sum(jnp.where(sel1, before, 0.0), axis=1, keepdims=True)
    total = carry_ref[0:1, :] + jnp.sum(onehot, axis=0, keepdims=True)
    carry_ref[...] = jnp.broadcast_to(total, carry_ref.shape)
    cnt_ref[...] = jnp.broadcast_to(total, cnt_ref.shape)
    out = jnp.zeros((tm, LANES), F32)
    for pos, val in enumerate((i0.astype(F32), i1.astype(F32), g0, g1, rank0, rank1)):
        out = jnp.where(lane == pos, val, out)
    route_ref[...] = out


def moe_router(h, norm_w, router):
    m, d = h.shape
    tm = min(ROUTE_TM, m)
    rpad = jnp.zeros((d, LANES), F32).at[:, :N_EXPERTS].set(router.astype(F32))
    return pl.pallas_call(
        _router_kernel,
        grid=(m // tm,),
        in_specs=[pl.BlockSpec((tm, d), lambda i: (i, 0)), pl.BlockSpec((1, d), lambda i: (0, 0)),
                  pl.BlockSpec((d, LANES), lambda i: (0, 0))],
        out_specs=[pl.BlockSpec((tm, d), lambda i: (i, 0)), pl.BlockSpec((tm, LANES), lambda i: (i, 0)),
                   pl.BlockSpec((8, LANES), lambda i: (0, 0))],
        out_shape=[jax.ShapeDtypeStruct((m, d), F32), jax.ShapeDtypeStruct((m, LANES), F32),
                   jax.ShapeDtypeStruct((8, LANES), F32)],
        scratch_shapes=[pltpu.VMEM((8, LANES), F32)],
        compiler_params=_cparams(("arbitrary",)),
    )(h, norm_w.reshape(1, d).astype(F32), rpad)


def _gather_rows_kernel(idx_ref, src_ref, o_ref, sem):
    tg = o_ref.shape[0]
    base = pl.program_id(0) * tg

    def row_copy(r):
        return pltpu.make_async_copy(src_ref.at[pl.ds(idx_ref[base + r], 1), :], o_ref.at[pl.ds(r, 1), :], sem)

    def start(r, c):
        row_copy(r).start()
        return c

    def wait(r, c):
        row_copy(r).wait()
        return c

    lax.fori_loop(0, tg, start, 0, unroll=8)
    lax.fori_loop(0, tg, wait, 0, unroll=8)


def gather_rows(src, idx, tg=256):
    n = idx.shape[0]
    d = src.shape[1]
    tg = min(tg, n)
    return pl.pallas_call(
        _gather_rows_kernel,
        grid_spec=pltpu.PrefetchScalarGridSpec(
            num_scalar_prefetch=1, grid=(n // tg,),
            in_specs=[pl.BlockSpec(memory_space=pl.ANY)],
            out_specs=pl.BlockSpec((tg, d), lambda i, idx_ref: (i, 0)),
            scratch_shapes=[pltpu.SemaphoreType.DMA(())]),
        out_shape=jax.ShapeDtypeStruct((n, d), src.dtype),
        compiler_params=_cparams(("arbitrary",)),
    )(idx, src)


def _new_expert(te_ref, m):
    return jnp.logical_or(m == 0, te_ref[m] != te_ref[jnp.maximum(m - 1, 0)])


def _moe_up_kernel(te_ref, nt_ref, x_ref, w1_ref, w3_ref, o_ref, w1b_ref, w3b_ref):
    m = pl.program_id(1)
    used = m < nt_ref[0]

    @pl.when(jnp.logical_and(used, _new_expert(te_ref, m)))
    def _():
        w1b_ref[...] = w1_ref[...].astype(BF)
        w3b_ref[...] = w3_ref[...].astype(BF)

    @pl.when(used)
    def _():
        x = x_ref[...].astype(BF)
        h1 = _dot(x, w1b_ref[...])
        h3 = _dot(x, w3b_ref[...])
        o_ref[...] = (h1 * _sigmoid(h1) * h3).astype(o_ref.dtype)

    @pl.when(jnp.logical_not(used))
    def _():
        o_ref[...] = jnp.zeros_like(o_ref)


def _moe_down_kernel(te_ref, nt_ref, a_ref, w2_ref, o_ref, wb_ref):
    m = pl.program_id(1)
    used = m < nt_ref[0]

    @pl.when(jnp.logical_and(used, _new_expert(te_ref, m)))
    def _():
        wb_ref[...] = w2_ref[...].astype(BF)

    @pl.when(used)
    def _():
        o_ref[...] = _dot(a_ref[...], wb_ref[...])

    @pl.when(jnp.logical_not(used))
    def _():
        o_ref[...] = jnp.zeros_like(o_ref)


def moe_experts(xg, tile_e, ntiles, w13, w2, tn_up=1024, tn_down=512):
    cap, d = xg.shape
    f = w2.shape[1]
    tm = MOE_TM
    nt_max = cap // tm
    nf = f // tn_up
    mt = lambda m, nt: jnp.minimum(m, nt[0] - 1)
    act = pl.pallas_call(
        _moe_up_kernel,
        grid_spec=pltpu.PrefetchScalarGridSpec(
            num_scalar_prefetch=2, grid=(nf, nt_max),
            in_specs=[pl.BlockSpec((tm, d), lambda j, m, te, nt: (mt(m, nt), 0)),
                      pl.BlockSpec((None, d, tn_up), lambda j, m, te, nt: (te[mt(m, nt)], 0, j)),
                      pl.BlockSpec((None, d, tn_up), lambda j, m, te, nt: (te[mt(m, nt)], 0, j + nf))],
            out_specs=pl.BlockSpec((tm, tn_up), lambda j, m, te, nt: (m, j)),
            scratch_shapes=[pltpu.VMEM((d, tn_up), BF), pltpu.VMEM((d, tn_up), BF)]),
        out_shape=jax.ShapeDtypeStruct((cap, f), BF),
        compiler_params=_cparams(("arbitrary", "arbitrary")),
    )(tile_e, ntiles, xg, w13, w13)
    return pl.pallas_call(
        _moe_down_kernel,
        grid_spec=pltpu.PrefetchScalarGridSpec(
            num_scalar_prefetch=2, grid=(d // tn_down, nt_max),
            in_specs=[pl.BlockSpec((tm, f), lambda j, m, te, nt: (mt(m, nt), 0)),
                      pl.BlockSpec((None, f, tn_down), lambda j, m, te, nt: (te[mt(m, nt)], 0, j))],
            out_specs=pl.BlockSpec((tm, tn_down), lambda j, m, te, nt: (m, j)),
            scratch_shapes=[pltpu.VMEM((f, tn_down), BF)]),
        out_shape=jax.ShapeDtypeStruct((cap, d), F32),
        compiler_params=_cparams(("arbitrary", "arbitrary"), VMEM_LIMIT_BIG),
    )(tile_e, ntiles, act, w2)


def _combine_kernel(pos_ref, h_ref, route_ref, y_ref, *rest, final_norm):
    if final_norm:
        nw_ref, o_ref, buf, sem = rest
    else:
        o_ref, buf, sem = rest
    tc = h_ref.shape[0]
    base = pl.program_id(0) * tc

    def row_copy(r, k):
        return pltpu.make_async_copy(y_ref.at[pl.ds(pos_ref[TOP_K * (base + r) + k], 1), :],
                                     buf.at[k, pl.ds(r, 1), :], sem)

    def start(r, c):
        for k in range(TOP_K):
            row_copy(r, k).start()
        return c

    def wait(r, c):
        for k in range(TOP_K):
            row_copy(r, k).wait()
        return c

    lax.fori_loop(0, tc, start, 0, unroll=8)
    lax.fori_loop(0, tc, wait, 0, unroll=8)
    acc = h_ref[...]
    for k in range(TOP_K):
        acc = acc + route_ref[:, TOP_K + k:TOP_K + k + 1] * buf[k]
    if final_norm:
        acc = _rms(acc, nw_ref[...])
    o_ref[...] = acc


def moe_combine(h, route, yg, pos, final_w=None, tc=256):
    n, d = h.shape
    tc = min(tc, n)
    in_specs = [pl.BlockSpec((tc, d), lambda i, pos_ref: (i, 0)), pl.BlockSpec((tc, LANES), lambda i, pos_ref: (i, 0)),
                pl.BlockSpec(memory_space=pl.ANY)]
    args = [pos, h, route, yg]
    if final_w is not None:
        in_specs.append(pl.BlockSpec((1, d), lambda i, pos_ref: (0, 0)))
        args.append(final_w.reshape(1, d).astype(F32))
    return pl.pallas_call(
        functools.partial(_combine_kernel, final_norm=final_w is not None),
        grid_spec=pltpu.PrefetchScalarGridSpec(
            num_scalar_prefetch=1, grid=(n // tc,),
            in_specs=in_specs,
            out_specs=pl.BlockSpec((tc, d), lambda i, pos_ref: (i, 0)),
            scratch_shapes=[pltpu.VMEM((TOP_K, tc, d), F32), pltpu.SemaphoreType.DMA(())]),
        out_shape=jax.ShapeDtypeStruct((n, d), F32),
        compiler_params=_cparams(("arbitrary",)),
    )(*args)


def moe_ffn(h, norm_w, router, w13, w2, final_w=None):
    n, d = h.shape
    tm = MOE_TM
    hn, route, cnt = moe_router(h, norm_w, router)
    expert = route[:, 0:TOP_K].astype(jnp.int32)
    rank = route[:, 2 * TOP_K:3 * TOP_K].astype(jnp.int32)
    counts = cnt[0, :N_EXPERTS].astype(jnp.int32)
    tiles = (counts + tm - 1) // tm
    tile_end = jnp.cumsum(tiles)
    dest = ((tile_end - tiles) * tm)[expert] + rank
    nt_max = -(-n * TOP_K // tm) + N_EXPERTS
    cap = nt_max * tm
    flat = dest.reshape(-1)
    row_tok = jnp.zeros((cap,), jnp.int32).at[flat].set(jnp.arange(n * TOP_K, dtype=jnp.int32) // TOP_K)
    tile_e = jnp.minimum(jnp.searchsorted(tile_end, jnp.arange(nt_max, dtype=jnp.int32), side="right"),
                         N_EXPERTS - 1).astype(jnp.int32)
    ntiles = tile_end[-1:].astype(jnp.int32)
    xg = gather_rows(hn, row_tok)
    yg = moe_experts(xg, tile_e, ntiles, w13, w2)
    return moe_combine(h, route, yg, flat, final_w)


def _final_kernel(x_ref, w_ref, o_ref):
    o_ref[...] = _rms(x_ref[...], w_ref[...])


def final_norm(h, w, tm=512):
    m, d = h.shape
    tm = min(tm, m)
    return pl.pallas_call(
        _final_kernel,
        grid=(m // tm,),
        in_specs=[pl.BlockSpec((tm, d), lambda i: (i, 0)), pl.BlockSpec((1, d), lambda i: (0, 0))],
        out_specs=pl.BlockSpec((tm, d), lambda i: (i, 0)),
        out_shape=jax.ShapeDtypeStruct((m, d), F32),
        compiler_params=_cparams(("parallel",)),
    )(h, w.reshape(1, d).astype(F32))


def _rot_cols(w):
    half = w.shape[-1] // 2
    return jnp.concatenate([-w[..., half:], w[..., :half]], axis=-1)


_W_IN_SEGMENTS = (
    (U_ML_Q, O_ML_Q, 512, 1.0), (U_ML_K, O_ML_K, 512, 1.0), (U_ML_V, O_ML_V, 1024, 1.0),
    (U_ML_O, O_ML_O, 1024, 1.0), (U_SSM_Z, O_SSM_Z, 1024, 1.0), (U_SSM_X, O_SSM_XBC, SSM_DINNER, 1.0),
    (U_SWA_Q, O_SWA_Q, 1024, SWA_HEAD_DIM ** -0.5 * LOG2E), (U_SSM_BC, O_SSM_XBC + SSM_DINNER, SSM_BC, 1.0),
    (U_MLA_CQ, O_MLA_CQ, MLA_Q_LORA, 1.0), (U_MLA_CKV, O_MLA_CKV, MLA_KV_LORA, 1.0),
    (U_SWA_K, O_SWA_K, 256, 1.0), (U_SWA_V, O_SWA_V, 256, 1.0), (U_GATE, O_GATE, N_BRANCH * D_MODEL, 1.0))


W_IN_TILE = 256
_ROPE_TILE = U_MLA_KR // W_IN_TILE
_GATE_TILE = U_TOTAL // W_IN_TILE


def _w_in_tile_sources():
    src = [0] * (U_TOTAL // W_IN_TILE)
    for dst, s, width, _ in _W_IN_SEGMENTS:
        for off in range(0, width, W_IN_TILE):
            src[(dst + off) // W_IN_TILE] = s + off
    src[_ROPE_TILE] = O_MLA_KR
    return src


def _w_in_layout_kernel(src_ref, w_ref, dt_ref, o_ref):
    t = pl.program_id(1)

    @pl.when(t == _GATE_TILE)
    def _():
        ngate = 2 * ML_HEADS
        o_ref[...] = jnp.concatenate(
            [w_ref[0, 0:ngate, :], dt_ref[0],
             jnp.zeros((W_IN_TILE - ngate - SSM_HEADS, o_ref.shape[1]), F32)], axis=0).astype(BF)

    @pl.when(jnp.logical_and(t != _ROPE_TILE, t != _GATE_TILE))
    def _():
        swa_q = jnp.logical_and(t >= U_SWA_Q // W_IN_TILE, t < U_SSM_BC // W_IN_TILE)
        scale = jnp.where(swa_q, SWA_HEAD_DIM ** -0.5 * LOG2E, 1.0)
        o_ref[...] = (w_ref[0] * scale).astype(BF)

    @pl.when(t == _ROPE_TILE)
    def _():
        half = MLA_ROPE // 2
        o_ref[0:MLA_ROPE, :] = w_ref[0, 0:MLA_ROPE, :].astype(BF)
        o_ref[MLA_ROPE:MLA_ROPE + half, :] = (-w_ref[0, half:MLA_ROPE, :]).astype(BF)
        o_ref[MLA_ROPE + half:2 * MLA_ROPE, :] = w_ref[0, 0:half, :].astype(BF)
        o_ref[2 * MLA_ROPE:, :] = jnp.zeros((W_IN_TILE - 2 * MLA_ROPE, o_ref.shape[1]), BF)


def layout_w_in(w_in):
    depth, d, _ = w_in.shape
    wt = jnp.swapaxes(w_in, 1, 2)
    src = jnp.asarray([s // SUBLANES for s in _w_in_tile_sources() + [O_ML_I]], jnp.int32)
    return pl.pallas_call(
        _w_in_layout_kernel,
        grid_spec=pltpu.PrefetchScalarGridSpec(
            num_scalar_prefetch=1, grid=(depth, _GATE_TILE + 1),
            in_specs=[pl.BlockSpec((pl.Element(1), pl.Element(W_IN_TILE), pl.Element(d)),
                                   lambda l, t, src_ref: (l, src_ref[t] * SUBLANES, 0)),
                      pl.BlockSpec((pl.Element(1), pl.Element(SSM_HEADS), pl.Element(d)),
                                   lambda l, t, src_ref: (l, O_SSM_DT, 0))],
            out_specs=pl.BlockSpec((None, W_IN_TILE, d), lambda l, t, src_ref: (l, t, 0))),
        out_shape=jax.ShapeDtypeStruct((depth, U_TOTAL + W_IN_TILE, d), BF),
        compiler_params=_cparams(("parallel", "parallel")),
    )(src, wt, wt)


def _layout_w_uq(w):
    k = w.shape[0]
    w = w.reshape(k, MLA_HEADS, MLA_NOPE + MLA_ROPE)
    rope = w[..., MLA_NOPE:]
    scale = (MLA_NOPE + MLA_ROPE) ** -0.5 * LOG2E
    return (jnp.concatenate([w, _rot_cols(rope)], axis=-1).reshape(k, -1) * scale).astype(BF)


def _rope_table(seq):
    inv_freq = 1.0 / (ROPE_THETA ** (jnp.arange(0, MLA_ROPE, 2, dtype=F32) / MLA_ROPE))
    ang = jnp.arange(seq, dtype=F32)[:, None] * inv_freq[None, :]
    c, s = jnp.cos(ang), jnp.sin(ang)
    return jnp.concatenate([c, c, s, s], axis=1)


def hybrid_mixer(h, bsz, seq, layer, norm_w, w_main, ml_ib, ml_fb, ml_norm, conv_w, conv_b, dt_bias, a_log,
                 ssm_d, ssm_norm, q_norm, w_uq, kv_norm, w_ukv, sinks, w_branch, w_out, cs):
    n = bsz * seq
    nc = seq // CHUNK
    u2, small = matmul(h, w_main, norm=norm_w, out_dtype=BF, w_layer=layer, w_t=True, n=U_TOTAL, tn=2048,
                       side_w=w_main, side_rows=(U_TOTAL // LANES, LANES))
    u3 = u2.reshape(bsz, seq, U_TOTAL)

    def to_rows(cols, heads):
        return cols.reshape(bsz, seq, heads).transpose(0, 2, 1).reshape(bsz * heads * nc, CHUNK)

    def per_row(vec, heads):
        return jnp.broadcast_to(vec.astype(F32)[None, :, None], (bsz, heads, nc)).reshape(-1, 1)

    ig, bcum, dt, acs = recurrence_gates(
        to_rows(small[:, 0:ML_HEADS], ML_HEADS), to_rows(small[:, ML_HEADS:2 * ML_HEADS], ML_HEADS),
        per_row(ml_ib, ML_HEADS), per_row(ml_fb, ML_HEADS),
        to_rows(small[:, 2 * ML_HEADS:2 * ML_HEADS + SSM_HEADS], SSM_HEADS),
        per_row(dt_bias, SSM_HEADS), per_row(a_log, SSM_HEADS))

    def as_rows(x, heads):
        return x.reshape(bsz, heads, nc, CHUNK).transpose(0, 2, 1, 3)

    def as_cols(x, heads):
        return x.reshape(bsz, heads, seq).transpose(0, 2, 1)

    ya = mlstm_branch(u3, as_rows(ig, ML_HEADS), as_rows(bcum, ML_HEADS), as_cols(bcum, ML_HEADS), ml_norm)
    yb = ssd_branch(u3, as_cols(dt, SSM_HEADS), as_cols(acs, SSM_HEADS), as_rows(acs, SSM_HEADS), conv_w, conv_b,
                    jnp.repeat(ssm_d, SSM_HEADDIM), ssm_norm)
    qf = matmul(u2, w_uq, norm=q_norm, x_col_blk=U_MLA_CQ // MLA_Q_LORA, tn=2048)
    kvf = matmul(u2, w_ukv, norm=kv_norm, x_col_blk=U_MLA_CKV // MLA_KV_LORA, tn=2048)
    yc = mla_attention(qf.reshape(bsz, seq, -1), kvf.reshape(bsz, seq, -1), u3, cs)
    yd = swa_branch(u3, sinks)
    merged = gated_merge(ya.reshape(n, -1), yb.reshape(n, -1), yc.reshape(n, -1), yd.reshape(n, -1), w_branch, u2)
    return matmul(merged, w_out, residual=h, out_dtype=F32, tm=512, tn=2048)


def kernel(x, mem, norm_mix, w_in, ml_igate_bias, ml_fgate_bias, ml_norm, ssm_conv_w, ssm_conv_b, ssm_dt_bias, ssm_a_log, ssm_d, ssm_norm, mla_q_norm, mla_w_uq, mla_kv_norm, mla_w_ukv, swa_sinks, w_branch, w_out, norm_cross, norm_mem, xa_wq, xa_wkv, xa_wo, norm_ffn, ffn_w13, ffn_w2, moe_router, moe_w13, moe_w2, norm_final):
    bsz, seq, d = x.shape
    depth = w_in.shape[0]
    n = bsz * seq
    mlen = mem.shape[1]
    cs = _rope_table(seq)
    h = x.reshape(n, d)
    mem2 = mem.reshape(bsz * mlen, d)
    w_main = layout_w_in(w_in)
    for l in range(depth):
        h = hybrid_mixer(h, bsz, seq, l, norm_mix[l], w_main, ml_igate_bias[l], ml_fgate_bias[l], ml_norm[l],
                         ssm_conv_w[l], ssm_conv_b[l], ssm_dt_bias[l], ssm_a_log[l], ssm_d[l], ssm_norm[l],
                         mla_q_norm[l], _layout_w_uq(mla_w_uq[l]), mla_kv_norm[l], mla_w_ukv[l].astype(BF),
                         swa_sinks[l], w_branch[l].astype(BF), w_out[l].astype(BF), cs)
        kv = matmul(mem2, xa_wkv[l].astype(BF), norm=norm_mem[l])
        wq = (xa_wq[l] * (XA_HEAD_DIM ** -0.5 * LOG2E)).astype(BF)
        h = cross_attention(h, seq, norm_cross[l], wq, kv.reshape(bsz, mlen, -1), xa_wo[l].astype(BF))
        if l % 2 == 0:
            h = dense_ffn(h, norm_ffn[l], ffn_w13[l // 2].astype(BF), ffn_w2[l // 2].astype(BF))
        else:
            h = moe_ffn(h, norm_ffn[l], moe_router[l // 2], moe_w13[l // 2], moe_w2[l // 2],
                        final_w=norm_final if l == depth - 1 else None)
    if depth % 2 == 1:
        h = final_norm(h, norm_final)
    return h.reshape(bsz, seq, d)
```

```python
import functools
import math

import jax
import jax.numpy as jnp
from jax import lax
from jax.experimental import pallas as pl
from jax.experimental.pallas import tpu as pltpu

F32 = jnp.float32
BF = jnp.bfloat16

D_MODEL = 2048
RMS_EPS = 1e-6
ML_HEADS, ML_DQK, ML_DV = 4, 128, 256
SSM_HEADS, SSM_HEADDIM, SSM_GROUPS, SSM_STATE, SSM_CONV = 16, 64, 2, 128, 4
SSM_DINNER = SSM_HEADS * SSM_HEADDIM
SSM_BC = 2 * SSM_GROUPS * SSM_STATE
MLA_HEADS, MLA_Q_LORA, MLA_KV_LORA, MLA_NOPE, MLA_ROPE, MLA_V = 8, 512, 256, 128, 64, 128
ROPE_THETA = 10000.0
SWA_HEADS, SWA_KV_HEADS, SWA_HEAD_DIM, SWA_WINDOW = 16, 4, 64, 128
N_BRANCH, BRANCH_W = 4, 1024
XA_HEADS, XA_HEAD_DIM = 4, 128
FFN_DIM = 7168
N_EXPERTS, TOP_K = 8, 2

CHUNK = 128
LANES = 128
SUBLANES = 8
VMEM_LIMIT = 56 * 1024 * 1024
VMEM_LIMIT_BIG = 60 * 1024 * 1024
LOG2E = math.log2(math.e)

U_ML_Q, U_ML_K, U_ML_V, U_ML_O = 0, 512, 1024, 2048
U_SSM_Z, U_SSM_X, U_SWA_Q, U_SSM_BC = 3072, 4096, 5120, 6144
U_MLA_CQ, U_MLA_CKV, U_SWA_K, U_SWA_V, U_MLA_KR = 6656, 7168, 7424, 7680, 7936
U_GATE = 8192
U_TOTAL = U_GATE + N_BRANCH * D_MODEL
_SPLITS = (512, 512, 1024, 1024, 4, 4, 1024, 1536, 16, 512, 256, 64, 1024, 256, 256, 8192)
_OFF = [0]
for _s in _SPLITS:
    _OFF.append(_OFF[-1] + _s)
(O_ML_Q, O_ML_K, O_ML_V, O_ML_O, O_ML_I, O_ML_F, O_SSM_Z, O_SSM_XBC, O_SSM_DT, O_MLA_CQ, O_MLA_CKV,
 O_MLA_KR, O_SWA_Q, O_SWA_K, O_SWA_V, O_GATE, _O_END) = _OFF


def _cparams(sem, vmem_limit=VMEM_LIMIT):
    return pltpu.CompilerParams(dimension_semantics=sem, vmem_limit_bytes=vmem_limit)


def _dot(a, b):
    return jnp.dot(a, b, preferred_element_type=F32)


def _dot_nt(a, b):
    return lax.dot_general(a, b, (((1,), (1,)), ((), ())), preferred_element_type=F32)


def _dot_tn(a, b):
    return lax.dot_general(a, b, (((0,), (0,)), ((), ())), preferred_element_type=F32)


def _split3(a):
    a1 = a.astype(BF)
    r = a - a1.astype(F32)
    a2 = r.astype(BF)
    a3 = (r - a2.astype(F32)).astype(BF)
    return a1, a2, a3


def _dot_sel(a, sel):
    a1, a2, a3 = _split3(a)
    return _dot(a1, sel) + _dot(a2, sel) + _dot(a3, sel)


def _rms(x, w):
    return x * lax.rsqrt(jnp.mean(x * x, axis=-1, keepdims=True) + RMS_EPS) * w


def _sigmoid(x):
    return 1.0 / (1.0 + jnp.exp(-x))


def _mm_kernel(*refs, has_norm, has_res, has_side, w_t):
    mm = _dot_nt if w_t else _dot
    it = iter(refs)
    x_ref = next(it)
    g_ref = next(it) if has_norm else None
    w_ref = next(it)
    r_ref = next(it) if has_res else None
    ws_ref = next(it) if has_side else None
    o_ref = next(it)
    os_ref = next(it) if has_side else None
    if has_norm:
        xn_ref = next(it)

        @pl.when(pl.program_id(1) == 0)
        def _():
            xn = _rms(x_ref[...].astype(F32), g_ref[...]).astype(BF)
            xn_ref[...] = xn
            if has_side:
                os_ref[...] = mm(xn, ws_ref[...])

        xv = xn_ref[...]
    else:
        xv = x_ref[...]
    acc = mm(xv, w_ref[...])
    if has_res:
        acc = acc + r_ref[...]
    o_ref[...] = acc.astype(o_ref.dtype)


def matmul(x, w, *, norm=None, residual=None, out_dtype=None, tm=1024, tn=1024, x_col_blk=0, w_layer=None,
           side_w=None, side_rows=None, w_t=False, n=None):
    out_dtype = out_dtype or BF
    m = x.shape[0]
    k, n_all = w.shape[-2:][::-1] if w_t else w.shape[-2:]
    n = n or n_all
    tm, tn = min(tm, m), min(tn, n)
    assert m % tm == 0 and n % tn == 0
    in_specs = [pl.BlockSpec((tm, k), lambda i, j: (i, x_col_blk))]
    args = [x]
    scratch = []
    if norm is not None:
        in_specs.append(pl.BlockSpec((1, k), lambda i, j: (0, 0)))
        args.append(norm.reshape(1, k).astype(F32))
        scratch.append(pltpu.VMEM((tm, k), BF))
    if w_layer is None:
        in_specs.append(pl.BlockSpec((k, tn), lambda i, j: (0, j)))
    else:
        in_specs.append(pl.BlockSpec((None, tn, k), lambda i, j: (w_layer, j, 0)) if w_t else
                        pl.BlockSpec((None, k, tn), lambda i, j: (w_layer, 0, j)))
    args.append(w)
    if residual is not None:
        in_specs.append(pl.BlockSpec((tm, tn), lambda i, j: (i, j)))
        args.append(residual)
    out_specs = pl.BlockSpec((tm, tn), lambda i, j: (i, j))
    out_shape = jax.ShapeDtypeStruct((m, n), out_dtype)
    if side_w is not None:
        assert norm is not None and w_layer is not None
        if side_rows is None:
            ns = side_w.shape[-2] if w_t else side_w.shape[-1]
            in_specs.append(pl.BlockSpec((None,) + side_w.shape[1:], lambda i, j: (w_layer, 0, 0)))
        else:
            blk, ns = side_rows
            in_specs.append(pl.BlockSpec((None, ns, k), lambda i, j: (w_layer, blk, 0)))
        args.append(side_w)
        out_specs = [out_specs, pl.BlockSpec((tm, ns), lambda i, j: (i, 0))]
        out_shape = [out_shape, jax.ShapeDtypeStruct((m, ns), F32)]
    return pl.pallas_call(
        functools.partial(_mm_kernel, has_norm=norm is not None, has_res=residual is not None,
                          has_side=side_w is not None, w_t=w_t),
        grid=(m // tm, n // tn),
        in_specs=in_specs,
        out_specs=out_specs,
        out_shape=out_shape,
        scratch_shapes=scratch,
        compiler_params=_cparams(("parallel", "arbitrary")),
    )(*args)


def _cumsum_lanes(x):
    lane = lax.broadcasted_iota(jnp.int32, x.shape, 1)
    s = 1
    while s < x.shape[1]:
        x = x + jnp.where(lane >= s, pltpu.roll(x, s, axis=1), 0.0)
        s *= 2
    return x


def _softplus(x):
    return jnp.maximum(x, 0.0) + jnp.log(1.0 + jnp.exp(-jnp.abs(x)))


def _gates_kernel(i_ref, f_ref, ib_ref, fb_ref, dt_ref, dtb_ref, alog_ref, ig_ref, b_ref, dto_ref, acs_ref):
    ig_ref[...] = i_ref[...] + ib_ref[...]
    b_ref[...] = _cumsum_lanes(-_softplus(-(f_ref[...] + fb_ref[...])))
    dt = _softplus(dt_ref[...] + dtb_ref[...])
    dto_ref[...] = dt
    acs_ref[...] = _cumsum_lanes(dt * (-jnp.exp(alog_ref[...])))


def recurrence_gates(i_rows, f_rows, ib, fb, dt_rows, dtb, alog):
    r1, r2 = i_rows.shape[0], dt_rows.shape[0]
    shp = lambda r: jax.ShapeDtypeStruct((r, CHUNK), F32)
    return pl.pallas_call(
        _gates_kernel,
        out_shape=(shp(r1), shp(r1), shp(r2), shp(r2)),
    )(i_rows, f_rows, ib, fb, dt_rows, dtb, alog)


def _mlstm_kernel(q_ref, k_ref, v_ref, o_ref, igr_ref, br_ref, bc_ref, nw_ref, y_ref, *state):
    ct_ref, n_ref, m_ref = state[0::3], state[1::3], state[2::3]
    L = CHUNK

    @pl.when(pl.program_id(1) == 0)
    def _():
        for ref in state:
            ref[...] = jnp.zeros_like(ref)

    row = lax.broadcasted_iota(jnp.int32, (L, L), 0)
    col = lax.broadcasted_iota(jnp.int32, (L, L), 1)
    causal = col <= row
    diag = col == row
    scale = ML_DQK ** -0.5
    outs = []
    for h in range(ML_HEADS):
        q = q_ref[:, h * ML_DQK:(h + 1) * ML_DQK]
        k = (k_ref[:, h * ML_DQK:(h + 1) * ML_DQK].astype(F32) * scale).astype(BF)
        v = v_ref[:, h * ML_DV:(h + 1) * ML_DV]
        bcol = bc_ref[:, h:h + 1]
        brow = br_ref[h:h + 1, :]
        igrow = igr_ref[h:h + 1, :]
        m_prev = m_ref[h][0:1, 0:1]
        n_prev = n_ref[h][0:1, :]
        ct_prev = ct_ref[h][...]

        dmat = jnp.where(causal, bcol - brow + igrow, -jnp.inf)
        m_intra = jnp.max(dmat, axis=1, keepdims=True)
        g = bcol + m_prev
        m_s = jnp.maximum(g, m_intra)
        p = jnp.exp(dmat - m_s) * _dot_nt(q, k)
        inter = jnp.exp(g - m_s)
        num = _dot(p.astype(BF), v) + inter * _dot(q, ct_prev.astype(BF))
        den = jnp.sum(p, axis=1, keepdims=True) + inter * jnp.sum(q.astype(F32) * n_prev, axis=1, keepdims=True)
        hh = num / jnp.maximum(jnp.abs(den), jnp.exp(-m_s))
        hn = _rms(hh, nw_ref[:, h * ML_DV:(h + 1) * ML_DV])
        outs.append(_sigmoid(o_ref[:, h * ML_DV:(h + 1) * ML_DV].astype(F32)) * hn)

        b_tot = brow[:, L - 1:L]
        a = b_tot - brow + igrow
        m_loc = jnp.max(a, axis=1, keepdims=True)
        wl = jnp.exp(a - m_loc)
        kw = _dot(jnp.where(diag, wl, 0.0).astype(BF), k)
        m_new = jnp.maximum(b_tot + m_prev, m_loc)
        da = jnp.exp(b_tot + m_prev - m_new)
        db = jnp.exp(m_loc - m_new)
        ct_ref[h][...] = da * ct_prev + db * _dot_tn(kw.astype(BF), v)
        n_ref[h][...] = jnp.broadcast_to(da * n_prev + db * jnp.sum(kw, axis=0, keepdims=True), n_ref[h].shape)
        m_ref[h][...] = jnp.broadcast_to(m_new, m_ref[h].shape)
    y_ref[...] = jnp.concatenate(outs, axis=1).astype(y_ref.dtype)


def mlstm_branch(u3, ig_rows, b_rows, b_cols, norm_w):
    bsz, seq, _ = u3.shape
    nc = seq // CHUNK
    L = CHUNK
    ublk = lambda width, off: pl.BlockSpec((None, L, width), lambda b, c: (b, c, off // width))
    rows = pl.BlockSpec((None, None, ML_HEADS, L), lambda b, c: (b, c, 0, 0))
    return pl.pallas_call(
        _mlstm_kernel,
        grid=(bsz, nc),
        in_specs=[ublk(512, U_ML_Q), ublk(512, U_ML_K), ublk(1024, U_ML_V), ublk(1024, U_ML_O), rows, rows,
                  pl.BlockSpec((None, L, ML_HEADS), lambda b, c: (b, c, 0)),
                  pl.BlockSpec((1, ML_HEADS * ML_DV), lambda b, c: (0, 0))],
        out_specs=pl.BlockSpec((None, L, BRANCH_W), lambda b, c: (b, c, 0)),
        out_shape=jax.ShapeDtypeStruct((bsz, seq, BRANCH_W), BF),
        scratch_shapes=[pltpu.VMEM((ML_DQK, ML_DV), F32), pltpu.VMEM((8, ML_DQK), F32),
                        pltpu.VMEM((8, LANES), F32)] * ML_HEADS,
        compiler_params=_cparams(("parallel", "arbitrary")),
    )(u3, u3, u3, u3, ig_rows, b_rows, b_cols, norm_w.reshape(1, -1).astype(F32))


def _ssd_kernel(z_ref, x_ref, bc_ref, dt_ref, ac_ref, ar_ref, cw_ref, cb_ref, d_ref, nw_ref, y_ref, xs_ref, st_ref):
    L = CHUNK
    P, R, NS = SSM_HEADDIM, SSM_HEADS // SSM_GROUPS, SSM_STATE
    GW = R * P

    @pl.when(pl.program_id(1) == 0)
    def _():
        xs_ref[0:8, :] = jnp.zeros((8, xs_ref.shape[1]), F32)
        st_ref[...] = jnp.zeros_like(st_ref)

    xs_ref[8:, :] = jnp.concatenate([x_ref[...], bc_ref[...]], axis=1).astype(F32)
    conv = cb_ref[...] + cw_ref[SSM_CONV - 1:SSM_CONV, :] * xs_ref[8:8 + L, :]
    for sft in range(1, SSM_CONV):
        conv = conv + cw_ref[SSM_CONV - 1 - sft:SSM_CONV - sft, :] * xs_ref[8 - sft:8 - sft + L, :]
    xs_ref[0:8, :] = xs_ref[L:L + 8, :]
    xbc = conv * _sigmoid(conv)
    xh = xbc[:, :SSM_DINNER]
    bmat = xbc[:, SSM_DINNER:SSM_DINNER + SSM_GROUPS * NS].astype(BF)
    cmat = xbc[:, SSM_DINNER + SSM_GROUPS * NS:].astype(BF)

    dtc = dt_ref[...]
    ac = ac_ref[...]
    ar = ar_ref[...]
    a_last = ac[L - 1:L, :]
    hsel = (lax.broadcasted_iota(jnp.int32, (SSM_HEADS, SSM_DINNER), 1) // P
            == lax.broadcasted_iota(jnp.int32, (SSM_HEADS, SSM_DINNER), 0))
    expand = jnp.where(hsel, 1.0, 0.0).astype(BF)
    stack = jnp.concatenate([dtc, jnp.exp(a_last - ac), jnp.exp(ac),
                             jnp.broadcast_to(jnp.exp(a_last), (8, SSM_HEADS))], axis=0)
    ex = _dot_sel(stack, expand)
    dt_full, dst_full, ind_full = ex[0:L], ex[L:2 * L], ex[2 * L:3 * L]
    cdec_full = ex[3 * L:3 * L + 1]
    xdt = xh * dt_full
    xdt_b = xdt.astype(BF)
    xw_b = (xdt * dst_full).astype(BF)

    row = lax.broadcasted_iota(jnp.int32, (L, L), 0)
    col = lax.broadcasted_iota(jnp.int32, (L, L), 1)
    causal = col <= row
    ys = []
    for g in range(SSM_GROUPS):
        bg = bmat[:, g * NS:(g + 1) * NS]
        cg = cmat[:, g * NS:(g + 1) * NS]
        cb = _dot_nt(cg, bg)
        st_prev = st_ref[g]
        yoff = _dot(cg, st_prev.astype(BF))
        st_ref[g] = cdec_full[:, g * GW:(g + 1) * GW] * st_prev + _dot_tn(bg, xw_b[:, g * GW:(g + 1) * GW])
        for r in range(R):
            h = g * R + r
            dec = jnp.exp(jnp.where(causal, ac[:, h:h + 1] - ar[h:h + 1, :], -jnp.inf))
            yd = _dot((dec * cb).astype(BF), xdt_b[:, h * P:(h + 1) * P])
            ys.append(yd + yoff[:, r * P:(r + 1) * P] * ind_full[:, h * P:(h + 1) * P])
    y = jnp.concatenate(ys, axis=1) + xh * d_ref[...]
    zf = z_ref[...].astype(F32)
    y = y * (zf * _sigmoid(zf))
    y_ref[...] = jnp.concatenate(
        [_rms(y[:, g * GW:(g + 1) * GW], nw_ref[:, g * GW:(g + 1) * GW]) for g in range(SSM_GROUPS)],
        axis=1).astype(y_ref.dtype)


def ssd_branch(u3, dt_cols, acs_cols, acs_rows, conv_w, conv_b, d_full, norm_w):
    bsz, seq, _ = u3.shape
    nc = seq // CHUNK
    L = CHUNK
    ublk = lambda width, off: pl.BlockSpec((None, L, width), lambda b, c: (b, c, off // width))
    cols = pl.BlockSpec((None, L, SSM_HEADS), lambda b, c: (b, c, 0))
    const = lambda shape: pl.BlockSpec(shape, lambda b, c: (0, 0))
    cch = SSM_DINNER + SSM_BC
    return pl.pallas_call(
        _ssd_kernel,
        grid=(bsz, nc),
        in_specs=[ublk(1024, U_SSM_Z), ublk(1024, U_SSM_X), ublk(512, U_SSM_BC), cols, cols,
                  pl.BlockSpec((None, None, SSM_HEADS, L), lambda b, c: (b, c, 0, 0)),
                  const((SSM_CONV, cch)), const((1, cch)), const((1, SSM_DINNER)), const((1, SSM_DINNER))],
        out_specs=pl.BlockSpec((None, L, BRANCH_W), lambda b, c: (b, c, 0)),
        out_shape=jax.ShapeDtypeStruct((bsz, seq, BRANCH_W), BF),
        scratch_shapes=[pltpu.VMEM((L + 8, cch), F32),
                        pltpu.VMEM((SSM_GROUPS, SSM_STATE, SSM_DINNER // SSM_GROUPS), F32)],
        compiler_params=_cparams(("parallel", "arbitrary")),
    )(u3, u3, u3, dt_cols, acs_cols, acs_rows, conv_w.astype(F32), conv_b.reshape(1, cch).astype(F32),
      d_full.reshape(1, -1).astype(F32), norm_w.reshape(1, -1).astype(F32))


def _rope128(x, cs):
    t = x.astype(F32) * cs
    return t + pltpu.roll(t, MLA_ROPE, axis=1)


def _mla_kernel(q_ref, csq_ref, kv_ref, kr_ref, csk_ref, o_ref, qe_s, *stats, tq):
    qi = pl.program_id(1)
    ki = pl.program_id(2)
    qw = MLA_NOPE + LANES
    kw = MLA_NOPE + MLA_V
    m_s, l_s, acc_s = stats[0::3], stats[1::3], stats[2::3]
    nt = tq // LANES

    @pl.when(ki == 0)
    def _():
        for h in range(MLA_HEADS):
            qr = _rope128(q_ref[:, h * qw + MLA_NOPE:(h + 1) * qw], csq_ref[...]).astype(BF)
            qe_s[h] = jnp.concatenate([q_ref[:, h * qw:h * qw + MLA_NOPE], qr], axis=1)
            m_s[h][...] = jnp.full((tq, LANES), -jnp.inf, F32)
            l_s[h][...] = jnp.zeros((tq, LANES), F32)
            acc_s[h][...] = jnp.zeros((tq, MLA_V), F32)

    def step(masked):
        lane = lax.broadcasted_iota(jnp.int32, (tq, LANES), 1)
        kr = jnp.where(lane < MLA_ROPE, _rope128(kr_ref[...], csk_ref[...]), 0.0).astype(BF)
        if masked:
            causal = (lax.broadcasted_iota(jnp.int32, (tq, tq), 1) <= lax.broadcasted_iota(jnp.int32, (tq, tq), 0))
        for h in range(MLA_HEADS):
            ke = jnp.concatenate([kv_ref[:, h * kw:h * kw + MLA_NOPE], kr], axis=1)
            s = _dot_nt(qe_s[h], ke)
            if masked:
                s = jnp.where(causal, s, -jnp.inf)
            m_old = m_s[h][...]
            m_new = jnp.maximum(m_old, jnp.max(s, axis=1, keepdims=True))
            alpha = jnp.exp2(m_old - m_new)
            p = jnp.exp2(s - jnp.concatenate([m_new] * nt, axis=1))
            psum = p[:, 0:LANES]
            for t in range(1, nt):
                psum = psum + p[:, t * LANES:(t + 1) * LANES]
            l_s[h][...] = alpha * l_s[h][...] + psum
            acc_s[h][...] = alpha * acc_s[h][...] + _dot(p.astype(BF), kv_ref[:, h * kw + MLA_NOPE:(h + 1) * kw])
            m_s[h][...] = m_new

    @pl.when(ki < qi)
    def _():
        step(False)

    @pl.when(ki == qi)
    def _():
        step(True)
        for h in range(MLA_HEADS):
            l = jnp.sum(l_s[h][...], axis=1, keepdims=True)
            o_ref[:, h * MLA_V:(h + 1) * MLA_V] = (acc_s[h][...] / l).astype(o_ref.dtype)


def mla_attention(qf3, kvf3, u3, cs, tq=1024):
    bsz, seq, _ = qf3.shape
    tq = min(tq, seq)
    nq = seq // tq
    kvi = lambda b, qi, ki: jnp.minimum(ki, qi)
    return pl.pallas_call(
        functools.partial(_mla_kernel, tq=tq),
        grid=(bsz, nq, nq),
        in_specs=[pl.BlockSpec((None, tq, qf3.shape[2]), lambda b, qi, ki: (b, qi, 0)),
                  pl.BlockSpec((tq, LANES), lambda b, qi, ki: (qi, 0)),
                  pl.BlockSpec((None, tq, kvf3.shape[2]), lambda b, qi, ki: (b, kvi(b, qi, ki), 0)),
                  pl.BlockSpec((None, tq, LANES), lambda b, qi, ki: (b, kvi(b, qi, ki), U_MLA_KR // LANES)),
                  pl.BlockSpec((tq, LANES), lambda b, qi, ki: (kvi(b, qi, ki), 0))],
        out_specs=pl.BlockSpec((None, tq, MLA_HEADS * MLA_V), lambda b, qi, ki: (b, qi, 0)),
        out_shape=jax.ShapeDtypeStruct((bsz, seq, MLA_HEADS * MLA_V), BF),
        scratch_shapes=[pltpu.VMEM((MLA_HEADS, tq, MLA_NOPE + LANES), BF)]
        + [pltpu.VMEM((tq, LANES), F32), pltpu.VMEM((tq, LANES), F32), pltpu.VMEM((tq, MLA_V), F32)] * MLA_HEADS,
        compiler_params=_cparams(("parallel", "parallel", "arbitrary")),
    )(qf3, cs, kvf3, u3, cs)


def _swa_kernel(q_ref, kc_ref, kp_ref, vc_ref, vp_ref, sink_ref, o_ref):
    W, d = SWA_WINDOW, SWA_HEAD_DIM
    n = pl.program_id(1)
    i = lax.broadcasted_iota(jnp.int32, (2 * W, 2 * W), 0) & (W - 1)
    j = lax.broadcasted_iota(jnp.int32, (2 * W, 2 * W), 1)
    valid = (j > i) & (j <= i + W) & ((n > 0) | (j >= W))
    first_tile = lax.broadcasted_iota(jnp.int32, (2 * W, LANES), 0) < W
    lower = lax.broadcasted_iota(jnp.int32, (2 * W, LANES), 1) < d
    ones = jnp.ones((2 * W, LANES), BF)
    for pair in range(SWA_KV_HEADS // 2):
        sl = slice(pair * LANES, (pair + 1) * LANES)
        kt = jnp.concatenate([kp_ref[:, sl], kc_ref[:, sl]], axis=0)
        vt = jnp.concatenate([vp_ref[:, sl], vc_ref[:, sl]], axis=0)
        kt_sw = pltpu.roll(kt.astype(F32), d, axis=1).astype(BF)
        vt_sw = pltpu.roll(vt.astype(F32), d, axis=1).astype(BF)
        for e in range(2):
            kh = 2 * pair + e
            k_lo = jnp.where(lower, kt if e == 0 else kt_sw, jnp.zeros_like(kt))
            k_hi = jnp.where(lower, jnp.zeros_like(kt), kt_sw if e == 0 else kt)
            v_lo, v_hi = (vt, vt_sw) if e == 0 else (vt_sw, vt)
            qs = jnp.concatenate([q_ref[:, 2 * kh * LANES:(2 * kh + 1) * LANES],
                                  q_ref[:, (2 * kh + 1) * LANES:(2 * kh + 2) * LANES]], axis=0)
            halves = []
            for half, (ke, ve) in enumerate(((k_lo, v_lo), (k_hi, v_hi))):
                ha, hb = 4 * kh + half, 4 * kh + 2 + half
                s = jnp.where(valid, _dot_nt(qs, ke), -jnp.inf)
                sink = jnp.where(first_tile, sink_ref[0:1, ha:ha + 1], sink_ref[0:1, hb:hb + 1]) * LOG2E
                m = jnp.maximum(jnp.max(s, axis=1, keepdims=True), sink)
                p = jnp.exp2(s - jnp.concatenate([m, m], axis=1)).astype(BF)
                den = _dot(p, ones) + jnp.exp2(sink - m)
                halves.append(_dot(p, ve) / den)
            ot = jnp.where(lower, halves[0], halves[1]).astype(o_ref.dtype)
            o_ref[:, 2 * kh * LANES:(2 * kh + 1) * LANES] = ot[:W]
            o_ref[:, (2 * kh + 1) * LANES:(2 * kh + 2) * LANES] = ot[W:]


def swa_branch(u3, sinks):
    bsz, seq, _ = u3.shape
    W = SWA_WINDOW
    kvw = SWA_KV_HEADS * SWA_HEAD_DIM
    cur = lambda width, off: pl.BlockSpec((None, W, width), lambda b, n: (b, n, off // width))
    prev = lambda width, off: pl.BlockSpec((None, W, width), lambda b, n: (b, jnp.maximum(n - 1, 0), off // width))
    return pl.pallas_call(
        _swa_kernel,
        grid=(bsz, seq // W),
        in_specs=[cur(1024, U_SWA_Q), cur(kvw, U_SWA_K), prev(kvw, U_SWA_K), cur(kvw, U_SWA_V), prev(kvw, U_SWA_V),
                  pl.BlockSpec((1, SWA_HEADS), lambda b, n: (0, 0))],
        out_specs=pl.BlockSpec((None, W, BRANCH_W), lambda b, n: (b, n, 0)),
        out_shape=jax.ShapeDtypeStruct((bsz, seq, BRANCH_W), BF),
        compiler_params=_cparams(("parallel", "parallel")),
    )(u3, u3, u3, u3, u3, sinks.reshape(1, -1).astype(F32))


def _merge_kernel(ya_ref, yb_ref, yc_ref, yd_ref, w_ref, g0_ref, g1_ref, g2_ref, g3_ref, o_ref):
    acc = None
    for n, (y_ref, g_ref) in enumerate(((ya_ref, g0_ref), (yb_ref, g1_ref), (yc_ref, g2_ref), (yd_ref, g3_ref))):
        t = _sigmoid(g_ref[...].astype(F32)) * _dot(y_ref[...], w_ref[n])
        acc = t if acc is None else acc + t
    o_ref[...] = acc.astype(o_ref.dtype)


def gated_merge(ya, yb, yc, yd, w_branch, u2, tm=1024, tn=512):
    m = ya.shape[0]
    tm = min(tm, m)
    ysp = pl.BlockSpec((tm, BRANCH_W), lambda i, j: (i, 0))
    gsp = lambda n: pl.BlockSpec((tm, tn), lambda i, j: (i, (U_GATE + n * D_MODEL) // tn + j))
    return pl.pallas_call(
        _merge_kernel,
        grid=(m // tm, D_MODEL // tn),
        in_specs=[ysp, ysp, ysp, ysp, pl.BlockSpec((N_BRANCH, BRANCH_W, tn), lambda i, j: (0, 0, j)),
                  gsp(0), gsp(1), gsp(2), gsp(3)],
        out_specs=pl.BlockSpec((tm, tn), lambda i, j: (i, j)),
        out_shape=jax.ShapeDtypeStruct((m, D_MODEL), BF),
        compiler_params=_cparams(("parallel", "parallel")),
    )(ya, yb, yc, yd, w_branch, u2, u2, u2, u2)


def _xattn_kernel(h_ref, g_ref, wq_ref, kv_ref, wo_ref, o_ref):
    d = XA_HEAD_DIM
    h = h_ref[...]
    q = _dot(_rms(h, g_ref[...]).astype(BF), wq_ref[...]).astype(BF)
    ones = jnp.ones((kv_ref.shape[0], LANES), BF)
    outs = []
    for hd in range(XA_HEADS):
        s = _dot_nt(q[:, hd * d:(hd + 1) * d], kv_ref[:, hd * d:(hd + 1) * d])
        p = jnp.exp2(s - jnp.max(s, axis=1, keepdims=True)).astype(BF)
        den = _dot(p, ones)
        outs.append((_dot(p, kv_ref[:, (XA_HEADS + hd) * d:(XA_HEADS + hd + 1) * d]) / den).astype(BF))
    o_ref[...] = h + _dot(jnp.concatenate(outs, axis=1), wo_ref[...])


def cross_attention(h, seq, norm_w, wq, kv3, wo, tq=1024):
    n, dm = h.shape
    _, mlen, kvw = kv3.shape
    tq = min(tq, seq)
    per_b = seq // tq
    const = lambda shape: pl.BlockSpec(shape, lambda i: (0, 0))
    return pl.pallas_call(
        _xattn_kernel,
        grid=(n // tq,),
        in_specs=[pl.BlockSpec((tq, dm), lambda i: (i, 0)), const((1, dm)), const(wq.shape),
                  pl.BlockSpec((None, mlen, kvw), lambda i: (i // per_b, 0, 0)), const(wo.shape)],
        out_specs=pl.BlockSpec((tq, dm), lambda i: (i, 0)),
        out_shape=jax.ShapeDtypeStruct((n, dm), F32),
        compiler_params=_cparams(("parallel",)),
    )(h, norm_w.reshape(1, dm).astype(F32), wq, kv3, wo)


def _ffn_kernel(h_ref, g_ref, w1_ref, w3_ref, w2_ref, o_ref, xn_ref, acc_ref):
    j = pl.program_id(1)

    @pl.when(j == 0)
    def _():
        hv = h_ref[...]
        xn_ref[...] = _rms(hv, g_ref[...]).astype(BF)
        acc_ref[...] = hv

    xn = xn_ref[...]
    h1 = _dot(xn, w1_ref[...])
    h3 = _dot(xn, w3_ref[...])
    act = (h1 * _sigmoid(h1) * h3).astype(BF)
    acc_ref[...] += _dot(act, w2_ref[...])

    @pl.when(j == pl.num_programs(1) - 1)
    def _():
        o_ref[...] = acc_ref[...]


def dense_ffn(h, norm_w, w13, w2, tm=512, tf=1024):
    m, d = h.shape
    f = w2.shape[0]
    tm = min(tm, m)
    nf = f // tf
    return pl.pallas_call(
        _ffn_kernel,
        grid=(m // tm, nf),
        in_specs=[pl.BlockSpec((tm, d), lambda i, j: (i, 0)), pl.BlockSpec((1, d), lambda i, j: (0, 0)),
                  pl.BlockSpec((d, tf), lambda i, j: (0, j)), pl.BlockSpec((d, tf), lambda i, j: (0, j + nf)),
                  pl.BlockSpec((tf, d), lambda i, j: (j, 0))],
        out_specs=pl.BlockSpec((tm, d), lambda i, j: (i, 0)),
        out_shape=jax.ShapeDtypeStruct((m, d), F32),
        scratch_shapes=[pltpu.VMEM((tm, d), BF), pltpu.VMEM((tm, d), F32)],
        compiler_params=_cparams(("parallel", "arbitrary")),
    )(h, norm_w.reshape(1, d).astype(F32), w13, w13, w2)


ROUTE_TM = 512
MOE_TM = 512


def _router_kernel(h_ref, g_ref, r_ref, hn_ref, route_ref, cnt_ref, carry_ref):
    tm = h_ref.shape[0]

    @pl.when(pl.program_id(0) == 0)
    def _():
        carry_ref[...] = jnp.zeros_like(carry_ref)

    hn = _rms(h_ref[...], g_ref[...])
    hn_ref[...] = hn
    a1, a2, _ = _split3(hn)
    r1, r2, _ = _split3(r_ref[...])
    lane = lax.broadcasted_iota(jnp.int32, (tm, LANES), 1)
    logits = jnp.where(lane < N_EXPERTS, _dot(a1, r1) + (_dot(a1, r2) + _dot(a2, r1)), -jnp.inf)
    v0 = jnp.max(logits, axis=1, keepdims=True)
    i0 = jnp.min(jnp.where(logits == v0, lane, LANES), axis=1, keepdims=True)
    rest = jnp.where(lane == i0, -jnp.inf, logits)
    v1 = jnp.max(rest, axis=1, keepdims=True)
    i1 = jnp.min(jnp.where(rest == v1, lane, LANES), axis=1, keepdims=True)
    ex = jnp.exp(v1 - v0)
    g0 = 1.0 / (1.0 + ex)
    g1 = ex / (1.0 + ex)
    sel0, sel1 = lane == i0, lane == i1
    onehot = jnp.where(sel0 | sel1, 1.0, 0.0)
    below = (lax.broadcasted_iota(jnp.int32, (tm, tm), 1) < lax.broadcasted_iota(jnp.int32, (tm, tm), 0))
    before = carry_ref[0:1, :] + _dot(jnp.where(below, 1.0, 0.0).astype(BF), onehot.astype(BF))
    rank0 = jnp.sum(jnp.where(sel0, before, 0.0), axis=1, keepdims=True)
    rank1 = jnp.sum(jnp.where(sel1, before, 0.0), axis=1, keepdims=True)
    total = carry_ref[0:1, :] + jnp.sum(onehot, axis=0, keepdims=True)
    carry_ref[...] = jnp.broadcast_to(total, carry_ref.shape)
    cnt_ref[...] = jnp.broadcast_to(total, cnt_ref.shape)
    out = jnp.zeros((tm, LANES), F32)
    for pos, val in enumerate((i0.astype(F32), i1.astype(F32), g0, g1, rank0, rank1)):
        out = jnp.where(lane == pos, val, out)
    route_ref[...] = out


def moe_router(h, norm_w, router):
    m, d = h.shape
    tm = min(ROUTE_TM, m)
    rpad = jnp.zeros((d, LANES), F32).at[:, :N_EXPERTS].set(router.astype(F32))
    return pl.pallas_call(
        _router_kernel,
        grid=(m // tm,),
        in_specs=[pl.BlockSpec((tm, d), lambda i: (i, 0)), pl.BlockSpec((1, d), lambda i: (0, 0)),
                  pl.BlockSpec((d, LANES), lambda i: (0, 0))],
        out_specs=[pl.BlockSpec((tm, d), lambda i: (i, 0)), pl.BlockSpec((tm, LANES), lambda i: (i, 0)),
                   pl.BlockSpec((8, LANES), lambda i: (0, 0))],
        out_shape=[jax.ShapeDtypeStruct((m, d), F32), jax.ShapeDtypeStruct((m, LANES), F32),
                   jax.ShapeDtypeStruct((8, LANES), F32)],
        scratch_shapes=[pltpu.VMEM((8, LANES), F32)],
        compiler_params=_cparams(("arbitrary",)),
    )(h, norm_w.reshape(1, d).astype(F32), rpad)


def _gather_rows_kernel(idx_ref, src_ref, o_ref, sem):
    tg = o_ref.shape[0]
    base = pl.program_id(0) * tg

    def row_copy(r):
        return pltpu.make_async_copy(src_ref.at[pl.ds(idx_ref[base + r], 1), :], o_ref.at[pl.ds(r, 1), :], sem)

    def start(r, c):
        row_copy(r).start()
        return c

    def wait(r, c):
        row_copy(r).wait()
        return c

    lax.fori_loop(0, tg, start, 0, unroll=8)
    lax.fori_loop(0, tg, wait, 0, unroll=8)


def gather_rows(src, idx, tg=512):
    n = idx.shape[0]
    d = src.shape[1]
    tg = min(tg, n)
    return pl.pallas_call(
        _gather_rows_kernel,
        grid_spec=pltpu.PrefetchScalarGridSpec(
            num_scalar_prefetch=1, grid=(n // tg,),
            in_specs=[pl.BlockSpec(memory_space=pl.ANY)],
            out_specs=pl.BlockSpec((tg, d), lambda i, idx_ref: (i, 0)),
            scratch_shapes=[pltpu.SemaphoreType.DMA(())]),
        out_shape=jax.ShapeDtypeStruct((n, d), src.dtype),
        compiler_params=_cparams(("arbitrary",)),
    )(idx, src)


def _new_expert(te_ref, m):
    return jnp.logical_or(m == 0, te_ref[m] != te_ref[jnp.maximum(m - 1, 0)])


def _moe_up_kernel(te_ref, nt_ref, x_ref, w1_ref, w3_ref, o_ref, w1b_ref, w3b_ref):
    m = pl.program_id(1)
    used = m < nt_ref[0]

    @pl.when(jnp.logical_and(used, _new_expert(te_ref, m)))
    def _():
        w1b_ref[...] = w1_ref[...].astype(BF)
        w3b_ref[...] = w3_ref[...].astype(BF)

    @pl.when(used)
    def _():
        x = x_ref[...].astype(BF)
        h1 = _dot(x, w1b_ref[...])
        h3 = _dot(x, w3b_ref[...])
        o_ref[...] = (h1 * _sigmoid(h1) * h3).astype(o_ref.dtype)

    @pl.when(jnp.logical_not(used))
    def _():
        o_ref[...] = jnp.zeros_like(o_ref)


def _moe_down_kernel(te_ref, nt_ref, a_ref, w2_ref, o_ref, wb_ref):
    m = pl.program_id(1)
    used = m < nt_ref[0]

    @pl.when(jnp.logical_and(used, _new_expert(te_ref, m)))
    def _():
        wb_ref[...] = w2_ref[...].astype(BF)

    @pl.when(used)
    def _():
        o_ref[...] = _dot(a_ref[...], wb_ref[...])

    @pl.when(jnp.logical_not(used))
    def _():
        o_ref[...] = jnp.zeros_like(o_ref)


def moe_experts(xg, tile_e, ntiles, w13, w2, tn_up=1024, tn_down=512):
    cap, d = xg.shape
    f = w2.shape[1]
    tm = MOE_TM
    nt_max = cap // tm
    nf = f // tn_up
    mt = lambda m, nt: jnp.minimum(m, nt[0] - 1)
    act = pl.pallas_call(
        _moe_up_kernel,
        grid_spec=pltpu.PrefetchScalarGridSpec(
            num_scalar_prefetch=2, grid=(nf, nt_max),
            in_specs=[pl.BlockSpec((tm, d), lambda j, m, te, nt: (mt(m, nt), 0)),
                      pl.BlockSpec((None, d, tn_up), lambda j, m, te, nt: (te[mt(m, nt)], 0, j)),
                      pl.BlockSpec((None, d, tn_up), lambda j, m, te, nt: (te[mt(m, nt)], 0, j + nf))],
            out_specs=pl.BlockSpec((tm, tn_up), lambda j, m, te, nt: (m, j)),
            scratch_shapes=[pltpu.VMEM((d, tn_up), BF), pltpu.VMEM((d, tn_up), BF)]),
        out_shape=jax.ShapeDtypeStruct((cap, f), BF),
        compiler_params=_cparams(("arbitrary", "arbitrary")),
    )(tile_e, ntiles, xg, w13, w13)
    return pl.pallas_call(
        _moe_down_kernel,
        grid_spec=pltpu.PrefetchScalarGridSpec(
            num_scalar_prefetch=2, grid=(d // tn_down, nt_max),
            in_specs=[pl.BlockSpec((tm, f), lambda j, m, te, nt: (mt(m, nt), 0)),
                      pl.BlockSpec((None, f, tn_down), lambda j, m, te, nt: (te[mt(m, nt)], 0, j))],
            out_specs=pl.BlockSpec((tm, tn_down), lambda j, m, te, nt: (m, j)),
            scratch_shapes=[pltpu.VMEM((f, tn_down), BF)]),
        out_shape=jax.ShapeDtypeStruct((cap, d), F32),
        compiler_params=_cparams(("arbitrary", "arbitrary"), VMEM_LIMIT_BIG),
    )(tile_e, ntiles, act, w2)


def _combine_kernel(pos_ref, h_ref, route_ref, y_ref, *rest, final_norm):
    if final_norm:
        nw_ref, o_ref, buf, sem = rest
    else:
        o_ref, buf, sem = rest
    tc = h_ref.shape[0]
    base = pl.program_id(0) * tc

    def row_copy(r, k):
        return pltpu.make_async_copy(y_ref.at[pl.ds(pos_ref[TOP_K * (base + r) + k], 1), :],
                                     buf.at[k, pl.ds(r, 1), :], sem)

    def start(r, c):
        for k in range(TOP_K):
            row_copy(r, k).start()
        return c

    def wait(r, c):
        for k in range(TOP_K):
            row_copy(r, k).wait()
        return c

    lax.fori_loop(0, tc, start, 0, unroll=8)
    lax.fori_loop(0, tc, wait, 0, unroll=8)
    acc = h_ref[...]
    for k in range(TOP_K):
        acc = acc + route_ref[:, TOP_K + k:TOP_K + k + 1] * buf[k]
    if final_norm:
        acc = _rms(acc, nw_ref[...])
    o_ref[...] = acc


def moe_combine(h, route, yg, pos, final_w=None, tc=512):
    n, d = h.shape
    tc = min(tc, n)
    in_specs = [pl.BlockSpec((tc, d), lambda i, pos_ref: (i, 0)), pl.BlockSpec((tc, LANES), lambda i, pos_ref: (i, 0)),
                pl.BlockSpec(memory_space=pl.ANY)]
    args = [pos, h, route, yg]
    if final_w is not None:
        in_specs.append(pl.BlockSpec((1, d), lambda i, pos_ref: (0, 0)))
        args.append(final_w.reshape(1, d).astype(F32))
    return pl.pallas_call(
        functools.partial(_combine_kernel, final_norm=final_w is not None),
        grid_spec=pltpu.PrefetchScalarGridSpec(
            num_scalar_prefetch=1, grid=(n // tc,),
            in_specs=in_specs,
            out_specs=pl.BlockSpec((tc, d), lambda i, pos_ref: (i, 0)),
            scratch_shapes=[pltpu.VMEM((TOP_K, tc, d), F32), pltpu.SemaphoreType.DMA(())]),
        out_shape=jax.ShapeDtypeStruct((n, d), F32),
        compiler_params=_cparams(("arbitrary",)),
    )(*args)


def moe_ffn(h, norm_w, router, w13, w2, final_w=None):
    n, d = h.shape
    tm = MOE_TM
    hn, route, cnt = moe_router(h, norm_w, router)
    expert = route[:, 0:TOP_K].astype(jnp.int32)
    rank = route[:, 2 * TOP_K:3 * TOP_K].astype(jnp.int32)
    counts = cnt[0, :N_EXPERTS].astype(jnp.int32)
    tiles = (counts + tm - 1) // tm
    tile_end = jnp.cumsum(tiles)
    dest = ((tile_end - tiles) * tm)[expert] + rank
    nt_max = -(-n * TOP_K // tm) + N_EXPERTS
    cap = nt_max * tm
    flat = dest.reshape(-1)
    row_tok = jnp.zeros((cap,), jnp.int32).at[flat].set(jnp.arange(n * TOP_K, dtype=jnp.int32) // TOP_K)
    tile_e = jnp.minimum(jnp.searchsorted(tile_end, jnp.arange(nt_max, dtype=jnp.int32), side="right"),
                         N_EXPERTS - 1).astype(jnp.int32)
    ntiles = tile_end[-1:].astype(jnp.int32)
    xg = gather_rows(hn, row_tok)
    yg = moe_experts(xg, tile_e, ntiles, w13, w2)
    return moe_combine(h, route, yg, flat, final_w)


def _final_kernel(x_ref, w_ref, o_ref):
    o_ref[...] = _rms(x_ref[...], w_ref[...])


def final_norm(h, w, tm=512):
    m, d = h.shape
    tm = min(tm, m)
    return pl.pallas_call(
        _final_kernel,
        grid=(m // tm,),
        in_specs=[pl.BlockSpec((tm, d), lambda i: (i, 0)), pl.BlockSpec((1, d), lambda i: (0, 0))],
        out_specs=pl.BlockSpec((tm, d), lambda i: (i, 0)),
        out_shape=jax.ShapeDtypeStruct((m, d), F32),
        compiler_params=_cparams(("parallel",)),
    )(h, w.reshape(1, d).astype(F32))


def _rot_cols(w):
    half = w.shape[-1] // 2
    return jnp.concatenate([-w[..., half:], w[..., :half]], axis=-1)


_W_IN_SEGMENTS = (
    (U_ML_Q, O_ML_Q, 512, 1.0), (U_ML_K, O_ML_K, 512, 1.0), (U_ML_V, O_ML_V, 1024, 1.0),
    (U_ML_O, O_ML_O, 1024, 1.0), (U_SSM_Z, O_SSM_Z, 1024, 1.0), (U_SSM_X, O_SSM_XBC, SSM_DINNER, 1.0),
    (U_SWA_Q, O_SWA_Q, 1024, SWA_HEAD_DIM ** -0.5 * LOG2E), (U_SSM_BC, O_SSM_XBC + SSM_DINNER, SSM_BC, 1.0),
    (U_MLA_CQ, O_MLA_CQ, MLA_Q_LORA, 1.0), (U_MLA_CKV, O_MLA_CKV, MLA_KV_LORA, 1.0),
    (U_SWA_K, O_SWA_K, 256, 1.0), (U_SWA_V, O_SWA_V, 256, 1.0), (U_GATE, O_GATE, N_BRANCH * D_MODEL, 1.0))


W_IN_TILE = 256
_ROPE_TILE = U_MLA_KR // W_IN_TILE
_GATE_TILE = U_TOTAL // W_IN_TILE


def _w_in_tile_sources():
    src = [0] * (U_TOTAL // W_IN_TILE)
    for dst, s, width, _ in _W_IN_SEGMENTS:
        for off in range(0, width, W_IN_TILE):
            src[(dst + off) // W_IN_TILE] = s + off
    src[_ROPE_TILE] = O_MLA_KR
    return src


def _w_in_layout_kernel(src_ref, w_ref, dt_ref, o_ref):
    t = pl.program_id(1)

    @pl.when(t == _GATE_TILE)
    def _():
        ngate = 2 * ML_HEADS
        o_ref[...] = jnp.concatenate(
            [w_ref[0, 0:ngate, :], dt_ref[0],
             jnp.zeros((W_IN_TILE - ngate - SSM_HEADS, o_ref.shape[1]), F32)], axis=0).astype(BF)

    @pl.when(jnp.logical_and(t != _ROPE_TILE, t != _GATE_TILE))
    def _():
        swa_q = jnp.logical_and(t >= U_SWA_Q // W_IN_TILE, t < U_SSM_BC // W_IN_TILE)
        scale = jnp.where(swa_q, SWA_HEAD_DIM ** -0.5 * LOG2E, 1.0)
        o_ref[...] = (w_ref[0] * scale).astype(BF)

    @pl.when(t == _ROPE_TILE)
    def _():
        half = MLA_ROPE // 2
        o_ref[0:MLA_ROPE, :] = w_ref[0, 0:MLA_ROPE, :].astype(BF)
        o_ref[MLA_ROPE:MLA_ROPE + half, :] = (-w_ref[0, half:MLA_ROPE, :]).astype(BF)
        o_ref[MLA_ROPE + half:2 * MLA_ROPE, :] = w_ref[0, 0:half, :].astype(BF)
        o_ref[2 * MLA_ROPE:, :] = jnp.zeros((W_IN_TILE - 2 * MLA_ROPE, o_ref.shape[1]), BF)


def layout_w_in(w_in):
    depth, d, _ = w_in.shape
    wt = jnp.swapaxes(w_in, 1, 2)
    src = jnp.asarray([s // SUBLANES for s in _w_in_tile_sources() + [O_ML_I]], jnp.int32)
    return pl.pallas_call(
        _w_in_layout_kernel,
        grid_spec=pltpu.PrefetchScalarGridSpec(
            num_scalar_prefetch=1, grid=(depth, _GATE_TILE + 1),
            in_specs=[pl.BlockSpec((pl.Element(1), pl.Element(W_IN_TILE), pl.Element(d)),
                                   lambda l, t, src_ref: (l, src_ref[t] * SUBLANES, 0)),
                      pl.BlockSpec((pl.Element(1), pl.Element(SSM_HEADS), pl.Element(d)),
                                   lambda l, t, src_ref: (l, O_SSM_DT, 0))],
            out_specs=pl.BlockSpec((None, W_IN_TILE, d), lambda l, t, src_ref: (l, t, 0))),
        out_shape=jax.ShapeDtypeStruct((depth, U_TOTAL + W_IN_TILE, d), BF),
        compiler_params=_cparams(("parallel", "parallel")),
    )(src, wt, wt)


def _layout_w_uq(w):
    k = w.shape[0]
    w = w.reshape(k, MLA_HEADS, MLA_NOPE + MLA_ROPE)
    rope = w[..., MLA_NOPE:]
    scale = (MLA_NOPE + MLA_ROPE) ** -0.5 * LOG2E
    return (jnp.concatenate([w, _rot_cols(rope)], axis=-1).reshape(k, -1) * scale).astype(BF)


def _rope_table(seq):
    inv_freq = 1.0 / (ROPE_THETA ** (jnp.arange(0, MLA_ROPE, 2, dtype=F32) / MLA_ROPE))
    ang = jnp.arange(seq, dtype=F32)[:, None] * inv_freq[None, :]
    c, s = jnp.cos(ang), jnp.sin(ang)
    return jnp.concatenate([c, c, s, s], axis=1)


def hybrid_mixer(h, bsz, seq, layer, norm_w, w_main, ml_ib, ml_fb, ml_norm, conv_w, conv_b, dt_bias, a_log,
                 ssm_d, ssm_norm, q_norm, w_uq, kv_norm, w_ukv, sinks, w_branch, w_out, cs):
    n = bsz * seq
    nc = seq // CHUNK
    u2, small = matmul(h, w_main, norm=norm_w, out_dtype=BF, w_layer=layer, w_t=True, n=U_TOTAL, tn=2048,
                       side_w=w_main, side_rows=(U_TOTAL // LANES, LANES))
    u3 = u2.reshape(bsz, seq, U_TOTAL)

    def to_rows(cols, heads):
        return cols.reshape(bsz, seq, heads).transpose(0, 2, 1).reshape(bsz * heads * nc, CHUNK)

    def per_row(vec, heads):
        return jnp.broadcast_to(vec.astype(F32)[None, :, None], (bsz, heads, nc)).reshape(-1, 1)

    ig, bcum, dt, acs = recurrence_gates(
        to_rows(small[:, 0:ML_HEADS], ML_HEADS), to_rows(small[:, ML_HEADS:2 * ML_HEADS], ML_HEADS),
        per_row(ml_ib, ML_HEADS), per_row(ml_fb, ML_HEADS),
        to_rows(small[:, 2 * ML_HEADS:2 * ML_HEADS + SSM_HEADS], SSM_HEADS),
        per_row(dt_bias, SSM_HEADS), per_row(a_log, SSM_HEADS))

    def as_rows(x, heads):
        return x.reshape(bsz, heads, nc, CHUNK).transpose(0, 2, 1, 3)

    def as_cols(x, heads):
        return x.reshape(bsz, heads, seq).transpose(0, 2, 1)

    ya = mlstm_branch(u3, as_rows(ig, ML_HEADS), as_rows(bcum, ML_HEADS), as_cols(bcum, ML_HEADS), ml_norm)
    yb = ssd_branch(u3, as_cols(dt, SSM_HEADS), as_cols(acs, SSM_HEADS), as_rows(acs, SSM_HEADS), conv_w, conv_b,
                    jnp.repeat(ssm_d, SSM_HEADDIM), ssm_norm)
    qf = matmul(u2, w_uq, norm=q_norm, x_col_blk=U_MLA_CQ // MLA_Q_LORA, tn=2048)
    kvf = matmul(u2, w_ukv, norm=kv_norm, x_col_blk=U_MLA_CKV // MLA_KV_LORA, tn=2048)
    yc = mla_attention(qf.reshape(bsz, seq, -1), kvf.reshape(bsz, seq, -1), u3, cs)
    yd = swa_branch(u3, sinks)
    merged = gated_merge(ya.reshape(n, -1), yb.reshape(n, -1), yc.reshape(n, -1), yd.reshape(n, -1), w_branch, u2)
    return matmul(merged, w_out, residual=h, out_dtype=F32, tm=512, tn=2048)


def kernel(x, mem, norm_mix, w_in, ml_igate_bias, ml_fgate_bias, ml_norm, ssm_conv_w, ssm_conv_b, ssm_dt_bias, ssm_a_log, ssm_d, ssm_norm, mla_q_norm, mla_w_uq, mla_kv_norm, mla_w_ukv, swa_sinks, w_branch, w_out, norm_cross, norm_mem, xa_wq, xa_wkv, xa_wo, norm_ffn, ffn_w13, ffn_w2, moe_router, moe_w13, moe_w2, norm_final):
    bsz, seq, d = x.shape
    depth = w_in.shape[0]
    n = bsz * seq
    mlen = mem.shape[1]
    cs = _rope_table(seq)
    h = x.reshape(n, d)
    mem2 = mem.reshape(bsz * mlen, d)
    w_main = layout_w_in(w_in)
    for l in range(depth):
        h = hybrid_mixer(h, bsz, seq, l, norm_mix[l], w_main, ml_igate_bias[l], ml_fgate_bias[l], ml_norm[l],
                         ssm_conv_w[l], ssm_conv_b[l], ssm_dt_bias[l], ssm_a_log[l], ssm_d[l], ssm_norm[l],
                         mla_q_norm[l], _layout_w_uq(mla_w_uq[l]), mla_kv_norm[l], mla_w_ukv[l].astype(BF),
                         swa_sinks[l], w_branch[l].astype(BF), w_out[l].astype(BF), cs)
        kv = matmul(mem2, xa_wkv[l].astype(BF), norm=norm_mem[l])
        wq = (xa_wq[l] * (XA_HEAD_DIM ** -0.5 * LOG2E)).astype(BF)
        h = cross_attention(h, seq, norm_cross[l], wq, kv.reshape(bsz, mlen, -1), xa_wo[l].astype(BF))
        if l % 2 == 0:
            h = dense_ffn(h, norm_ffn[l], ffn_w13[l // 2].astype(BF), ffn_w2[l // 2].astype(BF))
        else:
            h = moe_ffn(h, norm_ffn[l], moe_router[l // 2], moe_w13[l // 2], moe_w2[l // 2],
                        final_w=norm_final if l == depth - 1 else None)
    if depth % 2 == 1:
        h = final_norm(h, norm_final)
    return h.reshape(bsz, seq, d)
```

```python
import functools
import math

import jax
import jax.numpy as jnp
from jax import lax
from jax.experimental import pallas as pl
from jax.experimental.pallas import tpu as pltpu

F32 = jnp.float32
BF = jnp.bfloat16

D_MODEL = 2048
RMS_EPS = 1e-6
ML_HEADS, ML_DQK, ML_DV = 4, 128, 256
SSM_HEADS, SSM_HEADDIM, SSM_GROUPS, SSM_STATE, SSM_CONV = 16, 64, 2, 128, 4
SSM_DINNER = SSM_HEADS * SSM_HEADDIM
SSM_BC = 2 * SSM_GROUPS * SSM_STATE
MLA_HEADS, MLA_Q_LORA, MLA_KV_LORA, MLA_NOPE, MLA_ROPE, MLA_V = 8, 512, 256, 128, 64, 128
ROPE_THETA = 10000.0
SWA_HEADS, SWA_KV_HEADS, SWA_HEAD_DIM, SWA_WINDOW = 16, 4, 64, 128
N_BRANCH, BRANCH_W = 4, 1024
XA_HEADS, XA_HEAD_DIM = 4, 128
FFN_DIM = 7168
N_EXPERTS, TOP_K = 8, 2

CHUNK = 128
LANES = 128
SUBLANES = 8
VMEM_LIMIT = 56 * 1024 * 1024
VMEM_LIMIT_BIG = 60 * 1024 * 1024
LOG2E = math.log2(math.e)

U_ML_Q, U_ML_K, U_ML_V, U_ML_O = 0, 512, 1024, 2048
U_SSM_Z, U_SSM_X, U_SWA_Q, U_SSM_BC = 3072, 4096, 5120, 6144
U_MLA_CQ, U_MLA_CKV, U_SWA_K, U_SWA_V, U_MLA_KR = 6656, 7168, 7424, 7680, 7936
U_GATE = 8192
U_TOTAL = U_GATE + N_BRANCH * D_MODEL
_SPLITS = (512, 512, 1024, 1024, 4, 4, 1024, 1536, 16, 512, 256, 64, 1024, 256, 256, 8192)
_OFF = [0]
for _s in _SPLITS:
    _OFF.append(_OFF[-1] + _s)
(O_ML_Q, O_ML_K, O_ML_V, O_ML_O, O_ML_I, O_ML_F, O_SSM_Z, O_SSM_XBC, O_SSM_DT, O_MLA_CQ, O_MLA_CKV,
 O_MLA_KR, O_SWA_Q, O_SWA_K, O_SWA_V, O_GATE, _O_END) = _OFF


def _cparams(sem, vmem_limit=VMEM_LIMIT):
    return pltpu.CompilerParams(dimension_semantics=sem, vmem_limit_bytes=vmem_limit)


def _dot(a, b):
    return jnp.dot(a, b, preferred_element_type=F32)


def _dot_nt(a, b):
    return lax.dot_general(a, b, (((1,), (1,)), ((), ())), preferred_element_type=F32)


def _dot_tn(a, b):
    return lax.dot_general(a, b, (((0,), (0,)), ((), ())), preferred_element_type=F32)


def _split3(a):
    a1 = a.astype(BF)
    r = a - a1.astype(F32)
    a2 = r.astype(BF)
    a3 = (r - a2.astype(F32)).astype(BF)
    return a1, a2, a3


def _dot_sel(a, sel):
    a1, a2, a3 = _split3(a)
    return _dot(a1, sel) + _dot(a2, sel) + _dot(a3, sel)


def _rms(x, w):
    return x * lax.rsqrt(jnp.mean(x * x, axis=-1, keepdims=True) + RMS_EPS) * w


def _sigmoid(x):
    return 1.0 / (1.0 + jnp.exp(-x))


def _mm_kernel(*refs, has_norm, has_res, has_side, w_t):
    mm = _dot_nt if w_t else _dot
    it = iter(refs)
    x_ref = next(it)
    g_ref = next(it) if has_norm else None
    w_ref = next(it)
    r_ref = next(it) if has_res else None
    ws_ref = next(it) if has_side else None
    o_ref = next(it)
    os_ref = next(it) if has_side else None
    if has_norm:
        xn_ref = next(it)

        @pl.when(pl.program_id(1) == 0)
        def _():
            xn = _rms(x_ref[...].astype(F32), g_ref[...]).astype(BF)
            xn_ref[...] = xn
            if has_side:
                os_ref[...] = mm(xn, ws_ref[...])

        xv = xn_ref[...]
    else:
        xv = x_ref[...]
    acc = mm(xv, w_ref[...])
    if has_res:
        acc = acc + r_ref[...]
    o_ref[...] = acc.astype(o_ref.dtype)


def matmul(x, w, *, norm=None, residual=None, out_dtype=None, tm=1024, tn=1024, x_col_blk=0, w_layer=None,
           side_w=None, side_rows=None, w_t=False, n=None):
    out_dtype = out_dtype or BF
    m = x.shape[0]
    k, n_all = w.shape[-2:][::-1] if w_t else w.shape[-2:]
    n = n or n_all
    tm, tn = min(tm, m), min(tn, n)
    assert m % tm == 0 and n % tn == 0
    in_specs = [pl.BlockSpec((tm, k), lambda i, j: (i, x_col_blk))]
    args = [x]
    scratch = []
    if norm is not None:
        in_specs.append(pl.BlockSpec((1, k), lambda i, j: (0, 0)))
        args.append(norm.reshape(1, k).astype(F32))
        scratch.append(pltpu.VMEM((tm, k), BF))
    if w_layer is None:
        in_specs.append(pl.BlockSpec((k, tn), lambda i, j: (0, j)))
    else:
        in_specs.append(pl.BlockSpec((None, tn, k), lambda i, j: (w_layer, j, 0)) if w_t else
                        pl.BlockSpec((None, k, tn), lambda i, j: (w_layer, 0, j)))
    args.append(w)
    if residual is not None:
        in_specs.append(pl.BlockSpec((tm, tn), lambda i, j: (i, j)))
        args.append(residual)
    out_specs = pl.BlockSpec((tm, tn), lambda i, j: (i, j))
    out_shape = jax.ShapeDtypeStruct((m, n), out_dtype)
    if side_w is not None:
        assert norm is not None and w_layer is not None
        if side_rows is None:
            ns = side_w.shape[-2] if w_t else side_w.shape[-1]
            in_specs.append(pl.BlockSpec((None,) + side_w.shape[1:], lambda i, j: (w_layer, 0, 0)))
        else:
            blk, ns = side_rows
            in_specs.append(pl.BlockSpec((None, ns, k), lambda i, j: (w_layer, blk, 0)))
        args.append(side_w)
        out_specs = [out_specs, pl.BlockSpec((tm, ns), lambda i, j: (i, 0))]
        out_shape = [out_shape, jax.ShapeDtypeStruct((m, ns), F32)]
    return pl.pallas_call(
        functools.partial(_mm_kernel, has_norm=norm is not None, has_res=residual is not None,
                          has_side=side_w is not None, w_t=w_t),
        grid=(m // tm, n // tn),
        in_specs=in_specs,
        out_specs=out_specs,
        out_shape=out_shape,
        scratch_shapes=scratch,
        compiler_params=_cparams(("parallel", "arbitrary")),
    )(*args)


def _cumsum_lanes(x):
    lane = lax.broadcasted_iota(jnp.int32, x.shape, 1)
    s = 1
    while s < x.shape[1]:
        x = x + jnp.where(lane >= s, pltpu.roll(x, s, axis=1), 0.0)
        s *= 2
    return x


def _softplus(x):
    return jnp.maximum(x, 0.0) + jnp.log(1.0 + jnp.exp(-jnp.abs(x)))


def _gates_kernel(i_ref, f_ref, ib_ref, fb_ref, dt_ref, dtb_ref, alog_ref, ig_ref, b_ref, dto_ref, acs_ref):
    ig_ref[...] = i_ref[...] + ib_ref[...]
    b_ref[...] = _cumsum_lanes(-_softplus(-(f_ref[...] + fb_ref[...])))
    dt = _softplus(dt_ref[...] + dtb_ref[...])
    dto_ref[...] = dt
    acs_ref[...] = _cumsum_lanes(dt * (-jnp.exp(alog_ref[...])))


def recurrence_gates(i_rows, f_rows, ib, fb, dt_rows, dtb, alog):
    r1, r2 = i_rows.shape[0], dt_rows.shape[0]
    shp = lambda r: jax.ShapeDtypeStruct((r, CHUNK), F32)
    return pl.pallas_call(
        _gates_kernel,
        out_shape=(shp(r1), shp(r1), shp(r2), shp(r2)),
    )(i_rows, f_rows, ib, fb, dt_rows, dtb, alog)


def _mlstm_kernel(q_ref, k_ref, v_ref, o_ref, igr_ref, br_ref, bc_ref, nw_ref, y_ref, *state):
    ct_ref, n_ref, m_ref = state[0::3], state[1::3], state[2::3]
    L = CHUNK

    @pl.when(pl.program_id(1) == 0)
    def _():
        for ref in state:
            ref[...] = jnp.zeros_like(ref)

    row = lax.broadcasted_iota(jnp.int32, (L, L), 0)
    col = lax.broadcasted_iota(jnp.int32, (L, L), 1)
    causal = col <= row
    diag = col == row
    scale = ML_DQK ** -0.5
    outs = []
    for h in range(ML_HEADS):
        q = q_ref[:, h * ML_DQK:(h + 1) * ML_DQK]
        k = (k_ref[:, h * ML_DQK:(h + 1) * ML_DQK].astype(F32) * scale).astype(BF)
        v = v_ref[:, h * ML_DV:(h + 1) * ML_DV]
        bcol = bc_ref[:, h:h + 1]
        brow = br_ref[h:h + 1, :]
        igrow = igr_ref[h:h + 1, :]
        m_prev = m_ref[h][0:1, 0:1]
        n_prev = n_ref[h][0:1, :]
        ct_prev = ct_ref[h][...]

        dmat = jnp.where(causal, bcol - brow + igrow, -jnp.inf)
        m_intra = jnp.max(dmat, axis=1, keepdims=True)
        g = bcol + m_prev
        m_s = jnp.maximum(g, m_intra)
        p = jnp.exp(dmat - m_s) * _dot_nt(q, k)
        inter = jnp.exp(g - m_s)
        num = _dot(p.astype(BF), v) + inter * _dot(q, ct_prev.astype(BF))
        den = jnp.sum(p, axis=1, keepdims=True) + inter * jnp.sum(q.astype(F32) * n_prev, axis=1, keepdims=True)
        hh = num / jnp.maximum(jnp.abs(den), jnp.exp(-m_s))
        hn = _rms(hh, nw_ref[:, h * ML_DV:(h + 1) * ML_DV])
        outs.append(_sigmoid(o_ref[:, h * ML_DV:(h + 1) * ML_DV].astype(F32)) * hn)

        b_tot = brow[:, L - 1:L]
        a = b_tot - brow + igrow
        m_loc = jnp.max(a, axis=1, keepdims=True)
        wl = jnp.exp(a - m_loc)
        kw = _dot(jnp.where(diag, wl, 0.0).astype(BF), k)
        m_new = jnp.maximum(b_tot + m_prev, m_loc)
        da = jnp.exp(b_tot + m_prev - m_new)
        db = jnp.exp(m_loc - m_new)
        ct_ref[h][...] = da * ct_prev + db * _dot_tn(kw.astype(BF), v)
        n_ref[h][...] = jnp.broadcast_to(da * n_prev + db * jnp.sum(kw, axis=0, keepdims=True), n_ref[h].shape)
        m_ref[h][...] = jnp.broadcast_to(m_new, m_ref[h].shape)
    y_ref[...] = jnp.concatenate(outs, axis=1).astype(y_ref.dtype)


def mlstm_branch(u3, ig_rows, b_rows, b_cols, norm_w):
    bsz, seq, _ = u3.shape
    nc = seq // CHUNK
    L = CHUNK
    ublk = lambda width, off: pl.BlockSpec((None, L, width), lambda b, c: (b, c, off // width))
    rows = pl.BlockSpec((None, None, ML_HEADS, L), lambda b, c: (b, c, 0, 0))
    return pl.pallas_call(
        _mlstm_kernel,
        grid=(bsz, nc),
        in_specs=[ublk(512, U_ML_Q), ublk(512, U_ML_K), ublk(1024, U_ML_V), ublk(1024, U_ML_O), rows, rows,
                  pl.BlockSpec((None, L, ML_HEADS), lambda b, c: (b, c, 0)),
                  pl.BlockSpec((1, ML_HEADS * ML_DV), lambda b, c: (0, 0))],
        out_specs=pl.BlockSpec((None, L, BRANCH_W), lambda b, c: (b, c, 0)),
        out_shape=jax.ShapeDtypeStruct((bsz, seq, BRANCH_W), BF),
        scratch_shapes=[pltpu.VMEM((ML_DQK, ML_DV), F32), pltpu.VMEM((8, ML_DQK), F32),
                        pltpu.VMEM((8, LANES), F32)] * ML_HEADS,
        compiler_params=_cparams(("parallel", "arbitrary")),
    )(u3, u3, u3, u3, ig_rows, b_rows, b_cols, norm_w.reshape(1, -1).astype(F32))


def _ssd_kernel(z_ref, x_ref, bc_ref, dt_ref, ac_ref, ar_ref, cw_ref, cb_ref, d_ref, nw_ref, y_ref, xs_ref, st_ref):
    L = CHUNK
    P, R, NS = SSM_HEADDIM, SSM_HEADS // SSM_GROUPS, SSM_STATE
    GW = R * P

    @pl.when(pl.program_id(1) == 0)
    def _():
        xs_ref[0:8, :] = jnp.zeros((8, xs_ref.shape[1]), F32)
        st_ref[...] = jnp.zeros_like(st_ref)

    xs_ref[8:, :] = jnp.concatenate([x_ref[...], bc_ref[...]], axis=1).astype(F32)
    conv = cb_ref[...] + cw_ref[SSM_CONV - 1:SSM_CONV, :] * xs_ref[8:8 + L, :]
    for sft in range(1, SSM_CONV):
        conv = conv + cw_ref[SSM_CONV - 1 - sft:SSM_CONV - sft, :] * xs_ref[8 - sft:8 - sft + L, :]
    xs_ref[0:8, :] = xs_ref[L:L + 8, :]
    xbc = conv * _sigmoid(conv)
    xh = xbc[:, :SSM_DINNER]
    bmat = xbc[:, SSM_DINNER:SSM_DINNER + SSM_GROUPS * NS].astype(BF)
    cmat = xbc[:, SSM_DINNER + SSM_GROUPS * NS:].astype(BF)

    dtc = dt_ref[...]
    ac = ac_ref[...]
    ar = ar_ref[...]
    a_last = ac[L - 1:L, :]
    hsel = (lax.broadcasted_iota(jnp.int32, (SSM_HEADS, SSM_DINNER), 1) // P
            == lax.broadcasted_iota(jnp.int32, (SSM_HEADS, SSM_DINNER), 0))
    expand = jnp.where(hsel, 1.0, 0.0).astype(BF)
    stack = jnp.concatenate([dtc, jnp.exp(a_last - ac), jnp.exp(ac),
                             jnp.broadcast_to(jnp.exp(a_last), (8, SSM_HEADS))], axis=0)
    ex = _dot_sel(stack, expand)
    dt_full, dst_full, ind_full = ex[0:L], ex[L:2 * L], ex[2 * L:3 * L]
    cdec_full = ex[3 * L:3 * L + 1]
    xdt = xh * dt_full
    xdt_b = xdt.astype(BF)
    xw_b = (xdt * dst_full).astype(BF)

    row = lax.broadcasted_iota(jnp.int32, (L, L), 0)
    col = lax.broadcasted_iota(jnp.int32, (L, L), 1)
    causal = col <= row
    ys = []
    for g in range(SSM_GROUPS):
        bg = bmat[:, g * NS:(g + 1) * NS]
        cg = cmat[:, g * NS:(g + 1) * NS]
        cb = _dot_nt(cg, bg)
        st_prev = st_ref[g]
        yoff = _dot(cg, st_prev.astype(BF))
        st_ref[g] = cdec_full[:, g * GW:(g + 1) * GW] * st_prev + _dot_tn(bg, xw_b[:, g * GW:(g + 1) * GW])
        for r in range(R):
            h = g * R + r
            dec = jnp.exp(jnp.where(causal, ac[:, h:h + 1] - ar[h:h + 1, :], -jnp.inf))
            yd = _dot((dec * cb).astype(BF), xdt_b[:, h * P:(h + 1) * P])
            ys.append(yd + yoff[:, r * P:(r + 1) * P] * ind_full[:, h * P:(h + 1) * P])
    y = jnp.concatenate(ys, axis=1) + xh * d_ref[...]
    zf = z_ref[...].astype(F32)
    y = y * (zf * _sigmoid(zf))
    y_ref[...] = jnp.concatenate(
        [_rms(y[:, g * GW:(g + 1) * GW], nw_ref[:, g * GW:(g + 1) * GW]) for g in range(SSM_GROUPS)],
        axis=1).astype(y_ref.dtype)


def ssd_branch(u3, dt_cols, acs_cols, acs_rows, conv_w, conv_b, d_full, norm_w):
    bsz, seq, _ = u3.shape
    nc = seq // CHUNK
    L = CHUNK
    ublk = lambda width, off: pl.BlockSpec((None, L, width), lambda b, c: (b, c, off // width))
    cols = pl.BlockSpec((None, L, SSM_HEADS), lambda b, c: (b, c, 0))
    const = lambda shape: pl.BlockSpec(shape, lambda b, c: (0, 0))
    cch = SSM_DINNER + SSM_BC
    return pl.pallas_call(
        _ssd_kernel,
        grid=(bsz, nc),
        in_specs=[ublk(1024, U_SSM_Z), ublk(1024, U_SSM_X), ublk(512, U_SSM_BC), cols, cols,
                  pl.BlockSpec((None, None, SSM_HEADS, L), lambda b, c: (b, c, 0, 0)),
                  const((SSM_CONV, cch)), const((1, cch)), const((1, SSM_DINNER)), const((1, SSM_DINNER))],
        out_specs=pl.BlockSpec((None, L, BRANCH_W), lambda b, c: (b, c, 0)),
        out_shape=jax.ShapeDtypeStruct((bsz, seq, BRANCH_W), BF),
        scratch_shapes=[pltpu.VMEM((L + 8, cch), F32),
                        pltpu.VMEM((SSM_GROUPS, SSM_STATE, SSM_DINNER // SSM_GROUPS), F32)],
        compiler_params=_cparams(("parallel", "arbitrary")),
    )(u3, u3, u3, dt_cols, acs_cols, acs_rows, conv_w.astype(F32), conv_b.reshape(1, cch).astype(F32),
      d_full.reshape(1, -1).astype(F32), norm_w.reshape(1, -1).astype(F32))


def _rope128(x, cs):
    t = x.astype(F32) * cs
    return t + pltpu.roll(t, MLA_ROPE, axis=1)


def _mla_kernel(q_ref, csq_ref, kv_ref, kr_ref, csk_ref, o_ref, qe_s, *stats, tq):
    qi = pl.program_id(1)
    ki = pl.program_id(2)
    qw = MLA_NOPE + LANES
    kw = MLA_NOPE + MLA_V
    m_s, l_s, acc_s = stats[0::3], stats[1::3], stats[2::3]
    nt = tq // LANES

    @pl.when(ki == 0)
    def _():
        for h in range(MLA_HEADS):
            qr = _rope128(q_ref[:, h * qw + MLA_NOPE:(h + 1) * qw], csq_ref[...]).astype(BF)
            qe_s[h] = jnp.concatenate([q_ref[:, h * qw:h * qw + MLA_NOPE], qr], axis=1)
            m_s[h][...] = jnp.full((tq, LANES), -jnp.inf, F32)
            l_s[h][...] = jnp.zeros((tq, LANES), F32)
            acc_s[h][...] = jnp.zeros((tq, MLA_V), F32)

    def step(masked):
        lane = lax.broadcasted_iota(jnp.int32, (tq, LANES), 1)
        kr = jnp.where(lane < MLA_ROPE, _rope128(kr_ref[...], csk_ref[...]), 0.0).astype(BF)
        if masked:
            causal = (lax.broadcasted_iota(jnp.int32, (tq, tq), 1) <= lax.broadcasted_iota(jnp.int32, (tq, tq), 0))
        for h in range(MLA_HEADS):
            ke = jnp.concatenate([kv_ref[:, h * kw:h * kw + MLA_NOPE], kr], axis=1)
            s = _dot_nt(qe_s[h], ke)
            if masked:
                s = jnp.where(causal, s, -jnp.inf)
            m_old = m_s[h][...]
            m_new = jnp.maximum(m_old, jnp.max(s, axis=1, keepdims=True))
            alpha = jnp.exp2(m_old - m_new)
            p = jnp.exp2(s - jnp.concatenate([m_new] * nt, axis=1))
            psum = p[:, 0:LANES]
            for t in range(1, nt):
                psum = psum + p[:, t * LANES:(t + 1) * LANES]
            l_s[h][...] = alpha * l_s[h][...] + psum
            acc_s[h][...] = alpha * acc_s[h][...] + _dot(p.astype(BF), kv_ref[:, h * kw + MLA_NOPE:(h + 1) * kw])
            m_s[h][...] = m_new

    @pl.when(ki < qi)
    def _():
        step(False)

    @pl.when(ki == qi)
    def _():
        step(True)
        for h in range(MLA_HEADS):
            l = jnp.sum(l_s[h][...], axis=1, keepdims=True)
            o_ref[:, h * MLA_V:(h + 1) * MLA_V] = (acc_s[h][...] / l).astype(o_ref.dtype)


def mla_attention(qf3, kvf3, u3, cs, tq=1024):
    bsz, seq, _ = qf3.shape
    tq = min(tq, seq)
    nq = seq // tq
    kvi = lambda b, qi, ki: jnp.minimum(ki, qi)
    return pl.pallas_call(
        functools.partial(_mla_kernel, tq=tq),
        grid=(bsz, nq, nq),
        in_specs=[pl.BlockSpec((None, tq, qf3.shape[2]), lambda b, qi, ki: (b, qi, 0)),
                  pl.BlockSpec((tq, LANES), lambda b, qi, ki: (qi, 0)),
                  pl.BlockSpec((None, tq, kvf3.shape[2]), lambda b, qi, ki: (b, kvi(b, qi, ki), 0)),
                  pl.BlockSpec((None, tq, LANES), lambda b, qi, ki: (b, kvi(b, qi, ki), U_MLA_KR // LANES)),
                  pl.BlockSpec((tq, LANES), lambda b, qi, ki: (kvi(b, qi, ki), 0))],
        out_specs=pl.BlockSpec((None, tq, MLA_HEADS * MLA_V), lambda b, qi, ki: (b, qi, 0)),
        out_shape=jax.ShapeDtypeStruct((bsz, seq, MLA_HEADS * MLA_V), BF),
        scratch_shapes=[pltpu.VMEM((MLA_HEADS, tq, MLA_NOPE + LANES), BF)]
        + [pltpu.VMEM((tq, LANES), F32), pltpu.VMEM((tq, LANES), F32), pltpu.VMEM((tq, MLA_V), F32)] * MLA_HEADS,
        compiler_params=_cparams(("parallel", "parallel", "arbitrary")),
    )(qf3, cs, kvf3, u3, cs)


def _swa_kernel(q_ref, kc_ref, kp_ref, vc_ref, vp_ref, sink_ref, o_ref):
    W, d = SWA_WINDOW, SWA_HEAD_DIM
    n = pl.program_id(1)
    i = lax.broadcasted_iota(jnp.int32, (2 * W, 2 * W), 0) & (W - 1)
    j = lax.broadcasted_iota(jnp.int32, (2 * W, 2 * W), 1)
    valid = (j > i) & (j <= i + W) & ((n > 0) | (j >= W))
    first_tile = lax.broadcasted_iota(jnp.int32, (2 * W, LANES), 0) < W
    lower = lax.broadcasted_iota(jnp.int32, (2 * W, LANES), 1) < d
    ones = jnp.ones((2 * W, LANES), BF)
    for pair in range(SWA_KV_HEADS // 2):
        sl = slice(pair * LANES, (pair + 1) * LANES)
        kt = jnp.concatenate([kp_ref[:, sl], kc_ref[:, sl]], axis=0)
        vt = jnp.concatenate([vp_ref[:, sl], vc_ref[:, sl]], axis=0)
        kt_sw = pltpu.roll(kt.astype(F32), d, axis=1).astype(BF)
        vt_sw = pltpu.roll(vt.astype(F32), d, axis=1).astype(BF)
        for e in range(2):
            kh = 2 * pair + e
            k_lo = jnp.where(lower, kt if e == 0 else kt_sw, jnp.zeros_like(kt))
            k_hi = jnp.where(lower, jnp.zeros_like(kt), kt_sw if e == 0 else kt)
            v_lo, v_hi = (vt, vt_sw) if e == 0 else (vt_sw, vt)
            qs = jnp.concatenate([q_ref[:, 2 * kh * LANES:(2 * kh + 1) * LANES],
                                  q_ref[:, (2 * kh + 1) * LANES:(2 * kh + 2) * LANES]], axis=0)
            halves = []
            for half, (ke, ve) in enumerate(((k_lo, v_lo), (k_hi, v_hi))):
                ha, hb = 4 * kh + half, 4 * kh + 2 + half
                s = jnp.where(valid, _dot_nt(qs, ke), -jnp.inf)
                sink = jnp.where(first_tile, sink_ref[0:1, ha:ha + 1], sink_ref[0:1, hb:hb + 1]) * LOG2E
                m = jnp.maximum(jnp.max(s, axis=1, keepdims=True), sink)
                p = jnp.exp2(s - jnp.concatenate([m, m], axis=1)).astype(BF)
                den = _dot(p, ones) + jnp.exp2(sink - m)
                halves.append(_dot(p, ve) / den)
            ot = jnp.where(lower, halves[0], halves[1]).astype(o_ref.dtype)
            o_ref[:, 2 * kh * LANES:(2 * kh + 1) * LANES] = ot[:W]
            o_ref[:, (2 * kh + 1) * LANES:(2 * kh + 2) * LANES] = ot[W:]


def swa_branch(u3, sinks):
    bsz, seq, _ = u3.shape
    W = SWA_WINDOW
    kvw = SWA_KV_HEADS * SWA_HEAD_DIM
    cur = lambda width, off: pl.BlockSpec((None, W, width), lambda b, n: (b, n, off // width))
    prev = lambda width, off: pl.BlockSpec((None, W, width), lambda b, n: (b, jnp.maximum(n - 1, 0), off // width))
    return pl.pallas_call(
        _swa_kernel,
        grid=(bsz, seq // W),
        in_specs=[cur(1024, U_SWA_Q), cur(kvw, U_SWA_K), prev(kvw, U_SWA_K), cur(kvw, U_SWA_V), prev(kvw, U_SWA_V),
                  pl.BlockSpec((1, SWA_HEADS), lambda b, n: (0, 0))],
        out_specs=pl.BlockSpec((None, W, BRANCH_W), lambda b, n: (b, n, 0)),
        out_shape=jax.ShapeDtypeStruct((bsz, seq, BRANCH_W), BF),
        compiler_params=_cparams(("parallel", "parallel")),
    )(u3, u3, u3, u3, u3, sinks.reshape(1, -1).astype(F32))


def _merge_kernel(ya_ref, yb_ref, yc_ref, yd_ref, w_ref, g0_ref, g1_ref, g2_ref, g3_ref, o_ref):
    acc = None
    for n, (y_ref, g_ref) in enumerate(((ya_ref, g0_ref), (yb_ref, g1_ref), (yc_ref, g2_ref), (yd_ref, g3_ref))):
        t = _sigmoid(g_ref[...].astype(F32)) * _dot(y_ref[...], w_ref[n])
        acc = t if acc is None else acc + t
    o_ref[...] = acc.astype(o_ref.dtype)


def gated_merge(ya, yb, yc, yd, w_branch, u2, tm=1024, tn=512):
    m = ya.shape[0]
    tm = min(tm, m)
    ysp = pl.BlockSpec((tm, BRANCH_W), lambda i, j: (i, 0))
    gsp = lambda n: pl.BlockSpec((tm, tn), lambda i, j: (i, (U_GATE + n * D_MODEL) // tn + j))
    return pl.pallas_call(
        _merge_kernel,
        grid=(m // tm, D_MODEL // tn),
        in_specs=[ysp, ysp, ysp, ysp, pl.BlockSpec((N_BRANCH, BRANCH_W, tn), lambda i, j: (0, 0, j)),
                  gsp(0), gsp(1), gsp(2), gsp(3)],
        out_specs=pl.BlockSpec((tm, tn), lambda i, j: (i, j)),
        out_shape=jax.ShapeDtypeStruct((m, D_MODEL), BF),
        compiler_params=_cparams(("parallel", "parallel")),
    )(ya, yb, yc, yd, w_branch, u2, u2, u2, u2)


def _xattn_kernel(h_ref, g_ref, wq_ref, kv_ref, wo_ref, o_ref):
    d = XA_HEAD_DIM
    h = h_ref[...]
    q = _dot(_rms(h, g_ref[...]).astype(BF), wq_ref[...]).astype(BF)
    ones = jnp.ones((kv_ref.shape[0], LANES), BF)
    outs = []
    for hd in range(XA_HEADS):
        s = _dot_nt(q[:, hd * d:(hd + 1) * d], kv_ref[:, hd * d:(hd + 1) * d])
        p = jnp.exp2(s - jnp.max(s, axis=1, keepdims=True)).astype(BF)
        den = _dot(p, ones)
        outs.append((_dot(p, kv_ref[:, (XA_HEADS + hd) * d:(XA_HEADS + hd + 1) * d]) / den).astype(BF))
    o_ref[...] = h + _dot(jnp.concatenate(outs, axis=1), wo_ref[...])


def cross_attention(h, seq, norm_w, wq, kv3, wo, tq=1024):
    n, dm = h.shape
    _, mlen, kvw = kv3.shape
    tq = min(tq, seq)
    per_b = seq // tq
    const = lambda shape: pl.BlockSpec(shape, lambda i: (0, 0))
    return pl.pallas_call(
        _xattn_kernel,
        grid=(n // tq,),
        in_specs=[pl.BlockSpec((tq, dm), lambda i: (i, 0)), const((1, dm)), const(wq.shape),
                  pl.BlockSpec((None, mlen, kvw), lambda i: (i // per_b, 0, 0)), const(wo.shape)],
        out_specs=pl.BlockSpec((tq, dm), lambda i: (i, 0)),
        out_shape=jax.ShapeDtypeStruct((n, dm), F32),
        compiler_params=_cparams(("parallel",)),
    )(h, norm_w.reshape(1, dm).astype(F32), wq, kv3, wo)


def _ffn_kernel(h_ref, g_ref, w1_ref, w3_ref, w2_ref, o_ref, xn_ref, acc_ref):
    j = pl.program_id(1)

    @pl.when(j == 0)
    def _():
        hv = h_ref[...]
        xn_ref[...] = _rms(hv, g_ref[...]).astype(BF)
        acc_ref[...] = hv

    xn = xn_ref[...]
    h1 = _dot(xn, w1_ref[...])
    h3 = _dot(xn, w3_ref[...])
    act = (h1 * _sigmoid(h1) * h3).astype(BF)
    acc_ref[...] += _dot(act, w2_ref[...])

    @pl.when(j == pl.num_programs(1) - 1)
    def _():
        o_ref[...] = acc_ref[...]


def dense_ffn(h, norm_w, w13, w2, tm=512, tf=1024):
    m, d = h.shape
    f = w2.shape[0]
    tm = min(tm, m)
    nf = f // tf
    return pl.pallas_call(
        _ffn_kernel,
        grid=(m // tm, nf),
        in_specs=[pl.BlockSpec((tm, d), lambda i, j: (i, 0)), pl.BlockSpec((1, d), lambda i, j: (0, 0)),
                  pl.BlockSpec((d, tf), lambda i, j: (0, j)), pl.BlockSpec((d, tf), lambda i, j: (0, j + nf)),
                  pl.BlockSpec((tf, d), lambda i, j: (j, 0))],
        out_specs=pl.BlockSpec((tm, d), lambda i, j: (i, 0)),
        out_shape=jax.ShapeDtypeStruct((m, d), F32),
        scratch_shapes=[pltpu.VMEM((tm, d), BF), pltpu.VMEM((tm, d), F32)],
        compiler_params=_cparams(("parallel", "arbitrary")),
    )(h, norm_w.reshape(1, d).astype(F32), w13, w13, w2)


ROUTE_TM = 512
MOE_TM = 512


def _router_kernel(h_ref, g_ref, r_ref, hn_ref, route_ref, cnt_ref, carry_ref):
    tm = h_ref.shape[0]

    @pl.when(pl.program_id(0) == 0)
    def _():
        carry_ref[...] = jnp.zeros_like(carry_ref)

    hn = _rms(h_ref[...], g_ref[...])
    hn_ref[...] = hn
    a1, a2, _ = _split3(hn)
    r1, r2, _ = _split3(r_ref[...])
    lane = lax.broadcasted_iota(jnp.int32, (tm, LANES), 1)
    logits = jnp.where(lane < N_EXPERTS, _dot(a1, r1) + (_dot(a1, r2) + _dot(a2, r1)), -jnp.inf)
    v0 = jnp.max(logits, axis=1, keepdims=True)
    i0 = jnp.min(jnp.where(logits == v0, lane, LANES), axis=1, keepdims=True)
    rest = jnp.where(lane == i0, -jnp.inf, logits)
    v1 = jnp.max(rest, axis=1, keepdims=True)
    i1 = jnp.min(jnp.where(rest == v1, lane, LANES), axis=1, keepdims=True)
    ex = jnp.exp(v1 - v0)
    g0 = 1.0 / (1.0 + ex)
    g1 = ex / (1.0 + ex)
    sel0, sel1 = lane == i0, lane == i1
    onehot = jnp.where(sel0 | sel1, 1.0, 0.0)
    below = (lax.broadcasted_iota(jnp.int32, (tm, tm), 1) < lax.broadcasted_iota(jnp.int32, (tm, tm), 0))
    before = carry_ref[0:1, :] + _dot(jnp.where(below, 1.0, 0.0).astype(BF), onehot.astype(BF))
    rank0 = jnp.sum(jnp.where(sel0, before, 0.0), axis=1, keepdims=True)
    rank1 = jnp.sum(jnp.where(sel1, before, 0.0), axis=1, keepdims=True)
    total = carry_ref[0:1, :] + jnp.sum(onehot, axis=0, keepdims=True)
    carry_ref[...] = jnp.broadcast_to(total, carry_ref.shape)
    cnt_ref[...] = jnp.broadcast_to(total, cnt_ref.shape)
    out = jnp.zeros((tm, LANES), F32)
    for pos, val in enumerate((i0.astype(F32), i1.astype(F32), g0, g1, rank0, rank1)):
        out = jnp.where(lane == pos, val, out)
    route_ref[...] = out


def moe_router(h, norm_w, router):
    m, d = h.shape
    tm = min(ROUTE_TM, m)
    rpad = jnp.zeros((d, LANES), F32).at[:, :N_EXPERTS].set(router.astype(F32))
    return pl.pallas_call(
        _router_kernel,
        grid=(m // tm,),
        in_specs=[pl.BlockSpec((tm, d), lambda i: (i, 0)), pl.BlockSpec((1, d), lambda i: (0, 0)),
                  pl.BlockSpec((d, LANES), lambda i: (0, 0))],
        out_specs=[pl.BlockSpec((tm, d), lambda i: (i, 0)), pl.BlockSpec((tm, LANES), lambda i: (i, 0)),
                   pl.BlockSpec((8, LANES), lambda i: (0, 0))],
        out_shape=[jax.ShapeDtypeStruct((m, d), F32), jax.ShapeDtypeStruct((m, LANES), F32),
                   jax.ShapeDtypeStruct((8, LANES), F32)],
        scratch_shapes=[pltpu.VMEM((8, LANES), F32)],
        compiler_params=_cparams(("arbitrary",)),
    )(h, norm_w.reshape(1, d).astype(F32), rpad)


def _gather_rows_kernel(idx_ref, src_ref, o_ref, sem):
    tg = o_ref.shape[0]
    base = pl.program_id(0) * tg

    def row_copy(r):
        return pltpu.make_async_copy(src_ref.at[pl.ds(idx_ref[base + r], 1), :], o_ref.at[pl.ds(r, 1), :], sem)

    def start(i, c):
        for p in range(2):
            row_copy(2 * i + p).start(priority=p)
        return c

    def wait(r, c):
        row_copy(r).wait()
        return c

    lax.fori_loop(0, tg // 2, start, 0, unroll=4)
    lax.fori_loop(0, tg, wait, 0, unroll=8)


def gather_rows(src, idx, tg=1024):
    n = idx.shape[0]
    d = src.shape[1]
    tg = min(tg, n)
    return pl.pallas_call(
        _gather_rows_kernel,
        grid_spec=pltpu.PrefetchScalarGridSpec(
            num_scalar_prefetch=1, grid=(n // tg,),
            in_specs=[pl.BlockSpec(memory_space=pl.ANY)],
            out_specs=pl.BlockSpec((tg, d), lambda i, idx_ref: (i, 0)),
            scratch_shapes=[pltpu.SemaphoreType.DMA(())]),
        out_shape=jax.ShapeDtypeStruct((n, d), src.dtype),
        compiler_params=_cparams(("arbitrary",)),
    )(idx, src)


def _new_expert(te_ref, m):
    return jnp.logical_or(m == 0, te_ref[m] != te_ref[jnp.maximum(m - 1, 0)])


def _moe_up_kernel(te_ref, nt_ref, x_ref, w1_ref, w3_ref, o_ref, w1b_ref, w3b_ref):
    m = pl.program_id(1)
    used = m < nt_ref[0]

    @pl.when(jnp.logical_and(used, _new_expert(te_ref, m)))
    def _():
        w1b_ref[...] = w1_ref[...].astype(BF)
        w3b_ref[...] = w3_ref[...].astype(BF)

    @pl.when(used)
    def _():
        x = x_ref[...].astype(BF)
        h1 = _dot(x, w1b_ref[...])
        h3 = _dot(x, w3b_ref[...])
        o_ref[...] = (h1 * _sigmoid(h1) * h3).astype(o_ref.dtype)

    @pl.when(jnp.logical_not(used))
    def _():
        o_ref[...] = jnp.zeros_like(o_ref)


def _moe_down_kernel(te_ref, nt_ref, a_ref, w2_ref, o_ref, wb_ref):
    m = pl.program_id(1)
    used = m < nt_ref[0]

    @pl.when(jnp.logical_and(used, _new_expert(te_ref, m)))
    def _():
        wb_ref[...] = w2_ref[...].astype(BF)

    @pl.when(used)
    def _():
        o_ref[...] = _dot(a_ref[...], wb_ref[...])

    @pl.when(jnp.logical_not(used))
    def _():
        o_ref[...] = jnp.zeros_like(o_ref)


def moe_experts(xg, tile_e, ntiles, w13, w2, tn_up=1024, tn_down=512):
    cap, d = xg.shape
    f = w2.shape[1]
    tm = MOE_TM
    nt_max = cap // tm
    nf = f // tn_up
    mt = lambda m, nt: jnp.minimum(m, nt[0] - 1)
    act = pl.pallas_call(
        _moe_up_kernel,
        grid_spec=pltpu.PrefetchScalarGridSpec(
            num_scalar_prefetch=2, grid=(nf, nt_max),
            in_specs=[pl.BlockSpec((tm, d), lambda j, m, te, nt: (mt(m, nt), 0)),
                      pl.BlockSpec((None, d, tn_up), lambda j, m, te, nt: (te[mt(m, nt)], 0, j)),
                      pl.BlockSpec((None, d, tn_up), lambda j, m, te, nt: (te[mt(m, nt)], 0, j + nf))],
            out_specs=pl.BlockSpec((tm, tn_up), lambda j, m, te, nt: (m, j)),
            scratch_shapes=[pltpu.VMEM((d, tn_up), BF), pltpu.VMEM((d, tn_up), BF)]),
        out_shape=jax.ShapeDtypeStruct((cap, f), BF),
        compiler_params=_cparams(("arbitrary", "arbitrary")),
    )(tile_e, ntiles, xg, w13, w13)
    return pl.pallas_call(
        _moe_down_kernel,
        grid_spec=pltpu.PrefetchScalarGridSpec(
            num_scalar_prefetch=2, grid=(d // tn_down, nt_max),
            in_specs=[pl.BlockSpec((tm, f), lambda j, m, te, nt: (mt(m, nt), 0)),
                      pl.BlockSpec((None, f, tn_down), lambda j, m, te, nt: (te[mt(m, nt)], 0, j))],
            out_specs=pl.BlockSpec((tm, tn_down), lambda j, m, te, nt: (m, j)),
            scratch_shapes=[pltpu.VMEM((f, tn_down), BF)]),
        out_shape=jax.ShapeDtypeStruct((cap, d), F32),
        compiler_params=_cparams(("arbitrary", "arbitrary"), VMEM_LIMIT_BIG),
    )(tile_e, ntiles, act, w2)


def _combine_kernel(pos_ref, h_ref, route_ref, y_ref, *rest, final_norm):
    if final_norm:
        nw_ref, o_ref, buf, sem = rest
    else:
        o_ref, buf, sem = rest
    tc = h_ref.shape[0]
    base = pl.program_id(0) * tc

    def row_copy(r, k):
        return pltpu.make_async_copy(y_ref.at[pl.ds(pos_ref[TOP_K * (base + r) + k], 1), :],
                                     buf.at[k, pl.ds(r, 1), :], sem)

    def start(r, c):
        for k in range(TOP_K):
            row_copy(r, k).start(priority=k % 2)
        return c

    def wait(r, c):
        for k in range(TOP_K):
            row_copy(r, k).wait()
        return c

    lax.fori_loop(0, tc, start, 0, unroll=8)
    lax.fori_loop(0, tc, wait, 0, unroll=8)
    acc = h_ref[...]
    for k in range(TOP_K):
        acc = acc + route_ref[:, TOP_K + k:TOP_K + k + 1] * buf[k]
    if final_norm:
        acc = _rms(acc, nw_ref[...])
    o_ref[...] = acc


def moe_combine(h, route, yg, pos, final_w=None, tc=512):
    n, d = h.shape
    tc = min(tc, n)
    in_specs = [pl.BlockSpec((tc, d), lambda i, pos_ref: (i, 0)), pl.BlockSpec((tc, LANES), lambda i, pos_ref: (i, 0)),
                pl.BlockSpec(memory_space=pl.ANY)]
    args = [pos, h, route, yg]
    if final_w is not None:
        in_specs.append(pl.BlockSpec((1, d), lambda i, pos_ref: (0, 0)))
        args.append(final_w.reshape(1, d).astype(F32))
    return pl.pallas_call(
        functools.partial(_combine_kernel, final_norm=final_w is not None),
        grid_spec=pltpu.PrefetchScalarGridSpec(
            num_scalar_prefetch=1, grid=(n // tc,),
            in_specs=in_specs,
            out_specs=pl.BlockSpec((tc, d), lambda i, pos_ref: (i, 0)),
            scratch_shapes=[pltpu.VMEM((TOP_K, tc, d), F32), pltpu.SemaphoreType.DMA(())]),
        out_shape=jax.ShapeDtypeStruct((n, d), F32),
        compiler_params=_cparams(("arbitrary",)),
    )(*args)


def moe_ffn(h, norm_w, router, w13, w2, final_w=None):
    n, d = h.shape
    tm = MOE_TM
    hn, route, cnt = moe_router(h, norm_w, router)
    expert = route[:, 0:TOP_K].astype(jnp.int32)
    rank = route[:, 2 * TOP_K:3 * TOP_K].astype(jnp.int32)
    counts = cnt[0, :N_EXPERTS].astype(jnp.int32)
    tiles = (counts + tm - 1) // tm
    tile_end = jnp.cumsum(tiles)
    dest = ((tile_end - tiles) * tm)[expert] + rank
    nt_max = -(-n * TOP_K // tm) + N_EXPERTS
    cap = nt_max * tm
    flat = dest.reshape(-1)
    row_tok = jnp.zeros((cap,), jnp.int32).at[flat].set(jnp.arange(n * TOP_K, dtype=jnp.int32) // TOP_K)
    tile_e = jnp.minimum(jnp.searchsorted(tile_end, jnp.arange(nt_max, dtype=jnp.int32), side="right"),
                         N_EXPERTS - 1).astype(jnp.int32)
    ntiles = tile_end[-1:].astype(jnp.int32)
    xg = gather_rows(hn, row_tok)
    yg = moe_experts(xg, tile_e, ntiles, w13, w2)
    return moe_combine(h, route, yg, flat, final_w)


def _final_kernel(x_ref, w_ref, o_ref):
    o_ref[...] = _rms(x_ref[...], w_ref[...])


def final_norm(h, w, tm=512):
    m, d = h.shape
    tm = min(tm, m)
    return pl.pallas_call(
        _final_kernel,
        grid=(m // tm,),
        in_specs=[pl.BlockSpec((tm, d), lambda i: (i, 0)), pl.BlockSpec((1, d), lambda i: (0, 0))],
        out_specs=pl.BlockSpec((tm, d), lambda i: (i, 0)),
        out_shape=jax.ShapeDtypeStruct((m, d), F32),
        compiler_params=_cparams(("parallel",)),
    )(h, w.reshape(1, d).astype(F32))


def _rot_cols(w):
    half = w.shape[-1] // 2
    return jnp.concatenate([-w[..., half:], w[..., :half]], axis=-1)


_W_IN_SEGMENTS = (
    (U_ML_Q, O_ML_Q, 512, 1.0), (U_ML_K, O_ML_K, 512, 1.0), (U_ML_V, O_ML_V, 1024, 1.0),
    (U_ML_O, O_ML_O, 1024, 1.0), (U_SSM_Z, O_SSM_Z, 1024, 1.0), (U_SSM_X, O_SSM_XBC, SSM_DINNER, 1.0),
    (U_SWA_Q, O_SWA_Q, 1024, SWA_HEAD_DIM ** -0.5 * LOG2E), (U_SSM_BC, O_SSM_XBC + SSM_DINNER, SSM_BC, 1.0),
    (U_MLA_CQ, O_MLA_CQ, MLA_Q_LORA, 1.0), (U_MLA_CKV, O_MLA_CKV, MLA_KV_LORA, 1.0),
    (U_SWA_K, O_SWA_K, 256, 1.0), (U_SWA_V, O_SWA_V, 256, 1.0), (U_GATE, O_GATE, N_BRANCH * D_MODEL, 1.0))


W_IN_TILE = 256
_ROPE_TILE = U_MLA_KR // W_IN_TILE
_GATE_TILE = U_TOTAL // W_IN_TILE


def _w_in_tile_sources():
    src = [0] * (U_TOTAL // W_IN_TILE)
    for dst, s, width, _ in _W_IN_SEGMENTS:
        for off in range(0, width, W_IN_TILE):
            src[(dst + off) // W_IN_TILE] = s + off
    src[_ROPE_TILE] = O_MLA_KR
    return src


def _w_in_layout_kernel(src_ref, w_ref, dt_ref, o_ref):
    t = pl.program_id(1)

    @pl.when(t == _GATE_TILE)
    def _():
        ngate = 2 * ML_HEADS
        o_ref[...] = jnp.concatenate(
            [w_ref[0, 0:ngate, :], dt_ref[0],
             jnp.zeros((W_IN_TILE - ngate - SSM_HEADS, o_ref.shape[1]), F32)], axis=0).astype(BF)

    @pl.when(jnp.logical_and(t != _ROPE_TILE, t != _GATE_TILE))
    def _():
        swa_q = jnp.logical_and(t >= U_SWA_Q // W_IN_TILE, t < U_SSM_BC // W_IN_TILE)
        scale = jnp.where(swa_q, SWA_HEAD_DIM ** -0.5 * LOG2E, 1.0)
        o_ref[...] = (w_ref[0] * scale).astype(BF)

    @pl.when(t == _ROPE_TILE)
    def _():
        half = MLA_ROPE // 2
        o_ref[0:MLA_ROPE, :] = w_ref[0, 0:MLA_ROPE, :].astype(BF)
        o_ref[MLA_ROPE:MLA_ROPE + half, :] = (-w_ref[0, half:MLA_ROPE, :]).astype(BF)
        o_ref[MLA_ROPE + half:2 * MLA_ROPE, :] = w_ref[0, 0:half, :].astype(BF)
        o_ref[2 * MLA_ROPE:, :] = jnp.zeros((W_IN_TILE - 2 * MLA_ROPE, o_ref.shape[1]), BF)


def layout_w_in(w_in):
    depth, d, _ = w_in.shape
    wt = jnp.swapaxes(w_in, 1, 2)
    src = jnp.asarray([s // SUBLANES for s in _w_in_tile_sources() + [O_ML_I]], jnp.int32)
    return pl.pallas_call(
        _w_in_layout_kernel,
        grid_spec=pltpu.PrefetchScalarGridSpec(
            num_scalar_prefetch=1, grid=(depth, _GATE_TILE + 1),
            in_specs=[pl.BlockSpec((pl.Element(1), pl.Element(W_IN_TILE), pl.Element(d)),
                                   lambda l, t, src_ref: (l, src_ref[t] * SUBLANES, 0)),
                      pl.BlockSpec((pl.Element(1), pl.Element(SSM_HEADS), pl.Element(d)),
                                   lambda l, t, src_ref: (l, O_SSM_DT, 0))],
            out_specs=pl.BlockSpec((None, W_IN_TILE, d), lambda l, t, src_ref: (l, t, 0))),
        out_shape=jax.ShapeDtypeStruct((depth, U_TOTAL + W_IN_TILE, d), BF),
        compiler_params=_cparams(("parallel", "parallel")),
    )(src, wt, wt)


def _layout_w_uq(w):
    k = w.shape[0]
    w = w.reshape(k, MLA_HEADS, MLA_NOPE + MLA_ROPE)
    rope = w[..., MLA_NOPE:]
    scale = (MLA_NOPE + MLA_ROPE) ** -0.5 * LOG2E
    return (jnp.concatenate([w, _rot_cols(rope)], axis=-1).reshape(k, -1) * scale).astype(BF)


def _rope_table(seq):
    inv_freq = 1.0 / (ROPE_THETA ** (jnp.arange(0, MLA_ROPE, 2, dtype=F32) / MLA_ROPE))
    ang = jnp.arange(seq, dtype=F32)[:, None] * inv_freq[None, :]
    c, s = jnp.cos(ang), jnp.sin(ang)
    return jnp.concatenate([c, c, s, s], axis=1)


def hybrid_mixer(h, bsz, seq, layer, norm_w, w_main, ml_ib, ml_fb, ml_norm, conv_w, conv_b, dt_bias, a_log,
                 ssm_d, ssm_norm, q_norm, w_uq, kv_norm, w_ukv, sinks, w_branch, w_out, cs):
    n = bsz * seq
    nc = seq // CHUNK
    u2, small = matmul(h, w_main, norm=norm_w, out_dtype=BF, w_layer=layer, w_t=True, n=U_TOTAL, tn=2048,
                       side_w=w_main, side_rows=(U_TOTAL // LANES, LANES))
    u3 = u2.reshape(bsz, seq, U_TOTAL)

    def to_rows(cols, heads):
        return cols.reshape(bsz, seq, heads).transpose(0, 2, 1).reshape(bsz * heads * nc, CHUNK)

    def per_row(vec, heads):
        return jnp.broadcast_to(vec.astype(F32)[None, :, None], (bsz, heads, nc)).reshape(-1, 1)

    ig, bcum, dt, acs = recurrence_gates(
        to_rows(small[:, 0:ML_HEADS], ML_HEADS), to_rows(small[:, ML_HEADS:2 * ML_HEADS], ML_HEADS),
        per_row(ml_ib, ML_HEADS), per_row(ml_fb, ML_HEADS),
        to_rows(small[:, 2 * ML_HEADS:2 * ML_HEADS + SSM_HEADS], SSM_HEADS),
        per_row(dt_bias, SSM_HEADS), per_row(a_log, SSM_HEADS))

    def as_rows(x, heads):
        return x.reshape(bsz, heads, nc, CHUNK).transpose(0, 2, 1, 3)

    def as_cols(x, heads):
        return x.reshape(bsz, heads, seq).transpose(0, 2, 1)

    ya = mlstm_branch(u3, as_rows(ig, ML_HEADS), as_rows(bcum, ML_HEADS), as_cols(bcum, ML_HEADS), ml_norm)
    yb = ssd_branch(u3, as_cols(dt, SSM_HEADS), as_cols(acs, SSM_HEADS), as_rows(acs, SSM_HEADS), conv_w, conv_b,
                    jnp.repeat(ssm_d, SSM_HEADDIM), ssm_norm)
    qf = matmul(u2, w_uq, norm=q_norm, x_col_blk=U_MLA_CQ // MLA_Q_LORA, tn=2048)
    kvf = matmul(u2, w_ukv, norm=kv_norm, x_col_blk=U_MLA_CKV // MLA_KV_LORA, tn=2048)
    yc = mla_attention(qf.reshape(bsz, seq, -1), kvf.reshape(bsz, seq, -1), u3, cs)
    yd = swa_branch(u3, sinks)
    merged = gated_merge(ya.reshape(n, -1), yb.reshape(n, -1), yc.reshape(n, -1), yd.reshape(n, -1), w_branch, u2)
    return matmul(merged, w_out, residual=h, out_dtype=F32, tm=512, tn=2048)


def kernel(x, mem, norm_mix, w_in, ml_igate_bias, ml_fgate_bias, ml_norm, ssm_conv_w, ssm_conv_b, ssm_dt_bias, ssm_a_log, ssm_d, ssm_norm, mla_q_norm, mla_w_uq, mla_kv_norm, mla_w_ukv, swa_sinks, w_branch, w_out, norm_cross, norm_mem, xa_wq, xa_wkv, xa_wo, norm_ffn, ffn_w13, ffn_w2, moe_router, moe_w13, moe_w2, norm_final):
    bsz, seq, d = x.shape
    depth = w_in.shape[0]
    n = bsz * seq
    mlen = mem.shape[1]
    cs = _rope_table(seq)
    h = x.reshape(n, d)
    mem2 = mem.reshape(bsz * mlen, d)
    w_main = layout_w_in(w_in)
    for l in range(depth):
        h = hybrid_mixer(h, bsz, seq, l, norm_mix[l], w_main, ml_igate_bias[l], ml_fgate_bias[l], ml_norm[l],
                         ssm_conv_w[l], ssm_conv_b[l], ssm_dt_bias[l], ssm_a_log[l], ssm_d[l], ssm_norm[l],
                         mla_q_norm[l], _layout_w_uq(mla_w_uq[l]), mla_kv_norm[l], mla_w_ukv[l].astype(BF),
                         swa_sinks[l], w_branch[l].astype(BF), w_out[l].astype(BF), cs)
        kv = matmul(mem2, xa_wkv[l].astype(BF), norm=norm_mem[l])
        wq = (xa_wq[l] * (XA_HEAD_DIM ** -0.5 * LOG2E)).astype(BF)
        h = cross_attention(h, seq, norm_cross[l], wq, kv.reshape(bsz, mlen, -1), xa_wo[l].astype(BF))
        if l % 2 == 0:
            h = dense_ffn(h, norm_ffn[l], ffn_w13[l // 2].astype(BF), ffn_w2[l // 2].astype(BF))
        else:
            h = moe_ffn(h, norm_ffn[l], moe_router[l // 2], moe_w13[l // 2], moe_w2[l // 2],
                        final_w=norm_final if l == depth - 1 else None)
    if depth % 2 == 1:
        h = final_norm(h, norm_final)
    return h.reshape(bsz, seq, d)
```
